```python
import jax
import jax.numpy as jnp
from jax import lax
import numpy as np

D_MODEL = 1024
BATCH = 2
SEQ = 8192
DEPTH = 4
DEC_BATCH = 32
DEC_SEQ = 4
PAST_LEN = 8192
PAGE_SIZE = 128

M_HEADS = 4
M_HD = D_MODEL // 16
M_WIDTH = M_HEADS * M_HD
M_CONV = 4
M_CHUNK = 64
G_HEADS = 4
G_DV = D_MODEL // 16
G_DK = G_DV // 2
G_WIDTH = G_HEADS * G_DV
G_RANK = 16
G_TAU = 16.0
G_CHUNK = 64
N_HEADS = 8
N_HD = D_MODEL // 16
N_KV = 2
N_REP = N_HEADS // N_KV
N_WIDTH = N_HEADS * N_HD
CMP_LEN = 32
CMP_STRIDE = 16
CMP_HIDDEN = 2 * N_HD
SLC_BLOCK = 64
SLC_TOPK = 16
WINDOW = 512
Q_BLOCK = 128
ROPE_THETA = 10000.0
D_FF = ((8 * D_MODEL + 3 * 256 - 1) // (3 * 256)) * 256

NEG = -1e30
FORCE_BONUS = 1e4
EPS = 1e-6

PROJ_LAYOUT = (
    ('m_qk', 2 * M_WIDTH), ('m_v', M_WIDTH), ('m_i', M_HEADS), ('m_f', M_HEADS), ('m_o', M_WIDTH),
    ('g_q', G_HEADS * G_DK), ('g_k', G_HEADS * G_DK), ('g_v', G_WIDTH), ('g_lr', G_RANK), ('g_o', G_WIDTH),
    ('n_q', N_WIDTH), ('n_kc', N_KV * N_HD), ('n_vc', N_KV * N_HD), ('n_ks', N_KV * N_HD),
    ('n_vs', N_KV * N_HD), ('n_kw', N_KV * N_HD), ('n_vw', N_KV * N_HD), ('n_gate', 3 * N_HEADS),
)
N_IN = sum(w for _, w in PROJ_LAYOUT)

kernel_name = 'hymba_mlstm_gla_nsa_step'


def rms(x):
    x32 = x.astype(jnp.float32)
    return x32 * lax.rsqrt(jnp.mean(x32 * x32, axis=-1, keepdims=True) + EPS)


def rmsnorm(x, g):
    return (rms(x) * g.astype(jnp.float32)).astype(x.dtype)


def head_rmsnorm(h, n_heads, g):
    b, t, w = h.shape
    return rms(h.reshape(b, t, n_heads, w // n_heads)).reshape(b, t, w) * g.astype(jnp.float32)


def rope(x, pos):
    half = x.shape[-1] // 2
    inv = ROPE_THETA ** (-jnp.arange(half, dtype=jnp.float32) / half)
    ang = pos.astype(jnp.float32)[:, None] * inv[None, :]
    cos = jnp.cos(ang)[:, None, :]
    sin = jnp.sin(ang)[:, None, :]
    x1, x2 = x[..., :half], x[..., half:]
    return jnp.concatenate([x1 * cos - x2 * sin, x2 * cos + x1 * sin], axis=-1)


def split_proj(u):
    out, off = {}, 0
    for name, w in PROJ_LAYOUT:
        out[name] = u[..., off:off + w]
        off += w
    return out


def run_chunks(step, state, xs, chunk):
    b, t = xs[0].shape[:2]
    if t <= chunk or t % chunk != 0:
        return step(state, xs)
    nc = t // chunk
    xs_c = tuple(jnp.moveaxis(a.reshape((b, nc, chunk) + a.shape[2:]), 1, 0) for a in xs)
    state, ys = lax.scan(step, state, xs_c)
    return state, jnp.moveaxis(ys, 0, 1).reshape((b, t) + ys.shape[3:])


def short_conv(xpad, w):
    w = w.astype(jnp.float32)
    t = xpad.shape[1] - (M_CONV - 1)
    acc = xpad[:, 0:t] * w[0]
    for j in range(1, M_CONV):
        acc = acc + xpad[:, j:j + t] * w[j]
    return acc


def mlstm_chunk(carry, inp):
    c, n, m = carry
    q, k, v, ig, lf = inp
    L = q.shape[1]
    causal = jnp.tril(jnp.ones((L, L), bool))
    bcum = jnp.cumsum(lf, axis=1)
    dmat = bcum[:, :, None, :] - bcum[:, None, :, :] + ig[:, None, :, :]
    dmat = jnp.where(causal[None, :, :, None], dmat, NEG)
    m_inter = m[:, None, :] + bcum
    m_new = jnp.maximum(m_inter, jnp.max(dmat, axis=2))
    w = jnp.exp(dmat - m_new[:, :, None, :])
    a_inter = jnp.exp(m_inter - m_new)
    s = jnp.einsum('bthd,bshd->btsh', q, k) * w
    num = a_inter[..., None] * jnp.einsum('bthd,bhed->bthe', q, c) + jnp.einsum('btsh,bshe->bthe', s, v)
    den = a_inter * jnp.einsum('bthd,bhd->bth', q, n) + jnp.sum(s, axis=2)
    h = num / jnp.maximum(jnp.abs(den), jnp.exp(-m_new))[..., None]
    w_last, a_last = w[:, -1], a_inter[:, -1]
    c_new = a_last[..., None, None] * c + jnp.einsum('bsh,bshe,bshd->bhed', w_last, v, k)
    n_new = a_last[..., None] * n + jnp.einsum('bsh,bshd->bhd', w_last, k)
    return (c_new, n_new, m_new[:, -1]), h


def mlstm_mixer(p, conv_buf, c, n, m, w_conv, b_if, g_hnorm):
    b, t = p['m_v'].shape[:2]
    xin = jnp.concatenate([conv_buf.astype(jnp.float32), p['m_qk']], axis=1)
    qk = jax.nn.silu(short_conv(xin, w_conv))
    q = qk[..., :M_WIDTH].reshape(b, t, M_HEADS, M_HD)
    k = qk[..., M_WIDTH:].reshape(b, t, M_HEADS, M_HD) * (M_HD ** -0.5)
    v = p['m_v'].reshape(b, t, M_HEADS, M_HD)
    b_if = b_if.astype(jnp.float32)
    ig = p['m_i'] + b_if[:M_HEADS]
    lf = jax.nn.log_sigmoid(p['m_f'] + b_if[M_HEADS:])
    state = (c.astype(jnp.float32), n.astype(jnp.float32), m.astype(jnp.float32))
    (c, n, m), h = run_chunks(mlstm_chunk, state, (q, k, v, ig, lf), M_CHUNK)
    h = jax.nn.sigmoid(p['m_o']) * h.reshape(b, t, M_WIDTH)
    h = head_rmsnorm(h, M_HEADS, g_hnorm)
    return h, c, n, m, xin[:, -(M_CONV - 1):]


def gla_chunk(s_state, inp):
    q, k, v, g = inp
    L = q.shape[1]
    causal = jnp.tril(jnp.ones((L, L), bool))
    bc = jnp.cumsum(g, axis=1)
    diff = bc[:, :, None] - bc[:, None, :]
    decay = jnp.exp(jnp.where(causal[None, :, :, None, None], diff, NEG))
    a = jnp.einsum('bthk,bshk,btshk->btsh', q, k, decay)
    o = jnp.einsum('bthk,bhkv->bthv', q * jnp.exp(bc), s_state) + jnp.einsum('btsh,bshv->bthv', a, v)
    last = bc[:, -1]
    s_new = jnp.exp(last)[..., None] * s_state + jnp.einsum('bshk,bshv->bhkv', k * jnp.exp(last[:, None] - bc), v)
    return s_new, o


def gla_mixer(p, s_state, w_gate2, b_gate, g_hnorm):
    b, t = p['g_v'].shape[:2]
    q = p['g_q'].reshape(b, t, G_HEADS, G_DK) * (G_DK ** -0.5)
    k = p['g_k'].reshape(b, t, G_HEADS, G_DK)
    v = p['g_v'].reshape(b, t, G_HEADS, G_DV)
    z = p['g_lr'] @ w_gate2.astype(jnp.float32) + b_gate.astype(jnp.float32)
    g = (jax.nn.log_sigmoid(z) / G_TAU).reshape(b, t, G_HEADS, G_DK)
    s_state, o = run_chunks(gla_chunk, s_state.astype(jnp.float32), (q, k, v, g), G_CHUNK)
    o = head_rmsnorm(o.reshape(b, t, G_WIDTH), G_HEADS, g_hnorm) * jax.nn.silu(p['g_o'])
    return o, s_state


def nsa_rows(p, pos, g_qk):
    b, t = p['n_q'].shape[:2]
    heads = lambda a, nh: a.reshape(b, t, nh, N_HD)
    q = rope(rmsnorm(heads(p['n_q'], N_HEADS), g_qk[0]), pos)
    kc, vc = heads(p['n_kc'], N_KV), heads(p['n_vc'], N_KV)
    ks = rope(rmsnorm(heads(p['n_ks'], N_KV), g_qk[2]), pos)
    vs = heads(p['n_vs'], N_KV)
    kw = rope(rmsnorm(heads(p['n_kw'], N_KV), g_qk[3]), pos)
    vw = heads(p['n_vw'], N_KV)
    gates = jax.nn.sigmoid(p['n_gate']).reshape(b, t, N_HEADS, 3)
    return q, kc, vc, ks, vs, kw, vw, gates


def compress(x, pe, w1, w2):
    b, L = x.shape[:2]
    nc = (L - CMP_LEN) // CMP_STRIDE + 1
    idx = jnp.arange(nc)[:, None] * CMP_STRIDE + jnp.arange(CMP_LEN)[None, :]
    blk = x[:, idx] + pe.astype(jnp.float32)[None, None, :, None, :]
    blk = jnp.moveaxis(blk, 3, 2).reshape(b, nc, N_KV, CMP_LEN * N_HD)
    return jax.nn.silu(blk @ w1.astype(jnp.float32)) @ w2.astype(jnp.float32)


def nsa_context(kc_raw, vc_raw, ks, vs, g_kc, w_pe, w_c1, w_c2):
    b, L = kc_raw.shape[:2]
    nc = (L - CMP_LEN) // CMP_STRIDE + 1
    kc = compress(kc_raw, w_pe[0], w_c1[0], w_c2[0])
    kc = rope(rmsnorm(kc, g_kc), jnp.arange(nc) * CMP_STRIDE + CMP_LEN - 1)
    vc = compress(vc_raw, w_pe[1], w_c1[1], w_c2[1])
    ns = -(-L // SLC_BLOCK)
    pad = ((0, 0), (0, ns * SLC_BLOCK - L), (0, 0), (0, 0))
    ks_blk = jnp.pad(ks, pad).reshape(b, ns, SLC_BLOCK, N_KV, N_HD)
    vs_blk = jnp.pad(vs, pad).reshape(b, ns, SLC_BLOCK, N_KV, N_HD)
    return kc, vc, ks_blk, vs_blk


def nsa_core(q, qpos, kc, vc, ks_blk, vs_blk, kw, vw, kwpos, gates):
    b, t = q.shape[:2]
    nc, ns = kc.shape[1], ks_blk.shape[1]
    qg = q.reshape(b, t, N_KV, N_REP, N_HD) * (N_HD ** -0.5)
    c_end = jnp.arange(nc) * CMP_STRIDE + CMP_LEN - 1
    c_ok = c_end[None, :] <= qpos[:, None]
    s = jnp.einsum('btgrd,bngd->btgrn', qg, kc)
    s = jnp.where(c_ok[None, :, None, None, :], s, NEG)
    p_c = jax.nn.softmax(s, axis=-1) * jnp.any(c_ok, axis=-1).astype(jnp.float32)[None, :, None, None, None]
    o_c = jnp.einsum('btgrn,bngd->btgrd', p_c, vc)
    units = jnp.arange(nc)[:, None] + jnp.arange(CMP_LEN // CMP_STRIDE)[None, :]
    cmp2slc = jnp.sum((units // (SLC_BLOCK // CMP_STRIDE))[:, :, None] == jnp.arange(ns)[None, None, :], axis=1).astype(jnp.float32)
    imp = jnp.einsum('btgn,ns->btgs', jnp.sum(p_c, axis=3), cmp2slc)
    blk = jnp.arange(ns)[None, :]
    cur = (qpos // SLC_BLOCK)[:, None]
    forced = ((blk == 0) | (blk == cur) | (blk == cur - 1)).astype(jnp.float32)
    blk_ok = blk * SLC_BLOCK <= qpos[:, None]
    imp = jnp.where(blk_ok[None, :, None, :], imp + FORCE_BONUS * forced[None, :, None, :], NEG)
    _, sel = lax.top_k(imp, min(SLC_TOPK, ns))
    bi = jnp.arange(b)[:, None, None, None]
    gi = jnp.arange(N_KV)[None, None, :, None]
    k_sel = jnp.moveaxis(ks_blk, 3, 1)[bi, gi, sel]
    v_sel = jnp.moveaxis(vs_blk, 3, 1)[bi, gi, sel]
    tok = sel[..., None] * SLC_BLOCK + jnp.arange(SLC_BLOCK)
    s_ok = (tok <= qpos[None, :, None, None, None]).reshape(b, t, N_KV, 1, -1)
    s = jnp.einsum('btgrd,btgkjd->btgrkj', qg, k_sel).reshape(b, t, N_KV, N_REP, -1)
    p_s = jax.nn.softmax(jnp.where(s_ok, s, NEG), axis=-1)
    o_s = jnp.einsum('btgrm,btgmd->btgrd', p_s, v_sel.reshape(b, t, N_KV, -1, N_HD))
    dist = qpos[:, None] - kwpos[None, :]
    w_ok = (dist >= 0) & (dist < WINDOW) & (kwpos[None, :] >= 0)
    s = jnp.einsum('btgrd,bwgd->btgrw', qg, kw)
    p_w = jax.nn.softmax(jnp.where(w_ok[None, :, None, None, :], s, NEG), axis=-1)
    o_w = jnp.einsum('btgrw,bwgd->btgrd', p_w, vw)
    gg = gates.reshape(b, t, N_KV, N_REP, 3)
    o = gg[..., 0:1] * o_c + gg[..., 1:2] * o_s + gg[..., 2:3] * o_w
    return o.reshape(b, t, N_WIDTH)


def nsa_prompt(q, kc_raw, vc_raw, ks, vs, kw, vw, gates, g_kc, w_pe, w_c1, w_c2):
    b, t = q.shape[:2]
    kc, vc, ks_blk, vs_blk = nsa_context(kc_raw, vc_raw, ks, vs, g_kc, w_pe, w_c1, w_c2)
    pad = ((0, 0), (WINDOW, 0), (0, 0), (0, 0))
    kw_pad, vw_pad = jnp.pad(kw, pad), jnp.pad(vw, pad)
    nb = t // Q_BLOCK

    def one_block(args):
        j, qb, gb = args
        start = j * Q_BLOCK
        qpos = start + jnp.arange(Q_BLOCK)
        kwb = lax.dynamic_slice_in_dim(kw_pad, start, WINDOW + Q_BLOCK, axis=1)
        vwb = lax.dynamic_slice_in_dim(vw_pad, start, WINDOW + Q_BLOCK, axis=1)
        kwpos = start - WINDOW + jnp.arange(WINDOW + Q_BLOCK)
        return nsa_core(qb, qpos, kc, vc, ks_blk, vs_blk, kwb, vwb, kwpos, gb)

    qs = jnp.moveaxis(q.reshape(b, nb, Q_BLOCK, N_HEADS, N_HD), 1, 0)
    gs = jnp.moveaxis(gates.reshape(b, nb, Q_BLOCK, N_HEADS, 3), 1, 0)
    out = lax.map(one_block, (jnp.arange(nb), qs, gs))
    return jnp.moveaxis(out, 0, 1).reshape(b, t, N_WIDTH)


def nsa_sample(q, pos, kc_new, vc_new, ks_new, vs_new, kw_new, vw_new, gates,
               cmp_pages, slc_pages, win, page_table, g_kc, w_pe, w_c1, w_c2):
    db, td = q.shape[:2]
    past = page_table.shape[1] * cmp_pages.shape[1]

    def gather(pages):
        return pages[page_table].reshape(db, past, 2, N_KV, N_HD).astype(jnp.float32)

    pc, psl = gather(cmp_pages), gather(slc_pages)
    kc_full = jnp.concatenate([pc[:, :, 0], kc_new], axis=1)
    vc_full = jnp.concatenate([pc[:, :, 1], vc_new], axis=1)
    ks_full = jnp.concatenate([psl[:, :, 0], ks_new], axis=1)
    vs_full = jnp.concatenate([psl[:, :, 1], vs_new], axis=1)
    kc, vc, ks_blk, vs_blk = nsa_context(kc_full, vc_full, ks_full, vs_full, g_kc, w_pe, w_c1, w_c2)
    win = win.astype(jnp.float32)
    wb = win.shape[1]
    kw_all = jnp.concatenate([win[:, :, 0], kw_new], axis=1)
    vw_all = jnp.concatenate([win[:, :, 1], vw_new], axis=1)
    kwpos = past - wb + jnp.arange(wb + td)
    out = nsa_core(q, pos, kc, vc, ks_blk, vs_blk, kw_all, vw_all, kwpos, gates)
    new_win = jnp.stack([kw_all, vw_all], axis=2)[:, -wb:]
    return out, new_win


def layer_pre(x, g_norm, w_in):
    h = rmsnorm(x, g_norm)
    return split_proj((h @ w_in).astype(jnp.float32))


def layer_post(x, hm, hg, hn, w_out, g_ffn, w_up, w_down):
    mix = jnp.concatenate([hm, hg, hn], axis=-1).astype(x.dtype)
    x = x + mix @ w_out
    h = rmsnorm(x, g_ffn)
    a, u = jnp.split(h @ w_up, 2, axis=-1)
    return x + (jax.nn.silu(a) * u) @ w_down


def setup_inputs(seed: int = 0) -> dict:
    key = jax.random.key(seed)
    ks = jax.random.split(key, 32)
    n_pages = PAST_LEN // PAGE_SIZE
    n_used = DEC_BATCH * n_pages
    n_pool = n_used + n_used // 4
    win_buf = min(WINDOW, PAST_LEN)

    def nrm(k, shape, s):
        return s * jax.random.normal(k, shape, jnp.float32)

    return {
        'x_prompt': nrm(ks[0], (BATCH, SEQ, D_MODEL), 1.0),
        'x_sample': nrm(ks[1], (DEC_BATCH, DEC_SEQ, D_MODEL), 1.0),
        'cache_cmp_kv': nrm(ks[2], (DEPTH, n_pool, PAGE_SIZE, 2, N_KV, N_HD), 1.0),
        'cache_slc_kv': nrm(ks[3], (DEPTH, n_pool, PAGE_SIZE, 2, N_KV, N_HD), 1.0),
        'state_win_kv': nrm(ks[4], (DEPTH, DEC_BATCH, win_buf, 2, N_KV, N_HD), 1.0),
        'state_mlstm_C': nrm(ks[5], (DEPTH, DEC_BATCH, M_HEADS, M_HD, M_HD), 0.1),
        'state_mlstm_n': nrm(ks[6], (DEPTH, DEC_BATCH, M_HEADS, M_HD), 0.1),
        'state_mlstm_m': nrm(ks[7], (DEPTH, DEC_BATCH, M_HEADS), 1.0),
        'state_mlstm_conv': nrm(ks[8], (DEPTH, DEC_BATCH, M_CONV - 1, 2 * M_WIDTH), 1.0),
        'state_gla_S': nrm(ks[9], (DEPTH, DEC_BATCH, G_HEADS, G_DK, G_DV), 0.1),
        'page_table': jax.random.permutation(ks[10], n_pool)[:n_used].reshape(DEC_BATCH, n_pages).astype(jnp.int32),
        'w_norm_mix': 1.0 + nrm(ks[11], (DEPTH, D_MODEL), 0.02),
        'w_in': nrm(ks[12], (DEPTH, D_MODEL, N_IN), D_MODEL ** -0.5),
        'b_mlstm_if': jnp.concatenate([nrm(ks[13], (DEPTH, M_HEADS), 0.1),
                                       3.0 + nrm(ks[14], (DEPTH, M_HEADS), 0.5)], axis=-1),
        'w_mlstm_conv': nrm(ks[15], (DEPTH, M_CONV, 2 * M_WIDTH), M_CONV ** -0.5),
        'w_mlstm_hnorm': 1.0 + nrm(ks[16], (DEPTH, M_WIDTH), 0.02),
        'w_gla_gate2': nrm(ks[17], (DEPTH, G_RANK, G_HEADS * G_DK), G_RANK ** -0.5),
        'b_gla_gate': nrm(ks[18], (DEPTH, G_HEADS * G_DK), 0.1),
        'w_gla_hnorm': 1.0 + nrm(ks[19], (DEPTH, G_WIDTH), 0.02),
        'w_qk_norm': 1.0 + nrm(ks[20], (DEPTH, 4, N_HD), 0.02),
        'w_cmp_pe': nrm(ks[21], (DEPTH, 2, CMP_LEN, N_HD), 0.1),
        'w_cmp_1': nrm(ks[22], (DEPTH, 2, CMP_LEN * N_HD, CMP_HIDDEN), (CMP_LEN * N_HD) ** -0.5),
        'w_cmp_2': nrm(ks[23], (DEPTH, 2, CMP_HIDDEN, N_HD), CMP_HIDDEN ** -0.5),
        'w_out': nrm(ks[24], (DEPTH, D_MODEL, D_MODEL), D_MODEL ** -0.5),
        'w_norm_ffn': 1.0 + nrm(ks[25], (DEPTH, D_MODEL), 0.02),
        'w_ffn_up': nrm(ks[26], (DEPTH, D_MODEL, 2 * D_FF), D_MODEL ** -0.5),
        'w_ffn_down': nrm(ks[27], (DEPTH, D_FF, D_MODEL), D_FF ** -0.5),
    }


def reference(x_prompt, x_sample, cache_cmp_kv, cache_slc_kv, state_win_kv, state_mlstm_C, state_mlstm_n,
              state_mlstm_m, state_mlstm_conv, state_gla_S, page_table, w_norm_mix, w_in, b_mlstm_if,
              w_mlstm_conv, w_mlstm_hnorm, w_gla_gate2, b_gla_gate, w_gla_hnorm, w_qk_norm, w_cmp_pe,
              w_cmp_1, w_cmp_2, w_out, w_norm_ffn, w_ffn_up, w_ffn_down):
    f32 = jnp.float32
    b, t = x_prompt.shape[:2]
    db, td = x_sample.shape[:2]
    past = page_table.shape[1] * cache_cmp_kv.shape[2]
    pos_p = jnp.arange(t)
    pos_s = past + jnp.arange(td)
    zero_conv = jnp.zeros((b, M_CONV - 1, 2 * M_WIDTH), f32)
    zero_c = jnp.zeros((b, M_HEADS, M_HD, M_HD), f32)
    zero_n = jnp.zeros((b, M_HEADS, M_HD), f32)
    zero_m = jnp.zeros((b, M_HEADS), f32)
    zero_s = jnp.zeros((b, G_HEADS, G_DK, G_DV), f32)
    xp, xs = x_prompt, x_sample
    pl = [[] for _ in range(8)]
    sl = [[] for _ in range(8)]
    for l in range(DEPTH):
        p = layer_pre(xp, w_norm_mix[l], w_in[l])
        hm, c, n, m, cb = mlstm_mixer(p, zero_conv, zero_c, zero_n, zero_m, w_mlstm_conv[l], b_mlstm_if[l], w_mlstm_hnorm[l])
        hg, s_gla = gla_mixer(p, zero_s, w_gla_gate2[l], b_gla_gate[l], w_gla_hnorm[l])
        q, kc, vc, ks, vs, kw, vw, gates = nsa_rows(p, pos_p, w_qk_norm[l])
        hn = nsa_prompt(q, kc, vc, ks, vs, kw, vw, gates, w_qk_norm[l][1], w_cmp_pe[l], w_cmp_1[l], w_cmp_2[l])
        xp = layer_post(xp, hm, hg, hn, w_out[l], w_norm_ffn[l], w_ffn_up[l], w_ffn_down[l])
        for lst, val in zip(pl, (jnp.stack([kc, vc], axis=2), jnp.stack([ks, vs], axis=2),
                                 jnp.stack([kw, vw], axis=2)[:, -min(WINDOW, t):], c, n, m, cb, s_gla)):
            lst.append(val)
        p = layer_pre(xs, w_norm_mix[l], w_in[l])
        hm, c, n, m, cb = mlstm_mixer(p, state_mlstm_conv[l], state_mlstm_C[l], state_mlstm_n[l], state_mlstm_m[l],
                                      w_mlstm_conv[l], b_mlstm_if[l], w_mlstm_hnorm[l])
        hg, s_gla = gla_mixer(p, state_gla_S[l], w_gla_gate2[l], b_gla_gate[l], w_gla_hnorm[l])
        q, kc, vc, ks, vs, kw, vw, gates = nsa_rows(p, pos_s, w_qk_norm[l])
        hn, new_win = nsa_sample(q, pos_s, kc, vc, ks, vs, kw, vw, gates, cache_cmp_kv[l], cache_slc_kv[l],
                                 state_win_kv[l], page_table, w_qk_norm[l][1], w_cmp_pe[l], w_cmp_1[l], w_cmp_2[l])
        xs = layer_post(xs, hm, hg, hn, w_out[l], w_norm_ffn[l], w_ffn_up[l], w_ffn_down[l])
        for lst, val in zip(sl, (jnp.stack([kc, vc], axis=2), jnp.stack([ks, vs], axis=2),
                                 new_win, c, n, m, cb, s_gla)):
            lst.append(val)
    p_cmp_kv, p_slc_kv, p_win_kv, p_mlstm_C, p_mlstm_n, p_mlstm_m, p_mlstm_conv, p_gla_S = [jnp.stack(a) for a in pl]
    s_cmp_kv, s_slc_kv, s_win_kv, s_mlstm_C, s_mlstm_n, s_mlstm_m, s_mlstm_conv, s_gla_S = [jnp.stack(a) for a in sl]
    return (xp, xs, p_cmp_kv, p_slc_kv, p_win_kv, p_mlstm_C, p_mlstm_n, p_mlstm_m, p_mlstm_conv, p_gla_S,
            s_cmp_kv, s_slc_kv, s_win_kv, s_mlstm_C, s_mlstm_n, s_mlstm_m, s_mlstm_conv, s_gla_S)
```

```python
import functools
import math

import numpy as np
import jax
import jax.numpy as jnp
from jax import lax
from jax.experimental import pallas as pl
from jax.experimental.pallas import tpu as pltpu

F32 = jnp.float32
BF16 = jnp.bfloat16

D_MODEL = 1024
M_HEADS, M_HD, M_WIDTH, M_CONV, M_CHUNK = 4, 64, 256, 4, 64
G_HEADS, G_DK, G_DV, G_WIDTH, G_RANK, G_TAU, G_CHUNK = 4, 32, 64, 256, 16, 16.0, 64
N_HEADS, N_HD, N_KV, N_REP, N_WIDTH = 8, 64, 2, 4, 512
CMP_LEN, CMP_STRIDE, CMP_HIDDEN = 32, 16, 128
SLC_BLOCK, SLC_TOPK, WINDOW = 64, 16, 512
ROPE_THETA = 10000.0
D_FF = 2816
PAGE_SIZE = 128
NEG = -1e30
FORCE_BONUS = 1e4
EPS = 1e-6

LANES = 128
VMEM_LIMIT = 56 * 1024 * 1024
KV_TILE = 512
SAMPLE_T = 16

_OFF = {}
_o = 0
for _name, _w in (('m_qk', 512), ('m_v', 256), ('m_i', 4), ('m_f', 4), ('m_o', 256),
                  ('g_q', 128), ('g_k', 128), ('g_v', 256), ('g_lr', 16), ('g_o', 256),
                  ('n_q', 512), ('n_kc', 128), ('n_vc', 128), ('n_ks', 128),
                  ('n_vs', 128), ('n_kw', 128), ('n_vw', 128), ('n_gate', 24)):
    _OFF[_name] = (_o, _w)
    _o += _w
UM_W, UG_W, NQ_W, NCMP_W, NSW_W, NGATE_W = 1152, 896, 512, 256, 512, 128


def _cparams(sem):
    return pltpu.CompilerParams(dimension_semantics=sem, vmem_limit_bytes=VMEM_LIMIT)


def _dot(a, b):
    return jnp.dot(a, b, preferred_element_type=F32)


def _dot_nt(a, b):
    return lax.dot_general(a, b, (((1,), (1,)), ((), ())), preferred_element_type=F32)


def _dot_hi(a, b):
    return jnp.dot(a, b, preferred_element_type=F32, precision=lax.Precision.HIGHEST)


def _dot_split(a, b_bf):
    hi = a.astype(BF16)
    lo = (a - hi.astype(F32)).astype(BF16)
    return _dot(hi, b_bf) + _dot(lo, b_bf)


def _log_sigmoid(x):
    return jnp.minimum(x, 0.0) - jnp.log(1.0 + jnp.exp(-jnp.abs(x)))


def _sigmoid(x):
    return 1.0 / (1.0 + jnp.exp(-x))


def _transpose(x):
    r, c = x.shape
    if r < LANES:
        x = jnp.concatenate([x, jnp.zeros((LANES - r, c), x.dtype)], axis=0)
    parts = [x[:, i * LANES:(i + 1) * LANES].T[:, :r] for i in range(c // LANES)]
    return parts[0] if len(parts) == 1 else jnp.concatenate(parts, axis=0)


def _iota(shape, dim):
    return lax.broadcasted_iota(jnp.int32, shape, dim)


def _pre_kernel(x_ref, g_ref, w_ref, um_ref, ug_ref, nq_ref, ncmp_ref, nsw_ref, ngate_ref):
    x = x_ref[...]
    h = x * lax.rsqrt(jnp.mean(x * x, axis=-1, keepdims=True) + EPS) * g_ref[...]
    u = _dot(h.astype(BF16), w_ref[...])
    o = 0
    for ref, w in ((um_ref, UM_W), (ug_ref, UG_W), (nq_ref, NQ_W), (ncmp_ref, NCMP_W),
                   (nsw_ref, NSW_W), (ngate_ref, NGATE_W)):
        ref[...] = u[:, o:o + w]
        o += w


def _pre(x2, g, w_cat):
    m = x2.shape[0]
    tm = min(m, 256)
    widths = (UM_W, UG_W, NQ_W, NCMP_W, NSW_W, NGATE_W)
    return pl.pallas_call(
        _pre_kernel,
        grid=(m // tm,),
        in_specs=[pl.BlockSpec((tm, D_MODEL), lambda i: (i, 0)),
                  pl.BlockSpec((1, D_MODEL), lambda i: (0, 0)),
                  pl.BlockSpec((D_MODEL, sum(widths)), lambda i: (0, 0))],
        out_specs=[pl.BlockSpec((tm, w), lambda i: (i, 0)) for w in widths],
        out_shape=[jax.ShapeDtypeStruct((m, w), F32) for w in widths],
        compiler_params=_cparams(("parallel",)),
        name="pre_proj",
    )(x2, g, w_cat)


def _post_kernel(x_ref, hm_ref, hg_ref, hn_ref, wout_ref, g_ref, wup_ref, wdn_ref, o_ref, h2_ref, *, tf):
    @pl.when(pl.program_id(1) == 0)
    def _():
        xn = x_ref[...]
        xn = xn + _dot(hm_ref[...], wout_ref[0:256, :])
        xn = xn + _dot(hg_ref[...], wout_ref[256:512, :])
        xn = xn + _dot(hn_ref[...], wout_ref[512:1024, :])
        o_ref[...] = xn
        h2 = xn * lax.rsqrt(jnp.mean(xn * xn, axis=-1, keepdims=True) + EPS) * g_ref[...]
        h2_ref[...] = h2.astype(BF16)

    au = _dot(h2_ref[...], wup_ref[0])
    a = au[:, :tf]
    act = (a * _sigmoid(a) * au[:, tf:]).astype(BF16)
    o_ref[...] += _dot(act, wdn_ref[0])


def _post(x2, hm, hg, hn, w_out, g, w_up_r, w_dn_r):
    m = x2.shape[0]
    tm = min(m, 512)
    nf, _, tf2 = w_up_r.shape
    tf = tf2 // 2
    return pl.pallas_call(
        functools.partial(_post_kernel, tf=tf),
        grid=(m // tm, nf),
        in_specs=[pl.BlockSpec((tm, D_MODEL), lambda i, f: (i, 0)),
                  pl.BlockSpec((tm, M_WIDTH), lambda i, f: (i, 0)),
                  pl.BlockSpec((tm, G_WIDTH), lambda i, f: (i, 0)),
                  pl.BlockSpec((tm, N_WIDTH), lambda i, f: (i, 0)),
                  pl.BlockSpec((D_MODEL, D_MODEL), lambda i, f: (0, 0)),
                  pl.BlockSpec((1, D_MODEL), lambda i, f: (0, 0)),
                  pl.BlockSpec((1, D_MODEL, tf2), lambda i, f: (f, 0, 0)),
                  pl.BlockSpec((1, tf, D_MODEL), lambda i, f: (f, 0, 0))],
        out_specs=pl.BlockSpec((tm, D_MODEL), lambda i, f: (i, 0)),
        out_shape=jax.ShapeDtypeStruct((m, D_MODEL), F32),
        scratch_shapes=[pltpu.VMEM((tm, D_MODEL), BF16)],
        compiler_params=_cparams(("parallel", "arbitrary")),
        name="post_ffn",
    )(x2, hm, hg, hn, w_out, g, w_up_r, w_dn_r)


def _mlstm_kernel(um_ref, conv0_ref, ct0_ref, n0_ref, m0_ref, wconv_ref, bif_ref, gh_ref,
                  hm_ref, conv_out_ref, ct_out_ref, n_out_ref, m_out_ref,
                  xpad, q_s, k_s, ct_s, n_s, m_s, *, L, n_valid, blk):
    j = pl.program_id(1)

    @pl.when(j == 0)
    def _():
        xpad[0:8, :] = conv0_ref[0]
        ct_s[...] = ct0_ref[0]
        n_s[...] = n0_ref[0]
        m_s[...] = m0_ref[0]

    qk_pre = um_ref[0, :, 0:2 * M_WIDTH]
    xpad[8:8 + blk, :] = qk_pre
    wc = wconv_ref[...]
    acc = (xpad[5:5 + blk, :] * wc[0:1] + xpad[6:6 + blk, :] * wc[1:2]
           + xpad[7:7 + blk, :] * wc[2:3] + qk_pre * wc[3:4])
    qk = acc * _sigmoid(acc)
    q_s[...] = qk[:, :M_WIDTH]
    k_s[...] = qk[:, M_WIDTH:] * (M_HD ** -0.5)
    last = n_valid if blk == L else blk
    conv_out_ref[0] = xpad[last:last + 8, :]
    xpad[0:8, :] = xpad[blk:blk + 8, :]

    row = _iota((L, L), 0)
    col = _iota((L, L), 1)
    causal = row >= col
    tril = causal.astype(F32)
    lane_w = _iota((1, M_WIDTH), 1) // M_HD
    row_w = _iota((M_WIDTH, 1), 0) // M_HD
    valid_col = (_iota((L, 1), 0) < n_valid).astype(F32)

    def chunk(c, carry):
        r0 = pl.multiple_of(c * L, L)
        qc = q_s[pl.ds(r0, L), :]
        kc = k_s[pl.ds(r0, L), :]
        vc = um_ref[0, pl.ds(r0, L), 512:768]
        oc = um_ref[0, pl.ds(r0, L), 768:1024]
        act = um_ref[0, pl.ds(r0, L), 1024:1152] + bif_ref[...]
        lf = _log_sigmoid(act)
        bcum = _dot_hi(tril, lf)
        act_t = _transpose(act)
        bcum_t = _transpose(bcum)
        k_bf = kc.astype(BF16)
        kt_bf = _transpose(kc).astype(BF16)
        ct = ct_s[...]
        ct_bf = ct.astype(BF16)
        n_row = n_s[0:1, :]
        ct_new = ct
        n_new = n_row
        outs = []
        for h in range(M_HEADS):
            bcol = bcum[:, 4 + h:5 + h]
            igcol = act[:, h:h + 1]
            brow = bcum_t[4 + h:5 + h, :]
            igrow = act_t[h:h + 1, :]
            dmat = jnp.where(causal, bcol - brow + igrow, NEG)
            m_h = m_s[h:h + 1, 0:1]
            m_inter = m_h + bcol
            m_new = jnp.maximum(m_inter, jnp.max(dmat, axis=1, keepdims=True))
            w = jnp.exp(dmat - m_new)
            a_inter = jnp.exp(m_inter - m_new)
            hmask = lane_w == h
            q_h = jnp.where(hmask, qc, 0.0)
            q_bf = q_h.astype(BF16)
            s = _dot_nt(q_bf, k_bf) * w
            v_h = vc[:, h * M_HD:(h + 1) * M_HD]
            num = a_inter * _dot(q_bf, ct_bf) + _dot(s.astype(BF16), v_h.astype(BF16))
            qn = jnp.sum(q_h * n_row, axis=1, keepdims=True)
            den = a_inter * qn + jnp.sum(s, axis=1, keepdims=True)
            hh = num / jnp.maximum(jnp.abs(den), jnp.exp(-m_new))
            hh = _sigmoid(oc[:, h * M_HD:(h + 1) * M_HD]) * hh
            hh = hh * lax.rsqrt(jnp.mean(hh * hh, axis=1, keepdims=True) + EPS)
            outs.append(hh * gh_ref[:, h * M_HD:(h + 1) * M_HD])
            b_l = bcol[n_valid - 1:n_valid]
            m_l = m_new[n_valid - 1:n_valid]
            w_l = jnp.exp(b_l - bcol + igcol - m_l) * valid_col
            a_l = jnp.exp(m_inter[n_valid - 1:n_valid] - m_l)
            upd = _dot(kt_bf, (v_h * w_l).astype(BF16))
            ct_new = jnp.where(row_w == h, a_l * ct + upd, ct_new)
            n_new = jnp.where(hmask, a_l * n_row + jnp.sum(kc * w_l, axis=0, keepdims=True), n_new)
            m_s[h:h + 1, :] = jnp.broadcast_to(m_l, (1, LANES))
        ct_s[...] = ct_new
        n_s[...] = jnp.broadcast_to(n_new, n_s.shape)
        hm_ref[0, pl.ds(r0, L), :] = jnp.concatenate(outs, axis=1).astype(BF16)
        return carry

    lax.fori_loop(0, blk // L, chunk, 0)

    @pl.when(j == pl.num_programs(1) - 1)
    def _():
        ct_out_ref[0] = ct_s[...]
        n_out_ref[0] = n_s[...]
        m_out_ref[0] = m_s[...]


def _mlstm(um, conv0, ct0, n0, m0, wconv, bif, gh, *, L, n_valid):
    b, t, _ = um.shape
    blk = min(t, 512)
    kern = functools.partial(_mlstm_kernel, L=L, n_valid=n_valid, blk=blk)
    per_b = lambda shape: pl.BlockSpec((1,) + shape, lambda i, j: (i,) + (0,) * len(shape))
    const = lambda shape: pl.BlockSpec(shape, lambda i, j: (0,) * len(shape))
    return pl.pallas_call(
        kern,
        grid=(b, t // blk),
        in_specs=[pl.BlockSpec((1, blk, UM_W), lambda i, j: (i, j, 0)),
                  per_b((8, 2 * M_WIDTH)), per_b((M_WIDTH, M_HD)), per_b((8, M_WIDTH)), per_b((8, LANES)),
                  const((8, 2 * M_WIDTH)), const((1, LANES)), const((1, M_WIDTH))],
        out_specs=[pl.BlockSpec((1, blk, M_WIDTH), lambda i, j: (i, j, 0)),
                   per_b((8, 2 * M_WIDTH)), per_b((M_WIDTH, M_HD)), per_b((8, M_WIDTH)), per_b((8, LANES))],
        out_shape=[jax.ShapeDtypeStruct((b, t, M_WIDTH), BF16),
                   jax.ShapeDtypeStruct((b, 8, 2 * M_WIDTH), F32),
                   jax.ShapeDtypeStruct((b, M_WIDTH, M_HD), F32),
                   jax.ShapeDtypeStruct((b, 8, M_WIDTH), F32),
                   jax.ShapeDtypeStruct((b, 8, LANES), F32)],
        scratch_shapes=[pltpu.VMEM((blk + 8, 2 * M_WIDTH), F32),
                        pltpu.VMEM((blk, M_WIDTH), F32), pltpu.VMEM((blk, M_WIDTH), F32),
                        pltpu.VMEM((M_WIDTH, M_HD), F32), pltpu.VMEM((8, M_WIDTH), F32),
                        pltpu.VMEM((8, LANES), F32)],
        compiler_params=_cparams(("parallel", "arbitrary")),
        name="mlstm",
    )(um, conv0, ct0, n0, m0, wconv, bif, gh)


def _gla_kernel(ug_ref, s0_ref, w2_ref, bg_ref, gh_ref, hg_ref, s_out_ref, s_s, *, L, n_valid, blk):
    j = pl.program_id(1)

    @pl.when(j == 0)
    def _():
        s_s[...] = s0_ref[0]

    row = _iota((L, L), 0)
    col = _iota((L, L), 1)
    causal = row >= col
    tril = causal.astype(F32)
    lane_k = _iota((1, LANES), 1) // G_DK
    row_k = _iota((LANES, 1), 0) // G_DK
    valid_col = (_iota((L, 1), 0) < n_valid).astype(F32)
    mid = max(n_valid // 2, 1)

    def chunk(c, carry):
        r0 = pl.multiple_of(c * L, L)
        q = ug_ref[0, pl.ds(r0, L), 0:128] * (G_DK ** -0.5)
        k = ug_ref[0, pl.ds(r0, L), 128:256]
        v = ug_ref[0, pl.ds(r0, L), 256:512]
        go = ug_ref[0, pl.ds(r0, L), 512:768]
        glr = ug_ref[0, pl.ds(r0, L), 768:896]
        z = _dot_hi(glr, w2_ref[...]) + bg_ref[...]
        g = _log_sigmoid(z) * (1.0 / G_TAU)
        bc = _dot_hi(tril, g)
        c_ref = bc[mid - 1:mid]
        last = bc[n_valid - 1:n_valid]
        qe = (q * jnp.exp(bc - c_ref)).astype(BF16)
        ke = (k * jnp.exp(c_ref - bc)).astype(BF16)
        qin = q * jnp.exp(bc)
        kd_t = _transpose(k * jnp.exp(last - bc) * valid_col).astype(BF16)
        last_col = _transpose(jnp.broadcast_to(last, (8, LANES)))[:, 0:1]
        s_all = s_s[...]
        s_bf = s_all.astype(BF16)
        s_new = jnp.exp(last_col) * s_all
        outs = []
        for h in range(G_HEADS):
            hmask = lane_k == h
            a = jnp.where(causal, _dot_nt(jnp.where(hmask, qe, 0.0).astype(BF16), ke), 0.0)
            v_h = v[:, h * G_DV:(h + 1) * G_DV].astype(BF16)
            o = _dot(jnp.where(hmask, qin, 0.0).astype(BF16), s_bf) + _dot(a.astype(BF16), v_h)
            s_new = s_new + jnp.where(row_k == h, _dot(kd_t, v_h), 0.0)
            o = o * lax.rsqrt(jnp.mean(o * o, axis=1, keepdims=True) + EPS)
            g_o = go[:, h * G_DV:(h + 1) * G_DV]
            outs.append(o * gh_ref[:, h * G_DV:(h + 1) * G_DV] * (g_o * _sigmoid(g_o)))
        s_s[...] = s_new
        hg_ref[0, pl.ds(r0, L), :] = jnp.concatenate(outs, axis=1).astype(BF16)
        return carry

    lax.fori_loop(0, blk // L, chunk, 0)

    @pl.when(j == pl.num_programs(1) - 1)
    def _():
        s_out_ref[0] = s_s[...]


def _gla(ug, s0, w2p, bg, gh, *, L, n_valid):
    b, t, _ = ug.shape
    blk = min(t, 512)
    kern = functools.partial(_gla_kernel, L=L, n_valid=n_valid, blk=blk)
    return pl.pallas_call(
        kern,
        grid=(b, t // blk),
        in_specs=[pl.BlockSpec((1, blk, UG_W), lambda i, j: (i, j, 0)),
                  pl.BlockSpec((1, LANES, G_DV), lambda i, j: (i, 0, 0)),
                  pl.BlockSpec((LANES, LANES), lambda i, j: (0, 0)),
                  pl.BlockSpec((1, LANES), lambda i, j: (0, 0)),
                  pl.BlockSpec((1, G_WIDTH), lambda i, j: (0, 0))],
        out_specs=[pl.BlockSpec((1, blk, G_WIDTH), lambda i, j: (i, j, 0)),
                   pl.BlockSpec((1, LANES, G_DV), lambda i, j: (i, 0, 0))],
        out_shape=[jax.ShapeDtypeStruct((b, t, G_WIDTH), BF16),
                   jax.ShapeDtypeStruct((b, LANES, G_DV), F32)],
        scratch_shapes=[pltpu.VMEM((LANES, G_DV), F32)],
        compiler_params=_cparams(("parallel", "arbitrary")),
        name="gla",
    )(ug, s0, w2p, bg, gh)


def _group_mean_sq(x, bm_bf):
    return _dot_split(x * x, bm_bf)


def _rope_slab(y, cos, sin_signed):
    lane = _iota(y.shape, 1)
    rot = jnp.where((lane % N_HD) < (N_HD // 2), pltpu.roll(y, 96, axis=1), pltpu.roll(y, 32, axis=1))
    return y * cos + rot * sin_signed


def _norm_rope_slab(x, g, cos, sin_signed, bm_bf):
    y = x * lax.rsqrt(_group_mean_sq(x, bm_bf) + EPS) * g
    return _rope_slab(y, cos, sin_signed)


def _rows_kernel(nq_ref, nsw_ref, ngate_ref, cos_ref, sin_ref, gq_ref, gks_ref, gkw_ref, bm_ref,
                 q_ref, ks_ref, vs_ref, kw_ref, vw_ref, slc_ref, win_ref, gates_ref):
    cos = cos_ref[...]
    sin = sin_ref[...]
    bm = bm_ref[...]
    for sl in range(4):
        x = nq_ref[0, :, sl * LANES:(sl + 1) * LANES]
        y = _norm_rope_slab(x, gq_ref[:, sl * LANES:(sl + 1) * LANES], cos, sin, bm) * (N_HD ** -0.5)
        y = y.astype(BF16)
        q_ref[0, 2 * sl] = y[:, :N_HD]
        q_ref[0, 2 * sl + 1] = y[:, N_HD:]
    ks = _norm_rope_slab(nsw_ref[0, :, 0:128], gks_ref[...], cos, sin, bm)
    vs = nsw_ref[0, :, 128:256]
    kw = _norm_rope_slab(nsw_ref[0, :, 256:384], gkw_ref[...], cos, sin, bm)
    vw = nsw_ref[0, :, 384:512]
    slc_ref[0, :, 0:128] = ks
    slc_ref[0, :, 128:256] = vs
    win_ref[0, :, 0:128] = kw
    win_ref[0, :, 128:256] = vw
    for ref, val in ((ks_ref, ks), (vs_ref, vs), (kw_ref, kw), (vw_ref, vw)):
        vb = val.astype(BF16)
        ref[0, 0] = vb[:, :N_HD]
        ref[0, 1] = vb[:, N_HD:]
    gt = _sigmoid(ngate_ref[0])
    gates_ref[0, 0] = gt
    gates_ref[0, 1] = pltpu.roll(gt, LANES - 3 * N_REP, axis=1)


def _rows(nq, nsw, ngate, cos, sin, gq, gks, gkw, bm):
    b, t, _ = nq.shape
    tm = min(t, 512)
    tok = lambda w: pl.BlockSpec((1, tm, w), lambda i, j: (i, j, 0))
    const = lambda shape: pl.BlockSpec(shape, lambda i, j: (0,) * len(shape))
    hm = lambda nh: pl.BlockSpec((1, nh, tm, N_HD), lambda i, j: (i, 0, j, 0))
    kv_shape = jax.ShapeDtypeStruct((b, N_KV, t, N_HD), BF16)
    return pl.pallas_call(
        _rows_kernel,
        grid=(b, t // tm),
        in_specs=[tok(NQ_W), tok(NSW_W), tok(NGATE_W),
                  pl.BlockSpec((tm, LANES), lambda i, j: (j, 0)),
                  pl.BlockSpec((tm, LANES), lambda i, j: (j, 0)),
                  const((1, NQ_W)), const((1, LANES)), const((1, LANES)), const((LANES, LANES))],
        out_specs=[hm(N_HEADS), hm(N_KV), hm(N_KV), hm(N_KV), hm(N_KV), tok(256), tok(256),
                   pl.BlockSpec((1, N_KV, tm, LANES), lambda i, j: (i, 0, j, 0))],
        out_shape=[jax.ShapeDtypeStruct((b, N_HEADS, t, N_HD), BF16), kv_shape, kv_shape, kv_shape, kv_shape,
                   jax.ShapeDtypeStruct((b, t, 256), F32), jax.ShapeDtypeStruct((b, t, 256), F32),
                   jax.ShapeDtypeStruct((b, N_KV, t, LANES), F32)],
        compiler_params=_cparams(("parallel", "parallel")),
        name="nsa_rows",
    )(nq, nsw, ngate, cos, sin, gq, gks, gkw, bm)


def _compress_kernel(u_ref, w1_ref, pe_ref, w2_ref, gkc_ref, cos_ref, sin_ref, bm_ref, kc_ref, vc_ref):
    nu = u_ref.shape[1]
    outs = []
    for kv in range(2):
        x = jnp.concatenate(
            [u_ref[0, :, j * 256 + kv * LANES:j * 256 + (kv + 1) * LANES] for j in range(CMP_STRIDE)],
            axis=1).astype(BF16)
        w1 = w1_ref[kv]
        p = _dot(x, w1)
        bias = _dot(pe_ref[kv].astype(BF16), w1)
        pa = p[:, :256] + bias[0:1, :256]
        pb = p[:, 256:] + bias[1:2, 256:]
        pre = pa + pltpu.roll(pb, nu - 1, axis=0)
        hid = (pre * _sigmoid(pre)).astype(BF16)
        outs.append(_dot(hid, w2_ref[kv]))
    kc = _norm_rope_slab(outs[0], gkc_ref[...], cos_ref[...], sin_ref[...], bm_ref[...]).astype(BF16)
    vc = outs[1].astype(BF16)
    for g in range(N_KV):
        kc_ref[0, g] = kc[:, g * N_HD:(g + 1) * N_HD]
        vc_ref[0, g] = vc[:, g * N_HD:(g + 1) * N_HD]


def _compress(units, w1r, pe_r, w2r, gkc, cos_c, sin_c, bm):
    b, nu, _ = units.shape
    const = lambda shape: pl.BlockSpec(shape, lambda i: (0,) * len(shape))
    out = jax.ShapeDtypeStruct((b, N_KV, nu, N_HD), BF16)
    return pl.pallas_call(
        _compress_kernel,
        grid=(b,),
        in_specs=[pl.BlockSpec((1, nu, 4096), lambda i: (i, 0, 0)),
                  const((2, 2048, 512)), const((2, 8, 2048)), const((2, 256, LANES)),
                  const((1, LANES)), const((nu, LANES)), const((nu, LANES)), const((LANES, LANES))],
        out_specs=[pl.BlockSpec((1, N_KV, nu, N_HD), lambda i: (i, 0, 0, 0))] * 2,
        out_shape=[out, out],
        compiler_params=_cparams(("parallel",)),
        name="nsa_compress",
    )(units, w1r, pe_r, w2r, gkc, cos_c, sin_c, bm)


def _cmp_kernel(q_ref, kc_ref, vc_ref, c2s_ref, oc_ref, sel_ref, *, tq, pos0, n_cmp, nsel):
    qi = pl.program_id(2)
    nc_pad = kc_ref.shape[2]
    q = q_ref[0].reshape(N_REP * tq, N_HD)
    s = _dot_nt(q, kc_ref[0, 0]).reshape(N_REP, tq, nc_pad)
    qpos = pos0 + qi * tq + _iota((tq, 1), 0)
    c_idx = _iota((tq, nc_pad), 1)
    c_ok = jnp.where(c_idx < n_cmp, c_idx * CMP_STRIDE + (CMP_LEN - 1), jnp.int32(2 ** 30)) <= qpos
    s = jnp.where(c_ok[None], s, NEG)
    p = jnp.exp(s - jnp.max(s, axis=-1, keepdims=True))
    p = p / jnp.sum(p, axis=-1, keepdims=True)
    p = p * (qpos >= CMP_LEN - 1).astype(F32)[None]
    oc_ref[0] = _dot(p.reshape(N_REP * tq, nc_pad).astype(BF16), vc_ref[0, 0]).reshape(N_REP, tq, N_HD)
    imp = _dot_split(jnp.sum(p, axis=0), c2s_ref[...])
    blk = _iota((tq, nsel), 1)
    cur = qpos // SLC_BLOCK
    forced = jnp.where(blk == 0, 1.0, jnp.where(blk == cur, 1.0, jnp.where(blk == cur - 1, 1.0, 0.0)))
    work = jnp.where(blk * SLC_BLOCK <= qpos, imp + FORCE_BONUS * forced, NEG)
    blk_f = blk.astype(F32)
    sel = jnp.zeros((tq, nsel), F32)
    for _ in range(SLC_TOPK):
        m = jnp.max(work, axis=1, keepdims=True)
        idx = jnp.min(jnp.where(work == m, blk_f, float(nsel)), axis=1, keepdims=True)
        hit = blk_f == idx
        sel = jnp.where(hit, 1.0, sel)
        work = jnp.where(hit, -jnp.inf, work)
    sel_ref[0, 0] = sel.astype(BF16)


def _cmp_attn(q_hm, kc, vc, c2s, *, tq, pos0, n_cmp):
    b, _, t, _ = q_hm.shape
    nc_pad = kc.shape[2]
    nsel = c2s.shape[1]
    kern = functools.partial(_cmp_kernel, tq=tq, pos0=pos0, n_cmp=n_cmp, nsel=nsel)
    return pl.pallas_call(
        kern,
        grid=(b, N_KV, t // tq),
        in_specs=[pl.BlockSpec((1, N_REP, tq, N_HD), lambda i, g, j: (i, g, j, 0)),
                  pl.BlockSpec((1, 1, nc_pad, N_HD), lambda i, g, j: (i, g, 0, 0)),
                  pl.BlockSpec((1, 1, nc_pad, N_HD), lambda i, g, j: (i, g, 0, 0)),
                  pl.BlockSpec((nc_pad, nsel), lambda i, g, j: (0, 0))],
        out_specs=[pl.BlockSpec((1, N_REP, tq, N_HD), lambda i, g, j: (i, g, j, 0)),
                   pl.BlockSpec((1, 1, tq, nsel), lambda i, g, j: (i, g, j, 0))],
        out_shape=[jax.ShapeDtypeStruct((b, N_HEADS, t, N_HD), F32),
                   jax.ShapeDtypeStruct((b, N_KV, t, nsel), BF16)],
        compiler_params=_cparams(("parallel", "parallel", "parallel")),
        name="nsa_cmp_topk",
    )(q_hm, kc, vc, c2s)


def _slcwin_kernel(q_ref, ks_ref, vs_ref, kw_ref, vw_ref, sel_ref, e_ref, oc_ref, gates_ref, hn_ref,
                   *, tq, pos0, wpos0):
    qi = pl.program_id(2)
    rows = N_REP * tq
    q = q_ref[0].reshape(rows, N_HD)
    q0 = pos0 + qi * tq
    qpos = q0 + _iota((tq, 1), 0)
    sel = sel_ref[0, 0]
    n_kv = (q0 + tq + KV_TILE - 1) // KV_TILE

    def body(j, carry):
        m, l, acc = carry
        k0 = pl.multiple_of(j * KV_TILE, KV_TILE)
        s = _dot_nt(q, ks_ref[0, 0, pl.ds(k0, KV_TILE), :]).reshape(N_REP, tq, KV_TILE)
        picked = _dot(sel, e_ref[:, pl.ds(k0, KV_TILE)])
        kpos = k0 + _iota((tq, KV_TILE), 1)
        ok = jnp.where(kpos <= qpos, picked, 0.0) > 0.5
        s = jnp.where(ok[None], s, NEG)
        m_new = jnp.maximum(m, jnp.max(s, axis=-1, keepdims=True))
        p = jnp.exp(s - m_new)
        alpha = jnp.exp(m - m_new)
        l = alpha * l + jnp.sum(p, axis=-1, keepdims=True)
        pv = _dot(p.reshape(rows, KV_TILE).astype(BF16), vs_ref[0, 0, pl.ds(k0, KV_TILE), :])
        acc = alpha * acc + pv.reshape(N_REP, tq, N_HD)
        return m_new, l, acc

    m0 = jnp.full((N_REP, tq, 1), NEG, F32)
    l0 = jnp.zeros((N_REP, tq, 1), F32)
    a0 = jnp.zeros((N_REP, tq, N_HD), F32)
    _, l, acc = lax.fori_loop(0, n_kv, body, (m0, l0, a0))
    o_s = acc / l

    wlen = WINDOW + tq
    wq0 = wpos0 + qi * tq
    start = pl.multiple_of(jnp.maximum(wq0 - WINDOW, 0), tq)
    s = _dot_nt(q, kw_ref[0, 0, pl.ds(start, wlen), :]).reshape(N_REP, tq, wlen)
    dist = (wq0 + _iota((tq, wlen), 0)) - (start + _iota((tq, wlen), 1))
    in_window = jnp.where(dist >= 0, dist, WINDOW) < WINDOW
    s = jnp.where(in_window[None], s, NEG)
    p = jnp.exp(s - jnp.max(s, axis=-1, keepdims=True))
    lw = jnp.sum(p, axis=-1, keepdims=True)
    o_w = _dot(p.reshape(rows, wlen).astype(BF16), vw_ref[0, 0, pl.ds(start, wlen), :]).reshape(N_REP, tq, N_HD) / lw

    gt = gates_ref[0, 0]
    outs = []
    for r in range(N_REP):
        outs.append(gt[:, 3 * r:3 * r + 1] * oc_ref[0, r] + gt[:, 3 * r + 1:3 * r + 2] * o_s[r]
                    + gt[:, 3 * r + 2:3 * r + 3] * o_w[r])
    hn_ref[0] = jnp.concatenate(outs, axis=1).astype(BF16)


def _slcwin(q_hm, ks, vs, kw, vw, sel, e_mat, oc, gates, *, tq, pos0, wpos0):
    b, _, t, _ = q_hm.shape
    kvlen = ks.shape[2]
    wrows = kw.shape[2]
    nsel = sel.shape[3]
    kern = functools.partial(_slcwin_kernel, tq=tq, pos0=pos0, wpos0=wpos0)
    qspec = pl.BlockSpec((1, N_REP, tq, N_HD), lambda i, g, j: (i, g, j, 0))
    full = lambda n: pl.BlockSpec((1, 1, n, N_HD), lambda i, g, j: (i, g, 0, 0))
    return pl.pallas_call(
        kern,
        grid=(b, N_KV, t // tq),
        in_specs=[qspec, full(kvlen), full(kvlen), full(wrows), full(wrows),
                  pl.BlockSpec((1, 1, tq, nsel), lambda i, g, j: (i, g, j, 0)),
                  pl.BlockSpec((nsel, kvlen), lambda i, g, j: (0, 0)),
                  qspec,
                  pl.BlockSpec((1, 1, tq, LANES), lambda i, g, j: (i, g, j, 0))],
        out_specs=pl.BlockSpec((1, tq, N_REP * N_HD), lambda i, g, j: (i, j, g)),
        out_shape=jax.ShapeDtypeStruct((b, t, N_WIDTH), BF16),
        compiler_params=_cparams(("parallel", "parallel", "parallel")),
        name="nsa_slc_win",
    )(q_hm, ks, vs, kw, vw, sel, e_mat, oc, gates)


GATHER_PAGES = 8


def _gather_cmp_kernel(pt_ref, *refs):
    out_ref = refs[-1]
    for i in range(GATHER_PAGES):
        out_ref[0, i * PAGE_SIZE:(i + 1) * PAGE_SIZE, :] = refs[i][0, 0]


def _gather_cmp(page_table, cache, layer):
    db, n_pages = page_table.shape
    flat = cache.reshape(cache.shape[0], cache.shape[1], PAGE_SIZE, 256)

    def page_spec(i):
        return pl.BlockSpec((1, 1, PAGE_SIZE, 256),
                            lambda b, c, pt: (layer, pt[b, c * GATHER_PAGES + i], 0, 0))

    return pl.pallas_call(
        _gather_cmp_kernel,
        grid_spec=pltpu.PrefetchScalarGridSpec(
            num_scalar_prefetch=1,
            grid=(db, n_pages // GATHER_PAGES),
            in_specs=[page_spec(i) for i in range(GATHER_PAGES)],
            out_specs=pl.BlockSpec((1, GATHER_PAGES * PAGE_SIZE, 256), lambda b, c, pt: (b, c, 0))),
        out_shape=jax.ShapeDtypeStruct((db, n_pages * PAGE_SIZE, 256), F32),
        compiler_params=_cparams(("parallel", "parallel")),
        name="gather_cmp_pages",
    )(page_table, *([flat] * GATHER_PAGES))


def _gather_slc_kernel(pt_ref, *refs):
    new_ref = refs[GATHER_PAGES]
    ks_ref, vs_ref = refs[GATHER_PAGES + 1], refs[GATHER_PAGES + 2]
    c = pl.program_id(1)
    n_cached = pl.num_programs(1) - 1

    def emit(i, page):
        pb = page.astype(BF16)
        rows = slice(i * PAGE_SIZE, (i + 1) * PAGE_SIZE)
        for g in range(N_KV):
            ks_ref[0, g, rows, :] = pb[:, g * N_HD:(g + 1) * N_HD]
            vs_ref[0, g, rows, :] = pb[:, LANES + g * N_HD:LANES + (g + 1) * N_HD]

    @pl.when(c < n_cached)
    def _():
        for i in range(GATHER_PAGES):
            emit(i, refs[i][0, 0])

    @pl.when(c == n_cached)
    def _():
        emit(0, new_ref[0])
        zero = jnp.zeros((PAGE_SIZE, 256), F32)
        for i in range(1, GATHER_PAGES):
            emit(i, zero)


def _gather_slc(page_table, cache, layer, new_rows):
    db, n_pages = page_table.shape
    flat = cache.reshape(cache.shape[0], cache.shape[1], PAGE_SIZE, 256)
    n_steps = n_pages // GATHER_PAGES
    step_rows = GATHER_PAGES * PAGE_SIZE

    def page_spec(i):
        return pl.BlockSpec(
            (1, 1, PAGE_SIZE, 256),
            lambda b, c, pt: (layer, pt[b, jnp.minimum(c, n_steps - 1) * GATHER_PAGES + i], 0, 0))

    out = jax.ShapeDtypeStruct((db, N_KV, (n_steps + 1) * step_rows, N_HD), BF16)
    ospec = pl.BlockSpec((1, N_KV, step_rows, N_HD), lambda b, c, pt: (b, 0, c, 0))
    return pl.pallas_call(
        _gather_slc_kernel,
        grid_spec=pltpu.PrefetchScalarGridSpec(
            num_scalar_prefetch=1,
            grid=(db, n_steps + 1),
            in_specs=[page_spec(i) for i in range(GATHER_PAGES)]
            + [pl.BlockSpec((1, PAGE_SIZE, 256), lambda b, c, pt: (b, 0, 0))],
            out_specs=[ospec, ospec]),
        out_shape=[out, out],
        compiler_params=_cparams(("parallel", "arbitrary")),
        name="gather_slc_pages",
    )(page_table, *([flat] * GATHER_PAGES), new_rows)


def _rope_tables(pos):
    half = N_HD // 2
    inv = ROPE_THETA ** (-jnp.arange(half, dtype=F32) / half)
    ang = pos.astype(F32)[:, None] * inv[None, :]
    cos, sin = jnp.cos(ang), jnp.sin(ang)
    return jnp.tile(jnp.concatenate([cos, cos], axis=1), (1, 2)), jnp.tile(jnp.concatenate([-sin, sin], axis=1), (1, 2))


def _pad_cols(a, w):
    return jnp.pad(a, ((0, 0), (0, w - a.shape[1])))


def _prep_layer(l, w_norm_mix, w_in, b_mlstm_if, w_mlstm_conv, w_mlstm_hnorm, w_gla_gate2, b_gla_gate,
                w_gla_hnorm, w_qk_norm, w_cmp_pe, w_cmp_1, w_cmp_2, w_out, w_norm_ffn, w_ffn_up, w_ffn_down):
    wi = w_in[l]
    col = lambda n: wi[:, _OFF[n][0]:_OFF[n][0] + _OFF[n][1]]
    um = _pad_cols(jnp.concatenate([col('m_qk'), col('m_v'), col('m_o'), col('m_i'), col('m_f')], axis=1), UM_W)
    ug = _pad_cols(jnp.concatenate([col('g_q'), col('g_k'), col('g_v'), col('g_o'), col('g_lr')], axis=1), UG_W)
    ncmp = jnp.concatenate([col('n_kc'), col('n_vc')], axis=1)
    nsw = jnp.concatenate([col('n_ks'), col('n_vs'), col('n_kw'), col('n_vw')], axis=1)
    ngate = _pad_cols(col('n_gate'), NGATE_W)
    p = {}
    p['w_cat'] = jnp.concatenate([um, ug, col('n_q'), ncmp, nsw, ngate], axis=1).astype(BF16)
    p['g_mix'] = w_norm_mix[l][None, :]
    p['wconv'] = jnp.pad(w_mlstm_conv[l], ((0, 8 - M_CONV), (0, 0)))
    p['bif'] = _pad_cols(b_mlstm_if[l][None, :], LANES)
    p['gh_m'] = w_mlstm_hnorm[l][None, :]
    p['w2p'] = jnp.pad(w_gla_gate2[l], ((0, LANES - G_RANK), (0, 0)))
    p['bg'] = b_gla_gate[l][None, :]
    p['gh_g'] = w_gla_hnorm[l][None, :]
    gqk = w_qk_norm[l]
    p['gq'] = jnp.tile(gqk[0], N_HEADS)[None, :]
    p['gkc'] = jnp.tile(gqk[1], N_KV)[None, :]
    p['gks'] = jnp.tile(gqk[2], N_KV)[None, :]
    p['gkw'] = jnp.tile(gqk[3], N_KV)[None, :]
    w1 = w_cmp_1[l].reshape(2, 2, CMP_STRIDE, N_HD, CMP_HIDDEN)
    eye = jnp.eye(N_KV, dtype=F32)
    w1r = jnp.einsum('khjdc,gf->kjgdhfc', w1, eye).reshape(2, CMP_STRIDE * N_KV * N_HD, 2 * N_KV * CMP_HIDDEN)
    p['w1r'] = w1r.astype(BF16)
    pe = w_cmp_pe[l].reshape(2, 2, CMP_STRIDE, 1, N_HD)
    pe = jnp.broadcast_to(pe, (2, 2, CMP_STRIDE, N_KV, N_HD)).reshape(2, 2, 2048)
    p['pe_r'] = jnp.pad(pe, ((0, 0), (0, 6), (0, 0)))
    p['w2r'] = jnp.einsum('kcd,gf->kgcfd', w_cmp_2[l], eye).reshape(2, N_KV * CMP_HIDDEN, N_KV * N_HD).astype(BF16)
    p['w_out'] = w_out[l].astype(BF16)
    p['g_ffn'] = w_norm_ffn[l][None, :]
    nf = 2
    tf = D_FF // nf
    wu = w_ffn_up[l]
    p['w_up_r'] = jnp.stack([jnp.concatenate([wu[:, f * tf:(f + 1) * tf], wu[:, D_FF + f * tf:D_FF + (f + 1) * tf]],
                                             axis=1) for f in range(nf)]).astype(BF16)
    p['w_dn_r'] = w_ffn_down[l].reshape(nf, tf, D_MODEL).astype(BF16)
    return p


def _cmp2slc(n_cmp, nc_pad, nsel):
    m = np.zeros((nc_pad, nsel), np.float32)
    per = SLC_BLOCK // CMP_STRIDE
    for n in range(n_cmp):
        for u in range(CMP_LEN // CMP_STRIDE):
            m[n, (n + u) // per] += 1.0
    return jnp.asarray(m, BF16)


def _expand_mat(nsel, kvlen):
    return jnp.asarray((np.arange(kvlen)[None, :] // SLC_BLOCK) == np.arange(nsel)[:, None], BF16)


def _group_mean_mat():
    idx = np.arange(LANES) // N_HD
    return jnp.asarray((idx[:, None] == idx[None, :]) / float(N_HD), BF16)


def _mixers(p, um, ug, mstate, gstate, *, L, n_valid):
    conv0, ct0, n0, m0 = mstate
    hm, conv_o, ct_o, n_o, m_o = _mlstm(um, conv0, ct0, n0, m0, p['wconv'], p['bif'], p['gh_m'], L=L, n_valid=n_valid)
    hg, s_o = _gla(ug, gstate, p['w2p'], p['bg'], p['gh_g'], L=L, n_valid=n_valid)
    return hm, hg, (conv_o, ct_o, n_o, m_o), s_o


def _mlstm_state_in(conv, c, n, m):
    b = conv.shape[0]
    conv0 = jnp.pad(conv.astype(F32), ((0, 0), (8 - (M_CONV - 1), 0), (0, 0)))
    ct0 = jnp.swapaxes(c.astype(F32), -1, -2).reshape(b, M_WIDTH, M_HD)
    n0 = jnp.broadcast_to(n.astype(F32).reshape(b, 1, M_WIDTH), (b, 8, M_WIDTH))
    m0 = jnp.broadcast_to(jnp.pad(m.astype(F32), ((0, 0), (0, 8 - M_HEADS)))[:, :, None], (b, 8, LANES))
    return conv0, ct0, n0, m0


def _mlstm_state_out(conv_o, ct_o, n_o, m_o):
    b = conv_o.shape[0]
    c = jnp.swapaxes(ct_o.reshape(b, M_HEADS, M_HD, M_HD), -1, -2)
    return c, n_o[:, 0].reshape(b, M_HEADS, M_HD), m_o[:, :M_HEADS, 0], conv_o[:, 8 - (M_CONV - 1):]


def kernel(x_prompt, x_sample, cache_cmp_kv, cache_slc_kv, state_win_kv, state_mlstm_C, state_mlstm_n,
           state_mlstm_m, state_mlstm_conv, state_gla_S, page_table, w_norm_mix, w_in, b_mlstm_if,
           w_mlstm_conv, w_mlstm_hnorm, w_gla_gate2, b_gla_gate, w_gla_hnorm, w_qk_norm, w_cmp_pe,
           w_cmp_1, w_cmp_2, w_out, w_norm_ffn, w_ffn_up, w_ffn_down):
    b, t, _ = x_prompt.shape
    db, td, _ = x_sample.shape
    depth = w_in.shape[0]
    n_pages = page_table.shape[1]
    past = n_pages * PAGE_SIZE
    win_buf = state_win_kv.shape[2]
    assert t % 512 == 0 and td <= SAMPLE_T and past % (GATHER_PAGES * PAGE_SIZE) == 0 and win_buf == WINDOW

    bm = _group_mean_mat()
    cos_p, sin_p = _rope_tables(jnp.arange(t))
    nu_p = t // CMP_STRIDE
    ncmp_p = (t - CMP_LEN) // CMP_STRIDE + 1
    cos_cp, sin_cp = _rope_tables(jnp.arange(nu_p) * CMP_STRIDE + CMP_LEN - 1)
    nsel_p = -(-t // SLC_BLOCK)
    nsel_p = -(-nsel_p // LANES) * LANES
    c2s_p = _cmp2slc(ncmp_p, nu_p, nsel_p)
    e_p = _expand_mat(nsel_p, t)
    tq_p = 256
    cos_s, sin_s = _rope_tables(past + jnp.arange(SAMPLE_T))
    nu_s = past // CMP_STRIDE
    ncmp_s = (past + td - CMP_LEN) // CMP_STRIDE + 1
    cos_cs, sin_cs = _rope_tables(jnp.arange(nu_s) * CMP_STRIDE + CMP_LEN - 1)
    kvlen_s = past + GATHER_PAGES * PAGE_SIZE
    nsel_s = -(-(-(-(past + td) // SLC_BLOCK)) // LANES) * LANES
    c2s_s = _cmp2slc(ncmp_s, nu_s, nsel_s)
    e_s = _expand_mat(nsel_s, kvlen_s)

    xp = x_prompt.reshape(b * t, D_MODEL)
    xs = jnp.pad(x_sample, ((0, 0), (0, SAMPLE_T - td), (0, 0))).reshape(db * SAMPLE_T, D_MODEL)

    zero_m = _mlstm_state_in(jnp.zeros((b, M_CONV - 1, 2 * M_WIDTH), F32), jnp.zeros((b, M_HEADS, M_HD, M_HD), F32),
                             jnp.zeros((b, M_HEADS, M_HD), F32), jnp.zeros((b, M_HEADS), F32))
    zero_g = jnp.zeros((b, LANES, G_DV), F32)

    pl_out = [[] for _ in range(8)]
    sl_out = [[] for _ in range(8)]
    for l in range(depth):
        p = _prep_layer(l, w_norm_mix, w_in, b_mlstm_if, w_mlstm_conv, w_mlstm_hnorm, w_gla_gate2, b_gla_gate,
                        w_gla_hnorm, w_qk_norm, w_cmp_pe, w_cmp_1, w_cmp_2, w_out, w_norm_ffn, w_ffn_up, w_ffn_down)
        um, ug, nq, ncmp, nsw, ngate = _pre(xp, p['g_mix'], p['w_cat'])
        r3 = lambda a, bb, tt: a.reshape(bb, tt, a.shape[-1])
        hm, hg, mst, gst = _mixers(p, r3(um, b, t), r3(ug, b, t), zero_m, zero_g, L=M_CHUNK, n_valid=M_CHUNK)
        q_hm, ks, vs, kw, vw, slc_f, win_f, gates = _rows(r3(nq, b, t), r3(nsw, b, t), r3(ngate, b, t), cos_p, sin_p,
                                                          p['gq'], p['gks'], p['gkw'], bm)
        kc, vc = _compress(ncmp.reshape(b, nu_p, 4096), p['w1r'], p['pe_r'], p['w2r'], p['gkc'], cos_cp, sin_cp, bm)
        oc, sel = _cmp_attn(q_hm, kc, vc, c2s_p, tq=tq_p, pos0=0, n_cmp=ncmp_p)
        hn = _slcwin(q_hm, ks, vs, kw, vw, sel, e_p, oc, gates, tq=tq_p, pos0=0, wpos0=0)
        xp = _post(xp, hm.reshape(b * t, -1), hg.reshape(b * t, -1), hn.reshape(b * t, -1),
                   p['w_out'], p['g_ffn'], p['w_up_r'], p['w_dn_r'])
        c_o, n_o, m_o, conv_o = _mlstm_state_out(*mst)
        kv6 = lambda a: a.reshape(a.shape[0], a.shape[1], 2, N_KV, N_HD)
        for lst, val in zip(pl_out, (kv6(ncmp.reshape(b, t, 256)), kv6(slc_f), kv6(win_f[:, t - min(WINDOW, t):]),
                                     c_o, n_o, m_o, conv_o, gst.reshape(b, G_HEADS, G_DK, G_DV))):
            lst.append(val)

        um, ug, nq, ncmp, nsw, ngate = _pre(xs, p['g_mix'], p['w_cat'])
        mstate = _mlstm_state_in(state_mlstm_conv[l], state_mlstm_C[l], state_mlstm_n[l], state_mlstm_m[l])
        gstate = state_gla_S[l].astype(F32).reshape(db, LANES, G_DV)
        hm, hg, mst, gst = _mixers(p, r3(um, db, SAMPLE_T), r3(ug, db, SAMPLE_T), mstate, gstate,
                                   L=SAMPLE_T, n_valid=td)
        q_hm, ks_n, vs_n, kw_n, vw_n, slc_f, win_f, gates = _rows(
            r3(nq, db, SAMPLE_T), r3(nsw, db, SAMPLE_T), r3(ngate, db, SAMPLE_T), cos_s, sin_s,
            p['gq'], p['gks'], p['gkw'], bm)
        units = _gather_cmp(page_table, cache_cmp_kv, l).reshape(db, nu_s, 4096)
        kc, vc = _compress(units, p['w1r'], p['pe_r'], p['w2r'], p['gkc'], cos_cs, sin_cs, bm)
        oc, sel = _cmp_attn(q_hm, kc, vc, c2s_s, tq=SAMPLE_T, pos0=past, n_cmp=ncmp_s)
        new_slc = jnp.pad(slc_f[:, :td], ((0, 0), (0, PAGE_SIZE - td), (0, 0)))
        ks_all, vs_all = _gather_slc(page_table, cache_slc_kv, l, new_slc)
        win_all = jnp.concatenate([state_win_kv[l].astype(F32).reshape(db, win_buf, 256), win_f[:, :td]], axis=1)
        win_pad = jnp.pad(win_all, ((0, 0), (0, WINDOW + SAMPLE_T - win_all.shape[1]), (0, 0)))
        win_hm = jnp.moveaxis(win_pad.reshape(db, WINDOW + SAMPLE_T, 2, N_KV, N_HD), 1, 3).astype(BF16)
        hn = _slcwin(q_hm, ks_all, vs_all, win_hm[:, 0], win_hm[:, 1], sel, e_s, oc, gates,
                     tq=SAMPLE_T, pos0=past, wpos0=win_buf)
        xs = _post(xs, hm.reshape(db * SAMPLE_T, -1), hg.reshape(db * SAMPLE_T, -1), hn.reshape(db * SAMPLE_T, -1),
                   p['w_out'], p['g_ffn'], p['w_up_r'], p['w_dn_r'])
        c_o, n_o, m_o, conv_o = _mlstm_state_out(*mst)
        for lst, val in zip(sl_out, (kv6(ncmp.reshape(db, SAMPLE_T, 256)[:, :td]), kv6(slc_f[:, :td]),
                                     kv6(win_all[:, win_all.shape[1] - win_buf:]),
                                     c_o, n_o, m_o, conv_o, gst.reshape(db, G_HEADS, G_DK, G_DV))):
            lst.append(val)

    outs_p = [jnp.stack(a) for a in pl_out]
    outs_s = [jnp.stack(a) for a in sl_out]
    y_p = xp.reshape(b, t, D_MODEL)
    y_s = xs.reshape(db, SAMPLE_T, D_MODEL)[:, :td]
    return (y_p, y_s, *outs_p, *outs_s)
```

```python
import functools
import math

import numpy as np
import jax
import jax.numpy as jnp
from jax import lax
from jax.experimental import pallas as pl
from jax.experimental.pallas import tpu as pltpu

F32 = jnp.float32
BF16 = jnp.bfloat16

D_MODEL = 1024
M_HEADS, M_HD, M_WIDTH, M_CONV, M_CHUNK = 4, 64, 256, 4, 64
G_HEADS, G_DK, G_DV, G_WIDTH, G_RANK, G_TAU, G_CHUNK = 4, 32, 64, 256, 16, 16.0, 64
N_HEADS, N_HD, N_KV, N_REP, N_WIDTH = 8, 64, 2, 4, 512
CMP_LEN, CMP_STRIDE, CMP_HIDDEN = 32, 16, 128
SLC_BLOCK, SLC_TOPK, WINDOW = 64, 16, 512
ROPE_THETA = 10000.0
D_FF = 2816
PAGE_SIZE = 128
NEG = -1e30
FORCE_BONUS = 1e4
EPS = 1e-6
LOG2E = 1.4426950408889634

LANES = 128
VMEM_LIMIT = 56 * 1024 * 1024
KV_TILE = 512
SAMPLE_T = 16

_OFF = {}
_o = 0
for _name, _w in (('m_qk', 512), ('m_v', 256), ('m_i', 4), ('m_f', 4), ('m_o', 256),
                  ('g_q', 128), ('g_k', 128), ('g_v', 256), ('g_lr', 16), ('g_o', 256),
                  ('n_q', 512), ('n_kc', 128), ('n_vc', 128), ('n_ks', 128),
                  ('n_vs', 128), ('n_kw', 128), ('n_vw', 128), ('n_gate', 24)):
    _OFF[_name] = (_o, _w)
    _o += _w
UM_W, UG_W, NQ_W, NCMP_W, NSW_W, NGATE_W = 1152, 896, 512, 256, 512, 128


def _cparams(sem):
    return pltpu.CompilerParams(dimension_semantics=sem, vmem_limit_bytes=VMEM_LIMIT)


def _dot(a, b):
    return jnp.dot(a, b, preferred_element_type=F32)


def _dot_nt(a, b):
    return lax.dot_general(a, b, (((1,), (1,)), ((), ())), preferred_element_type=F32)


def _dot_hi(a, b):
    return jnp.dot(a, b, preferred_element_type=F32, precision=lax.Precision.HIGHEST)


def _dot_split(a, b_bf):
    hi = a.astype(BF16)
    lo = (a - hi.astype(F32)).astype(BF16)
    return _dot(hi, b_bf) + _dot(lo, b_bf)


def _log_sigmoid(x):
    return jnp.minimum(x, 0.0) - jnp.log(1.0 + jnp.exp(-jnp.abs(x)))


def _sigmoid(x):
    return 1.0 / (1.0 + jnp.exp(-x))


def _transpose(x):
    r, c = x.shape
    if r < LANES:
        x = jnp.concatenate([x, jnp.zeros((LANES - r, c), x.dtype)], axis=0)
    parts = [x[:, i * LANES:(i + 1) * LANES].T[:, :r] for i in range(c // LANES)]
    return parts[0] if len(parts) == 1 else jnp.concatenate(parts, axis=0)


def _transpose2d(x):
    r, c = x.shape
    rp, cp = -(-r // LANES) * LANES, -(-c // LANES) * LANES
    if cp > c:
        x = jnp.concatenate([x, jnp.zeros((r, cp - c), x.dtype)], axis=1)
    if rp > r:
        x = jnp.concatenate([x, jnp.zeros((rp - r, cp), x.dtype)], axis=0)
    out_rows = []
    for j in range(cp // LANES):
        blocks = [x[i * LANES:(i + 1) * LANES, j * LANES:(j + 1) * LANES].T for i in range(rp // LANES)]
        out_rows.append(blocks[0] if len(blocks) == 1 else jnp.concatenate(blocks, axis=1))
    out = out_rows[0] if len(out_rows) == 1 else jnp.concatenate(out_rows, axis=0)
    return out[:c, :r]


def _iota(shape, dim):
    return lax.broadcasted_iota(jnp.int32, shape, dim)


def _pre_kernel(x_ref, g_ref, w_ref, um_ref, ug_ref, nq_ref, ncmp_ref, nsw_ref, ngate_ref):
    x = x_ref[...]
    h = x * lax.rsqrt(jnp.mean(x * x, axis=-1, keepdims=True) + EPS) * g_ref[...]
    u = _dot(h.astype(BF16), w_ref[...])
    o = 0
    for ref, w in ((um_ref, UM_W), (ug_ref, UG_W), (nq_ref, NQ_W), (ncmp_ref, NCMP_W),
                   (nsw_ref, NSW_W), (ngate_ref, NGATE_W)):
        ref[...] = u[:, o:o + w]
        o += w


def _pre(x2, g, w_cat):
    m = x2.shape[0]
    tm = min(m, 256)
    widths = (UM_W, UG_W, NQ_W, NCMP_W, NSW_W, NGATE_W)
    return pl.pallas_call(
        _pre_kernel,
        grid=(m // tm,),
        in_specs=[pl.BlockSpec((tm, D_MODEL), lambda i: (i, 0)),
                  pl.BlockSpec((1, D_MODEL), lambda i: (0, 0)),
                  pl.BlockSpec((D_MODEL, sum(widths)), lambda i: (0, 0))],
        out_specs=[pl.BlockSpec((tm, w), lambda i: (i, 0)) for w in widths],
        out_shape=[jax.ShapeDtypeStruct((m, w), F32) for w in widths],
        compiler_params=_cparams(("parallel",)),
        name="pre_proj",
    )(x2, g, w_cat)


def _post_kernel(x_ref, hm_ref, hg_ref, hn_ref, wout_ref, g_ref, wup_ref, wdn_ref, o_ref, h2_ref, *, tf):
    @pl.when(pl.program_id(1) == 0)
    def _():
        xn = x_ref[...]
        xn = xn + _dot(hm_ref[...], wout_ref[0:256, :])
        xn = xn + _dot(hg_ref[...], wout_ref[256:512, :])
        xn = xn + _dot(hn_ref[...], wout_ref[512:1024, :])
        o_ref[...] = xn
        h2 = xn * lax.rsqrt(jnp.mean(xn * xn, axis=-1, keepdims=True) + EPS) * g_ref[...]
        h2_ref[...] = h2.astype(BF16)

    au = _dot(h2_ref[...], wup_ref[0])
    a = au[:, :tf]
    act = (a * _sigmoid(a) * au[:, tf:]).astype(BF16)
    o_ref[...] += _dot(act, wdn_ref[0])


def _post(x2, hm, hg, hn, w_out, g, w_up_r, w_dn_r):
    m = x2.shape[0]
    tm = min(m, 512)
    nf, _, tf2 = w_up_r.shape
    tf = tf2 // 2
    return pl.pallas_call(
        functools.partial(_post_kernel, tf=tf),
        grid=(m // tm, nf),
        in_specs=[pl.BlockSpec((tm, D_MODEL), lambda i, f: (i, 0)),
                  pl.BlockSpec((tm, M_WIDTH), lambda i, f: (i, 0)),
                  pl.BlockSpec((tm, G_WIDTH), lambda i, f: (i, 0)),
                  pl.BlockSpec((tm, N_WIDTH), lambda i, f: (i, 0)),
                  pl.BlockSpec((D_MODEL, D_MODEL), lambda i, f: (0, 0)),
                  pl.BlockSpec((1, D_MODEL), lambda i, f: (0, 0)),
                  pl.BlockSpec((1, D_MODEL, tf2), lambda i, f: (f, 0, 0)),
                  pl.BlockSpec((1, tf, D_MODEL), lambda i, f: (f, 0, 0))],
        out_specs=pl.BlockSpec((tm, D_MODEL), lambda i, f: (i, 0)),
        out_shape=jax.ShapeDtypeStruct((m, D_MODEL), F32),
        scratch_shapes=[pltpu.VMEM((tm, D_MODEL), BF16)],
        compiler_params=_cparams(("parallel", "arbitrary")),
        name="post_ffn",
    )(x2, hm, hg, hn, w_out, g, w_up_r, w_dn_r)


CHUNK_UNROLL = 4


def _unrolled_chunks(chunk, n):
    u = CHUNK_UNROLL if n % CHUNK_UNROLL == 0 else 1

    def body(i, carry):
        for k in range(u):
            chunk(i * u + k, carry)
        return carry

    lax.fori_loop(0, n // u, body, 0)


def _mlstm_kernel(um_ref, conv0_ref, ct0_ref, n0_ref, m0_ref, wconv_ref, bif_ref, gh_ref,
                  hm_ref, conv_out_ref, ct_out_ref, n_out_ref, m_out_ref,
                  xpad, q_s, k_s, ct_s, n_s, m_s, *, L, n_valid, blk):
    j = pl.program_id(1)

    @pl.when(j == 0)
    def _():
        xpad[0:8, :] = conv0_ref[0]
        ct_s[...] = ct0_ref[0]
        n_s[...] = n0_ref[0]
        m_s[...] = m0_ref[0]

    qk_pre = um_ref[0, :, 0:2 * M_WIDTH]
    xpad[8:8 + blk, :] = qk_pre
    wc = wconv_ref[...]
    acc = (xpad[5:5 + blk, :] * wc[0:1] + xpad[6:6 + blk, :] * wc[1:2]
           + xpad[7:7 + blk, :] * wc[2:3] + qk_pre * wc[3:4])
    qk = acc * _sigmoid(acc)
    q_s[...] = qk[:, :M_WIDTH]
    k_s[...] = qk[:, M_WIDTH:] * (M_HD ** -0.5)
    last = n_valid if blk == L else blk
    conv_out_ref[0] = xpad[last:last + 8, :]
    xpad[0:8, :] = xpad[blk:blk + 8, :]

    row = _iota((L, L), 0)
    col = _iota((L, L), 1)
    causal = row >= col
    tril = causal.astype(F32)
    lane_w = _iota((1, M_WIDTH), 1) // M_HD
    row_w = _iota((M_WIDTH, 1), 0) // M_HD
    valid_col = (_iota((L, 1), 0) < n_valid).astype(F32)

    def chunk(c, carry):
        r0 = pl.multiple_of(c * L, L)
        qc = q_s[pl.ds(r0, L), :]
        kc = k_s[pl.ds(r0, L), :]
        vc = um_ref[0, pl.ds(r0, L), 512:768]
        oc = um_ref[0, pl.ds(r0, L), 768:1024]
        act = um_ref[0, pl.ds(r0, L), 1024:1152] + bif_ref[...]
        lf = _log_sigmoid(act)
        bcum = _dot_hi(tril, lf)
        act_t = _transpose(act)
        bcum_t = _transpose(bcum)
        k_bf = kc.astype(BF16)
        kt_bf = _transpose(kc).astype(BF16)
        ct = ct_s[...]
        ct_bf = ct.astype(BF16)
        n_row = n_s[0:1, :]
        ct_new = ct
        n_new = n_row
        outs = []
        for h in range(M_HEADS):
            bcol = bcum[:, 4 + h:5 + h]
            igcol = act[:, h:h + 1]
            brow = bcum_t[4 + h:5 + h, :]
            igrow = act_t[h:h + 1, :]
            dmat = jnp.where(causal, bcol - brow + igrow, NEG)
            m_h = m_s[h:h + 1, 0:1]
            m_inter = m_h + bcol
            m_new = jnp.maximum(m_inter, jnp.max(dmat, axis=1, keepdims=True))
            w = jnp.exp(dmat - m_new)
            a_inter = jnp.exp(m_inter - m_new)
            hmask = lane_w == h
            q_h = jnp.where(hmask, qc, 0.0)
            q_bf = q_h.astype(BF16)
            s = _dot_nt(q_bf, k_bf) * w
            v_h = vc[:, h * M_HD:(h + 1) * M_HD]
            num = a_inter * _dot(q_bf, ct_bf) + _dot(s.astype(BF16), v_h.astype(BF16))
            qn = jnp.sum(q_h * n_row, axis=1, keepdims=True)
            den = a_inter * qn + jnp.sum(s, axis=1, keepdims=True)
            hh = num / jnp.maximum(jnp.abs(den), jnp.exp(-m_new))
            hh = _sigmoid(oc[:, h * M_HD:(h + 1) * M_HD]) * hh
            hh = hh * lax.rsqrt(jnp.mean(hh * hh, axis=1, keepdims=True) + EPS)
            outs.append(hh * gh_ref[:, h * M_HD:(h + 1) * M_HD])
            b_l = bcol[n_valid - 1:n_valid]
            m_l = m_new[n_valid - 1:n_valid]
            w_l = jnp.exp(b_l - bcol + igcol - m_l) * valid_col
            a_l = jnp.exp(m_inter[n_valid - 1:n_valid] - m_l)
            upd = _dot(kt_bf, (v_h * w_l).astype(BF16))
            ct_new = jnp.where(row_w == h, a_l * ct + upd, ct_new)
            n_new = jnp.where(hmask, a_l * n_row + jnp.sum(kc * w_l, axis=0, keepdims=True), n_new)
            m_s[h:h + 1, :] = jnp.broadcast_to(m_l, (1, LANES))
        ct_s[...] = ct_new
        n_s[...] = jnp.broadcast_to(n_new, n_s.shape)
        hm_ref[0, pl.ds(r0, L), :] = jnp.concatenate(outs, axis=1).astype(BF16)
        return carry

    _unrolled_chunks(chunk, blk // L)

    @pl.when(j == pl.num_programs(1) - 1)
    def _():
        ct_out_ref[0] = ct_s[...]
        n_out_ref[0] = n_s[...]
        m_out_ref[0] = m_s[...]


def _mlstm(um, conv0, ct0, n0, m0, wconv, bif, gh, *, L, n_valid):
    b, t, _ = um.shape
    blk = min(t, 512)
    kern = functools.partial(_mlstm_kernel, L=L, n_valid=n_valid, blk=blk)
    per_b = lambda shape: pl.BlockSpec((1,) + shape, lambda i, j: (i,) + (0,) * len(shape))
    const = lambda shape: pl.BlockSpec(shape, lambda i, j: (0,) * len(shape))
    return pl.pallas_call(
        kern,
        grid=(b, t // blk),
        in_specs=[pl.BlockSpec((1, blk, UM_W), lambda i, j: (i, j, 0)),
                  per_b((8, 2 * M_WIDTH)), per_b((M_WIDTH, M_HD)), per_b((8, M_WIDTH)), per_b((8, LANES)),
                  const((8, 2 * M_WIDTH)), const((1, LANES)), const((1, M_WIDTH))],
        out_specs=[pl.BlockSpec((1, blk, M_WIDTH), lambda i, j: (i, j, 0)),
                   per_b((8, 2 * M_WIDTH)), per_b((M_WIDTH, M_HD)), per_b((8, M_WIDTH)), per_b((8, LANES))],
        out_shape=[jax.ShapeDtypeStruct((b, t, M_WIDTH), BF16),
                   jax.ShapeDtypeStruct((b, 8, 2 * M_WIDTH), F32),
                   jax.ShapeDtypeStruct((b, M_WIDTH, M_HD), F32),
                   jax.ShapeDtypeStruct((b, 8, M_WIDTH), F32),
                   jax.ShapeDtypeStruct((b, 8, LANES), F32)],
        scratch_shapes=[pltpu.VMEM((blk + 8, 2 * M_WIDTH), F32),
                        pltpu.VMEM((blk, M_WIDTH), F32), pltpu.VMEM((blk, M_WIDTH), F32),
                        pltpu.VMEM((M_WIDTH, M_HD), F32), pltpu.VMEM((8, M_WIDTH), F32),
                        pltpu.VMEM((8, LANES), F32)],
        compiler_params=_cparams(("parallel", "arbitrary")),
        name="mlstm",
    )(um, conv0, ct0, n0, m0, wconv, bif, gh)


def _gla_kernel(ug_ref, s0_ref, w2_ref, bg_ref, gh_ref, hg_ref, s_out_ref, s_s, *, L, n_valid, blk):
    j = pl.program_id(1)

    @pl.when(j == 0)
    def _():
        s_s[...] = s0_ref[0]

    row = _iota((L, L), 0)
    col = _iota((L, L), 1)
    causal = row >= col
    tril = causal.astype(F32)
    lane_k = _iota((1, LANES), 1) // G_DK
    row_k = _iota((LANES, 1), 0) // G_DK
    valid_col = (_iota((L, 1), 0) < n_valid).astype(F32)
    mid = max(n_valid // 2, 1)

    def chunk(c, carry):
        r0 = pl.multiple_of(c * L, L)
        q = ug_ref[0, pl.ds(r0, L), 0:128] * (G_DK ** -0.5)
        k = ug_ref[0, pl.ds(r0, L), 128:256]
        v = ug_ref[0, pl.ds(r0, L), 256:512]
        go = ug_ref[0, pl.ds(r0, L), 512:768]
        glr = ug_ref[0, pl.ds(r0, L), 768:896]
        z = _dot_hi(glr, w2_ref[...]) + bg_ref[...]
        g = _log_sigmoid(z) * (1.0 / G_TAU)
        bc = _dot_hi(tril, g)
        c_ref = bc[mid - 1:mid]
        last = bc[n_valid - 1:n_valid]
        qe = (q * jnp.exp(bc - c_ref)).astype(BF16)
        ke = (k * jnp.exp(c_ref - bc)).astype(BF16)
        qin = q * jnp.exp(bc)
        kd_t = _transpose(k * jnp.exp(last - bc) * valid_col).astype(BF16)
        last_col = _transpose(jnp.broadcast_to(last, (8, LANES)))[:, 0:1]
        s_all = s_s[...]
        s_bf = s_all.astype(BF16)
        s_new = jnp.exp(last_col) * s_all
        outs = []
        for h in range(G_HEADS):
            hmask = lane_k == h
            a = jnp.where(causal, _dot_nt(jnp.where(hmask, qe, 0.0).astype(BF16), ke), 0.0)
            v_h = v[:, h * G_DV:(h + 1) * G_DV].astype(BF16)
            o = _dot(jnp.where(hmask, qin, 0.0).astype(BF16), s_bf) + _dot(a.astype(BF16), v_h)
            s_new = s_new + jnp.where(row_k == h, _dot(kd_t, v_h), 0.0)
            o = o * lax.rsqrt(jnp.mean(o * o, axis=1, keepdims=True) + EPS)
            g_o = go[:, h * G_DV:(h + 1) * G_DV]
            outs.append(o * gh_ref[:, h * G_DV:(h + 1) * G_DV] * (g_o * _sigmoid(g_o)))
        s_s[...] = s_new
        hg_ref[0, pl.ds(r0, L), :] = jnp.concatenate(outs, axis=1).astype(BF16)
        return carry

    _unrolled_chunks(chunk, blk // L)

    @pl.when(j == pl.num_programs(1) - 1)
    def _():
        s_out_ref[0] = s_s[...]


def _gla(ug, s0, w2p, bg, gh, *, L, n_valid):
    b, t, _ = ug.shape
    blk = min(t, 512)
    kern = functools.partial(_gla_kernel, L=L, n_valid=n_valid, blk=blk)
    return pl.pallas_call(
        kern,
        grid=(b, t // blk),
        in_specs=[pl.BlockSpec((1, blk, UG_W), lambda i, j: (i, j, 0)),
                  pl.BlockSpec((1, LANES, G_DV), lambda i, j: (i, 0, 0)),
                  pl.BlockSpec((LANES, LANES), lambda i, j: (0, 0)),
                  pl.BlockSpec((1, LANES), lambda i, j: (0, 0)),
                  pl.BlockSpec((1, G_WIDTH), lambda i, j: (0, 0))],
        out_specs=[pl.BlockSpec((1, blk, G_WIDTH), lambda i, j: (i, j, 0)),
                   pl.BlockSpec((1, LANES, G_DV), lambda i, j: (i, 0, 0))],
        out_shape=[jax.ShapeDtypeStruct((b, t, G_WIDTH), BF16),
                   jax.ShapeDtypeStruct((b, LANES, G_DV), F32)],
        scratch_shapes=[pltpu.VMEM((LANES, G_DV), F32)],
        compiler_params=_cparams(("parallel", "arbitrary")),
        name="gla",
    )(ug, s0, w2p, bg, gh)


def _group_mean_sq(x, bm_bf):
    return _dot_split(x * x, bm_bf)


def _rope_slab(y, cos, sin_signed):
    lane = _iota(y.shape, 1)
    rot = jnp.where((lane % N_HD) < (N_HD // 2), pltpu.roll(y, 96, axis=1), pltpu.roll(y, 32, axis=1))
    return y * cos + rot * sin_signed


def _norm_rope_slab(x, g, cos, sin_signed, bm_bf):
    y = x * lax.rsqrt(_group_mean_sq(x, bm_bf) + EPS) * g
    return _rope_slab(y, cos, sin_signed)


def _rows_kernel(nq_ref, nsw_ref, ngate_ref, cos_ref, sin_ref, gq_ref, gks_ref, gkw_ref, bm_ref,
                 q_ref, ks_ref, vs_ref, kw_ref, vw_ref, slc_ref, win_ref, gates_ref):
    cos = cos_ref[...]
    sin = sin_ref[...]
    bm = bm_ref[...]
    low = _iota((1, LANES), 1) < N_HD
    for sl in range(4):
        x = nq_ref[0, :, sl * LANES:(sl + 1) * LANES]
        y = _norm_rope_slab(x, gq_ref[:, sl * LANES:(sl + 1) * LANES], cos, sin, bm) * (N_HD ** -0.5 * LOG2E)
        y_sw = pltpu.roll(y, N_HD, axis=1)
        if sl < 2:
            even, odd = jnp.where(low, y, 0.0), jnp.where(low, y_sw, 0.0)
        else:
            even, odd = jnp.where(low, 0.0, y_sw), jnp.where(low, 0.0, y)
        q_ref[0, 2 * sl] = even.astype(BF16)
        q_ref[0, 2 * sl + 1] = odd.astype(BF16)
    ks = _norm_rope_slab(nsw_ref[0, :, 0:128], gks_ref[...], cos, sin, bm)
    vs = nsw_ref[0, :, 128:256]
    kw = _norm_rope_slab(nsw_ref[0, :, 256:384], gkw_ref[...], cos, sin, bm)
    vw = nsw_ref[0, :, 384:512]
    slc_ref[0, :, 0:128] = ks
    slc_ref[0, :, 128:256] = vs
    win_ref[0, :, 0:128] = kw
    win_ref[0, :, 128:256] = vw
    for ref, val in ((ks_ref, ks), (vs_ref, vs), (kw_ref, kw), (vw_ref, vw)):
        ref[0] = val.astype(BF16)
    gt = _sigmoid(ngate_ref[0])
    gates_ref[0, 0] = gt
    gates_ref[0, 1] = pltpu.roll(gt, LANES - 3 * N_REP, axis=1)


def _rows(nq, nsw, ngate, cos, sin, gq, gks, gkw, bm):
    b, t, _ = nq.shape
    tm = min(t, 512)
    tok = lambda w: pl.BlockSpec((1, tm, w), lambda i, j: (i, j, 0))
    const = lambda shape: pl.BlockSpec(shape, lambda i, j: (0,) * len(shape))
    kv_shape = jax.ShapeDtypeStruct((b, t, LANES), BF16)
    return pl.pallas_call(
        _rows_kernel,
        grid=(b, t // tm),
        in_specs=[tok(NQ_W), tok(NSW_W), tok(NGATE_W),
                  pl.BlockSpec((tm, LANES), lambda i, j: (j, 0)),
                  pl.BlockSpec((tm, LANES), lambda i, j: (j, 0)),
                  const((1, NQ_W)), const((1, LANES)), const((1, LANES)), const((LANES, LANES))],
        out_specs=[pl.BlockSpec((1, N_HEADS, tm, LANES), lambda i, j: (i, 0, j, 0)),
                   tok(LANES), tok(LANES), tok(LANES), tok(LANES), tok(256), tok(256),
                   pl.BlockSpec((1, N_KV, tm, LANES), lambda i, j: (i, 0, j, 0))],
        out_shape=[jax.ShapeDtypeStruct((b, N_HEADS, t, LANES), BF16), kv_shape, kv_shape, kv_shape, kv_shape,
                   jax.ShapeDtypeStruct((b, t, 256), F32), jax.ShapeDtypeStruct((b, t, 256), F32),
                   jax.ShapeDtypeStruct((b, N_KV, t, LANES), F32)],
        compiler_params=_cparams(("parallel", "parallel")),
        name="nsa_rows",
    )(nq, nsw, ngate, cos, sin, gq, gks, gkw, bm)


def _compress_kernel(u_ref, w1_ref, pe_ref, w2_ref, gkc_ref, cos_ref, sin_ref, bm_ref, kc_ref, vc_ref):
    nu = u_ref.shape[1]
    outs = []
    for kv in range(2):
        x = jnp.concatenate(
            [u_ref[0, :, j * 256 + kv * LANES:j * 256 + (kv + 1) * LANES] for j in range(CMP_STRIDE)],
            axis=1).astype(BF16)
        w1 = w1_ref[kv]
        p = _dot(x, w1)
        bias = _dot(pe_ref[kv].astype(BF16), w1)
        pa = p[:, :256] + bias[0:1, :256]
        pb = p[:, 256:] + bias[1:2, 256:]
        pre = pa + pltpu.roll(pb, nu - 1, axis=0)
        hid = (pre * _sigmoid(pre)).astype(BF16)
        outs.append(_dot(hid, w2_ref[kv]))
    kc_ref[0] = _norm_rope_slab(outs[0], gkc_ref[...], cos_ref[...], sin_ref[...], bm_ref[...]).astype(BF16)
    vc_ref[0] = outs[1].astype(BF16)


def _compress(units, w1r, pe_r, w2r, gkc, cos_c, sin_c, bm):
    b, nu, _ = units.shape
    const = lambda shape: pl.BlockSpec(shape, lambda i: (0,) * len(shape))
    out = jax.ShapeDtypeStruct((b, nu, LANES), BF16)
    return pl.pallas_call(
        _compress_kernel,
        grid=(b,),
        in_specs=[pl.BlockSpec((1, nu, 4096), lambda i: (i, 0, 0)),
                  const((2, 2048, 512)), const((2, 8, 2048)), const((2, 256, LANES)),
                  const((1, LANES)), const((nu, LANES)), const((nu, LANES)), const((LANES, LANES))],
        out_specs=[pl.BlockSpec((1, nu, LANES), lambda i: (i, 0, 0))] * 2,
        out_shape=[out, out],
        compiler_params=_cparams(("parallel",)),
        name="nsa_compress",
    )(units, w1r, pe_r, w2r, gkc, cos_c, sin_c, bm)


def _cmp_kernel(q_ref, kc_ref, vc_ref, c2s_ref, oc_ref, sel_ref, *, tq, pos0, n_cmp, nsel):
    qi = pl.program_id(1)
    nc_pad = kc_ref.shape[1]
    q = q_ref[0].reshape(N_HEADS * tq, LANES)
    s = _dot_nt(q, kc_ref[0]).reshape(N_HEADS, tq, nc_pad)
    qpos = pos0 + qi * tq + _iota((tq, 1), 0)
    c_idx = _iota((tq, nc_pad), 1)
    c_ok = jnp.where(c_idx < n_cmp, c_idx * CMP_STRIDE + (CMP_LEN - 1), jnp.int32(2 ** 30)) <= qpos
    s = s + jnp.where(c_ok, 0.0, NEG)[None]
    p = jnp.exp2(s - jnp.max(s, axis=-1, keepdims=True))
    p = p * ((qpos >= CMP_LEN - 1).astype(F32)[None] / jnp.sum(p, axis=-1, keepdims=True))
    oc_ref[0] = _dot(p.reshape(N_HEADS * tq, nc_pad).astype(BF16), vc_ref[0]).reshape(N_HEADS, tq, LANES)
    psum = jnp.sum(p.reshape(N_KV, N_REP, tq, nc_pad), axis=1).reshape(N_KV * tq, nc_pad)
    imp = _dot_split(psum, c2s_ref[...])
    work = _transpose2d(imp)
    ncol = N_KV * tq
    qpos_row = pos0 + qi * tq + _iota((1, ncol), 1) % tq
    blk = _iota((nsel, ncol), 0)
    cur = qpos_row // SLC_BLOCK
    forced = jnp.where(blk == 0, 1.0, jnp.where(blk == cur, 1.0, jnp.where(blk == cur - 1, 1.0, 0.0)))
    work = jnp.where(blk * SLC_BLOCK <= qpos_row, work + FORCE_BONUS * forced, NEG)
    blk_f = blk.astype(F32)
    sel = jnp.zeros((nsel, ncol), F32)
    for _ in range(SLC_TOPK):
        m = jnp.max(work, axis=0, keepdims=True)
        idx = jnp.min(jnp.where(work == m, blk_f, float(nsel)), axis=0, keepdims=True)
        hit = blk_f == idx
        sel = jnp.where(hit, 1.0, sel)
        work = jnp.where(hit, -jnp.inf, work)
    sel_ref[0] = _transpose2d(sel).reshape(N_KV, tq, nsel).astype(BF16)


def _cmp_attn(q_pad, kc, vc, c2s, *, tq, pos0, n_cmp):
    b, _, t, _ = q_pad.shape
    nc_pad = kc.shape[1]
    nsel = c2s.shape[1]
    kern = functools.partial(_cmp_kernel, tq=tq, pos0=pos0, n_cmp=n_cmp, nsel=nsel)
    return pl.pallas_call(
        kern,
        grid=(b, t // tq),
        in_specs=[pl.BlockSpec((1, N_HEADS, tq, LANES), lambda i, j: (i, 0, j, 0)),
                  pl.BlockSpec((1, nc_pad, LANES), lambda i, j: (i, 0, 0)),
                  pl.BlockSpec((1, nc_pad, LANES), lambda i, j: (i, 0, 0)),
                  pl.BlockSpec((nc_pad, nsel), lambda i, j: (0, 0))],
        out_specs=[pl.BlockSpec((1, N_HEADS, tq, LANES), lambda i, j: (i, 0, j, 0)),
                   pl.BlockSpec((1, N_KV, tq, nsel), lambda i, j: (i, 0, j, 0))],
        out_shape=[jax.ShapeDtypeStruct((b, N_HEADS, t, LANES), F32),
                   jax.ShapeDtypeStruct((b, N_KV, t, nsel), BF16)],
        compiler_params=_cparams(("parallel", "parallel")),
        name="nsa_cmp_topk",
    )(q_pad, kc, vc, c2s)


SLAB = 64


def _attend_tile(q_parts, k, v, bias_scr, s_scr, p_scr, m_scr, l_scr, alpha_scr, acc_scr, *, tq, width):
    nch = width // LANES

    def chunk(rows, i, c):
        x = s_scr[rows, c * LANES:(c + 1) * LANES]
        if bias_scr is not None:
            x = x + bias_scr[i * SLAB:(i + 1) * SLAB, c * LANES:(c + 1) * LANES]
        return x

    slabs = [(slice(r * tq + i * SLAB, r * tq + (i + 1) * SLAB), i) for r in range(N_REP) for i in range(tq // SLAB)]
    for r in range(N_REP):
        s_scr[r * tq:(r + 1) * tq, 0:width] = _dot_nt(q_parts[r], k)
    for rows, i in slabs:
        mx = chunk(rows, i, 0)
        for c in range(1, nch):
            mx = jnp.maximum(mx, chunk(rows, i, c))
        m_old = m_scr[rows, :]
        m_new = jnp.maximum(m_old, jnp.max(mx, axis=1, keepdims=True))
        alpha_scr[rows, :] = jnp.exp2(m_old - m_new)
        m_scr[rows, :] = m_new
    for rows, i in slabs:
        m_new = m_scr[rows, :]
        lsum = alpha_scr[rows, :] * l_scr[rows, :]
        for c in range(nch):
            p = jnp.exp2(chunk(rows, i, c) - m_new)
            lsum = lsum + p
            p_scr[rows, c * LANES:(c + 1) * LANES] = p.astype(BF16)
        l_scr[rows, :] = lsum
    for r in range(N_REP):
        rs = slice(r * tq, (r + 1) * tq)
        acc_scr[rs, :] = alpha_scr[rs, :] * acc_scr[rs, :] + _dot(p_scr[rs, 0:width], v)


def _slcwin_kernel(q_ref, ks_ref, vs_ref, kw_ref, vw_ref, sel_ref, et_ref, oc_ref, gates_ref, hn_ref,
                   s_scr, bias_scr, p_scr, m_scr, l_scr, alpha_scr, acc_scr, ow_scr, *, tq):
    g = pl.program_id(1)
    qi = pl.program_id(2)
    rows = N_REP * tq
    q0 = qi * tq
    qpos = q0 + _iota((tq, 1), 0)
    scr = dict(s_scr=s_scr, p_scr=p_scr, m_scr=m_scr, l_scr=l_scr, alpha_scr=alpha_scr, acc_scr=acc_scr, tq=tq)

    def reset():
        m_scr[...] = jnp.full((rows, LANES), NEG, F32)
        l_scr[...] = jnp.zeros((rows, LANES), F32)
        acc_scr[...] = jnp.zeros((rows, LANES), F32)

    def result():
        return acc_scr[...] / jnp.sum(l_scr[...], axis=1, keepdims=True)

    wlen = WINDOW + tq
    start = pl.multiple_of(jnp.maximum(q0 - WINDOW, 0), tq)
    reset()
    dist = qpos - (start + _iota((tq, wlen), 1))
    bias_scr[:, 0:wlen] = jnp.where(jnp.where(dist >= 0, dist, WINDOW) < WINDOW, 0.0, NEG)
    _attend_tile([q_ref[0, r] for r in range(N_REP)], kw_ref[0, pl.ds(start, wlen), :],
                 vw_ref[0, pl.ds(start, wlen), :], bias_scr, width=wlen, **scr)
    ow_scr[...] = result()

    reset()
    sel_m = ((sel_ref[0, 0].astype(F32) - 1.0) * (-NEG)).astype(BF16)
    q_aug = [jnp.concatenate([q_ref[0, r], sel_m], axis=1) for r in range(N_REP)]
    n_kv = (q0 + tq + KV_TILE - 1) // KV_TILE

    def tile_inputs(j):
        k0 = pl.multiple_of(j * KV_TILE, KV_TILE)
        k_aug = jnp.concatenate([ks_ref[0, pl.ds(k0, KV_TILE), :], et_ref[pl.ds(k0, KV_TILE), :]], axis=1)
        return k0, k_aug, vs_ref[0, pl.ds(k0, KV_TILE), :]

    def body(j, carry):
        _, k_aug, v = tile_inputs(j)
        _attend_tile(q_aug, k_aug, v, None, width=KV_TILE, **scr)
        return carry

    lax.fori_loop(0, n_kv - 1, body, 0)
    k0, k_aug, v = tile_inputs(n_kv - 1)
    bias_scr[:, 0:KV_TILE] = jnp.where(k0 + _iota((tq, KV_TILE), 1) <= qpos, 0.0, NEG)
    _attend_tile(q_aug, k_aug, v, bias_scr, width=KV_TILE, **scr)
    o_s_all = result()

    gt = gates_ref[0, 0]
    low = _iota((1, LANES), 1) < N_HD
    vals = []
    for r in range(N_REP):
        rs = slice(r * tq, (r + 1) * tq)
        vals.append(gt[:, 3 * r:3 * r + 1] * oc_ref[0, r] + gt[:, 3 * r + 1:3 * r + 2] * o_s_all[rs]
                    + gt[:, 3 * r + 2:3 * r + 3] * ow_scr[rs, :])
    for pair in range(N_REP // 2):
        a, b = vals[2 * pair], vals[2 * pair + 1]
        a_sw, b_sw = pltpu.roll(a, N_HD, axis=1), pltpu.roll(b, N_HD, axis=1)
        lo = jnp.where(g == 0, a, a_sw)
        hi = jnp.where(g == 0, b_sw, b)
        hn_ref[0, :, pair * LANES:(pair + 1) * LANES] = jnp.where(low, lo, hi).astype(BF16)


def _slcwin(q_pad, ks, vs, kw, vw, sel, e_t, oc, gates, *, tq):
    b, _, t, _ = q_pad.shape
    nsel = sel.shape[3]
    assert nsel == LANES and e_t.shape == (t, LANES)
    rows = N_REP * tq
    wlen = WINDOW + tq
    kern = functools.partial(_slcwin_kernel, tq=tq)
    qspec = pl.BlockSpec((1, N_REP, tq, LANES), lambda i, g, j: (i, g, j, 0))
    full = pl.BlockSpec((1, t, LANES), lambda i, g, j: (i, 0, 0))
    return pl.pallas_call(
        kern,
        grid=(b, N_KV, t // tq),
        in_specs=[qspec, full, full, full, full,
                  pl.BlockSpec((1, 1, tq, nsel), lambda i, g, j: (i, g, j, 0)),
                  pl.BlockSpec((t, LANES), lambda i, g, j: (0, 0)),
                  qspec,
                  pl.BlockSpec((1, 1, tq, LANES), lambda i, g, j: (i, g, j, 0))],
        out_specs=pl.BlockSpec((1, tq, N_REP * N_HD), lambda i, g, j: (i, j, g)),
        out_shape=jax.ShapeDtypeStruct((b, t, N_WIDTH), BF16),
        scratch_shapes=[pltpu.VMEM((rows, wlen), F32), pltpu.VMEM((tq, wlen), F32), pltpu.VMEM((rows, wlen), BF16),
                        pltpu.VMEM((rows, LANES), F32), pltpu.VMEM((rows, LANES), F32), pltpu.VMEM((rows, LANES), F32),
                        pltpu.VMEM((rows, LANES), F32), pltpu.VMEM((rows, LANES), F32)],
        compiler_params=_cparams(("parallel", "parallel", "parallel")),
        name="nsa_slc_win",
    )(q_pad, ks, vs, kw, vw, sel, e_t, oc, gates)


GATHER_PAGES = 16


PAGE_UNITS = PAGE_SIZE // CMP_STRIDE


def _gather_cmp_kernel(pt_ref, *refs):
    out_ref = refs[-1]
    for i in range(GATHER_PAGES):
        out_ref[0, i * PAGE_UNITS:(i + 1) * PAGE_UNITS, :] = refs[i][0, 0]


def _gather_cmp(page_table, cache_units, layer):
    db, n_pages = page_table.shape
    uw = cache_units.shape[-1]

    def page_spec(i):
        return pl.BlockSpec((1, 1, PAGE_UNITS, uw),
                            lambda b, c, pt: (layer, pt[b, c * GATHER_PAGES + i], 0, 0))

    return pl.pallas_call(
        _gather_cmp_kernel,
        grid_spec=pltpu.PrefetchScalarGridSpec(
            num_scalar_prefetch=1,
            grid=(db, n_pages // GATHER_PAGES),
            in_specs=[page_spec(i) for i in range(GATHER_PAGES)],
            out_specs=pl.BlockSpec((1, GATHER_PAGES * PAGE_UNITS, uw), lambda b, c, pt: (b, c, 0))),
        out_shape=jax.ShapeDtypeStruct((db, n_pages * PAGE_UNITS, uw), F32),
        compiler_params=_cparams(("parallel", "parallel")),
        name="gather_cmp_pages",
    )(page_table, *([cache_units] * GATHER_PAGES))


SLC_PAGES = 16


def _heads_bias(b2, ts):
    w = b2.shape[1]
    return jnp.broadcast_to(b2.reshape(N_KV, 1, ts, w), (N_KV, N_REP, ts, w)).reshape(N_HEADS * ts, w)


def _slc_sample_kernel(pt_ref, *refs, ts):
    pages = refs[:SLC_PAGES]
    (q_ref, sel_ref, e_ref, enew_ref, newslc_ref, win_ref, newwin_ref, oc_ref, gates_ref,
     hn_ref, m_scr, l_scr, acc_scr) = refs[SLC_PAGES:]
    c = pl.program_id(1)
    rows = N_HEADS * ts
    q = q_ref[0].reshape(rows, LANES)
    sel2 = sel_ref[0].reshape(N_KV * ts, sel_ref.shape[3])

    @pl.when(c == 0)
    def _():
        m_scr[...] = jnp.full((rows, 1), NEG, F32)
        l_scr[...] = jnp.zeros((rows, 1), F32)
        acc_scr[...] = jnp.zeros((rows, LANES), F32)

    def online(s, pv_fn):
        m_old = m_scr[...]
        m_new = jnp.maximum(m_old, jnp.max(s, axis=1, keepdims=True))
        p = jnp.exp2(s - m_new)
        alpha = jnp.exp2(m_old - m_new)
        l_scr[...] = alpha * l_scr[...] + jnp.sum(p, axis=1, keepdims=True)
        acc_scr[...] = alpha * acc_scr[...] + pv_fn(p.astype(BF16))
        m_scr[...] = m_new

    picked = _dot(sel2, e_ref[...])
    bias = _heads_bias(picked * (-NEG) + NEG, ts)
    kvs = [pg[0, 0].astype(BF16) for pg in pages]
    s = jnp.concatenate([_dot_nt(q, kv[:, :LANES]) for kv in kvs], axis=1) + bias

    def pv_pages(p):
        out = _dot(p[:, 0:PAGE_SIZE], kvs[0][:, LANES:])
        for i in range(1, SLC_PAGES):
            out = out + _dot(p[:, i * PAGE_SIZE:(i + 1) * PAGE_SIZE], kvs[i][:, LANES:])
        return out

    online(s, pv_pages)

    @pl.when(c == pl.num_programs(1) - 1)
    def _():
        tok = _iota((N_KV * ts, 1), 0) % ts
        key = _iota((N_KV * ts, PAGE_SIZE), 1)
        new = newslc_ref[0].astype(BF16)
        picked_n = _dot(sel2, enew_ref[...])
        bias_n = _heads_bias(jnp.where(key <= tok, picked_n, 0.0) * (-NEG) + NEG, ts)
        online(_dot_nt(q, new[:, :LANES]) + bias_n, lambda p: _dot(p, new[:, LANES:]))
        win = win_ref[0, 0].astype(BF16)
        nwin = newwin_ref[0].astype(BF16)
        wkey = _iota((N_KV * ts, WINDOW), 1)
        bias_w = jnp.concatenate([jnp.where(wkey > tok, 0.0, NEG), jnp.where(key <= tok, 0.0, NEG)], axis=1)
        s_w = jnp.concatenate([_dot_nt(q, win[:, :LANES]), _dot_nt(q, nwin[:, :LANES])], axis=1) + _heads_bias(bias_w, ts)
        p_w = jnp.exp2(s_w - jnp.max(s_w, axis=1, keepdims=True))
        l_w = jnp.sum(p_w, axis=1, keepdims=True)
        p_w = p_w.astype(BF16)
        o_w = (_dot(p_w[:, :WINDOW], win[:, LANES:]) + _dot(p_w[:, WINDOW:], nwin[:, LANES:])) / l_w
        o_s = acc_scr[...] / l_scr[...]
        low = _iota((1, LANES), 1) < N_HD
        vals = []
        for h in range(N_HEADS):
            g, r = divmod(h, N_REP)
            gt = gates_ref[0, g]
            rs = slice(h * ts, (h + 1) * ts)
            vals.append(gt[:, 3 * r:3 * r + 1] * oc_ref[0, h] + gt[:, 3 * r + 1:3 * r + 2] * o_s[rs]
                        + gt[:, 3 * r + 2:3 * r + 3] * o_w[rs])
        for pair in range(N_HEADS // 2):
            a, b = vals[2 * pair], vals[2 * pair + 1]
            if pair < N_REP // 2:
                lo, hi = a, pltpu.roll(b, N_HD, axis=1)
            else:
                lo, hi = pltpu.roll(a, N_HD, axis=1), b
            hn_ref[0, :, pair * LANES:(pair + 1) * LANES] = jnp.where(low, lo, hi).astype(BF16)


def _slc_sample(page_table, cache_flat, layer, q_pad, sel, e_main, e_new, new_slc, win_flat, new_win, oc, gates):
    db, n_pages = page_table.shape
    ts = q_pad.shape[2]
    nsel = sel.shape[3]
    n_steps = n_pages // SLC_PAGES
    step_keys = SLC_PAGES * PAGE_SIZE
    rows = N_HEADS * ts

    def page_spec(i):
        return pl.BlockSpec((1, 1, PAGE_SIZE, 256), lambda b, c, pt: (layer, pt[b, c * SLC_PAGES + i], 0, 0))

    per_b = lambda shape: pl.BlockSpec((1,) + shape, lambda b, c, pt: (b,) + (0,) * len(shape))
    return pl.pallas_call(
        functools.partial(_slc_sample_kernel, ts=ts),
        grid_spec=pltpu.PrefetchScalarGridSpec(
            num_scalar_prefetch=1,
            grid=(db, n_steps),
            in_specs=[page_spec(i) for i in range(SLC_PAGES)] + [
                per_b((N_HEADS, ts, LANES)), per_b((N_KV, ts, nsel)),
                pl.BlockSpec((nsel, step_keys), lambda b, c, pt: (0, c)),
                pl.BlockSpec((nsel, PAGE_SIZE), lambda b, c, pt: (0, 0)),
                per_b((PAGE_SIZE, 256)),
                pl.BlockSpec((1, 1, WINDOW, 256), lambda b, c, pt: (layer, b, 0, 0)),
                per_b((PAGE_SIZE, 256)), per_b((N_HEADS, ts, LANES)), per_b((N_KV, ts, LANES))],
            out_specs=per_b((ts, N_WIDTH)),
            scratch_shapes=[pltpu.VMEM((rows, 1), F32), pltpu.VMEM((rows, 1), F32), pltpu.VMEM((rows, LANES), F32)]),
        out_shape=jax.ShapeDtypeStruct((db, ts, N_WIDTH), BF16),
        compiler_params=_cparams(("parallel", "arbitrary")),
        name="nsa_slc_sample",
    )(page_table, *([cache_flat] * SLC_PAGES), q_pad, sel, e_main, e_new, new_slc, win_flat, new_win, oc, gates)


def _rope_tables(pos):
    half = N_HD // 2
    inv = ROPE_THETA ** (-jnp.arange(half, dtype=F32) / half)
    ang = pos.astype(F32)[:, None] * inv[None, :]
    cos, sin = jnp.cos(ang), jnp.sin(ang)
    return jnp.tile(jnp.concatenate([cos, cos], axis=1), (1, 2)), jnp.tile(jnp.concatenate([-sin, sin], axis=1), (1, 2))


def _pad_cols(a, w):
    return jnp.pad(a, ((0, 0), (0, w - a.shape[1])))


def _prep_layer(l, w_norm_mix, w_in, b_mlstm_if, w_mlstm_conv, w_mlstm_hnorm, w_gla_gate2, b_gla_gate,
                w_gla_hnorm, w_qk_norm, w_cmp_pe, w_cmp_1, w_cmp_2, w_out, w_norm_ffn, w_ffn_up, w_ffn_down):
    wi = w_in[l]
    col = lambda n: wi[:, _OFF[n][0]:_OFF[n][0] + _OFF[n][1]]
    um = _pad_cols(jnp.concatenate([col('m_qk'), col('m_v'), col('m_o'), col('m_i'), col('m_f')], axis=1), UM_W)
    ug = _pad_cols(jnp.concatenate([col('g_q'), col('g_k'), col('g_v'), col('g_o'), col('g_lr')], axis=1), UG_W)
    ncmp = jnp.concatenate([col('n_kc'), col('n_vc')], axis=1)
    nsw = jnp.concatenate([col('n_ks'), col('n_vs'), col('n_kw'), col('n_vw')], axis=1)
    ngate = _pad_cols(col('n_gate'), NGATE_W)
    p = {}
    p['w_cat'] = jnp.concatenate([um, ug, col('n_q'), ncmp, nsw, ngate], axis=1).astype(BF16)
    p['g_mix'] = w_norm_mix[l][None, :]
    p['wconv'] = jnp.pad(w_mlstm_conv[l], ((0, 8 - M_CONV), (0, 0)))
    p['bif'] = _pad_cols(b_mlstm_if[l][None, :], LANES)
    p['gh_m'] = w_mlstm_hnorm[l][None, :]
    p['w2p'] = jnp.pad(w_gla_gate2[l], ((0, LANES - G_RANK), (0, 0)))
    p['bg'] = b_gla_gate[l][None, :]
    p['gh_g'] = w_gla_hnorm[l][None, :]
    gqk = w_qk_norm[l]
    p['gq'] = jnp.tile(gqk[0], N_HEADS)[None, :]
    p['gkc'] = jnp.tile(gqk[1], N_KV)[None, :]
    p['gks'] = jnp.tile(gqk[2], N_KV)[None, :]
    p['gkw'] = jnp.tile(gqk[3], N_KV)[None, :]
    w1 = w_cmp_1[l].reshape(2, 2, CMP_STRIDE, N_HD, CMP_HIDDEN)
    eye = jnp.eye(N_KV, dtype=F32)
    w1r = jnp.einsum('khjdc,gf->kjgdhfc', w1, eye).reshape(2, CMP_STRIDE * N_KV * N_HD, 2 * N_KV * CMP_HIDDEN)
    p['w1r'] = w1r.astype(BF16)
    pe = w_cmp_pe[l].reshape(2, 2, CMP_STRIDE, 1, N_HD)
    pe = jnp.broadcast_to(pe, (2, 2, CMP_STRIDE, N_KV, N_HD)).reshape(2, 2, 2048)
    p['pe_r'] = jnp.pad(pe, ((0, 0), (0, 6), (0, 0)))
    p['w2r'] = jnp.einsum('kcd,gf->kgcfd', w_cmp_2[l], eye).reshape(2, N_KV * CMP_HIDDEN, N_KV * N_HD).astype(BF16)
    p['w_out'] = w_out[l].astype(BF16)
    p['g_ffn'] = w_norm_ffn[l][None, :]
    nf = 2
    tf = D_FF // nf
    wu = w_ffn_up[l]
    p['w_up_r'] = jnp.stack([jnp.concatenate([wu[:, f * tf:(f + 1) * tf], wu[:, D_FF + f * tf:D_FF + (f + 1) * tf]],
                                             axis=1) for f in range(nf)]).astype(BF16)
    p['w_dn_r'] = w_ffn_down[l].reshape(nf, tf, D_MODEL).astype(BF16)
    return p


def _cmp2slc(n_cmp, nc_pad, nsel):
    m = np.zeros((nc_pad, nsel), np.float32)
    per = SLC_BLOCK // CMP_STRIDE
    for n in range(n_cmp):
        for u in range(CMP_LEN // CMP_STRIDE):
            m[n, (n + u) // per] += 1.0
    return jnp.asarray(m, BF16)


def _expand_mat(nsel, kvlen):
    return jnp.asarray((np.arange(kvlen)[None, :] // SLC_BLOCK) == np.arange(nsel)[:, None], BF16)


def _group_mean_mat():
    idx = np.arange(LANES) // N_HD
    return jnp.asarray((idx[:, None] == idx[None, :]) / float(N_HD), BF16)


def _mixers(p, um, ug, mstate, gstate, *, L, n_valid):
    conv0, ct0, n0, m0 = mstate
    hm, conv_o, ct_o, n_o, m_o = _mlstm(um, conv0, ct0, n0, m0, p['wconv'], p['bif'], p['gh_m'], L=L, n_valid=n_valid)
    hg, s_o = _gla(ug, gstate, p['w2p'], p['bg'], p['gh_g'], L=L, n_valid=n_valid)
    return hm, hg, (conv_o, ct_o, n_o, m_o), s_o


def _mlstm_state_in(conv, c, n, m):
    b = conv.shape[0]
    conv0 = jnp.pad(conv.astype(F32), ((0, 0), (8 - (M_CONV - 1), 0), (0, 0)))
    ct0 = jnp.swapaxes(c.astype(F32), -1, -2).reshape(b, M_WIDTH, M_HD)
    n0 = jnp.broadcast_to(n.astype(F32).reshape(b, 1, M_WIDTH), (b, 8, M_WIDTH))
    m0 = jnp.broadcast_to(jnp.pad(m.astype(F32), ((0, 0), (0, 8 - M_HEADS)))[:, :, None], (b, 8, LANES))
    return conv0, ct0, n0, m0


def _mlstm_state_out(conv_o, ct_o, n_o, m_o):
    b = conv_o.shape[0]
    c = jnp.swapaxes(ct_o.reshape(b, M_HEADS, M_HD, M_HD), -1, -2)
    return c, n_o[:, 0].reshape(b, M_HEADS, M_HD), m_o[:, :M_HEADS, 0], conv_o[:, 8 - (M_CONV - 1):]


def kernel(x_prompt, x_sample, cache_cmp_kv, cache_slc_kv, state_win_kv, state_mlstm_C, state_mlstm_n,
           state_mlstm_m, state_mlstm_conv, state_gla_S, page_table, w_norm_mix, w_in, b_mlstm_if,
           w_mlstm_conv, w_mlstm_hnorm, w_gla_gate2, b_gla_gate, w_gla_hnorm, w_qk_norm, w_cmp_pe,
           w_cmp_1, w_cmp_2, w_out, w_norm_ffn, w_ffn_up, w_ffn_down):
    b, t, _ = x_prompt.shape
    db, td, _ = x_sample.shape
    depth = w_in.shape[0]
    n_pages = page_table.shape[1]
    past = n_pages * PAGE_SIZE
    win_buf = state_win_kv.shape[2]
    assert t % 512 == 0 and td <= SAMPLE_T and n_pages % SLC_PAGES == 0 and win_buf == WINDOW

    bm = _group_mean_mat()
    cos_p, sin_p = _rope_tables(jnp.arange(t))
    nu_p = t // CMP_STRIDE
    ncmp_p = (t - CMP_LEN) // CMP_STRIDE + 1
    cos_cp, sin_cp = _rope_tables(jnp.arange(nu_p) * CMP_STRIDE + CMP_LEN - 1)
    nsel_p = -(-t // SLC_BLOCK)
    nsel_p = -(-nsel_p // LANES) * LANES
    c2s_p = _cmp2slc(ncmp_p, nu_p, nsel_p)
    e_p = _expand_mat(nsel_p, t).T
    tq_p = 256
    cos_s, sin_s = _rope_tables(past + jnp.arange(SAMPLE_T))
    nu_s = past // CMP_STRIDE
    ncmp_s = (past + td - CMP_LEN) // CMP_STRIDE + 1
    cos_cs, sin_cs = _rope_tables(jnp.arange(nu_s) * CMP_STRIDE + CMP_LEN - 1)
    nsel_s = -(-(-(-(past + td) // SLC_BLOCK)) // LANES) * LANES
    c2s_s = _cmp2slc(ncmp_s, nu_s, nsel_s)
    e_s = _expand_mat(nsel_s, past + PAGE_SIZE)
    e_s_main, e_s_new = e_s[:, :past], e_s[:, past:]
    cmp_units = cache_cmp_kv.reshape(depth, cache_cmp_kv.shape[1], PAGE_UNITS, CMP_STRIDE * 256)
    slc_flat = cache_slc_kv.reshape(depth, cache_slc_kv.shape[1], PAGE_SIZE, 256)
    win_flat = state_win_kv.reshape(depth, db, win_buf, 256)

    xp = x_prompt.reshape(b * t, D_MODEL)
    xs = jnp.pad(x_sample, ((0, 0), (0, SAMPLE_T - td), (0, 0))).reshape(db * SAMPLE_T, D_MODEL)

    zero_m = _mlstm_state_in(jnp.zeros((b, M_CONV - 1, 2 * M_WIDTH), F32), jnp.zeros((b, M_HEADS, M_HD, M_HD), F32),
                             jnp.zeros((b, M_HEADS, M_HD), F32), jnp.zeros((b, M_HEADS), F32))
    zero_g = jnp.zeros((b, LANES, G_DV), F32)

    pl_out = [[] for _ in range(8)]
    sl_out = [[] for _ in range(8)]
    for l in range(depth):
        p = _prep_layer(l, w_norm_mix, w_in, b_mlstm_if, w_mlstm_conv, w_mlstm_hnorm, w_gla_gate2, b_gla_gate,
                        w_gla_hnorm, w_qk_norm, w_cmp_pe, w_cmp_1, w_cmp_2, w_out, w_norm_ffn, w_ffn_up, w_ffn_down)
        um, ug, nq, ncmp, nsw, ngate = _pre(xp, p['g_mix'], p['w_cat'])
        r3 = lambda a, bb, tt: a.reshape(bb, tt, a.shape[-1])
        hm, hg, mst, gst = _mixers(p, r3(um, b, t), r3(ug, b, t), zero_m, zero_g, L=M_CHUNK, n_valid=M_CHUNK)
        q_hm, ks, vs, kw, vw, slc_f, win_f, gates = _rows(r3(nq, b, t), r3(nsw, b, t), r3(ngate, b, t), cos_p, sin_p,
                                                          p['gq'], p['gks'], p['gkw'], bm)
        kc, vc = _compress(ncmp.reshape(b, nu_p, 4096), p['w1r'], p['pe_r'], p['w2r'], p['gkc'], cos_cp, sin_cp, bm)
        oc, sel = _cmp_attn(q_hm, kc, vc, c2s_p, tq=tq_p, pos0=0, n_cmp=ncmp_p)
        hn = _slcwin(q_hm, ks, vs, kw, vw, sel, e_p, oc, gates, tq=tq_p)
        xp = _post(xp, hm.reshape(b * t, -1), hg.reshape(b * t, -1), hn.reshape(b * t, -1),
                   p['w_out'], p['g_ffn'], p['w_up_r'], p['w_dn_r'])
        c_o, n_o, m_o, conv_o = _mlstm_state_out(*mst)
        kv6 = lambda a: a.reshape(a.shape[0], a.shape[1], 2, N_KV, N_HD)
        for lst, val in zip(pl_out, (kv6(ncmp.reshape(b, t, 256)), kv6(slc_f), kv6(win_f[:, t - min(WINDOW, t):]),
                                     c_o, n_o, m_o, conv_o, gst.reshape(b, G_HEADS, G_DK, G_DV))):
            lst.append(val)

        um, ug, nq, ncmp, nsw, ngate = _pre(xs, p['g_mix'], p['w_cat'])
        mstate = _mlstm_state_in(state_mlstm_conv[l], state_mlstm_C[l], state_mlstm_n[l], state_mlstm_m[l])
        gstate = state_gla_S[l].astype(F32).reshape(db, LANES, G_DV)
        hm, hg, mst, gst = _mixers(p, r3(um, db, SAMPLE_T), r3(ug, db, SAMPLE_T), mstate, gstate,
                                   L=SAMPLE_T, n_valid=td)
        q_hm, ks_n, vs_n, kw_n, vw_n, slc_f, win_f, gates = _rows(
            r3(nq, db, SAMPLE_T), r3(nsw, db, SAMPLE_T), r3(ngate, db, SAMPLE_T), cos_s, sin_s,
            p['gq'], p['gks'], p['gkw'], bm)
        units = _gather_cmp(page_table, cmp_units, l)
        kc, vc = _compress(units, p['w1r'], p['pe_r'], p['w2r'], p['gkc'], cos_cs, sin_cs, bm)
        oc, sel = _cmp_attn(q_hm, kc, vc, c2s_s, tq=SAMPLE_T, pos0=past, n_cmp=ncmp_s)
        pad_page = lambda a: jnp.pad(a, ((0, 0), (0, PAGE_SIZE - a.shape[1]), (0, 0)))
        hn = _slc_sample(page_table, slc_flat, l, q_hm, sel, e_s_main, e_s_new, pad_page(slc_f), win_flat,
                         pad_page(win_f), oc, gates)
        win_all = jnp.concatenate([win_flat[l].astype(F32), win_f[:, :td]], axis=1)
        xs = _post(xs, hm.reshape(db * SAMPLE_T, -1), hg.reshape(db * SAMPLE_T, -1), hn.reshape(db * SAMPLE_T, -1),
                   p['w_out'], p['g_ffn'], p['w_up_r'], p['w_dn_r'])
        c_o, n_o, m_o, conv_o = _mlstm_state_out(*mst)
        for lst, val in zip(sl_out, (kv6(ncmp.reshape(db, SAMPLE_T, 256)[:, :td]), kv6(slc_f[:, :td]),
                                     kv6(win_all[:, win_all.shape[1] - win_buf:]),
                                     c_o, n_o, m_o, conv_o, gst.reshape(db, G_HEADS, G_DK, G_DV))):
            lst.append(val)

    outs_p = [jnp.stack(a) for a in pl_out]
    outs_s = [jnp.stack(a) for a in sl_out]
    y_p = xp.reshape(b, t, D_MODEL)
    y_s = xs.reshape(db, SAMPLE_T, D_MODEL)[:, :td]
    return (y_p, y_s, *outs_p, *outs_s)
```

```python
import functools
import math

import numpy as np
import jax
import jax.numpy as jnp
from jax import lax
from jax.experimental import pallas as pl
from jax.experimental.pallas import tpu as pltpu

F32 = jnp.float32
BF16 = jnp.bfloat16

D_MODEL = 1024
M_HEADS, M_HD, M_WIDTH, M_CONV, M_CHUNK = 4, 64, 256, 4, 64
G_HEADS, G_DK, G_DV, G_WIDTH, G_RANK, G_TAU, G_CHUNK = 4, 32, 64, 256, 16, 16.0, 64
N_HEADS, N_HD, N_KV, N_REP, N_WIDTH = 8, 64, 2, 4, 512
CMP_LEN, CMP_STRIDE, CMP_HIDDEN = 32, 16, 128
SLC_BLOCK, SLC_TOPK, WINDOW = 64, 16, 512
ROPE_THETA = 10000.0
D_FF = 2816
PAGE_SIZE = 128
NEG = -1e30
FORCE_BONUS = 1e4
EPS = 1e-6
LOG2E = 1.4426950408889634

LANES = 128
VMEM_LIMIT = 56 * 1024 * 1024
KV_TILE = 512
SAMPLE_T = 16

_OFF = {}
_o = 0
for _name, _w in (('m_qk', 512), ('m_v', 256), ('m_i', 4), ('m_f', 4), ('m_o', 256),
                  ('g_q', 128), ('g_k', 128), ('g_v', 256), ('g_lr', 16), ('g_o', 256),
                  ('n_q', 512), ('n_kc', 128), ('n_vc', 128), ('n_ks', 128),
                  ('n_vs', 128), ('n_kw', 128), ('n_vw', 128), ('n_gate', 24)):
    _OFF[_name] = (_o, _w)
    _o += _w
UM_W, UG_W, NQ_W, NCMP_W, NSW_W, NGATE_W = 1152, 896, 512, 256, 512, 128


def _cparams(sem):
    return pltpu.CompilerParams(dimension_semantics=sem, vmem_limit_bytes=VMEM_LIMIT)


def _dot(a, b):
    return jnp.dot(a, b, preferred_element_type=F32)


def _dot_nt(a, b):
    return lax.dot_general(a, b, (((1,), (1,)), ((), ())), preferred_element_type=F32)


def _dot_hi(a, b):
    return jnp.dot(a, b, preferred_element_type=F32, precision=lax.Precision.HIGHEST)


def _dot_split(a, b_bf):
    hi = a.astype(BF16)
    lo = (a - hi.astype(F32)).astype(BF16)
    return _dot(hi, b_bf) + _dot(lo, b_bf)


def _log_sigmoid(x):
    return jnp.minimum(x, 0.0) - jnp.log(1.0 + jnp.exp(-jnp.abs(x)))


def _sigmoid(x):
    return 1.0 / (1.0 + jnp.exp(-x))


def _transpose(x):
    r, c = x.shape
    if r < LANES:
        x = jnp.concatenate([x, jnp.zeros((LANES - r, c), x.dtype)], axis=0)
    parts = [x[:, i * LANES:(i + 1) * LANES].T[:, :r] for i in range(c // LANES)]
    return parts[0] if len(parts) == 1 else jnp.concatenate(parts, axis=0)


def _transpose2d(x):
    r, c = x.shape
    rp, cp = -(-r // LANES) * LANES, -(-c // LANES) * LANES
    if cp > c:
        x = jnp.concatenate([x, jnp.zeros((r, cp - c), x.dtype)], axis=1)
    if rp > r:
        x = jnp.concatenate([x, jnp.zeros((rp - r, cp), x.dtype)], axis=0)
    out_rows = []
    for j in range(cp // LANES):
        blocks = [x[i * LANES:(i + 1) * LANES, j * LANES:(j + 1) * LANES].T for i in range(rp // LANES)]
        out_rows.append(blocks[0] if len(blocks) == 1 else jnp.concatenate(blocks, axis=1))
    out = out_rows[0] if len(out_rows) == 1 else jnp.concatenate(out_rows, axis=0)
    return out[:c, :r]


def _iota(shape, dim):
    return lax.broadcasted_iota(jnp.int32, shape, dim)


def _pre_kernel(x_ref, g_ref, w_ref, um_ref, ug_ref, nq_ref, ncmp_ref, nsw_ref, ngate_ref):
    x = x_ref[...]
    h = x * lax.rsqrt(jnp.mean(x * x, axis=-1, keepdims=True) + EPS) * g_ref[...]
    u = _dot(h.astype(BF16), w_ref[...])
    o = 0
    for ref, w in ((um_ref, UM_W), (ug_ref, UG_W), (nq_ref, NQ_W), (ncmp_ref, NCMP_W),
                   (nsw_ref, NSW_W), (ngate_ref, NGATE_W)):
        ref[...] = u[:, o:o + w]
        o += w


def _pre(x2, g, w_cat):
    m = x2.shape[0]
    tm = min(m, 256)
    widths = (UM_W, UG_W, NQ_W, NCMP_W, NSW_W, NGATE_W)
    return pl.pallas_call(
        _pre_kernel,
        grid=(m // tm,),
        in_specs=[pl.BlockSpec((tm, D_MODEL), lambda i: (i, 0)),
                  pl.BlockSpec((1, D_MODEL), lambda i: (0, 0)),
                  pl.BlockSpec((D_MODEL, sum(widths)), lambda i: (0, 0))],
        out_specs=[pl.BlockSpec((tm, w), lambda i: (i, 0)) for w in widths],
        out_shape=[jax.ShapeDtypeStruct((m, w), F32) for w in widths],
        compiler_params=_cparams(("parallel",)),
        name="pre_proj",
    )(x2, g, w_cat)


def _post_kernel(x_ref, hm_ref, hg_ref, hn_ref, wout_ref, g_ref, wup_ref, wdn_ref, o_ref, h2_ref, *, tf):
    @pl.when(pl.program_id(1) == 0)
    def _():
        xn = x_ref[...]
        xn = xn + _dot(hm_ref[...], wout_ref[0:256, :])
        xn = xn + _dot(hg_ref[...], wout_ref[256:512, :])
        xn = xn + _dot(hn_ref[...], wout_ref[512:1024, :])
        o_ref[...] = xn
        h2 = xn * lax.rsqrt(jnp.mean(xn * xn, axis=-1, keepdims=True) + EPS) * g_ref[...]
        h2_ref[...] = h2.astype(BF16)

    au = _dot(h2_ref[...], wup_ref[0])
    a = au[:, :tf]
    act = (a * _sigmoid(a) * au[:, tf:]).astype(BF16)
    o_ref[...] += _dot(act, wdn_ref[0])


def _post(x2, hm, hg, hn, w_out, g, w_up_r, w_dn_r):
    m = x2.shape[0]
    tm = min(m, 512)
    nf, _, tf2 = w_up_r.shape
    tf = tf2 // 2
    return pl.pallas_call(
        functools.partial(_post_kernel, tf=tf),
        grid=(m // tm, nf),
        in_specs=[pl.BlockSpec((tm, D_MODEL), lambda i, f: (i, 0)),
                  pl.BlockSpec((tm, M_WIDTH), lambda i, f: (i, 0)),
                  pl.BlockSpec((tm, G_WIDTH), lambda i, f: (i, 0)),
                  pl.BlockSpec((tm, N_WIDTH), lambda i, f: (i, 0)),
                  pl.BlockSpec((D_MODEL, D_MODEL), lambda i, f: (0, 0)),
                  pl.BlockSpec((1, D_MODEL), lambda i, f: (0, 0)),
                  pl.BlockSpec((1, D_MODEL, tf2), lambda i, f: (f, 0, 0)),
                  pl.BlockSpec((1, tf, D_MODEL), lambda i, f: (f, 0, 0))],
        out_specs=pl.BlockSpec((tm, D_MODEL), lambda i, f: (i, 0)),
        out_shape=jax.ShapeDtypeStruct((m, D_MODEL), F32),
        scratch_shapes=[pltpu.VMEM((tm, D_MODEL), BF16)],
        compiler_params=_cparams(("parallel", "arbitrary")),
        name="post_ffn",
    )(x2, hm, hg, hn, w_out, g, w_up_r, w_dn_r)


CHUNK_UNROLL = 4


def _mlstm_kernel(um_ref, conv0_ref, ct0_ref, n0_ref, m0_ref, wconv_ref, bif_ref, gh_ref,
                  hm_ref, conv_out_ref, ct_out_ref, n_out_ref, m_out_ref,
                  xpad, q_s, k_s, ct_s, n_s, m_s, *, L, n_valid, blk):
    j = pl.program_id(1)

    @pl.when(j == 0)
    def _():
        xpad[0:8, :] = conv0_ref[0]
        ct_s[...] = ct0_ref[0]
        n_s[...] = n0_ref[0]
        m_s[...] = m0_ref[0]

    qk_pre = um_ref[0, :, 0:2 * M_WIDTH]
    xpad[8:8 + blk, :] = qk_pre
    wc = wconv_ref[...]
    acc = (xpad[5:5 + blk, :] * wc[0:1] + xpad[6:6 + blk, :] * wc[1:2]
           + xpad[7:7 + blk, :] * wc[2:3] + qk_pre * wc[3:4])
    qk = acc * _sigmoid(acc)
    q_s[...] = qk[:, :M_WIDTH]
    k_s[...] = qk[:, M_WIDTH:] * (M_HD ** -0.5)
    last = n_valid if blk == L else blk
    conv_out_ref[0] = xpad[last:last + 8, :]
    xpad[0:8, :] = xpad[blk:blk + 8, :]

    row = _iota((L, L), 0)
    col = _iota((L, L), 1)
    causal = row >= col
    tril = causal.astype(F32)
    lane_w = _iota((1, M_WIDTH), 1) // M_HD
    row_w = _iota((M_WIDTH, 1), 0) // M_HD
    valid_col = _iota((L, 1), 0) < n_valid

    row8 = _iota((8, LANES), 0)

    n_chunks = blk // L
    unroll = CHUNK_UNROLL if n_chunks % CHUNK_UNROLL == 0 else 1
    H = range(M_HEADS)
    U = range(unroll)
    hmask = [lane_w == h for h in H]
    lst = slice(n_valid - 1, n_valid)

    def group(i, state):
        ct, n_row, m_tile = state
        r0 = [pl.multiple_of((i * unroll + u) * L, L) for u in U]
        qc = [q_s[pl.ds(r, L), :] for r in r0]
        kc = [k_s[pl.ds(r, L), :] for r in r0]
        act = [um_ref[0, pl.ds(r, L), 1024:1152] + bif_ref[...] for r in r0]
        bcum = [_dot_hi(tril, _log_sigmoid(x)) for x in act]
        act_t = [_transpose(x) for x in act]
        bcum_t = [_transpose(x) for x in bcum]
        k_bf = [x.astype(BF16) for x in kc]
        kt_bf = [_transpose(x).astype(BF16) for x in kc]
        bcol = [[bcum[u][:, 4 + h:5 + h] for h in H] for u in U]
        dmat = [[jnp.where(causal, bcol[u][h] - bcum_t[u][4 + h:5 + h, :] + act_t[u][h:h + 1, :], NEG) for h in H]
                for u in U]
        m_loc = [[jnp.max(dmat[u][h], axis=1, keepdims=True) for h in H] for u in U]
        q_h = [[jnp.where(hmask[h], qc[u], 0.0) for h in H] for u in U]
        q_bf = [[q_h[u][h].astype(BF16) for h in H] for u in U]
        s = [[_dot_nt(q_bf[u][h], k_bf[u]) * jnp.exp(dmat[u][h] - m_loc[u][h]) for h in H] for u in U]
        v_h = [[um_ref[0, pl.ds(r, L), 512 + h * M_HD:512 + (h + 1) * M_HD] for h in H] for r in r0]
        sv = [[_dot(s[u][h].astype(BF16), v_h[u][h].astype(BF16)) for h in H] for u in U]
        ssum = [[jnp.sum(s[u][h], axis=1, keepdims=True) for h in H] for u in U]
        w_l = [[jnp.where(valid_col, jnp.exp(bcol[u][h][lst] - bcol[u][h] + act[u][:, h:h + 1] - m_loc[u][h][lst]), 0.0)
                for h in H] for u in U]
        upd = [[_dot(kt_bf[u], (v_h[u][h] * w_l[u][h]).astype(BF16)) for h in H] for u in U]
        ksum = [[jnp.sum(kc[u] * w_l[u][h], axis=0, keepdims=True) for h in H] for u in U]
        gate = [[_sigmoid(um_ref[0, pl.ds(r, L), 768 + h * M_HD:768 + (h + 1) * M_HD]) for h in H] for r in r0]
        for u in U:
            ct_bf = ct.astype(BF16)
            m_inter = [m_tile[h:h + 1, 0:1] + bcol[u][h] for h in H]
            m_new = [jnp.maximum(m_inter[h], m_loc[u][h]) for h in H]
            f = [jnp.exp(m_loc[u][h] - m_new[h]) for h in H]
            a_inter = [jnp.exp(m_inter[h] - m_new[h]) for h in H]
            num = [a_inter[h] * _dot(q_bf[u][h], ct_bf) + f[h] * sv[u][h] for h in H]
            qn = [jnp.sum(q_h[u][h] * n_row, axis=1, keepdims=True) for h in H]
            den = [a_inter[h] * qn[h] + f[h] * ssum[u][h] for h in H]
            hh = [gate[u][h] * (num[h] / jnp.maximum(jnp.abs(den[h]), jnp.exp(-m_new[h]))) for h in H]
            ms = [jnp.mean(x * x, axis=1, keepdims=True) for x in hh]
            outs = [hh[h] * lax.rsqrt(ms[h] + EPS) * gh_ref[:, h * M_HD:(h + 1) * M_HD] for h in H]
            hm_ref[0, pl.ds(r0[u], L), :] = jnp.concatenate(outs, axis=1).astype(BF16)
            ct_old, n_old = ct, n_row
            for h in H:
                ct = jnp.where(row_w == h, a_inter[h][lst] * ct_old + f[h][lst] * upd[u][h], ct)
                n_row = jnp.where(hmask[h], a_inter[h][lst] * n_old + f[h][lst] * ksum[u][h], n_row)
                m_tile = jnp.where(row8 == h, m_new[h][lst], m_tile)
        return ct, n_row, m_tile

    ct, n_row, m_tile = lax.fori_loop(0, n_chunks // unroll, group, (ct_s[...], n_s[0:1, :], m_s[...]))
    ct_s[...] = ct
    n_s[...] = jnp.broadcast_to(n_row, n_s.shape)
    m_s[...] = m_tile

    @pl.when(j == pl.num_programs(1) - 1)
    def _():
        ct_out_ref[0] = ct
        n_out_ref[0] = jnp.broadcast_to(n_row, n_s.shape)
        m_out_ref[0] = m_tile


def _mlstm(um, conv0, ct0, n0, m0, wconv, bif, gh, *, L, n_valid):
    b, t, _ = um.shape
    blk = min(t, 512)
    kern = functools.partial(_mlstm_kernel, L=L, n_valid=n_valid, blk=blk)
    per_b = lambda shape: pl.BlockSpec((1,) + shape, lambda i, j: (i,) + (0,) * len(shape))
    const = lambda shape: pl.BlockSpec(shape, lambda i, j: (0,) * len(shape))
    return pl.pallas_call(
        kern,
        grid=(b, t // blk),
        in_specs=[pl.BlockSpec((1, blk, UM_W), lambda i, j: (i, j, 0)),
                  per_b((8, 2 * M_WIDTH)), per_b((M_WIDTH, M_HD)), per_b((8, M_WIDTH)), per_b((8, LANES)),
                  const((8, 2 * M_WIDTH)), const((1, LANES)), const((1, M_WIDTH))],
        out_specs=[pl.BlockSpec((1, blk, M_WIDTH), lambda i, j: (i, j, 0)),
                   per_b((8, 2 * M_WIDTH)), per_b((M_WIDTH, M_HD)), per_b((8, M_WIDTH)), per_b((8, LANES))],
        out_shape=[jax.ShapeDtypeStruct((b, t, M_WIDTH), BF16),
                   jax.ShapeDtypeStruct((b, 8, 2 * M_WIDTH), F32),
                   jax.ShapeDtypeStruct((b, M_WIDTH, M_HD), F32),
                   jax.ShapeDtypeStruct((b, 8, M_WIDTH), F32),
                   jax.ShapeDtypeStruct((b, 8, LANES), F32)],
        scratch_shapes=[pltpu.VMEM((blk + 8, 2 * M_WIDTH), F32),
                        pltpu.VMEM((blk, M_WIDTH), F32), pltpu.VMEM((blk, M_WIDTH), F32),
                        pltpu.VMEM((M_WIDTH, M_HD), F32), pltpu.VMEM((8, M_WIDTH), F32),
                        pltpu.VMEM((8, LANES), F32)],
        compiler_params=_cparams(("parallel", "arbitrary")),
        name="mlstm",
    )(um, conv0, ct0, n0, m0, wconv, bif, gh)


def _gla_kernel(ug_ref, s0_ref, w2_ref, bg_ref, gh_ref, hg_ref, s_out_ref, s_s, *, L, n_valid, blk):
    j = pl.program_id(1)

    @pl.when(j == 0)
    def _():
        s_s[...] = s0_ref[0]

    row = _iota((L, L), 0)
    col = _iota((L, L), 1)
    causal = row >= col
    tril = causal.astype(F32)
    lane_k = _iota((1, LANES), 1) // G_DK
    row_k = _iota((LANES, 1), 0) // G_DK
    valid_col = _iota((L, 1), 0) < n_valid
    mid = max(n_valid // 2, 1)

    n_chunks = blk // L
    unroll = CHUNK_UNROLL if n_chunks % CHUNK_UNROLL == 0 else 1
    H = range(G_HEADS)
    U = range(unroll)
    hmask = [lane_k == h for h in H]

    def group(i, s_all):
        r0 = [pl.multiple_of((i * unroll + u) * L, L) for u in U]
        q = [ug_ref[0, pl.ds(r, L), 0:128] * (G_DK ** -0.5) for r in r0]
        k = [ug_ref[0, pl.ds(r, L), 128:256] for r in r0]
        z = [_dot_hi(ug_ref[0, pl.ds(r, L), 768:896], w2_ref[...]) + bg_ref[...] for r in r0]
        g = [_log_sigmoid(x) * (1.0 / G_TAU) for x in z]
        bc = [_dot_hi(tril, x) for x in g]
        c_ref = [x[mid - 1:mid] for x in bc]
        last = [x[n_valid - 1:n_valid] for x in bc]
        qe = [q[u] * jnp.exp(bc[u] - c_ref[u]) for u in U]
        ke = [(k[u] * jnp.exp(c_ref[u] - bc[u])).astype(BF16) for u in U]
        qin = [q[u] * jnp.exp(bc[u]) for u in U]
        kd_t = [_transpose(jnp.where(valid_col, k[u] * jnp.exp(last[u] - bc[u]), 0.0)).astype(BF16) for u in U]
        decay = [jnp.exp(_transpose(jnp.broadcast_to(x, (8, LANES)))[:, 0:1]) for x in last]
        v_h = [[ug_ref[0, pl.ds(r, L), 256 + h * G_DV:256 + (h + 1) * G_DV].astype(BF16) for h in H] for r in r0]
        a = [[jnp.where(causal, _dot_nt(jnp.where(hmask[h], qe[u], 0.0).astype(BF16), ke[u]), 0.0) for h in H]
             for u in U]
        intra = [[_dot(a[u][h].astype(BF16), v_h[u][h]) for h in H] for u in U]
        q_in = [[jnp.where(hmask[h], qin[u], 0.0).astype(BF16) for h in H] for u in U]
        upd = [[_dot(kd_t[u], v_h[u][h]) for h in H] for u in U]
        gate = [[ug_ref[0, pl.ds(r, L), 512 + h * G_DV:512 + (h + 1) * G_DV] for h in H] for r in r0]
        gate = [[x * _sigmoid(x) * gh_ref[:, h * G_DV:(h + 1) * G_DV] for h, x in enumerate(gs)] for gs in gate]
        for u in U:
            s_bf = s_all.astype(BF16)
            o = [_dot(q_in[u][h], s_bf) + intra[u][h] for h in H]
            ms = [jnp.mean(x * x, axis=1, keepdims=True) for x in o]
            outs = [o[h] * lax.rsqrt(ms[h] + EPS) * gate[u][h] for h in H]
            hg_ref[0, pl.ds(r0[u], L), :] = jnp.concatenate(outs, axis=1).astype(BF16)
            s_all = decay[u] * s_all
            for h in H:
                s_all = s_all + jnp.where(row_k == h, upd[u][h], 0.0)
        return s_all

    s_fin = lax.fori_loop(0, n_chunks // unroll, group, s_s[...])
    s_s[...] = s_fin

    @pl.when(j == pl.num_programs(1) - 1)
    def _():
        s_out_ref[0] = s_fin


def _gla(ug, s0, w2p, bg, gh, *, L, n_valid):
    b, t, _ = ug.shape
    blk = min(t, 512)
    kern = functools.partial(_gla_kernel, L=L, n_valid=n_valid, blk=blk)
    return pl.pallas_call(
        kern,
        grid=(b, t // blk),
        in_specs=[pl.BlockSpec((1, blk, UG_W), lambda i, j: (i, j, 0)),
                  pl.BlockSpec((1, LANES, G_DV), lambda i, j: (i, 0, 0)),
                  pl.BlockSpec((LANES, LANES), lambda i, j: (0, 0)),
                  pl.BlockSpec((1, LANES), lambda i, j: (0, 0)),
                  pl.BlockSpec((1, G_WIDTH), lambda i, j: (0, 0))],
        out_specs=[pl.BlockSpec((1, blk, G_WIDTH), lambda i, j: (i, j, 0)),
                   pl.BlockSpec((1, LANES, G_DV), lambda i, j: (i, 0, 0))],
        out_shape=[jax.ShapeDtypeStruct((b, t, G_WIDTH), BF16),
                   jax.ShapeDtypeStruct((b, LANES, G_DV), F32)],
        scratch_shapes=[pltpu.VMEM((LANES, G_DV), F32)],
        compiler_params=_cparams(("parallel", "arbitrary")),
        name="gla",
    )(ug, s0, w2p, bg, gh)


def _group_mean_sq(x, bm_bf):
    return _dot_split(x * x, bm_bf)


def _rope_slab(y, cos, sin_signed):
    lane = _iota(y.shape, 1)
    rot = jnp.where((lane % N_HD) < (N_HD // 2), pltpu.roll(y, 96, axis=1), pltpu.roll(y, 32, axis=1))
    return y * cos + rot * sin_signed


def _norm_rope_slab(x, g, cos, sin_signed, bm_bf):
    y = x * lax.rsqrt(_group_mean_sq(x, bm_bf) + EPS) * g
    return _rope_slab(y, cos, sin_signed)


def _rows_kernel(nq_ref, nsw_ref, ngate_ref, cos_ref, sin_ref, gq_ref, gks_ref, gkw_ref, bm_ref,
                 q_ref, ks_ref, vs_ref, kw_ref, vw_ref, slc_ref, win_ref, gates_ref):
    cos = cos_ref[...]
    sin = sin_ref[...]
    bm = bm_ref[...]
    low = _iota((1, LANES), 1) < N_HD
    for sl in range(4):
        x = nq_ref[0, :, sl * LANES:(sl + 1) * LANES]
        y = _norm_rope_slab(x, gq_ref[:, sl * LANES:(sl + 1) * LANES], cos, sin, bm) * (N_HD ** -0.5 * LOG2E)
        y_sw = pltpu.roll(y, N_HD, axis=1)
        if sl < 2:
            even, odd = jnp.where(low, y, 0.0), jnp.where(low, y_sw, 0.0)
        else:
            even, odd = jnp.where(low, 0.0, y_sw), jnp.where(low, 0.0, y)
        q_ref[0, 2 * sl] = even.astype(BF16)
        q_ref[0, 2 * sl + 1] = odd.astype(BF16)
    ks = _norm_rope_slab(nsw_ref[0, :, 0:128], gks_ref[...], cos, sin, bm)
    vs = nsw_ref[0, :, 128:256]
    kw = _norm_rope_slab(nsw_ref[0, :, 256:384], gkw_ref[...], cos, sin, bm)
    vw = nsw_ref[0, :, 384:512]
    slc_ref[0, :, 0:128] = ks
    slc_ref[0, :, 128:256] = vs
    win_ref[0, :, 0:128] = kw
    win_ref[0, :, 128:256] = vw
    for ref, val in ((ks_ref, ks), (vs_ref, vs), (kw_ref, kw), (vw_ref, vw)):
        ref[0] = val.astype(BF16)
    gt = _sigmoid(ngate_ref[0])
    gates_ref[0, 0] = gt
    gates_ref[0, 1] = pltpu.roll(gt, LANES - 3 * N_REP, axis=1)


def _rows(nq, nsw, ngate, cos, sin, gq, gks, gkw, bm):
    b, t, _ = nq.shape
    tm = min(t, 512)
    tok = lambda w: pl.BlockSpec((1, tm, w), lambda i, j: (i, j, 0))
    const = lambda shape: pl.BlockSpec(shape, lambda i, j: (0,) * len(shape))
    kv_shape = jax.ShapeDtypeStruct((b, t, LANES), BF16)
    return pl.pallas_call(
        _rows_kernel,
        grid=(b, t // tm),
        in_specs=[tok(NQ_W), tok(NSW_W), tok(NGATE_W),
                  pl.BlockSpec((tm, LANES), lambda i, j: (j, 0)),
                  pl.BlockSpec((tm, LANES), lambda i, j: (j, 0)),
                  const((1, NQ_W)), const((1, LANES)), const((1, LANES)), const((LANES, LANES))],
        out_specs=[pl.BlockSpec((1, N_HEADS, tm, LANES), lambda i, j: (i, 0, j, 0)),
                   tok(LANES), tok(LANES), tok(LANES), tok(LANES), tok(256), tok(256),
                   pl.BlockSpec((1, N_KV, tm, LANES), lambda i, j: (i, 0, j, 0))],
        out_shape=[jax.ShapeDtypeStruct((b, N_HEADS, t, LANES), BF16), kv_shape, kv_shape, kv_shape, kv_shape,
                   jax.ShapeDtypeStruct((b, t, 256), F32), jax.ShapeDtypeStruct((b, t, 256), F32),
                   jax.ShapeDtypeStruct((b, N_KV, t, LANES), F32)],
        compiler_params=_cparams(("parallel", "parallel")),
        name="nsa_rows",
    )(nq, nsw, ngate, cos, sin, gq, gks, gkw, bm)


def _compress_kernel(u_ref, w1_ref, pe_ref, w2_ref, gkc_ref, cos_ref, sin_ref, bm_ref, kc_ref, vc_ref):
    nu = u_ref.shape[1]
    outs = []
    for kv in range(2):
        x = jnp.concatenate(
            [u_ref[0, :, j * 256 + kv * LANES:j * 256 + (kv + 1) * LANES] for j in range(CMP_STRIDE)],
            axis=1).astype(BF16)
        w1 = w1_ref[kv]
        p = _dot(x, w1)
        bias = _dot(pe_ref[kv].astype(BF16), w1)
        pa = p[:, :256] + bias[0:1, :256]
        pb = p[:, 256:] + bias[1:2, 256:]
        pre = pa + pltpu.roll(pb, nu - 1, axis=0)
        hid = (pre * _sigmoid(pre)).astype(BF16)
        outs.append(_dot(hid, w2_ref[kv]))
    kc_ref[0] = _norm_rope_slab(outs[0], gkc_ref[...], cos_ref[...], sin_ref[...], bm_ref[...]).astype(BF16)
    vc_ref[0] = outs[1].astype(BF16)


def _compress(units, w1r, pe_r, w2r, gkc, cos_c, sin_c, bm):
    b, nu, _ = units.shape
    const = lambda shape: pl.BlockSpec(shape, lambda i: (0,) * len(shape))
    out = jax.ShapeDtypeStruct((b, nu, LANES), BF16)
    return pl.pallas_call(
        _compress_kernel,
        grid=(b,),
        in_specs=[pl.BlockSpec((1, nu, 4096), lambda i: (i, 0, 0)),
                  const((2, 2048, 512)), const((2, 8, 2048)), const((2, 256, LANES)),
                  const((1, LANES)), const((nu, LANES)), const((nu, LANES)), const((LANES, LANES))],
        out_specs=[pl.BlockSpec((1, nu, LANES), lambda i: (i, 0, 0))] * 2,
        out_shape=[out, out],
        compiler_params=_cparams(("parallel",)),
        name="nsa_compress",
    )(units, w1r, pe_r, w2r, gkc, cos_c, sin_c, bm)


def _cmp_kernel(q_ref, kc_ref, vc_ref, c2s_ref, oc_ref, sel_ref, *, tq, pos0, n_cmp, nsel):
    qi = pl.program_id(1)
    nc_pad = kc_ref.shape[1]
    q = q_ref[0].reshape(N_HEADS * tq, LANES)
    s = _dot_nt(q, kc_ref[0]).reshape(N_HEADS, tq, nc_pad)
    qpos = pos0 + qi * tq + _iota((tq, 1), 0)
    c_idx = _iota((tq, nc_pad), 1)
    c_ok = jnp.where(c_idx < n_cmp, c_idx * CMP_STRIDE + (CMP_LEN - 1), jnp.int32(2 ** 30)) <= qpos
    s = s + jnp.where(c_ok, 0.0, NEG)[None]
    p = jnp.exp2(s - jnp.max(s, axis=-1, keepdims=True))
    p = p * ((qpos >= CMP_LEN - 1).astype(F32)[None] / jnp.sum(p, axis=-1, keepdims=True))
    oc_ref[0] = _dot(p.reshape(N_HEADS * tq, nc_pad).astype(BF16), vc_ref[0]).reshape(N_HEADS, tq, LANES)
    psum = jnp.sum(p.reshape(N_KV, N_REP, tq, nc_pad), axis=1).reshape(N_KV * tq, nc_pad)
    imp = _dot_split(psum, c2s_ref[...])
    work = _transpose2d(imp)
    ncol = N_KV * tq
    qpos_row = pos0 + qi * tq + _iota((1, ncol), 1) % tq
    blk = _iota((nsel, ncol), 0)
    cur = qpos_row // SLC_BLOCK
    forced = jnp.where(blk == 0, 1.0, jnp.where(blk == cur, 1.0, jnp.where(blk == cur - 1, 1.0, 0.0)))
    work = jnp.where(blk * SLC_BLOCK <= qpos_row, work + FORCE_BONUS * forced, NEG)
    blk_f = blk.astype(F32)
    sel = jnp.zeros((nsel, ncol), F32)
    for _ in range(SLC_TOPK):
        m = jnp.max(work, axis=0, keepdims=True)
        idx = jnp.min(jnp.where(work == m, blk_f, float(nsel)), axis=0, keepdims=True)
        hit = blk_f == idx
        sel = jnp.where(hit, 1.0, sel)
        work = jnp.where(hit, -jnp.inf, work)
    sel_ref[0] = _transpose2d(sel).reshape(N_KV, tq, nsel).astype(BF16)


def _cmp_attn(q_pad, kc, vc, c2s, *, tq, pos0, n_cmp):
    b, _, t, _ = q_pad.shape
    nc_pad = kc.shape[1]
    nsel = c2s.shape[1]
    kern = functools.partial(_cmp_kernel, tq=tq, pos0=pos0, n_cmp=n_cmp, nsel=nsel)
    return pl.pallas_call(
        kern,
        grid=(b, t // tq),
        in_specs=[pl.BlockSpec((1, N_HEADS, tq, LANES), lambda i, j: (i, 0, j, 0)),
                  pl.BlockSpec((1, nc_pad, LANES), lambda i, j: (i, 0, 0)),
                  pl.BlockSpec((1, nc_pad, LANES), lambda i, j: (i, 0, 0)),
                  pl.BlockSpec((nc_pad, nsel), lambda i, j: (0, 0))],
        out_specs=[pl.BlockSpec((1, N_HEADS, tq, LANES), lambda i, j: (i, 0, j, 0)),
                   pl.BlockSpec((1, N_KV, tq, nsel), lambda i, j: (i, 0, j, 0))],
        out_shape=[jax.ShapeDtypeStruct((b, N_HEADS, t, LANES), F32),
                   jax.ShapeDtypeStruct((b, N_KV, t, nsel), BF16)],
        compiler_params=_cparams(("parallel", "parallel")),
        name="nsa_cmp_topk",
    )(q_pad, kc, vc, c2s)


SLAB = 64


def _attend_tile(q_parts, k, v, bias_scr, s_scr, p_scr, m_scr, l_scr, alpha_scr, acc_scr, *, tq, width):
    nch = width // LANES

    def chunk(rows, i, c):
        x = s_scr[rows, c * LANES:(c + 1) * LANES]
        if bias_scr is not None:
            x = x + bias_scr[i * SLAB:(i + 1) * SLAB, c * LANES:(c + 1) * LANES]
        return x

    slabs = [(slice(r * tq + i * SLAB, r * tq + (i + 1) * SLAB), i) for r in range(N_REP) for i in range(tq // SLAB)]
    for r in range(N_REP):
        s_scr[r * tq:(r + 1) * tq, 0:width] = _dot_nt(q_parts[r], k)
    for rows, i in slabs:
        mx = chunk(rows, i, 0)
        for c in range(1, nch):
            mx = jnp.maximum(mx, chunk(rows, i, c))
        m_old = m_scr[rows, :]
        m_new = jnp.maximum(m_old, jnp.max(mx, axis=1, keepdims=True))
        alpha_scr[rows, :] = jnp.exp2(m_old - m_new)
        m_scr[rows, :] = m_new
    for rows, i in slabs:
        m_new = m_scr[rows, :]
        lsum = alpha_scr[rows, :] * l_scr[rows, :]
        for c in range(nch):
            p = jnp.exp2(chunk(rows, i, c) - m_new)
            lsum = lsum + p
            p_scr[rows, c * LANES:(c + 1) * LANES] = p.astype(BF16)
        l_scr[rows, :] = lsum
    for r in range(N_REP):
        rs = slice(r * tq, (r + 1) * tq)
        acc_scr[rs, :] = alpha_scr[rs, :] * acc_scr[rs, :] + _dot(p_scr[rs, 0:width], v)


def _slcwin_kernel(q_ref, ks_ref, vs_ref, kw_ref, vw_ref, sel_ref, et_ref, oc_ref, gates_ref, hn_ref,
                   s_scr, bias_scr, p_scr, m_scr, l_scr, alpha_scr, acc_scr, ow_scr, *, tq):
    g = pl.program_id(1)
    qi = pl.program_id(2)
    rows = N_REP * tq
    q0 = qi * tq
    qpos = q0 + _iota((tq, 1), 0)
    scr = dict(s_scr=s_scr, p_scr=p_scr, m_scr=m_scr, l_scr=l_scr, alpha_scr=alpha_scr, acc_scr=acc_scr, tq=tq)

    def reset():
        m_scr[...] = jnp.full((rows, LANES), NEG, F32)
        l_scr[...] = jnp.zeros((rows, LANES), F32)
        acc_scr[...] = jnp.zeros((rows, LANES), F32)

    def result():
        return acc_scr[...] / jnp.sum(l_scr[...], axis=1, keepdims=True)

    wlen = WINDOW + tq
    start = pl.multiple_of(jnp.maximum(q0 - WINDOW, 0), tq)
    reset()
    dist = qpos - (start + _iota((tq, wlen), 1))
    bias_scr[:, 0:wlen] = jnp.where(jnp.where(dist >= 0, dist, WINDOW) < WINDOW, 0.0, NEG)
    _attend_tile([q_ref[0, r] for r in range(N_REP)], kw_ref[0, pl.ds(start, wlen), :],
                 vw_ref[0, pl.ds(start, wlen), :], bias_scr, width=wlen, **scr)
    ow_scr[...] = result()

    reset()
    sel_m = ((sel_ref[0, 0].astype(F32) - 1.0) * (-NEG)).astype(BF16)
    q_aug = [jnp.concatenate([q_ref[0, r], sel_m], axis=1) for r in range(N_REP)]
    n_kv = (q0 + tq + KV_TILE - 1) // KV_TILE

    def tile_inputs(j):
        k0 = pl.multiple_of(j * KV_TILE, KV_TILE)
        k_aug = jnp.concatenate([ks_ref[0, pl.ds(k0, KV_TILE), :], et_ref[pl.ds(k0, KV_TILE), :]], axis=1)
        return k0, k_aug, vs_ref[0, pl.ds(k0, KV_TILE), :]

    def body(j, carry):
        _, k_aug, v = tile_inputs(j)
        _attend_tile(q_aug, k_aug, v, None, width=KV_TILE, **scr)
        return carry

    lax.fori_loop(0, n_kv - 1, body, 0)
    k0, k_aug, v = tile_inputs(n_kv - 1)
    bias_scr[:, 0:KV_TILE] = jnp.where(k0 + _iota((tq, KV_TILE), 1) <= qpos, 0.0, NEG)
    _attend_tile(q_aug, k_aug, v, bias_scr, width=KV_TILE, **scr)
    o_s_all = result()

    gt = gates_ref[0, 0]
    low = _iota((1, LANES), 1) < N_HD
    vals = []
    for r in range(N_REP):
        rs = slice(r * tq, (r + 1) * tq)
        vals.append(gt[:, 3 * r:3 * r + 1] * oc_ref[0, r] + gt[:, 3 * r + 1:3 * r + 2] * o_s_all[rs]
                    + gt[:, 3 * r + 2:3 * r + 3] * ow_scr[rs, :])
    for pair in range(N_REP // 2):
        a, b = vals[2 * pair], vals[2 * pair + 1]
        a_sw, b_sw = pltpu.roll(a, N_HD, axis=1), pltpu.roll(b, N_HD, axis=1)
        lo = jnp.where(g == 0, a, a_sw)
        hi = jnp.where(g == 0, b_sw, b)
        hn_ref[0, :, pair * LANES:(pair + 1) * LANES] = jnp.where(low, lo, hi).astype(BF16)


def _slcwin(q_pad, ks, vs, kw, vw, sel, e_t, oc, gates, *, tq):
    b, _, t, _ = q_pad.shape
    nsel = sel.shape[3]
    assert nsel == LANES and e_t.shape == (t, LANES)
    rows = N_REP * tq
    wlen = WINDOW + tq
    kern = functools.partial(_slcwin_kernel, tq=tq)
    qspec = pl.BlockSpec((1, N_REP, tq, LANES), lambda i, g, j: (i, g, j, 0))
    full = pl.BlockSpec((1, t, LANES), lambda i, g, j: (i, 0, 0))
    return pl.pallas_call(
        kern,
        grid=(b, N_KV, t // tq),
        in_specs=[qspec, full, full, full, full,
                  pl.BlockSpec((1, 1, tq, nsel), lambda i, g, j: (i, g, j, 0)),
                  pl.BlockSpec((t, LANES), lambda i, g, j: (0, 0)),
                  qspec,
                  pl.BlockSpec((1, 1, tq, LANES), lambda i, g, j: (i, g, j, 0))],
        out_specs=pl.BlockSpec((1, tq, N_REP * N_HD), lambda i, g, j: (i, j, g)),
        out_shape=jax.ShapeDtypeStruct((b, t, N_WIDTH), BF16),
        scratch_shapes=[pltpu.VMEM((rows, wlen), F32), pltpu.VMEM((tq, wlen), F32), pltpu.VMEM((rows, wlen), BF16),
                        pltpu.VMEM((rows, LANES), F32), pltpu.VMEM((rows, LANES), F32), pltpu.VMEM((rows, LANES), F32),
                        pltpu.VMEM((rows, LANES), F32), pltpu.VMEM((rows, LANES), F32)],
        compiler_params=_cparams(("parallel", "parallel", "parallel")),
        name="nsa_slc_win",
    )(q_pad, ks, vs, kw, vw, sel, e_t, oc, gates)


GATHER_PAGES = 16


PAGE_UNITS = PAGE_SIZE // CMP_STRIDE


def _gather_cmp_kernel(pt_ref, *refs):
    out_ref = refs[-1]
    for i in range(GATHER_PAGES):
        for j in range(2 * CMP_STRIDE):
            out_ref[0, i * PAGE_UNITS:(i + 1) * PAGE_UNITS, j * LANES:(j + 1) * LANES] = \
                refs[i][0, 0, pl.ds(j, PAGE_UNITS, stride=2 * CMP_STRIDE), :]


def _gather_cmp(page_table, cache_rows, layer):
    db, n_pages = page_table.shape
    uw = CMP_STRIDE * 2 * LANES

    def page_spec(i):
        return pl.BlockSpec((1, 1, 2 * PAGE_SIZE, LANES),
                            lambda b, c, pt: (layer, pt[b, c * GATHER_PAGES + i], 0, 0))

    return pl.pallas_call(
        _gather_cmp_kernel,
        grid_spec=pltpu.PrefetchScalarGridSpec(
            num_scalar_prefetch=1,
            grid=(db, n_pages // GATHER_PAGES),
            in_specs=[page_spec(i) for i in range(GATHER_PAGES)],
            out_specs=pl.BlockSpec((1, GATHER_PAGES * PAGE_UNITS, uw), lambda b, c, pt: (b, c, 0))),
        out_shape=jax.ShapeDtypeStruct((db, n_pages * PAGE_UNITS, uw), F32),
        compiler_params=_cparams(("parallel", "parallel")),
        name="gather_cmp_pages",
    )(page_table, *([cache_rows] * GATHER_PAGES))


SLC_PAGES = 32


def _heads_bias(b2, ts):
    w = b2.shape[1]
    return jnp.broadcast_to(b2.reshape(N_KV, 1, ts, w), (N_KV, N_REP, ts, w)).reshape(N_HEADS * ts, w)


def _slc_sample_kernel(pt_ref, *refs, ts):
    pages = refs[:SLC_PAGES]
    (q_ref, sel_ref, e_ref, enew_ref, newslc_ref, win_ref, newwin_ref, oc_ref, gates_ref,
     hn_ref, m_scr, l_scr, acc_scr) = refs[SLC_PAGES:]
    c = pl.program_id(1)
    rows = N_HEADS * ts
    q = q_ref[0].reshape(rows, LANES)
    sel2 = sel_ref[0].reshape(N_KV * ts, sel_ref.shape[3])

    @pl.when(c == 0)
    def _():
        m_scr[...] = jnp.full((rows, 1), NEG, F32)
        l_scr[...] = jnp.zeros((rows, 1), F32)
        acc_scr[...] = jnp.zeros((rows, LANES), F32)

    def online(s, pv_fn):
        m_old = m_scr[...]
        m_new = jnp.maximum(m_old, jnp.max(s, axis=1, keepdims=True))
        p = jnp.exp2(s - m_new)
        alpha = jnp.exp2(m_old - m_new)
        l_scr[...] = alpha * l_scr[...] + jnp.sum(p, axis=1, keepdims=True)
        acc_scr[...] = alpha * acc_scr[...] + pv_fn(p.astype(BF16))
        m_scr[...] = m_new

    picked = _dot(sel2, e_ref[...])
    bias = _heads_bias(picked * (-NEG) + NEG, ts)
    kvs = [pg[0, 0].astype(BF16) for pg in pages]
    s = jnp.concatenate([_dot_nt(q, kv[:, :LANES]) for kv in kvs], axis=1) + bias

    def pv_pages(p):
        out = _dot(p[:, 0:PAGE_SIZE], kvs[0][:, LANES:])
        for i in range(1, SLC_PAGES):
            out = out + _dot(p[:, i * PAGE_SIZE:(i + 1) * PAGE_SIZE], kvs[i][:, LANES:])
        return out

    online(s, pv_pages)

    @pl.when(c == pl.num_programs(1) - 1)
    def _():
        tok = _iota((N_KV * ts, 1), 0) % ts
        key = _iota((N_KV * ts, PAGE_SIZE), 1)
        new = newslc_ref[0].astype(BF16)
        picked_n = _dot(sel2, enew_ref[...])
        bias_n = _heads_bias(jnp.where(key <= tok, picked_n, 0.0) * (-NEG) + NEG, ts)
        online(_dot_nt(q, new[:, :LANES]) + bias_n, lambda p: _dot(p, new[:, LANES:]))
        win = win_ref[0, 0].astype(BF16)
        nwin = newwin_ref[0].astype(BF16)
        wkey = _iota((N_KV * ts, WINDOW), 1)
        bias_w = jnp.concatenate([jnp.where(wkey > tok, 0.0, NEG), jnp.where(key <= tok, 0.0, NEG)], axis=1)
        s_w = jnp.concatenate([_dot_nt(q, win[:, :LANES]), _dot_nt(q, nwin[:, :LANES])], axis=1) + _heads_bias(bias_w, ts)
        p_w = jnp.exp2(s_w - jnp.max(s_w, axis=1, keepdims=True))
        l_w = jnp.sum(p_w, axis=1, keepdims=True)
        p_w = p_w.astype(BF16)
        o_w = (_dot(p_w[:, :WINDOW], win[:, LANES:]) + _dot(p_w[:, WINDOW:], nwin[:, LANES:])) / l_w
        o_s = acc_scr[...] / l_scr[...]
        low = _iota((1, LANES), 1) < N_HD
        vals = []
        for h in range(N_HEADS):
            g, r = divmod(h, N_REP)
            gt = gates_ref[0, g]
            rs = slice(h * ts, (h + 1) * ts)
            vals.append(gt[:, 3 * r:3 * r + 1] * oc_ref[0, h] + gt[:, 3 * r + 1:3 * r + 2] * o_s[rs]
                        + gt[:, 3 * r + 2:3 * r + 3] * o_w[rs])
        for pair in range(N_HEADS // 2):
            a, b = vals[2 * pair], vals[2 * pair + 1]
            if pair < N_REP // 2:
                lo, hi = a, pltpu.roll(b, N_HD, axis=1)
            else:
                lo, hi = pltpu.roll(a, N_HD, axis=1), b
            hn_ref[0, :, pair * LANES:(pair + 1) * LANES] = jnp.where(low, lo, hi).astype(BF16)


def _slc_sample(page_table, cache_flat, layer, q_pad, sel, e_main, e_new, new_slc, win_flat, new_win, oc, gates):
    db, n_pages = page_table.shape
    ts = q_pad.shape[2]
    nsel = sel.shape[3]
    n_steps = n_pages // SLC_PAGES
    step_keys = SLC_PAGES * PAGE_SIZE
    rows = N_HEADS * ts

    def page_spec(i):
        return pl.BlockSpec((1, 1, PAGE_SIZE, 256), lambda b, c, pt: (layer, pt[b, c * SLC_PAGES + i], 0, 0))

    per_b = lambda shape: pl.BlockSpec((1,) + shape, lambda b, c, pt: (b,) + (0,) * len(shape))
    return pl.pallas_call(
        functools.partial(_slc_sample_kernel, ts=ts),
        grid_spec=pltpu.PrefetchScalarGridSpec(
            num_scalar_prefetch=1,
            grid=(db, n_steps),
            in_specs=[page_spec(i) for i in range(SLC_PAGES)] + [
                per_b((N_HEADS, ts, LANES)), per_b((N_KV, ts, nsel)),
                pl.BlockSpec((nsel, step_keys), lambda b, c, pt: (0, c)),
                pl.BlockSpec((nsel, PAGE_SIZE), lambda b, c, pt: (0, 0)),
                per_b((PAGE_SIZE, 256)),
                pl.BlockSpec((1, 1, WINDOW, 256), lambda b, c, pt: (layer, b, 0, 0)),
                per_b((PAGE_SIZE, 256)), per_b((N_HEADS, ts, LANES)), per_b((N_KV, ts, LANES))],
            out_specs=per_b((ts, N_WIDTH)),
            scratch_shapes=[pltpu.VMEM((rows, 1), F32), pltpu.VMEM((rows, 1), F32), pltpu.VMEM((rows, LANES), F32)]),
        out_shape=jax.ShapeDtypeStruct((db, ts, N_WIDTH), BF16),
        compiler_params=_cparams(("parallel", "arbitrary")),
        name="nsa_slc_sample",
    )(page_table, *([cache_flat] * SLC_PAGES), q_pad, sel, e_main, e_new, new_slc, win_flat, new_win, oc, gates)


def _rope_tables(pos):
    half = N_HD // 2
    inv = ROPE_THETA ** (-jnp.arange(half, dtype=F32) / half)
    ang = pos.astype(F32)[:, None] * inv[None, :]
    cos, sin = jnp.cos(ang), jnp.sin(ang)
    return jnp.tile(jnp.concatenate([cos, cos], axis=1), (1, 2)), jnp.tile(jnp.concatenate([-sin, sin], axis=1), (1, 2))


def _pad_cols(a, w):
    return jnp.pad(a, ((0, 0), (0, w - a.shape[1])))


def _prep_layer(l, w_norm_mix, w_in, b_mlstm_if, w_mlstm_conv, w_mlstm_hnorm, w_gla_gate2, b_gla_gate,
                w_gla_hnorm, w_qk_norm, w_cmp_pe, w_cmp_1, w_cmp_2, w_out, w_norm_ffn, w_ffn_up, w_ffn_down):
    wi = w_in[l]
    col = lambda n: wi[:, _OFF[n][0]:_OFF[n][0] + _OFF[n][1]]
    um = _pad_cols(jnp.concatenate([col('m_qk'), col('m_v'), col('m_o'), col('m_i'), col('m_f')], axis=1), UM_W)
    ug = _pad_cols(jnp.concatenate([col('g_q'), col('g_k'), col('g_v'), col('g_o'), col('g_lr')], axis=1), UG_W)
    ncmp = jnp.concatenate([col('n_kc'), col('n_vc')], axis=1)
    nsw = jnp.concatenate([col('n_ks'), col('n_vs'), col('n_kw'), col('n_vw')], axis=1)
    ngate = _pad_cols(col('n_gate'), NGATE_W)
    p = {}
    p['w_cat'] = jnp.concatenate([um, ug, col('n_q'), ncmp, nsw, ngate], axis=1).astype(BF16)
    p['g_mix'] = w_norm_mix[l][None, :]
    p['wconv'] = jnp.pad(w_mlstm_conv[l], ((0, 8 - M_CONV), (0, 0)))
    p['bif'] = _pad_cols(b_mlstm_if[l][None, :], LANES)
    p['gh_m'] = w_mlstm_hnorm[l][None, :]
    p['w2p'] = jnp.pad(w_gla_gate2[l], ((0, LANES - G_RANK), (0, 0)))
    p['bg'] = b_gla_gate[l][None, :]
    p['gh_g'] = w_gla_hnorm[l][None, :]
    gqk = w_qk_norm[l]
    p['gq'] = jnp.tile(gqk[0], N_HEADS)[None, :]
    p['gkc'] = jnp.tile(gqk[1], N_KV)[None, :]
    p['gks'] = jnp.tile(gqk[2], N_KV)[None, :]
    p['gkw'] = jnp.tile(gqk[3], N_KV)[None, :]
    w1 = w_cmp_1[l].reshape(2, 2, CMP_STRIDE, N_HD, CMP_HIDDEN)
    eye = jnp.eye(N_KV, dtype=F32)
    w1r = jnp.einsum('khjdc,gf->kjgdhfc', w1, eye).reshape(2, CMP_STRIDE * N_KV * N_HD, 2 * N_KV * CMP_HIDDEN)
    p['w1r'] = w1r.astype(BF16)
    pe = w_cmp_pe[l].reshape(2, 2, CMP_STRIDE, 1, N_HD)
    pe = jnp.broadcast_to(pe, (2, 2, CMP_STRIDE, N_KV, N_HD)).reshape(2, 2, 2048)
    p['pe_r'] = jnp.pad(pe, ((0, 0), (0, 6), (0, 0)))
    p['w2r'] = jnp.einsum('kcd,gf->kgcfd', w_cmp_2[l], eye).reshape(2, N_KV * CMP_HIDDEN, N_KV * N_HD).astype(BF16)
    p['w_out'] = w_out[l].astype(BF16)
    p['g_ffn'] = w_norm_ffn[l][None, :]
    nf = 2
    tf = D_FF // nf
    wu = w_ffn_up[l]
    p['w_up_r'] = jnp.stack([jnp.concatenate([wu[:, f * tf:(f + 1) * tf], wu[:, D_FF + f * tf:D_FF + (f + 1) * tf]],
                                             axis=1) for f in range(nf)]).astype(BF16)
    p['w_dn_r'] = w_ffn_down[l].reshape(nf, tf, D_MODEL).astype(BF16)
    return p


def _cmp2slc(n_cmp, nc_pad, nsel):
    m = np.zeros((nc_pad, nsel), np.float32)
    per = SLC_BLOCK // CMP_STRIDE
    for n in range(n_cmp):
        for u in range(CMP_LEN // CMP_STRIDE):
            m[n, (n + u) // per] += 1.0
    return jnp.asarray(m, BF16)


def _expand_mat(nsel, kvlen):
    return jnp.asarray((np.arange(kvlen)[None, :] // SLC_BLOCK) == np.arange(nsel)[:, None], BF16)


def _group_mean_mat():
    idx = np.arange(LANES) // N_HD
    return jnp.asarray((idx[:, None] == idx[None, :]) / float(N_HD), BF16)


def _mixers(p, um, ug, mstate, gstate, *, L, n_valid):
    conv0, ct0, n0, m0 = mstate
    hm, conv_o, ct_o, n_o, m_o = _mlstm(um, conv0, ct0, n0, m0, p['wconv'], p['bif'], p['gh_m'], L=L, n_valid=n_valid)
    hg, s_o = _gla(ug, gstate, p['w2p'], p['bg'], p['gh_g'], L=L, n_valid=n_valid)
    return hm, hg, (conv_o, ct_o, n_o, m_o), s_o


def _mlstm_state_in(conv, c, n, m):
    b = conv.shape[0]
    conv0 = jnp.pad(conv.astype(F32), ((0, 0), (8 - (M_CONV - 1), 0), (0, 0)))
    ct0 = jnp.swapaxes(c.astype(F32), -1, -2).reshape(b, M_WIDTH, M_HD)
    n0 = jnp.broadcast_to(n.astype(F32).reshape(b, 1, M_WIDTH), (b, 8, M_WIDTH))
    m0 = jnp.broadcast_to(jnp.pad(m.astype(F32), ((0, 0), (0, 8 - M_HEADS)))[:, :, None], (b, 8, LANES))
    return conv0, ct0, n0, m0


def _mlstm_state_out(conv_o, ct_o, n_o, m_o):
    b = conv_o.shape[0]
    c = jnp.swapaxes(ct_o.reshape(b, M_HEADS, M_HD, M_HD), -1, -2)
    return c, n_o[:, 0].reshape(b, M_HEADS, M_HD), m_o[:, :M_HEADS, 0], conv_o[:, 8 - (M_CONV - 1):]


def kernel(x_prompt, x_sample, cache_cmp_kv, cache_slc_kv, state_win_kv, state_mlstm_C, state_mlstm_n,
           state_mlstm_m, state_mlstm_conv, state_gla_S, page_table, w_norm_mix, w_in, b_mlstm_if,
           w_mlstm_conv, w_mlstm_hnorm, w_gla_gate2, b_gla_gate, w_gla_hnorm, w_qk_norm, w_cmp_pe,
           w_cmp_1, w_cmp_2, w_out, w_norm_ffn, w_ffn_up, w_ffn_down):
    b, t, _ = x_prompt.shape
    db, td, _ = x_sample.shape
    depth = w_in.shape[0]
    n_pages = page_table.shape[1]
    past = n_pages * PAGE_SIZE
    win_buf = state_win_kv.shape[2]
    assert t % 512 == 0 and td <= SAMPLE_T and n_pages % SLC_PAGES == 0 and win_buf == WINDOW

    bm = _group_mean_mat()
    cos_p, sin_p = _rope_tables(jnp.arange(t))
    nu_p = t // CMP_STRIDE
    ncmp_p = (t - CMP_LEN) // CMP_STRIDE + 1
    cos_cp, sin_cp = _rope_tables(jnp.arange(nu_p) * CMP_STRIDE + CMP_LEN - 1)
    nsel_p = -(-t // SLC_BLOCK)
    nsel_p = -(-nsel_p // LANES) * LANES
    c2s_p = _cmp2slc(ncmp_p, nu_p, nsel_p)
    e_p = _expand_mat(nsel_p, t).T
    tq_p = 256
    cos_s, sin_s = _rope_tables(past + jnp.arange(SAMPLE_T))
    nu_s = past // CMP_STRIDE
    ncmp_s = (past + td - CMP_LEN) // CMP_STRIDE + 1
    cos_cs, sin_cs = _rope_tables(jnp.arange(nu_s) * CMP_STRIDE + CMP_LEN - 1)
    nsel_s = -(-(-(-(past + td) // SLC_BLOCK)) // LANES) * LANES
    c2s_s = _cmp2slc(ncmp_s, nu_s, nsel_s)
    e_s = _expand_mat(nsel_s, past + PAGE_SIZE)
    e_s_main, e_s_new = e_s[:, :past], e_s[:, past:]
    cmp_flat = cache_cmp_kv.reshape(depth, cache_cmp_kv.shape[1], 2 * PAGE_SIZE, LANES)
    slc_flat = cache_slc_kv.reshape(depth, cache_slc_kv.shape[1], PAGE_SIZE, 256).astype(BF16)
    win_flat = state_win_kv.reshape(depth, db, win_buf, 256)

    xp = x_prompt.reshape(b * t, D_MODEL)
    xs = jnp.pad(x_sample, ((0, 0), (0, SAMPLE_T - td), (0, 0))).reshape(db * SAMPLE_T, D_MODEL)

    zero_m = _mlstm_state_in(jnp.zeros((b, M_CONV - 1, 2 * M_WIDTH), F32), jnp.zeros((b, M_HEADS, M_HD, M_HD), F32),
                             jnp.zeros((b, M_HEADS, M_HD), F32), jnp.zeros((b, M_HEADS), F32))
    zero_g = jnp.zeros((b, LANES, G_DV), F32)

    pl_out = [[] for _ in range(8)]
    sl_out = [[] for _ in range(8)]
    for l in range(depth):
        p = _prep_layer(l, w_norm_mix, w_in, b_mlstm_if, w_mlstm_conv, w_mlstm_hnorm, w_gla_gate2, b_gla_gate,
                        w_gla_hnorm, w_qk_norm, w_cmp_pe, w_cmp_1, w_cmp_2, w_out, w_norm_ffn, w_ffn_up, w_ffn_down)
        um, ug, nq, ncmp, nsw, ngate = _pre(xp, p['g_mix'], p['w_cat'])
        r3 = lambda a, bb, tt: a.reshape(bb, tt, a.shape[-1])
        hm, hg, mst, gst = _mixers(p, r3(um, b, t), r3(ug, b, t), zero_m, zero_g, L=M_CHUNK, n_valid=M_CHUNK)
        q_hm, ks, vs, kw, vw, slc_f, win_f, gates = _rows(r3(nq, b, t), r3(nsw, b, t), r3(ngate, b, t), cos_p, sin_p,
                                                          p['gq'], p['gks'], p['gkw'], bm)
        kc, vc = _compress(ncmp.reshape(b, nu_p, 4096), p['w1r'], p['pe_r'], p['w2r'], p['gkc'], cos_cp, sin_cp, bm)
        oc, sel = _cmp_attn(q_hm, kc, vc, c2s_p, tq=tq_p, pos0=0, n_cmp=ncmp_p)
        hn = _slcwin(q_hm, ks, vs, kw, vw, sel, e_p, oc, gates, tq=tq_p)
        xp = _post(xp, hm.reshape(b * t, -1), hg.reshape(b * t, -1), hn.reshape(b * t, -1),
                   p['w_out'], p['g_ffn'], p['w_up_r'], p['w_dn_r'])
        c_o, n_o, m_o, conv_o = _mlstm_state_out(*mst)
        kv6 = lambda a: a.reshape(a.shape[0], a.shape[1], 2, N_KV, N_HD)
        for lst, val in zip(pl_out, (kv6(ncmp.reshape(b, t, 256)), kv6(slc_f), kv6(win_f[:, t - min(WINDOW, t):]),
                                     c_o, n_o, m_o, conv_o, gst.reshape(b, G_HEADS, G_DK, G_DV))):
            lst.append(val)

        um, ug, nq, ncmp, nsw, ngate = _pre(xs, p['g_mix'], p['w_cat'])
        mstate = _mlstm_state_in(state_mlstm_conv[l], state_mlstm_C[l], state_mlstm_n[l], state_mlstm_m[l])
        gstate = state_gla_S[l].astype(F32).reshape(db, LANES, G_DV)
        hm, hg, mst, gst = _mixers(p, r3(um, db, SAMPLE_T), r3(ug, db, SAMPLE_T), mstate, gstate,
                                   L=SAMPLE_T, n_valid=td)
        q_hm, ks_n, vs_n, kw_n, vw_n, slc_f, win_f, gates = _rows(
            r3(nq, db, SAMPLE_T), r3(nsw, db, SAMPLE_T), r3(ngate, db, SAMPLE_T), cos_s, sin_s,
            p['gq'], p['gks'], p['gkw'], bm)
        units = _gather_cmp(page_table, cmp_flat, l)
        kc, vc = _compress(units, p['w1r'], p['pe_r'], p['w2r'], p['gkc'], cos_cs, sin_cs, bm)
        oc, sel = _cmp_attn(q_hm, kc, vc, c2s_s, tq=SAMPLE_T, pos0=past, n_cmp=ncmp_s)
        pad_page = lambda a: jnp.pad(a, ((0, 0), (0, PAGE_SIZE - a.shape[1]), (0, 0)))
        hn = _slc_sample(page_table, slc_flat, l, q_hm, sel, e_s_main, e_s_new, pad_page(slc_f), win_flat,
                         pad_page(win_f), oc, gates)
        win_all = jnp.concatenate([win_flat[l].astype(F32), win_f[:, :td]], axis=1)
        xs = _post(xs, hm.reshape(db * SAMPLE_T, -1), hg.reshape(db * SAMPLE_T, -1), hn.reshape(db * SAMPLE_T, -1),
                   p['w_out'], p['g_ffn'], p['w_up_r'], p['w_dn_r'])
        c_o, n_o, m_o, conv_o = _mlstm_state_out(*mst)
        for lst, val in zip(sl_out, (kv6(ncmp.reshape(db, SAMPLE_T, 256)[:, :td]), kv6(slc_f[:, :td]),
                                     kv6(win_all[:, win_all.shape[1] - win_buf:]),
                                     c_o, n_o, m_o, conv_o, gst.reshape(db, G_HEADS, G_DK, G_DV))):
            lst.append(val)

    outs_p = [jnp.stack(a) for a in pl_out]
    outs_s = [jnp.stack(a) for a in sl_out]
    y_p = xp.reshape(b, t, D_MODEL)
    y_s = xs.reshape(db, SAMPLE_T, D_MODEL)[:, :td]
    return (y_p, y_s, *outs_p, *outs_s)
```

```python
import functools
import math

import numpy as np
import jax
import jax.numpy as jnp
from jax import lax
from jax.experimental import pallas as pl
from jax.experimental.pallas import tpu as pltpu

F32 = jnp.float32
BF16 = jnp.bfloat16

D_MODEL = 1024
M_HEADS, M_HD, M_WIDTH, M_CONV, M_CHUNK = 4, 64, 256, 4, 64
G_HEADS, G_DK, G_DV, G_WIDTH, G_RANK, G_TAU, G_CHUNK = 4, 32, 64, 256, 16, 16.0, 64
N_HEADS, N_HD, N_KV, N_REP, N_WIDTH = 8, 64, 2, 4, 512
CMP_LEN, CMP_STRIDE, CMP_HIDDEN = 32, 16, 128
SLC_BLOCK, SLC_TOPK, WINDOW = 64, 16, 512
ROPE_THETA = 10000.0
D_FF = 2816
PAGE_SIZE = 128
NEG = -1e30
FORCE_BONUS = 1e4
EPS = 1e-6
LOG2E = 1.4426950408889634

LANES = 128
VMEM_LIMIT = 56 * 1024 * 1024
KV_TILE = 512
SAMPLE_T = 16

_OFF = {}
_o = 0
for _name, _w in (('m_qk', 512), ('m_v', 256), ('m_i', 4), ('m_f', 4), ('m_o', 256),
                  ('g_q', 128), ('g_k', 128), ('g_v', 256), ('g_lr', 16), ('g_o', 256),
                  ('n_q', 512), ('n_kc', 128), ('n_vc', 128), ('n_ks', 128),
                  ('n_vs', 128), ('n_kw', 128), ('n_vw', 128), ('n_gate', 24)):
    _OFF[_name] = (_o, _w)
    _o += _w
UM_W, UG_W, NQ_W, NCMP_W, NSW_W, NGATE_W = 1152, 896, 512, 256, 512, 128


def _cparams(sem):
    return pltpu.CompilerParams(dimension_semantics=sem, vmem_limit_bytes=VMEM_LIMIT)


def _dot(a, b):
    return jnp.dot(a, b, preferred_element_type=F32)


def _dot_nt(a, b):
    return lax.dot_general(a, b, (((1,), (1,)), ((), ())), preferred_element_type=F32)


def _dot_hi(a, b):
    return jnp.dot(a, b, preferred_element_type=F32, precision=lax.Precision.HIGHEST)


def _dot_split(a, b_bf):
    hi = a.astype(BF16)
    lo = (a - hi.astype(F32)).astype(BF16)
    return _dot(hi, b_bf) + _dot(lo, b_bf)


def _log_sigmoid(x):
    return jnp.minimum(x, 0.0) - jnp.log(1.0 + jnp.exp(-jnp.abs(x)))


def _sigmoid(x):
    return 1.0 / (1.0 + jnp.exp(-x))


def _transpose(x):
    r, c = x.shape
    if r < LANES:
        x = jnp.concatenate([x, jnp.zeros((LANES - r, c), x.dtype)], axis=0)
    parts = [x[:, i * LANES:(i + 1) * LANES].T[:, :r] for i in range(c // LANES)]
    return parts[0] if len(parts) == 1 else jnp.concatenate(parts, axis=0)


def _transpose2d(x):
    r, c = x.shape
    rp, cp = -(-r // LANES) * LANES, -(-c // LANES) * LANES
    if cp > c:
        x = jnp.concatenate([x, jnp.zeros((r, cp - c), x.dtype)], axis=1)
    if rp > r:
        x = jnp.concatenate([x, jnp.zeros((rp - r, cp), x.dtype)], axis=0)
    out_rows = []
    for j in range(cp // LANES):
        blocks = [x[i * LANES:(i + 1) * LANES, j * LANES:(j + 1) * LANES].T for i in range(rp // LANES)]
        out_rows.append(blocks[0] if len(blocks) == 1 else jnp.concatenate(blocks, axis=1))
    out = out_rows[0] if len(out_rows) == 1 else jnp.concatenate(out_rows, axis=0)
    return out[:c, :r]


def _iota(shape, dim):
    return lax.broadcasted_iota(jnp.int32, shape, dim)


def _pre_kernel(x_ref, g_ref, w_ref, um_ref, ug_ref, nq_ref, ncmp_ref, nsw_ref, ngate_ref):
    x = x_ref[...]
    h = x * lax.rsqrt(jnp.mean(x * x, axis=-1, keepdims=True) + EPS) * g_ref[...]
    u = _dot(h.astype(BF16), w_ref[...])
    o = 0
    for ref, w in ((um_ref, UM_W), (ug_ref, UG_W), (nq_ref, NQ_W), (ncmp_ref, NCMP_W),
                   (nsw_ref, NSW_W), (ngate_ref, NGATE_W)):
        ref[...] = u[:, o:o + w]
        o += w


def _pre(x2, g, w_cat):
    m = x2.shape[0]
    tm = min(m, 256)
    widths = (UM_W, UG_W, NQ_W, NCMP_W, NSW_W, NGATE_W)
    return pl.pallas_call(
        _pre_kernel,
        grid=(m // tm,),
        in_specs=[pl.BlockSpec((tm, D_MODEL), lambda i: (i, 0)),
                  pl.BlockSpec((1, D_MODEL), lambda i: (0, 0)),
                  pl.BlockSpec((D_MODEL, sum(widths)), lambda i: (0, 0))],
        out_specs=[pl.BlockSpec((tm, w), lambda i: (i, 0)) for w in widths],
        out_shape=[jax.ShapeDtypeStruct((m, w), F32) for w in widths],
        compiler_params=_cparams(("parallel",)),
        name="pre_proj",
    )(x2, g, w_cat)


def _post_kernel(x_ref, hm_ref, hg_ref, hn_ref, wout_ref, g_ref, wup_ref, wdn_ref, o_ref, h2_ref, *, tf):
    @pl.when(pl.program_id(1) == 0)
    def _():
        xn = x_ref[...]
        xn = xn + _dot(hm_ref[...], wout_ref[0:256, :])
        xn = xn + _dot(hg_ref[...], wout_ref[256:512, :])
        xn = xn + _dot(hn_ref[...], wout_ref[512:1024, :])
        o_ref[...] = xn
        h2 = xn * lax.rsqrt(jnp.mean(xn * xn, axis=-1, keepdims=True) + EPS) * g_ref[...]
        h2_ref[...] = h2.astype(BF16)

    au = _dot(h2_ref[...], wup_ref[0])
    a = au[:, :tf]
    act = (a * _sigmoid(a) * au[:, tf:]).astype(BF16)
    o_ref[...] += _dot(act, wdn_ref[0])


def _post(x2, hm, hg, hn, w_out, g, w_up_r, w_dn_r):
    m = x2.shape[0]
    tm = min(m, 512)
    nf, _, tf2 = w_up_r.shape
    tf = tf2 // 2
    return pl.pallas_call(
        functools.partial(_post_kernel, tf=tf),
        grid=(m // tm, nf),
        in_specs=[pl.BlockSpec((tm, D_MODEL), lambda i, f: (i, 0)),
                  pl.BlockSpec((tm, M_WIDTH), lambda i, f: (i, 0)),
                  pl.BlockSpec((tm, G_WIDTH), lambda i, f: (i, 0)),
                  pl.BlockSpec((tm, N_WIDTH), lambda i, f: (i, 0)),
                  pl.BlockSpec((D_MODEL, D_MODEL), lambda i, f: (0, 0)),
                  pl.BlockSpec((1, D_MODEL), lambda i, f: (0, 0)),
                  pl.BlockSpec((1, D_MODEL, tf2), lambda i, f: (f, 0, 0)),
                  pl.BlockSpec((1, tf, D_MODEL), lambda i, f: (f, 0, 0))],
        out_specs=pl.BlockSpec((tm, D_MODEL), lambda i, f: (i, 0)),
        out_shape=jax.ShapeDtypeStruct((m, D_MODEL), F32),
        scratch_shapes=[pltpu.VMEM((tm, D_MODEL), BF16)],
        compiler_params=_cparams(("parallel", "arbitrary")),
        name="post_ffn",
    )(x2, hm, hg, hn, w_out, g, w_up_r, w_dn_r)


CHUNK_UNROLL = 4


def _mlstm_kernel(um_ref, conv0_ref, ct0_ref, n0_ref, m0_ref, wconv_ref, bif_ref, gh_ref,
                  hm_ref, conv_out_ref, ct_out_ref, n_out_ref, m_out_ref,
                  xpad, q_s, k_s, ct_s, n_s, m_s, *, L, n_valid, blk):
    j = pl.program_id(1)

    @pl.when(j == 0)
    def _():
        xpad[0:8, :] = conv0_ref[0]
        ct_s[...] = ct0_ref[0]
        n_s[...] = n0_ref[0]
        m_s[...] = m0_ref[0]

    qk_pre = um_ref[0, :, 0:2 * M_WIDTH]
    xpad[8:8 + blk, :] = qk_pre
    wc = wconv_ref[...]
    acc = (xpad[5:5 + blk, :] * wc[0:1] + xpad[6:6 + blk, :] * wc[1:2]
           + xpad[7:7 + blk, :] * wc[2:3] + qk_pre * wc[3:4])
    qk = acc * _sigmoid(acc)
    q_s[...] = qk[:, :M_WIDTH]
    k_s[...] = qk[:, M_WIDTH:] * (M_HD ** -0.5)
    last = n_valid if blk == L else blk
    conv_out_ref[0] = xpad[last:last + 8, :]
    xpad[0:8, :] = xpad[blk:blk + 8, :]

    row = _iota((L, L), 0)
    col = _iota((L, L), 1)
    causal = row >= col
    tril = causal.astype(F32)
    lane_w = _iota((1, M_WIDTH), 1) // M_HD
    row_w = _iota((M_WIDTH, 1), 0) // M_HD
    valid_col = _iota((L, 1), 0) < n_valid

    row8 = _iota((8, LANES), 0)

    n_chunks = blk // L
    unroll = CHUNK_UNROLL if n_chunks % CHUNK_UNROLL == 0 else 1
    H = range(M_HEADS)
    U = range(unroll)
    hmask = [lane_w == h for h in H]
    lst = slice(n_valid - 1, n_valid)

    def group(i, state):
        ct, n_row, m_tile = state
        r0 = [pl.multiple_of((i * unroll + u) * L, L) for u in U]
        qc = [q_s[pl.ds(r, L), :] for r in r0]
        kc = [k_s[pl.ds(r, L), :] for r in r0]
        act = [um_ref[0, pl.ds(r, L), 1024:1152] + bif_ref[...] for r in r0]
        bcum = [_dot_hi(tril, _log_sigmoid(x)) for x in act]
        act_t = [_transpose(x) for x in act]
        bcum_t = [_transpose(x) for x in bcum]
        k_bf = [x.astype(BF16) for x in kc]
        kt_bf = [_transpose(x).astype(BF16) for x in kc]
        bcol = [[bcum[u][:, 4 + h:5 + h] for h in H] for u in U]
        dmat = [[jnp.where(causal, bcol[u][h] - bcum_t[u][4 + h:5 + h, :] + act_t[u][h:h + 1, :], NEG) for h in H]
                for u in U]
        m_loc = [[jnp.max(dmat[u][h], axis=1, keepdims=True) for h in H] for u in U]
        q_h = [[jnp.where(hmask[h], qc[u], 0.0) for h in H] for u in U]
        q_bf = [[q_h[u][h].astype(BF16) for h in H] for u in U]
        s = [[_dot_nt(q_bf[u][h], k_bf[u]) * jnp.exp(dmat[u][h] - m_loc[u][h]) for h in H] for u in U]
        v_h = [[um_ref[0, pl.ds(r, L), 512 + h * M_HD:512 + (h + 1) * M_HD] for h in H] for r in r0]
        sv = [[_dot(s[u][h].astype(BF16), v_h[u][h].astype(BF16)) for h in H] for u in U]
        ssum = [[jnp.sum(s[u][h], axis=1, keepdims=True) for h in H] for u in U]
        w_l = [[jnp.where(valid_col, jnp.exp(bcol[u][h][lst] - bcol[u][h] + act[u][:, h:h + 1] - m_loc[u][h][lst]), 0.0)
                for h in H] for u in U]
        upd = [[_dot(kt_bf[u], (v_h[u][h] * w_l[u][h]).astype(BF16)) for h in H] for u in U]
        ksum = [[jnp.sum(kc[u] * w_l[u][h], axis=0, keepdims=True) for h in H] for u in U]
        gate = [[_sigmoid(um_ref[0, pl.ds(r, L), 768 + h * M_HD:768 + (h + 1) * M_HD]) for h in H] for r in r0]
        for u in U:
            ct_bf = ct.astype(BF16)
            m_inter = [m_tile[h:h + 1, 0:1] + bcol[u][h] for h in H]
            m_new = [jnp.maximum(m_inter[h], m_loc[u][h]) for h in H]
            f = [jnp.exp(m_loc[u][h] - m_new[h]) for h in H]
            a_inter = [jnp.exp(m_inter[h] - m_new[h]) for h in H]
            num = [a_inter[h] * _dot(q_bf[u][h], ct_bf) + f[h] * sv[u][h] for h in H]
            qn = [jnp.sum(q_h[u][h] * n_row, axis=1, keepdims=True) for h in H]
            den = [a_inter[h] * qn[h] + f[h] * ssum[u][h] for h in H]
            hh = [gate[u][h] * (num[h] / jnp.maximum(jnp.abs(den[h]), jnp.exp(-m_new[h]))) for h in H]
            ms = [jnp.mean(x * x, axis=1, keepdims=True) for x in hh]
            outs = [hh[h] * lax.rsqrt(ms[h] + EPS) * gh_ref[:, h * M_HD:(h + 1) * M_HD] for h in H]
            hm_ref[0, pl.ds(r0[u], L), :] = jnp.concatenate(outs, axis=1).astype(BF16)
            ct_old, n_old = ct, n_row
            for h in H:
                ct = jnp.where(row_w == h, a_inter[h][lst] * ct_old + f[h][lst] * upd[u][h], ct)
                n_row = jnp.where(hmask[h], a_inter[h][lst] * n_old + f[h][lst] * ksum[u][h], n_row)
                m_tile = jnp.where(row8 == h, m_new[h][lst], m_tile)
        return ct, n_row, m_tile

    ct, n_row, m_tile = lax.fori_loop(0, n_chunks // unroll, group, (ct_s[...], n_s[0:1, :], m_s[...]))
    ct_s[...] = ct
    n_s[...] = jnp.broadcast_to(n_row, n_s.shape)
    m_s[...] = m_tile

    @pl.when(j == pl.num_programs(1) - 1)
    def _():
        ct_out_ref[0] = ct
        n_out_ref[0] = jnp.broadcast_to(n_row, n_s.shape)
        m_out_ref[0] = m_tile


def _mlstm(um, conv0, ct0, n0, m0, wconv, bif, gh, *, L, n_valid):
    b, t, _ = um.shape
    blk = min(t, 512)
    kern = functools.partial(_mlstm_kernel, L=L, n_valid=n_valid, blk=blk)
    per_b = lambda shape: pl.BlockSpec((1,) + shape, lambda i, j: (i,) + (0,) * len(shape))
    const = lambda shape: pl.BlockSpec(shape, lambda i, j: (0,) * len(shape))
    return pl.pallas_call(
        kern,
        grid=(b, t // blk),
        in_specs=[pl.BlockSpec((1, blk, UM_W), lambda i, j: (i, j, 0)),
                  per_b((8, 2 * M_WIDTH)), per_b((M_WIDTH, M_HD)), per_b((8, M_WIDTH)), per_b((8, LANES)),
                  const((8, 2 * M_WIDTH)), const((1, LANES)), const((1, M_WIDTH))],
        out_specs=[pl.BlockSpec((1, blk, M_WIDTH), lambda i, j: (i, j, 0)),
                   per_b((8, 2 * M_WIDTH)), per_b((M_WIDTH, M_HD)), per_b((8, M_WIDTH)), per_b((8, LANES))],
        out_shape=[jax.ShapeDtypeStruct((b, t, M_WIDTH), BF16),
                   jax.ShapeDtypeStruct((b, 8, 2 * M_WIDTH), F32),
                   jax.ShapeDtypeStruct((b, M_WIDTH, M_HD), F32),
                   jax.ShapeDtypeStruct((b, 8, M_WIDTH), F32),
                   jax.ShapeDtypeStruct((b, 8, LANES), F32)],
        scratch_shapes=[pltpu.VMEM((blk + 8, 2 * M_WIDTH), F32),
                        pltpu.VMEM((blk, M_WIDTH), F32), pltpu.VMEM((blk, M_WIDTH), F32),
                        pltpu.VMEM((M_WIDTH, M_HD), F32), pltpu.VMEM((8, M_WIDTH), F32),
                        pltpu.VMEM((8, LANES), F32)],
        compiler_params=_cparams(("parallel", "arbitrary")),
        name="mlstm",
    )(um, conv0, ct0, n0, m0, wconv, bif, gh)


def _gla_kernel(ug_ref, s0_ref, w2_ref, bg_ref, gh_ref, hg_ref, s_out_ref, s_s, *, L, n_valid, blk):
    j = pl.program_id(1)

    @pl.when(j == 0)
    def _():
        s_s[...] = s0_ref[0]

    row = _iota((L, L), 0)
    col = _iota((L, L), 1)
    causal = row >= col
    tril = causal.astype(F32)
    lane_k = _iota((1, LANES), 1) // G_DK
    row_k = _iota((LANES, 1), 0) // G_DK
    valid_col = _iota((L, 1), 0) < n_valid
    mid = max(n_valid // 2, 1)

    n_chunks = blk // L
    unroll = CHUNK_UNROLL if n_chunks % CHUNK_UNROLL == 0 else 1
    H = range(G_HEADS)
    U = range(unroll)
    hmask = [lane_k == h for h in H]

    def group(i, s_all):
        r0 = [pl.multiple_of((i * unroll + u) * L, L) for u in U]
        q = [ug_ref[0, pl.ds(r, L), 0:128] * (G_DK ** -0.5) for r in r0]
        k = [ug_ref[0, pl.ds(r, L), 128:256] for r in r0]
        z = [_dot_hi(ug_ref[0, pl.ds(r, L), 768:896], w2_ref[...]) + bg_ref[...] for r in r0]
        g = [_log_sigmoid(x) * (1.0 / G_TAU) for x in z]
        bc = [_dot_hi(tril, x) for x in g]
        c_ref = [x[mid - 1:mid] for x in bc]
        last = [x[n_valid - 1:n_valid] for x in bc]
        qe = [q[u] * jnp.exp(bc[u] - c_ref[u]) for u in U]
        ke = [(k[u] * jnp.exp(c_ref[u] - bc[u])).astype(BF16) for u in U]
        qin = [q[u] * jnp.exp(bc[u]) for u in U]
        kd_t = [_transpose(jnp.where(valid_col, k[u] * jnp.exp(last[u] - bc[u]), 0.0)).astype(BF16) for u in U]
        decay = [jnp.exp(_transpose(jnp.broadcast_to(x, (8, LANES)))[:, 0:1]) for x in last]
        v_h = [[ug_ref[0, pl.ds(r, L), 256 + h * G_DV:256 + (h + 1) * G_DV].astype(BF16) for h in H] for r in r0]
        a = [[jnp.where(causal, _dot_nt(jnp.where(hmask[h], qe[u], 0.0).astype(BF16), ke[u]), 0.0) for h in H]
             for u in U]
        intra = [[_dot(a[u][h].astype(BF16), v_h[u][h]) for h in H] for u in U]
        q_in = [[jnp.where(hmask[h], qin[u], 0.0).astype(BF16) for h in H] for u in U]
        upd = [[_dot(kd_t[u], v_h[u][h]) for h in H] for u in U]
        gate = [[ug_ref[0, pl.ds(r, L), 512 + h * G_DV:512 + (h + 1) * G_DV] for h in H] for r in r0]
        gate = [[x * _sigmoid(x) * gh_ref[:, h * G_DV:(h + 1) * G_DV] for h, x in enumerate(gs)] for gs in gate]
        for u in U:
            s_bf = s_all.astype(BF16)
            o = [_dot(q_in[u][h], s_bf) + intra[u][h] for h in H]
            ms = [jnp.mean(x * x, axis=1, keepdims=True) for x in o]
            outs = [o[h] * lax.rsqrt(ms[h] + EPS) * gate[u][h] for h in H]
            hg_ref[0, pl.ds(r0[u], L), :] = jnp.concatenate(outs, axis=1).astype(BF16)
            s_all = decay[u] * s_all
            for h in H:
                s_all = s_all + jnp.where(row_k == h, upd[u][h], 0.0)
        return s_all

    s_fin = lax.fori_loop(0, n_chunks // unroll, group, s_s[...])
    s_s[...] = s_fin

    @pl.when(j == pl.num_programs(1) - 1)
    def _():
        s_out_ref[0] = s_fin


def _gla(ug, s0, w2p, bg, gh, *, L, n_valid):
    b, t, _ = ug.shape
    blk = min(t, 512)
    kern = functools.partial(_gla_kernel, L=L, n_valid=n_valid, blk=blk)
    return pl.pallas_call(
        kern,
        grid=(b, t // blk),
        in_specs=[pl.BlockSpec((1, blk, UG_W), lambda i, j: (i, j, 0)),
                  pl.BlockSpec((1, LANES, G_DV), lambda i, j: (i, 0, 0)),
                  pl.BlockSpec((LANES, LANES), lambda i, j: (0, 0)),
                  pl.BlockSpec((1, LANES), lambda i, j: (0, 0)),
                  pl.BlockSpec((1, G_WIDTH), lambda i, j: (0, 0))],
        out_specs=[pl.BlockSpec((1, blk, G_WIDTH), lambda i, j: (i, j, 0)),
                   pl.BlockSpec((1, LANES, G_DV), lambda i, j: (i, 0, 0))],
        out_shape=[jax.ShapeDtypeStruct((b, t, G_WIDTH), BF16),
                   jax.ShapeDtypeStruct((b, LANES, G_DV), F32)],
        scratch_shapes=[pltpu.VMEM((LANES, G_DV), F32)],
        compiler_params=_cparams(("parallel", "arbitrary")),
        name="gla",
    )(ug, s0, w2p, bg, gh)


def _group_mean_sq(x, bm_bf):
    return _dot_split(x * x, bm_bf)


def _rope_slab(y, cos, sin_signed):
    lane = _iota(y.shape, 1)
    rot = jnp.where((lane % N_HD) < (N_HD // 2), pltpu.roll(y, 96, axis=1), pltpu.roll(y, 32, axis=1))
    return y * cos + rot * sin_signed


def _norm_rope_slab(x, g, cos, sin_signed, bm_bf):
    y = x * lax.rsqrt(_group_mean_sq(x, bm_bf) + EPS) * g
    return _rope_slab(y, cos, sin_signed)


def _rows_kernel(nq_ref, nsw_ref, ngate_ref, cos_ref, sin_ref, gq_ref, gks_ref, gkw_ref, bm_ref,
                 q_ref, ks_ref, vs_ref, kw_ref, vw_ref, slc_ref, win_ref, gates_ref):
    cos = cos_ref[...]
    sin = sin_ref[...]
    bm = bm_ref[...]
    low = _iota((1, LANES), 1) < N_HD
    for sl in range(4):
        x = nq_ref[0, :, sl * LANES:(sl + 1) * LANES]
        y = _norm_rope_slab(x, gq_ref[:, sl * LANES:(sl + 1) * LANES], cos, sin, bm) * (N_HD ** -0.5 * LOG2E)
        y_sw = pltpu.roll(y, N_HD, axis=1)
        if sl < 2:
            even, odd = jnp.where(low, y, 0.0), jnp.where(low, y_sw, 0.0)
        else:
            even, odd = jnp.where(low, 0.0, y_sw), jnp.where(low, 0.0, y)
        q_ref[0, 2 * sl] = even.astype(BF16)
        q_ref[0, 2 * sl + 1] = odd.astype(BF16)
    ks = _norm_rope_slab(nsw_ref[0, :, 0:128], gks_ref[...], cos, sin, bm)
    vs = nsw_ref[0, :, 128:256]
    kw = _norm_rope_slab(nsw_ref[0, :, 256:384], gkw_ref[...], cos, sin, bm)
    vw = nsw_ref[0, :, 384:512]
    slc_ref[0, :, 0:128] = ks
    slc_ref[0, :, 128:256] = vs
    win_ref[0, :, 0:128] = kw
    win_ref[0, :, 128:256] = vw
    for ref, val in ((ks_ref, ks), (vs_ref, vs), (kw_ref, kw), (vw_ref, vw)):
        ref[0] = val.astype(BF16)
    gt = _sigmoid(ngate_ref[0])
    gates_ref[0, 0] = gt
    gates_ref[0, 1] = pltpu.roll(gt, LANES - 3 * N_REP, axis=1)


def _rows(nq, nsw, ngate, cos, sin, gq, gks, gkw, bm):
    b, t, _ = nq.shape
    tm = min(t, 512)
    tok = lambda w: pl.BlockSpec((1, tm, w), lambda i, j: (i, j, 0))
    const = lambda shape: pl.BlockSpec(shape, lambda i, j: (0,) * len(shape))
    kv_shape = jax.ShapeDtypeStruct((b, t, LANES), BF16)
    return pl.pallas_call(
        _rows_kernel,
        grid=(b, t // tm),
        in_specs=[tok(NQ_W), tok(NSW_W), tok(NGATE_W),
                  pl.BlockSpec((tm, LANES), lambda i, j: (j, 0)),
                  pl.BlockSpec((tm, LANES), lambda i, j: (j, 0)),
                  const((1, NQ_W)), const((1, LANES)), const((1, LANES)), const((LANES, LANES))],
        out_specs=[pl.BlockSpec((1, N_HEADS, tm, LANES), lambda i, j: (i, 0, j, 0)),
                   tok(LANES), tok(LANES), tok(LANES), tok(LANES), tok(256), tok(256),
                   pl.BlockSpec((1, N_KV, tm, LANES), lambda i, j: (i, 0, j, 0))],
        out_shape=[jax.ShapeDtypeStruct((b, N_HEADS, t, LANES), BF16), kv_shape, kv_shape, kv_shape, kv_shape,
                   jax.ShapeDtypeStruct((b, t, 256), F32), jax.ShapeDtypeStruct((b, t, 256), F32),
                   jax.ShapeDtypeStruct((b, N_KV, t, LANES), F32)],
        compiler_params=_cparams(("parallel", "parallel")),
        name="nsa_rows",
    )(nq, nsw, ngate, cos, sin, gq, gks, gkw, bm)


def _compress_kernel(u_ref, w1_ref, pe_ref, w2_ref, gkc_ref, cos_ref, sin_ref, bm_ref, kc_ref, vc_ref):
    nu = u_ref.shape[1]
    outs = []
    for kv in range(2):
        x = jnp.concatenate(
            [u_ref[0, :, j * 256 + kv * LANES:j * 256 + (kv + 1) * LANES] for j in range(CMP_STRIDE)],
            axis=1).astype(BF16)
        w1 = w1_ref[kv]
        p = _dot(x, w1)
        bias = _dot(pe_ref[kv].astype(BF16), w1)
        pa = p[:, :256] + bias[0:1, :256]
        pb = p[:, 256:] + bias[1:2, 256:]
        pre = pa + pltpu.roll(pb, nu - 1, axis=0)
        hid = (pre * _sigmoid(pre)).astype(BF16)
        outs.append(_dot(hid, w2_ref[kv]))
    kc_ref[0] = _norm_rope_slab(outs[0], gkc_ref[...], cos_ref[...], sin_ref[...], bm_ref[...]).astype(BF16)
    vc_ref[0] = outs[1].astype(BF16)


def _compress(units, w1r, pe_r, w2r, gkc, cos_c, sin_c, bm):
    b, nu, _ = units.shape
    const = lambda shape: pl.BlockSpec(shape, lambda i: (0,) * len(shape))
    out = jax.ShapeDtypeStruct((b, nu, LANES), BF16)
    return pl.pallas_call(
        _compress_kernel,
        grid=(b,),
        in_specs=[pl.BlockSpec((1, nu, 4096), lambda i: (i, 0, 0)),
                  const((2, 2048, 512)), const((2, 8, 2048)), const((2, 256, LANES)),
                  const((1, LANES)), const((nu, LANES)), const((nu, LANES)), const((LANES, LANES))],
        out_specs=[pl.BlockSpec((1, nu, LANES), lambda i: (i, 0, 0))] * 2,
        out_shape=[out, out],
        compiler_params=_cparams(("parallel",)),
        name="nsa_compress",
    )(units, w1r, pe_r, w2r, gkc, cos_c, sin_c, bm)


def _cmp_kernel(q_ref, kc_ref, vc_ref, c2s_ref, oc_ref, sel_ref, *, tq, pos0, n_cmp, nsel):
    qi = pl.program_id(1)
    nc_pad = kc_ref.shape[1]
    q = q_ref[0].reshape(N_HEADS * tq, LANES)
    s = _dot_nt(q, kc_ref[0]).reshape(N_HEADS, tq, nc_pad)
    qpos = pos0 + qi * tq + _iota((tq, 1), 0)
    c_idx = _iota((tq, nc_pad), 1)
    c_ok = jnp.where(c_idx < n_cmp, c_idx * CMP_STRIDE + (CMP_LEN - 1), jnp.int32(2 ** 30)) <= qpos
    s = s + jnp.where(c_ok, 0.0, NEG)[None]
    p = jnp.exp2(s - jnp.max(s, axis=-1, keepdims=True))
    p = p * ((qpos >= CMP_LEN - 1).astype(F32)[None] / jnp.sum(p, axis=-1, keepdims=True))
    oc_ref[0] = _dot(p.reshape(N_HEADS * tq, nc_pad).astype(BF16), vc_ref[0]).reshape(N_HEADS, tq, LANES)
    psum = jnp.sum(p.reshape(N_KV, N_REP, tq, nc_pad), axis=1).reshape(N_KV * tq, nc_pad)
    imp = _dot_split(psum, c2s_ref[...])
    work = _transpose2d(imp)
    ncol = N_KV * tq
    qpos_row = pos0 + qi * tq + _iota((1, ncol), 1) % tq
    blk = _iota((nsel, ncol), 0)
    cur = qpos_row // SLC_BLOCK
    forced = jnp.where(blk == 0, 1.0, jnp.where(blk == cur, 1.0, jnp.where(blk == cur - 1, 1.0, 0.0)))
    work = jnp.where(blk * SLC_BLOCK <= qpos_row, work + FORCE_BONUS * forced, NEG)
    blk_f = blk.astype(F32)
    sel = jnp.zeros((nsel, ncol), F32)
    for _ in range(SLC_TOPK):
        m = jnp.max(work, axis=0, keepdims=True)
        idx = jnp.min(jnp.where(work == m, blk_f, float(nsel)), axis=0, keepdims=True)
        hit = blk_f == idx
        sel = jnp.where(hit, 1.0, sel)
        work = jnp.where(hit, -jnp.inf, work)
    sel_ref[0] = _transpose2d(sel).reshape(N_KV, tq, nsel).astype(BF16)


def _cmp_attn(q_pad, kc, vc, c2s, *, tq, pos0, n_cmp):
    b, _, t, _ = q_pad.shape
    nc_pad = kc.shape[1]
    nsel = c2s.shape[1]
    kern = functools.partial(_cmp_kernel, tq=tq, pos0=pos0, n_cmp=n_cmp, nsel=nsel)
    return pl.pallas_call(
        kern,
        grid=(b, t // tq),
        in_specs=[pl.BlockSpec((1, N_HEADS, tq, LANES), lambda i, j: (i, 0, j, 0)),
                  pl.BlockSpec((1, nc_pad, LANES), lambda i, j: (i, 0, 0)),
                  pl.BlockSpec((1, nc_pad, LANES), lambda i, j: (i, 0, 0)),
                  pl.BlockSpec((nc_pad, nsel), lambda i, j: (0, 0))],
        out_specs=[pl.BlockSpec((1, N_HEADS, tq, LANES), lambda i, j: (i, 0, j, 0)),
                   pl.BlockSpec((1, N_KV, tq, nsel), lambda i, j: (i, 0, j, 0))],
        out_shape=[jax.ShapeDtypeStruct((b, N_HEADS, t, LANES), F32),
                   jax.ShapeDtypeStruct((b, N_KV, t, nsel), BF16)],
        compiler_params=_cparams(("parallel", "parallel")),
        name="nsa_cmp_topk",
    )(q_pad, kc, vc, c2s)


SLAB = 64


def _attend_tile(q_parts, k, v, bias_scr, s_scr, p_scr, m_scr, l_scr, alpha_scr, acc_scr, *, tq, width):
    nch = width // LANES

    def chunk(rows, i, c):
        x = s_scr[rows, c * LANES:(c + 1) * LANES]
        if bias_scr is not None:
            x = x + bias_scr[i * SLAB:(i + 1) * SLAB, c * LANES:(c + 1) * LANES]
        return x

    slabs = [(slice(r * tq + i * SLAB, r * tq + (i + 1) * SLAB), i) for r in range(N_REP) for i in range(tq // SLAB)]
    for r in range(N_REP):
        s_scr[r * tq:(r + 1) * tq, 0:width] = _dot_nt(q_parts[r], k)
    for rows, i in slabs:
        mx = chunk(rows, i, 0)
        for c in range(1, nch):
            mx = jnp.maximum(mx, chunk(rows, i, c))
        m_old = m_scr[rows, :]
        m_new = jnp.maximum(m_old, jnp.max(mx, axis=1, keepdims=True))
        alpha_scr[rows, :] = jnp.exp2(m_old - m_new)
        m_scr[rows, :] = m_new
    for rows, i in slabs:
        m_new = m_scr[rows, :]
        lsum = alpha_scr[rows, :] * l_scr[rows, :]
        for c in range(nch):
            p = jnp.exp2(chunk(rows, i, c) - m_new)
            lsum = lsum + p
            p_scr[rows, c * LANES:(c + 1) * LANES] = p.astype(BF16)
        l_scr[rows, :] = lsum
    for r in range(N_REP):
        rs = slice(r * tq, (r + 1) * tq)
        acc_scr[rs, :] = alpha_scr[rs, :] * acc_scr[rs, :] + _dot(p_scr[rs, 0:width], v)


def _slcwin_kernel(q_ref, ks_ref, vs_ref, kw_ref, vw_ref, sel_ref, et_ref, oc_ref, gates_ref, hn_ref,
                   s_scr, bias_scr, p_scr, m_scr, l_scr, alpha_scr, acc_scr, ow_scr, *, tq):
    g = pl.program_id(1)
    qi = pl.program_id(2)
    rows = N_REP * tq
    q0 = qi * tq
    qpos = q0 + _iota((tq, 1), 0)
    scr = dict(s_scr=s_scr, p_scr=p_scr, m_scr=m_scr, l_scr=l_scr, alpha_scr=alpha_scr, acc_scr=acc_scr, tq=tq)

    def reset():
        m_scr[...] = jnp.full((rows, LANES), NEG, F32)
        l_scr[...] = jnp.zeros((rows, LANES), F32)
        acc_scr[...] = jnp.zeros((rows, LANES), F32)

    def result():
        return acc_scr[...] / jnp.sum(l_scr[...], axis=1, keepdims=True)

    wlen = WINDOW + tq
    start = pl.multiple_of(jnp.maximum(q0 - WINDOW, 0), tq)
    reset()
    dist = qpos - (start + _iota((tq, wlen), 1))
    bias_scr[:, 0:wlen] = jnp.where(jnp.where(dist >= 0, dist, WINDOW) < WINDOW, 0.0, NEG)
    _attend_tile([q_ref[0, r] for r in range(N_REP)], kw_ref[0, pl.ds(start, wlen), :],
                 vw_ref[0, pl.ds(start, wlen), :], bias_scr, width=wlen, **scr)
    ow_scr[...] = result()

    reset()
    sel_m = ((sel_ref[0, 0].astype(F32) - 1.0) * (-NEG)).astype(BF16)
    q_aug = [jnp.concatenate([q_ref[0, r], sel_m], axis=1) for r in range(N_REP)]
    n_kv = (q0 + tq + KV_TILE - 1) // KV_TILE

    def tile_inputs(k0, width):
        k0 = pl.multiple_of(k0, KV_TILE)
        k_aug = jnp.concatenate([ks_ref[0, pl.ds(k0, width), :], et_ref[pl.ds(k0, width), :]], axis=1)
        return k_aug, vs_ref[0, pl.ds(k0, width), :]

    n_plain = n_kv - 1

    def body(j, carry):
        _attend_tile(q_aug, *tile_inputs(j * KV_TILE, KV_TILE), None, width=KV_TILE, **scr)
        return carry

    lax.fori_loop(0, n_plain, body, 0)

    k0 = n_plain * KV_TILE
    bias_scr[:, 0:KV_TILE] = jnp.where(k0 + _iota((tq, KV_TILE), 1) <= qpos, 0.0, NEG)
    _attend_tile(q_aug, *tile_inputs(k0, KV_TILE), bias_scr, width=KV_TILE, **scr)
    o_s_all = result()

    gt = gates_ref[0, 0]
    low = _iota((1, LANES), 1) < N_HD
    vals = []
    for r in range(N_REP):
        rs = slice(r * tq, (r + 1) * tq)
        vals.append(gt[:, 3 * r:3 * r + 1] * oc_ref[0, r] + gt[:, 3 * r + 1:3 * r + 2] * o_s_all[rs]
                    + gt[:, 3 * r + 2:3 * r + 3] * ow_scr[rs, :])
    for pair in range(N_REP // 2):
        a, b = vals[2 * pair], vals[2 * pair + 1]
        a_sw, b_sw = pltpu.roll(a, N_HD, axis=1), pltpu.roll(b, N_HD, axis=1)
        lo = jnp.where(g == 0, a, a_sw)
        hi = jnp.where(g == 0, b_sw, b)
        hn_ref[0, :, pair * LANES:(pair + 1) * LANES] = jnp.where(low, lo, hi).astype(BF16)


def _slcwin(q_pad, ks, vs, kw, vw, sel, e_t, oc, gates, *, tq):
    b, _, t, _ = q_pad.shape
    nsel = sel.shape[3]
    assert nsel == LANES and e_t.shape == (t, LANES)
    rows = N_REP * tq
    wlen = WINDOW + tq
    kern = functools.partial(_slcwin_kernel, tq=tq)
    qspec = pl.BlockSpec((1, N_REP, tq, LANES), lambda i, g, j: (i, g, j, 0))
    full = pl.BlockSpec((1, t, LANES), lambda i, g, j: (i, 0, 0))
    return pl.pallas_call(
        kern,
        grid=(b, N_KV, t // tq),
        in_specs=[qspec, full, full, full, full,
                  pl.BlockSpec((1, 1, tq, nsel), lambda i, g, j: (i, g, j, 0)),
                  pl.BlockSpec((t, LANES), lambda i, g, j: (0, 0)),
                  qspec,
                  pl.BlockSpec((1, 1, tq, LANES), lambda i, g, j: (i, g, j, 0))],
        out_specs=pl.BlockSpec((1, tq, N_REP * N_HD), lambda i, g, j: (i, j, g)),
        out_shape=jax.ShapeDtypeStruct((b, t, N_WIDTH), BF16),
        scratch_shapes=[pltpu.VMEM((rows, wlen), F32), pltpu.VMEM((tq, wlen), F32), pltpu.VMEM((rows, wlen), BF16),
                        pltpu.VMEM((rows, LANES), F32), pltpu.VMEM((rows, LANES), F32), pltpu.VMEM((rows, LANES), F32),
                        pltpu.VMEM((rows, LANES), F32), pltpu.VMEM((rows, LANES), F32)],
        compiler_params=_cparams(("parallel", "parallel", "parallel")),
        name="nsa_slc_win",
    )(q_pad, ks, vs, kw, vw, sel, e_t, oc, gates)


GATHER_PAGES = 16


PAGE_UNITS = PAGE_SIZE // CMP_STRIDE


def _gather_cmp_kernel(pt_ref, *refs):
    out_ref, t_scr = refs[-2], refs[-1]
    for i in range(GATHER_PAGES):
        for kv in range(2):
            t_scr[2 * i + kv] = refs[i][0, 0, kv].reshape(LANES, PAGE_SIZE).T
            for j in range(CMP_STRIDE):
                out_ref[0, i * PAGE_UNITS:(i + 1) * PAGE_UNITS, (2 * j + kv) * LANES:(2 * j + kv + 1) * LANES] = \
                    t_scr[2 * i + kv, pl.ds(j, PAGE_UNITS, stride=CMP_STRIDE), :]


def _gather_cmp(page_table, cache_t, layer):
    db, n_pages = page_table.shape
    uw = CMP_STRIDE * 2 * LANES

    def page_spec(i):
        return pl.BlockSpec((1, 1, 2, N_KV, N_HD, PAGE_SIZE),
                            lambda b, c, pt: (layer, pt[b, c * GATHER_PAGES + i], 0, 0, 0, 0))

    return pl.pallas_call(
        _gather_cmp_kernel,
        grid_spec=pltpu.PrefetchScalarGridSpec(
            num_scalar_prefetch=1,
            grid=(db, n_pages // GATHER_PAGES),
            in_specs=[page_spec(i) for i in range(GATHER_PAGES)],
            out_specs=pl.BlockSpec((1, GATHER_PAGES * PAGE_UNITS, uw), lambda b, c, pt: (b, c, 0)),
            scratch_shapes=[pltpu.VMEM((2 * GATHER_PAGES, PAGE_SIZE, LANES), F32)]),
        out_shape=jax.ShapeDtypeStruct((db, n_pages * PAGE_UNITS, uw), F32),
        compiler_params=_cparams(("parallel", "parallel")),
        name="gather_cmp_pages",
    )(page_table, *([cache_t] * GATHER_PAGES))


SLC_PAGES = 32


def _heads_bias(b2, ts):
    w = b2.shape[1]
    return jnp.broadcast_to(b2.reshape(N_KV, 1, ts, w), (N_KV, N_REP, ts, w)).reshape(N_HEADS * ts, w)


def _slc_sample_kernel(pt_ref, *refs, ts):
    pages = refs[:SLC_PAGES]
    (q_ref, sel_ref, e_ref, enew_ref, newslc_ref, win_ref, newwin_ref, oc_ref, gates_ref,
     hn_ref, m_scr, l_scr, acc_scr) = refs[SLC_PAGES:]
    c = pl.program_id(1)
    rows = N_HEADS * ts
    q = q_ref[0].reshape(rows, LANES)
    sel2 = sel_ref[0].reshape(N_KV * ts, sel_ref.shape[3])

    @pl.when(c == 0)
    def _():
        m_scr[...] = jnp.full((rows, 1), NEG, F32)
        l_scr[...] = jnp.zeros((rows, 1), F32)
        acc_scr[...] = jnp.zeros((rows, LANES), F32)

    def online(s, pv_fn):
        m_old = m_scr[...]
        m_new = jnp.maximum(m_old, jnp.max(s, axis=1, keepdims=True))
        p = jnp.exp2(s - m_new)
        alpha = jnp.exp2(m_old - m_new)
        l_scr[...] = alpha * l_scr[...] + jnp.sum(p, axis=1, keepdims=True)
        acc_scr[...] = alpha * acc_scr[...] + pv_fn(p.astype(BF16))
        m_scr[...] = m_new

    picked = _dot(sel2, e_ref[...])
    bias = _heads_bias(picked * (-NEG) + NEG, ts)
    k_t = [pg[0, 0, 0].reshape(LANES, PAGE_SIZE).astype(BF16) for pg in pages]
    v_t = [pg[0, 0, 1].reshape(LANES, PAGE_SIZE).astype(BF16) for pg in pages]
    s = jnp.concatenate([_dot(q, kt) for kt in k_t], axis=1) + bias

    def pv_pages(p):
        out = _dot_nt(p[:, 0:PAGE_SIZE], v_t[0])
        for i in range(1, SLC_PAGES):
            out = out + _dot_nt(p[:, i * PAGE_SIZE:(i + 1) * PAGE_SIZE], v_t[i])
        return out

    online(s, pv_pages)

    @pl.when(c == pl.num_programs(1) - 1)
    def _():
        tok = _iota((N_KV * ts, 1), 0) % ts
        key = _iota((N_KV * ts, PAGE_SIZE), 1)
        new = newslc_ref[0].astype(BF16)
        picked_n = _dot(sel2, enew_ref[...])
        bias_n = _heads_bias(jnp.where(key <= tok, picked_n, 0.0) * (-NEG) + NEG, ts)
        online(_dot_nt(q, new[:, :LANES]) + bias_n, lambda p: _dot(p, new[:, LANES:]))
        kw_t = win_ref[0, 0, 0].reshape(LANES, WINDOW).astype(BF16)
        vw_t = win_ref[0, 0, 1].reshape(LANES, WINDOW).astype(BF16)
        nwin = newwin_ref[0].astype(BF16)
        wkey = _iota((N_KV * ts, WINDOW), 1)
        bias_w = jnp.concatenate([jnp.where(wkey > tok, 0.0, NEG), jnp.where(key <= tok, 0.0, NEG)], axis=1)
        s_w = jnp.concatenate([_dot(q, kw_t), _dot_nt(q, nwin[:, :LANES])], axis=1) + _heads_bias(bias_w, ts)
        p_w = jnp.exp2(s_w - jnp.max(s_w, axis=1, keepdims=True))
        l_w = jnp.sum(p_w, axis=1, keepdims=True)
        p_w = p_w.astype(BF16)
        o_w = (_dot_nt(p_w[:, :WINDOW], vw_t) + _dot(p_w[:, WINDOW:], nwin[:, LANES:])) / l_w
        o_s = acc_scr[...] / l_scr[...]
        low = _iota((1, LANES), 1) < N_HD
        vals = []
        for h in range(N_HEADS):
            g, r = divmod(h, N_REP)
            gt = gates_ref[0, g]
            rs = slice(h * ts, (h + 1) * ts)
            vals.append(gt[:, 3 * r:3 * r + 1] * oc_ref[0, h] + gt[:, 3 * r + 1:3 * r + 2] * o_s[rs]
                        + gt[:, 3 * r + 2:3 * r + 3] * o_w[rs])
        for pair in range(N_HEADS // 2):
            a, b = vals[2 * pair], vals[2 * pair + 1]
            if pair < N_REP // 2:
                lo, hi = a, pltpu.roll(b, N_HD, axis=1)
            else:
                lo, hi = pltpu.roll(a, N_HD, axis=1), b
            hn_ref[0, :, pair * LANES:(pair + 1) * LANES] = jnp.where(low, lo, hi).astype(BF16)


def _slc_sample(page_table, cache_t, layer, q_pad, sel, e_main, e_new, new_slc, win_t, new_win, oc, gates):
    db, n_pages = page_table.shape
    ts = q_pad.shape[2]
    nsel = sel.shape[3]
    n_steps = n_pages // SLC_PAGES
    step_keys = SLC_PAGES * PAGE_SIZE
    rows = N_HEADS * ts

    def page_spec(i):
        return pl.BlockSpec((1, 1, 2, N_KV, N_HD, PAGE_SIZE),
                            lambda b, c, pt: (layer, pt[b, c * SLC_PAGES + i], 0, 0, 0, 0))

    per_b = lambda shape: pl.BlockSpec((1,) + shape, lambda b, c, pt: (b,) + (0,) * len(shape))
    return pl.pallas_call(
        functools.partial(_slc_sample_kernel, ts=ts),
        grid_spec=pltpu.PrefetchScalarGridSpec(
            num_scalar_prefetch=1,
            grid=(db, n_steps),
            in_specs=[page_spec(i) for i in range(SLC_PAGES)] + [
                per_b((N_HEADS, ts, LANES)), per_b((N_KV, ts, nsel)),
                pl.BlockSpec((nsel, step_keys), lambda b, c, pt: (0, c)),
                pl.BlockSpec((nsel, PAGE_SIZE), lambda b, c, pt: (0, 0)),
                per_b((PAGE_SIZE, 256)),
                pl.BlockSpec((1, 1, 2, N_KV, N_HD, WINDOW), lambda b, c, pt: (layer, b, 0, 0, 0, 0)),
                per_b((PAGE_SIZE, 256)), per_b((N_HEADS, ts, LANES)), per_b((N_KV, ts, LANES))],
            out_specs=per_b((ts, N_WIDTH)),
            scratch_shapes=[pltpu.VMEM((rows, 1), F32), pltpu.VMEM((rows, 1), F32), pltpu.VMEM((rows, LANES), F32)]),
        out_shape=jax.ShapeDtypeStruct((db, ts, N_WIDTH), BF16),
        compiler_params=_cparams(("parallel", "arbitrary")),
        name="nsa_slc_sample",
    )(page_table, *([cache_t] * SLC_PAGES), q_pad, sel, e_main, e_new, new_slc, win_t, new_win, oc, gates)


def _rope_tables(pos):
    half = N_HD // 2
    inv = ROPE_THETA ** (-jnp.arange(half, dtype=F32) / half)
    ang = pos.astype(F32)[:, None] * inv[None, :]
    cos, sin = jnp.cos(ang), jnp.sin(ang)
    return jnp.tile(jnp.concatenate([cos, cos], axis=1), (1, 2)), jnp.tile(jnp.concatenate([-sin, sin], axis=1), (1, 2))


def _pad_cols(a, w):
    return jnp.pad(a, ((0, 0), (0, w - a.shape[1])))


def _prep_layer(l, w_norm_mix, w_in, b_mlstm_if, w_mlstm_conv, w_mlstm_hnorm, w_gla_gate2, b_gla_gate,
                w_gla_hnorm, w_qk_norm, w_cmp_pe, w_cmp_1, w_cmp_2, w_out, w_norm_ffn, w_ffn_up, w_ffn_down):
    wi = w_in[l]
    col = lambda n: wi[:, _OFF[n][0]:_OFF[n][0] + _OFF[n][1]]
    um = _pad_cols(jnp.concatenate([col('m_qk'), col('m_v'), col('m_o'), col('m_i'), col('m_f')], axis=1), UM_W)
    ug = _pad_cols(jnp.concatenate([col('g_q'), col('g_k'), col('g_v'), col('g_o'), col('g_lr')], axis=1), UG_W)
    ncmp = jnp.concatenate([col('n_kc'), col('n_vc')], axis=1)
    nsw = jnp.concatenate([col('n_ks'), col('n_vs'), col('n_kw'), col('n_vw')], axis=1)
    ngate = _pad_cols(col('n_gate'), NGATE_W)
    p = {}
    p['w_cat'] = jnp.concatenate([um, ug, col('n_q'), ncmp, nsw, ngate], axis=1).astype(BF16)
    p['g_mix'] = w_norm_mix[l][None, :]
    p['wconv'] = jnp.pad(w_mlstm_conv[l], ((0, 8 - M_CONV), (0, 0)))
    p['bif'] = _pad_cols(b_mlstm_if[l][None, :], LANES)
    p['gh_m'] = w_mlstm_hnorm[l][None, :]
    p['w2p'] = jnp.pad(w_gla_gate2[l], ((0, LANES - G_RANK), (0, 0)))
    p['bg'] = b_gla_gate[l][None, :]
    p['gh_g'] = w_gla_hnorm[l][None, :]
    gqk = w_qk_norm[l]
    p['gq'] = jnp.tile(gqk[0], N_HEADS)[None, :]
    p['gkc'] = jnp.tile(gqk[1], N_KV)[None, :]
    p['gks'] = jnp.tile(gqk[2], N_KV)[None, :]
    p['gkw'] = jnp.tile(gqk[3], N_KV)[None, :]
    w1 = w_cmp_1[l].reshape(2, 2, CMP_STRIDE, N_HD, CMP_HIDDEN)
    eye = jnp.eye(N_KV, dtype=F32)
    w1r = jnp.einsum('khjdc,gf->kjgdhfc', w1, eye).reshape(2, CMP_STRIDE * N_KV * N_HD, 2 * N_KV * CMP_HIDDEN)
    p['w1r'] = w1r.astype(BF16)
    pe = w_cmp_pe[l].reshape(2, 2, CMP_STRIDE, 1, N_HD)
    pe = jnp.broadcast_to(pe, (2, 2, CMP_STRIDE, N_KV, N_HD)).reshape(2, 2, 2048)
    p['pe_r'] = jnp.pad(pe, ((0, 0), (0, 6), (0, 0)))
    p['w2r'] = jnp.einsum('kcd,gf->kgcfd', w_cmp_2[l], eye).reshape(2, N_KV * CMP_HIDDEN, N_KV * N_HD).astype(BF16)
    p['w_out'] = w_out[l].astype(BF16)
    p['g_ffn'] = w_norm_ffn[l][None, :]
    nf = 2
    tf = D_FF // nf
    wu = w_ffn_up[l]
    p['w_up_r'] = jnp.stack([jnp.concatenate([wu[:, f * tf:(f + 1) * tf], wu[:, D_FF + f * tf:D_FF + (f + 1) * tf]],
                                             axis=1) for f in range(nf)]).astype(BF16)
    p['w_dn_r'] = w_ffn_down[l].reshape(nf, tf, D_MODEL).astype(BF16)
    return p


def _cmp2slc(n_cmp, nc_pad, nsel):
    m = np.zeros((nc_pad, nsel), np.float32)
    per = SLC_BLOCK // CMP_STRIDE
    for n in range(n_cmp):
        for u in range(CMP_LEN // CMP_STRIDE):
            m[n, (n + u) // per] += 1.0
    return jnp.asarray(m, BF16)


def _expand_mat(nsel, kvlen):
    return jnp.asarray((np.arange(kvlen)[None, :] // SLC_BLOCK) == np.arange(nsel)[:, None], BF16)


def _group_mean_mat():
    idx = np.arange(LANES) // N_HD
    return jnp.asarray((idx[:, None] == idx[None, :]) / float(N_HD), BF16)


def _mixers(p, um, ug, mstate, gstate, *, L, n_valid):
    conv0, ct0, n0, m0 = mstate
    hm, conv_o, ct_o, n_o, m_o = _mlstm(um, conv0, ct0, n0, m0, p['wconv'], p['bif'], p['gh_m'], L=L, n_valid=n_valid)
    hg, s_o = _gla(ug, gstate, p['w2p'], p['bg'], p['gh_g'], L=L, n_valid=n_valid)
    return hm, hg, (conv_o, ct_o, n_o, m_o), s_o


def _mlstm_state_in(conv, c, n, m):
    b = conv.shape[0]
    conv0 = jnp.pad(conv.astype(F32), ((0, 0), (8 - (M_CONV - 1), 0), (0, 0)))
    ct0 = jnp.swapaxes(c.astype(F32), -1, -2).reshape(b, M_WIDTH, M_HD)
    n0 = jnp.broadcast_to(n.astype(F32).reshape(b, 1, M_WIDTH), (b, 8, M_WIDTH))
    m0 = jnp.broadcast_to(jnp.pad(m.astype(F32), ((0, 0), (0, 8 - M_HEADS)))[:, :, None], (b, 8, LANES))
    return conv0, ct0, n0, m0


def _mlstm_state_out(conv_o, ct_o, n_o, m_o):
    b = conv_o.shape[0]
    c = jnp.swapaxes(ct_o.reshape(b, M_HEADS, M_HD, M_HD), -1, -2)
    return c, n_o[:, 0].reshape(b, M_HEADS, M_HD), m_o[:, :M_HEADS, 0], conv_o[:, 8 - (M_CONV - 1):]


def kernel(x_prompt, x_sample, cache_cmp_kv, cache_slc_kv, state_win_kv, state_mlstm_C, state_mlstm_n,
           state_mlstm_m, state_mlstm_conv, state_gla_S, page_table, w_norm_mix, w_in, b_mlstm_if,
           w_mlstm_conv, w_mlstm_hnorm, w_gla_gate2, b_gla_gate, w_gla_hnorm, w_qk_norm, w_cmp_pe,
           w_cmp_1, w_cmp_2, w_out, w_norm_ffn, w_ffn_up, w_ffn_down):
    b, t, _ = x_prompt.shape
    db, td, _ = x_sample.shape
    depth = w_in.shape[0]
    n_pages = page_table.shape[1]
    past = n_pages * PAGE_SIZE
    win_buf = state_win_kv.shape[2]
    assert t % 512 == 0 and td <= SAMPLE_T and n_pages % SLC_PAGES == 0 and win_buf == WINDOW

    bm = _group_mean_mat()
    cos_p, sin_p = _rope_tables(jnp.arange(t))
    nu_p = t // CMP_STRIDE
    ncmp_p = (t - CMP_LEN) // CMP_STRIDE + 1
    cos_cp, sin_cp = _rope_tables(jnp.arange(nu_p) * CMP_STRIDE + CMP_LEN - 1)
    nsel_p = -(-t // SLC_BLOCK)
    nsel_p = -(-nsel_p // LANES) * LANES
    c2s_p = _cmp2slc(ncmp_p, nu_p, nsel_p)
    e_p = _expand_mat(nsel_p, t).T
    tq_p = 256
    cos_s, sin_s = _rope_tables(past + jnp.arange(SAMPLE_T))
    nu_s = past // CMP_STRIDE
    ncmp_s = (past + td - CMP_LEN) // CMP_STRIDE + 1
    cos_cs, sin_cs = _rope_tables(jnp.arange(nu_s) * CMP_STRIDE + CMP_LEN - 1)
    nsel_s = -(-(-(-(past + td) // SLC_BLOCK)) // LANES) * LANES
    c2s_s = _cmp2slc(ncmp_s, nu_s, nsel_s)
    e_s = _expand_mat(nsel_s, past + PAGE_SIZE)
    e_s_main, e_s_new = e_s[:, :past], e_s[:, past:]
    row_minor = lambda a: jnp.transpose(a, (0, 1, 3, 4, 5, 2))
    cmp_t, slc_t, win_t = row_minor(cache_cmp_kv), row_minor(cache_slc_kv), row_minor(state_win_kv)

    xp = x_prompt.reshape(b * t, D_MODEL)
    xs = jnp.pad(x_sample, ((0, 0), (0, SAMPLE_T - td), (0, 0))).reshape(db * SAMPLE_T, D_MODEL)

    zero_m = _mlstm_state_in(jnp.zeros((b, M_CONV - 1, 2 * M_WIDTH), F32), jnp.zeros((b, M_HEADS, M_HD, M_HD), F32),
                             jnp.zeros((b, M_HEADS, M_HD), F32), jnp.zeros((b, M_HEADS), F32))
    zero_g = jnp.zeros((b, LANES, G_DV), F32)

    pl_out = [[] for _ in range(8)]
    sl_out = [[] for _ in range(8)]
    for l in range(depth):
        p = _prep_layer(l, w_norm_mix, w_in, b_mlstm_if, w_mlstm_conv, w_mlstm_hnorm, w_gla_gate2, b_gla_gate,
                        w_gla_hnorm, w_qk_norm, w_cmp_pe, w_cmp_1, w_cmp_2, w_out, w_norm_ffn, w_ffn_up, w_ffn_down)
        um, ug, nq, ncmp, nsw, ngate = _pre(xp, p['g_mix'], p['w_cat'])
        r3 = lambda a, bb, tt: a.reshape(bb, tt, a.shape[-1])
        hm, hg, mst, gst = _mixers(p, r3(um, b, t), r3(ug, b, t), zero_m, zero_g, L=M_CHUNK, n_valid=M_CHUNK)
        q_hm, ks, vs, kw, vw, slc_f, win_f, gates = _rows(r3(nq, b, t), r3(nsw, b, t), r3(ngate, b, t), cos_p, sin_p,
                                                          p['gq'], p['gks'], p['gkw'], bm)
        kc, vc = _compress(ncmp.reshape(b, nu_p, 4096), p['w1r'], p['pe_r'], p['w2r'], p['gkc'], cos_cp, sin_cp, bm)
        oc, sel = _cmp_attn(q_hm, kc, vc, c2s_p, tq=tq_p, pos0=0, n_cmp=ncmp_p)
        hn = _slcwin(q_hm, ks, vs, kw, vw, sel, e_p, oc, gates, tq=tq_p)
        xp = _post(xp, hm.reshape(b * t, -1), hg.reshape(b * t, -1), hn.reshape(b * t, -1),
                   p['w_out'], p['g_ffn'], p['w_up_r'], p['w_dn_r'])
        c_o, n_o, m_o, conv_o = _mlstm_state_out(*mst)
        kv6 = lambda a: a.reshape(a.shape[0], a.shape[1], 2, N_KV, N_HD)
        for lst, val in zip(pl_out, (kv6(ncmp.reshape(b, t, 256)), kv6(slc_f), kv6(win_f[:, t - min(WINDOW, t):]),
                                     c_o, n_o, m_o, conv_o, gst.reshape(b, G_HEADS, G_DK, G_DV))):
            lst.append(val)

        um, ug, nq, ncmp, nsw, ngate = _pre(xs, p['g_mix'], p['w_cat'])
        mstate = _mlstm_state_in(state_mlstm_conv[l], state_mlstm_C[l], state_mlstm_n[l], state_mlstm_m[l])
        gstate = state_gla_S[l].astype(F32).reshape(db, LANES, G_DV)
        hm, hg, mst, gst = _mixers(p, r3(um, db, SAMPLE_T), r3(ug, db, SAMPLE_T), mstate, gstate,
                                   L=SAMPLE_T, n_valid=td)
        q_hm, ks_n, vs_n, kw_n, vw_n, slc_f, win_f, gates = _rows(
            r3(nq, db, SAMPLE_T), r3(nsw, db, SAMPLE_T), r3(ngate, db, SAMPLE_T), cos_s, sin_s,
            p['gq'], p['gks'], p['gkw'], bm)
        units = _gather_cmp(page_table, cmp_t, l)
        kc, vc = _compress(units, p['w1r'], p['pe_r'], p['w2r'], p['gkc'], cos_cs, sin_cs, bm)
        oc, sel = _cmp_attn(q_hm, kc, vc, c2s_s, tq=SAMPLE_T, pos0=past, n_cmp=ncmp_s)
        pad_page = lambda a: jnp.pad(a, ((0, 0), (0, PAGE_SIZE - a.shape[1]), (0, 0)))
        hn = _slc_sample(page_table, slc_t, l, q_hm, sel, e_s_main, e_s_new, pad_page(slc_f), win_t,
                         pad_page(win_f), oc, gates)
        new_win = jnp.concatenate([state_win_kv[l][:, td:].astype(F32), kv6(win_f[:, :td])], axis=1)
        xs = _post(xs, hm.reshape(db * SAMPLE_T, -1), hg.reshape(db * SAMPLE_T, -1), hn.reshape(db * SAMPLE_T, -1),
                   p['w_out'], p['g_ffn'], p['w_up_r'], p['w_dn_r'])
        c_o, n_o, m_o, conv_o = _mlstm_state_out(*mst)
        for lst, val in zip(sl_out, (kv6(ncmp.reshape(db, SAMPLE_T, 256)[:, :td]), kv6(slc_f[:, :td]),
                                     new_win,
                                     c_o, n_o, m_o, conv_o, gst.reshape(db, G_HEADS, G_DK, G_DV))):
            lst.append(val)

    outs_p = [jnp.stack(a) for a in pl_out]
    outs_s = [jnp.stack(a) for a in sl_out]
    y_p = xp.reshape(b, t, D_MODEL)
    y_s = xs.reshape(db, SAMPLE_T, D_MODEL)[:, :td]
    return (y_p, y_s, *outs_p, *outs_s)
```

```python
import functools
import math

import numpy as np
import jax
import jax.numpy as jnp
from jax import lax
from jax.experimental import pallas as pl
from jax.experimental.pallas import tpu as pltpu

F32 = jnp.float32
BF16 = jnp.bfloat16

D_MODEL = 1024
M_HEADS, M_HD, M_WIDTH, M_CONV, M_CHUNK = 4, 64, 256, 4, 64
G_HEADS, G_DK, G_DV, G_WIDTH, G_RANK, G_TAU, G_CHUNK = 4, 32, 64, 256, 16, 16.0, 64
N_HEADS, N_HD, N_KV, N_REP, N_WIDTH = 8, 64, 2, 4, 512
CMP_LEN, CMP_STRIDE, CMP_HIDDEN = 32, 16, 128
SLC_BLOCK, SLC_TOPK, WINDOW = 64, 16, 512
ROPE_THETA = 10000.0
D_FF = 2816
PAGE_SIZE = 128
NEG = -1e30
FORCE_BONUS = 1e4
EPS = 1e-6
LOG2E = 1.4426950408889634

LANES = 128
VMEM_LIMIT = 56 * 1024 * 1024
KV_TILE = 512
SAMPLE_T = 16

_OFF = {}
_o = 0
for _name, _w in (('m_qk', 512), ('m_v', 256), ('m_i', 4), ('m_f', 4), ('m_o', 256),
                  ('g_q', 128), ('g_k', 128), ('g_v', 256), ('g_lr', 16), ('g_o', 256),
                  ('n_q', 512), ('n_kc', 128), ('n_vc', 128), ('n_ks', 128),
                  ('n_vs', 128), ('n_kw', 128), ('n_vw', 128), ('n_gate', 24)):
    _OFF[_name] = (_o, _w)
    _o += _w
UM_W, UG_W, NQ_W, NCMP_W, NSW_W, NGATE_W = 1152, 896, 512, 256, 512, 128


def _cparams(sem):
    return pltpu.CompilerParams(dimension_semantics=sem, vmem_limit_bytes=VMEM_LIMIT)


def _dot(a, b):
    return jnp.dot(a, b, preferred_element_type=F32)


def _dot_nt(a, b):
    return lax.dot_general(a, b, (((1,), (1,)), ((), ())), preferred_element_type=F32)


def _dot_hi(a, b):
    return jnp.dot(a, b, preferred_element_type=F32, precision=lax.Precision.HIGHEST)


def _dot_split(a, b_bf):
    hi = a.astype(BF16)
    lo = (a - hi.astype(F32)).astype(BF16)
    return _dot(hi, b_bf) + _dot(lo, b_bf)


def _log_sigmoid(x):
    return jnp.minimum(x, 0.0) - jnp.log(1.0 + jnp.exp(-jnp.abs(x)))


def _sigmoid(x):
    return 1.0 / (1.0 + jnp.exp(-x))


def _transpose(x):
    r, c = x.shape
    if r < LANES:
        x = jnp.concatenate([x, jnp.zeros((LANES - r, c), x.dtype)], axis=0)
    parts = [x[:, i * LANES:(i + 1) * LANES].T[:, :r] for i in range(c // LANES)]
    return parts[0] if len(parts) == 1 else jnp.concatenate(parts, axis=0)


def _transpose2d(x):
    r, c = x.shape
    rp, cp = -(-r // LANES) * LANES, -(-c // LANES) * LANES
    if cp > c:
        x = jnp.concatenate([x, jnp.zeros((r, cp - c), x.dtype)], axis=1)
    if rp > r:
        x = jnp.concatenate([x, jnp.zeros((rp - r, cp), x.dtype)], axis=0)
    out_rows = []
    for j in range(cp // LANES):
        blocks = [x[i * LANES:(i + 1) * LANES, j * LANES:(j + 1) * LANES].T for i in range(rp // LANES)]
        out_rows.append(blocks[0] if len(blocks) == 1 else jnp.concatenate(blocks, axis=1))
    out = out_rows[0] if len(out_rows) == 1 else jnp.concatenate(out_rows, axis=0)
    return out[:c, :r]


def _iota(shape, dim):
    return lax.broadcasted_iota(jnp.int32, shape, dim)


def _pre_kernel(x_ref, g_ref, w_ref, um_ref, ug_ref, nq_ref, ncmp_ref, nsw_ref, ngate_ref):
    x = x_ref[...]
    h = x * lax.rsqrt(jnp.mean(x * x, axis=-1, keepdims=True) + EPS) * g_ref[...]
    u = _dot(h.astype(BF16), w_ref[...])
    o = 0
    for ref, w in ((um_ref, UM_W), (ug_ref, UG_W), (nq_ref, NQ_W), (ncmp_ref, NCMP_W),
                   (nsw_ref, NSW_W), (ngate_ref, NGATE_W)):
        ref[...] = u[:, o:o + w]
        o += w


def _pre(x2, g, w_cat):
    m = x2.shape[0]
    tm = min(m, 256)
    widths = (UM_W, UG_W, NQ_W, NCMP_W, NSW_W, NGATE_W)
    return pl.pallas_call(
        _pre_kernel,
        grid=(m // tm,),
        in_specs=[pl.BlockSpec((tm, D_MODEL), lambda i: (i, 0)),
                  pl.BlockSpec((1, D_MODEL), lambda i: (0, 0)),
                  pl.BlockSpec((D_MODEL, sum(widths)), lambda i: (0, 0))],
        out_specs=[pl.BlockSpec((tm, w), lambda i: (i, 0)) for w in widths],
        out_shape=[jax.ShapeDtypeStruct((m, w), F32) for w in widths],
        compiler_params=_cparams(("parallel",)),
        name="pre_proj",
    )(x2, g, w_cat)


def _post_kernel(x_ref, hm_ref, hg_ref, hn_ref, wout_ref, g_ref, wup_ref, wdn_ref, o_ref, h2_ref, *, tf):
    @pl.when(pl.program_id(1) == 0)
    def _():
        xn = x_ref[...]
        xn = xn + _dot(hm_ref[...], wout_ref[0:256, :])
        xn = xn + _dot(hg_ref[...], wout_ref[256:512, :])
        xn = xn + _dot(hn_ref[...], wout_ref[512:1024, :])
        o_ref[...] = xn
        h2 = xn * lax.rsqrt(jnp.mean(xn * xn, axis=-1, keepdims=True) + EPS) * g_ref[...]
        h2_ref[...] = h2.astype(BF16)

    au = _dot(h2_ref[...], wup_ref[0])
    a = au[:, :tf]
    act = (a * _sigmoid(a) * au[:, tf:]).astype(BF16)
    o_ref[...] += _dot(act, wdn_ref[0])


def _post(x2, hm, hg, hn, w_out, g, w_up_r, w_dn_r):
    m = x2.shape[0]
    tm = min(m, 512)
    nf, _, tf2 = w_up_r.shape
    tf = tf2 // 2
    return pl.pallas_call(
        functools.partial(_post_kernel, tf=tf),
        grid=(m // tm, nf),
        in_specs=[pl.BlockSpec((tm, D_MODEL), lambda i, f: (i, 0)),
                  pl.BlockSpec((tm, M_WIDTH), lambda i, f: (i, 0)),
                  pl.BlockSpec((tm, G_WIDTH), lambda i, f: (i, 0)),
                  pl.BlockSpec((tm, N_WIDTH), lambda i, f: (i, 0)),
                  pl.BlockSpec((D_MODEL, D_MODEL), lambda i, f: (0, 0)),
                  pl.BlockSpec((1, D_MODEL), lambda i, f: (0, 0)),
                  pl.BlockSpec((1, D_MODEL, tf2), lambda i, f: (f, 0, 0)),
                  pl.BlockSpec((1, tf, D_MODEL), lambda i, f: (f, 0, 0))],
        out_specs=pl.BlockSpec((tm, D_MODEL), lambda i, f: (i, 0)),
        out_shape=jax.ShapeDtypeStruct((m, D_MODEL), F32),
        scratch_shapes=[pltpu.VMEM((tm, D_MODEL), BF16)],
        compiler_params=_cparams(("parallel", "arbitrary")),
        name="post_ffn",
    )(x2, hm, hg, hn, w_out, g, w_up_r, w_dn_r)


CHUNK_UNROLL = 4


def _mlstm_kernel(um_ref, conv0_ref, ct0_ref, n0_ref, m0_ref, wconv_ref, bif_ref, gh_ref,
                  hm_ref, conv_out_ref, ct_out_ref, n_out_ref, m_out_ref,
                  xpad, q_s, k_s, ct_s, n_s, m_s, *, L, n_valid, blk):
    j = pl.program_id(1)

    @pl.when(j == 0)
    def _():
        xpad[0:8, :] = conv0_ref[0]
        ct_s[...] = ct0_ref[0]
        n_s[...] = n0_ref[0]
        m_s[...] = m0_ref[0]

    qk_pre = um_ref[0, :, 0:2 * M_WIDTH]
    xpad[8:8 + blk, :] = qk_pre
    wc = wconv_ref[...]
    acc = (xpad[5:5 + blk, :] * wc[0:1] + xpad[6:6 + blk, :] * wc[1:2]
           + xpad[7:7 + blk, :] * wc[2:3] + qk_pre * wc[3:4])
    qk = acc * _sigmoid(acc)
    q_s[...] = qk[:, :M_WIDTH]
    k_s[...] = qk[:, M_WIDTH:] * (M_HD ** -0.5)
    last = n_valid if blk == L else blk
    conv_out_ref[0] = xpad[last:last + 8, :]
    xpad[0:8, :] = xpad[blk:blk + 8, :]

    row = _iota((L, L), 0)
    col = _iota((L, L), 1)
    causal = row >= col
    tril = causal.astype(F32)
    lane_w = _iota((1, M_WIDTH), 1) // M_HD
    row_w = _iota((M_WIDTH, 1), 0) // M_HD
    valid_col = _iota((L, 1), 0) < n_valid

    row8 = _iota((8, LANES), 0)

    n_chunks = blk // L
    unroll = CHUNK_UNROLL if n_chunks % CHUNK_UNROLL == 0 else 1
    H = range(M_HEADS)
    U = range(unroll)
    hmask = [lane_w == h for h in H]
    lst = slice(n_valid - 1, n_valid)

    def group(i, state):
        ct, n_row, m_tile = state
        r0 = [pl.multiple_of((i * unroll + u) * L, L) for u in U]
        qc = [q_s[pl.ds(r, L), :] for r in r0]
        kc = [k_s[pl.ds(r, L), :] for r in r0]
        act = [um_ref[0, pl.ds(r, L), 1024:1152] + bif_ref[...] for r in r0]
        bcum = [_dot_hi(tril, _log_sigmoid(x)) for x in act]
        act_t = [_transpose(x) for x in act]
        bcum_t = [_transpose(x) for x in bcum]
        k_bf = [x.astype(BF16) for x in kc]
        kt_bf = [_transpose(x).astype(BF16) for x in kc]
        bcol = [[bcum[u][:, 4 + h:5 + h] for h in H] for u in U]
        dmat = [[jnp.where(causal, bcol[u][h] - bcum_t[u][4 + h:5 + h, :] + act_t[u][h:h + 1, :], NEG) for h in H]
                for u in U]
        m_loc = [[jnp.max(dmat[u][h], axis=1, keepdims=True) for h in H] for u in U]
        q_h = [[jnp.where(hmask[h], qc[u], 0.0) for h in H] for u in U]
        q_bf = [[q_h[u][h].astype(BF16) for h in H] for u in U]
        s = [[_dot_nt(q_bf[u][h], k_bf[u]) * jnp.exp(dmat[u][h] - m_loc[u][h]) for h in H] for u in U]
        v_h = [[um_ref[0, pl.ds(r, L), 512 + h * M_HD:512 + (h + 1) * M_HD] for h in H] for r in r0]
        sv = [[_dot(s[u][h].astype(BF16), v_h[u][h].astype(BF16)) for h in H] for u in U]
        ssum = [[jnp.sum(s[u][h], axis=1, keepdims=True) for h in H] for u in U]
        w_l = [[jnp.where(valid_col, jnp.exp(bcol[u][h][lst] - bcol[u][h] + act[u][:, h:h + 1] - m_loc[u][h][lst]), 0.0)
                for h in H] for u in U]
        upd = [[_dot(kt_bf[u], (v_h[u][h] * w_l[u][h]).astype(BF16)) for h in H] for u in U]
        ksum = [[jnp.sum(kc[u] * w_l[u][h], axis=0, keepdims=True) for h in H] for u in U]
        gate = [[_sigmoid(um_ref[0, pl.ds(r, L), 768 + h * M_HD:768 + (h + 1) * M_HD]) for h in H] for r in r0]
        for u in U:
            ct_bf = ct.astype(BF16)
            m_inter = [m_tile[h:h + 1, 0:1] + bcol[u][h] for h in H]
            m_new = [jnp.maximum(m_inter[h], m_loc[u][h]) for h in H]
            f = [jnp.exp(m_loc[u][h] - m_new[h]) for h in H]
            a_inter = [jnp.exp(m_inter[h] - m_new[h]) for h in H]
            num = [a_inter[h] * _dot(q_bf[u][h], ct_bf) + f[h] * sv[u][h] for h in H]
            qn = [jnp.sum(q_h[u][h] * n_row, axis=1, keepdims=True) for h in H]
            den = [a_inter[h] * qn[h] + f[h] * ssum[u][h] for h in H]
            hh = [gate[u][h] * (num[h] / jnp.maximum(jnp.abs(den[h]), jnp.exp(-m_new[h]))) for h in H]
            ms = [jnp.mean(x * x, axis=1, keepdims=True) for x in hh]
            outs = [hh[h] * lax.rsqrt(ms[h] + EPS) * gh_ref[:, h * M_HD:(h + 1) * M_HD] for h in H]
            hm_ref[0, pl.ds(r0[u], L), :] = jnp.concatenate(outs, axis=1).astype(BF16)
            ct_old, n_old = ct, n_row
            for h in H:
                ct = jnp.where(row_w == h, a_inter[h][lst] * ct_old + f[h][lst] * upd[u][h], ct)
                n_row = jnp.where(hmask[h], a_inter[h][lst] * n_old + f[h][lst] * ksum[u][h], n_row)
                m_tile = jnp.where(row8 == h, m_new[h][lst], m_tile)
        return ct, n_row, m_tile

    ct, n_row, m_tile = lax.fori_loop(0, n_chunks // unroll, group, (ct_s[...], n_s[0:1, :], m_s[...]))
    ct_s[...] = ct
    n_s[...] = jnp.broadcast_to(n_row, n_s.shape)
    m_s[...] = m_tile

    @pl.when(j == pl.num_programs(1) - 1)
    def _():
        ct_out_ref[0] = ct
        n_out_ref[0] = jnp.broadcast_to(n_row, n_s.shape)
        m_out_ref[0] = m_tile


def _mlstm(um, conv0, ct0, n0, m0, wconv, bif, gh, *, L, n_valid):
    b, t, _ = um.shape
    blk = min(t, 512)
    kern = functools.partial(_mlstm_kernel, L=L, n_valid=n_valid, blk=blk)
    per_b = lambda shape: pl.BlockSpec((1,) + shape, lambda i, j: (i,) + (0,) * len(shape))
    const = lambda shape: pl.BlockSpec(shape, lambda i, j: (0,) * len(shape))
    return pl.pallas_call(
        kern,
        grid=(b, t // blk),
        in_specs=[pl.BlockSpec((1, blk, UM_W), lambda i, j: (i, j, 0)),
                  per_b((8, 2 * M_WIDTH)), per_b((M_WIDTH, M_HD)), per_b((8, M_WIDTH)), per_b((8, LANES)),
                  const((8, 2 * M_WIDTH)), const((1, LANES)), const((1, M_WIDTH))],
        out_specs=[pl.BlockSpec((1, blk, M_WIDTH), lambda i, j: (i, j, 0)),
                   per_b((8, 2 * M_WIDTH)), per_b((M_WIDTH, M_HD)), per_b((8, M_WIDTH)), per_b((8, LANES))],
        out_shape=[jax.ShapeDtypeStruct((b, t, M_WIDTH), BF16),
                   jax.ShapeDtypeStruct((b, 8, 2 * M_WIDTH), F32),
                   jax.ShapeDtypeStruct((b, M_WIDTH, M_HD), F32),
                   jax.ShapeDtypeStruct((b, 8, M_WIDTH), F32),
                   jax.ShapeDtypeStruct((b, 8, LANES), F32)],
        scratch_shapes=[pltpu.VMEM((blk + 8, 2 * M_WIDTH), F32),
                        pltpu.VMEM((blk, M_WIDTH), F32), pltpu.VMEM((blk, M_WIDTH), F32),
                        pltpu.VMEM((M_WIDTH, M_HD), F32), pltpu.VMEM((8, M_WIDTH), F32),
                        pltpu.VMEM((8, LANES), F32)],
        compiler_params=_cparams(("parallel", "arbitrary")),
        name="mlstm",
    )(um, conv0, ct0, n0, m0, wconv, bif, gh)


def _gla_kernel(ug_ref, s0_ref, w2_ref, bg_ref, gh_ref, hg_ref, s_out_ref, s_s, *, L, n_valid, blk):
    j = pl.program_id(1)

    @pl.when(j == 0)
    def _():
        s_s[...] = s0_ref[0]

    row = _iota((L, L), 0)
    col = _iota((L, L), 1)
    causal = row >= col
    tril = causal.astype(F32)
    lane_k = _iota((1, LANES), 1) // G_DK
    row_k = _iota((LANES, 1), 0) // G_DK
    valid_col = _iota((L, 1), 0) < n_valid
    mid = max(n_valid // 2, 1)

    n_chunks = blk // L
    unroll = CHUNK_UNROLL if n_chunks % CHUNK_UNROLL == 0 else 1
    H = range(G_HEADS)
    U = range(unroll)
    hmask = [lane_k == h for h in H]

    def group(i, s_all):
        r0 = [pl.multiple_of((i * unroll + u) * L, L) for u in U]
        q = [ug_ref[0, pl.ds(r, L), 0:128] * (G_DK ** -0.5) for r in r0]
        k = [ug_ref[0, pl.ds(r, L), 128:256] for r in r0]
        z = [_dot_hi(ug_ref[0, pl.ds(r, L), 768:896], w2_ref[...]) + bg_ref[...] for r in r0]
        g = [_log_sigmoid(x) * (1.0 / G_TAU) for x in z]
        bc = [_dot_hi(tril, x) for x in g]
        c_ref = [x[mid - 1:mid] for x in bc]
        last = [x[n_valid - 1:n_valid] for x in bc]
        qe = [q[u] * jnp.exp(bc[u] - c_ref[u]) for u in U]
        ke = [(k[u] * jnp.exp(c_ref[u] - bc[u])).astype(BF16) for u in U]
        qin = [q[u] * jnp.exp(bc[u]) for u in U]
        kd_t = [_transpose(jnp.where(valid_col, k[u] * jnp.exp(last[u] - bc[u]), 0.0)).astype(BF16) for u in U]
        decay = [jnp.exp(_transpose(jnp.broadcast_to(x, (8, LANES)))[:, 0:1]) for x in last]
        v_h = [[ug_ref[0, pl.ds(r, L), 256 + h * G_DV:256 + (h + 1) * G_DV].astype(BF16) for h in H] for r in r0]
        a = [[jnp.where(causal, _dot_nt(jnp.where(hmask[h], qe[u], 0.0).astype(BF16), ke[u]), 0.0) for h in H]
             for u in U]
        intra = [[_dot(a[u][h].astype(BF16), v_h[u][h]) for h in H] for u in U]
        q_in = [[jnp.where(hmask[h], qin[u], 0.0).astype(BF16) for h in H] for u in U]
        upd = [[_dot(kd_t[u], v_h[u][h]) for h in H] for u in U]
        gate = [[ug_ref[0, pl.ds(r, L), 512 + h * G_DV:512 + (h + 1) * G_DV] for h in H] for r in r0]
        gate = [[x * _sigmoid(x) * gh_ref[:, h * G_DV:(h + 1) * G_DV] for h, x in enumerate(gs)] for gs in gate]
        for u in U:
            s_bf = s_all.astype(BF16)
            o = [_dot(q_in[u][h], s_bf) + intra[u][h] for h in H]
            ms = [jnp.mean(x * x, axis=1, keepdims=True) for x in o]
            outs = [o[h] * lax.rsqrt(ms[h] + EPS) * gate[u][h] for h in H]
            hg_ref[0, pl.ds(r0[u], L), :] = jnp.concatenate(outs, axis=1).astype(BF16)
            s_all = decay[u] * s_all
            for h in H:
                s_all = s_all + jnp.where(row_k == h, upd[u][h], 0.0)
        return s_all

    s_fin = lax.fori_loop(0, n_chunks // unroll, group, s_s[...])
    s_s[...] = s_fin

    @pl.when(j == pl.num_programs(1) - 1)
    def _():
        s_out_ref[0] = s_fin


def _gla(ug, s0, w2p, bg, gh, *, L, n_valid):
    b, t, _ = ug.shape
    blk = min(t, 512)
    kern = functools.partial(_gla_kernel, L=L, n_valid=n_valid, blk=blk)
    return pl.pallas_call(
        kern,
        grid=(b, t // blk),
        in_specs=[pl.BlockSpec((1, blk, UG_W), lambda i, j: (i, j, 0)),
                  pl.BlockSpec((1, LANES, G_DV), lambda i, j: (i, 0, 0)),
                  pl.BlockSpec((LANES, LANES), lambda i, j: (0, 0)),
                  pl.BlockSpec((1, LANES), lambda i, j: (0, 0)),
                  pl.BlockSpec((1, G_WIDTH), lambda i, j: (0, 0))],
        out_specs=[pl.BlockSpec((1, blk, G_WIDTH), lambda i, j: (i, j, 0)),
                   pl.BlockSpec((1, LANES, G_DV), lambda i, j: (i, 0, 0))],
        out_shape=[jax.ShapeDtypeStruct((b, t, G_WIDTH), BF16),
                   jax.ShapeDtypeStruct((b, LANES, G_DV), F32)],
        scratch_shapes=[pltpu.VMEM((LANES, G_DV), F32)],
        compiler_params=_cparams(("parallel", "arbitrary")),
        name="gla",
    )(ug, s0, w2p, bg, gh)


def _group_mean_sq(x, bm_bf):
    return _dot_split(x * x, bm_bf)


def _rope_slab(y, cos, sin_signed):
    lane = _iota(y.shape, 1)
    rot = jnp.where((lane % N_HD) < (N_HD // 2), pltpu.roll(y, 96, axis=1), pltpu.roll(y, 32, axis=1))
    return y * cos + rot * sin_signed


def _norm_rope_slab(x, g, cos, sin_signed, bm_bf):
    y = x * lax.rsqrt(_group_mean_sq(x, bm_bf) + EPS) * g
    return _rope_slab(y, cos, sin_signed)


def _rows_kernel(nq_ref, nsw_ref, ngate_ref, cos_ref, sin_ref, gq_ref, gks_ref, gkw_ref, bm_ref,
                 q_ref, ks_ref, vs_ref, kw_ref, vw_ref, slc_ref, win_ref, gates_ref):
    cos = cos_ref[...]
    sin = sin_ref[...]
    bm = bm_ref[...]
    low = _iota((1, LANES), 1) < N_HD
    for sl in range(4):
        x = nq_ref[0, :, sl * LANES:(sl + 1) * LANES]
        y = _norm_rope_slab(x, gq_ref[:, sl * LANES:(sl + 1) * LANES], cos, sin, bm) * (N_HD ** -0.5 * LOG2E)
        y_sw = pltpu.roll(y, N_HD, axis=1)
        if sl < 2:
            even, odd = jnp.where(low, y, 0.0), jnp.where(low, y_sw, 0.0)
        else:
            even, odd = jnp.where(low, 0.0, y_sw), jnp.where(low, 0.0, y)
        q_ref[0, 2 * sl] = even.astype(BF16)
        q_ref[0, 2 * sl + 1] = odd.astype(BF16)
    ks = _norm_rope_slab(nsw_ref[0, :, 0:128], gks_ref[...], cos, sin, bm)
    vs = nsw_ref[0, :, 128:256]
    kw = _norm_rope_slab(nsw_ref[0, :, 256:384], gkw_ref[...], cos, sin, bm)
    vw = nsw_ref[0, :, 384:512]
    slc_ref[0, :, 0:128] = ks
    slc_ref[0, :, 128:256] = vs
    win_ref[0, :, 0:128] = kw
    win_ref[0, :, 128:256] = vw
    for ref, val in ((ks_ref, ks), (vs_ref, vs), (kw_ref, kw), (vw_ref, vw)):
        ref[0] = val.astype(BF16)
    gt = _sigmoid(ngate_ref[0])
    gates_ref[0, 0] = gt
    gates_ref[0, 1] = pltpu.roll(gt, LANES - 3 * N_REP, axis=1)


def _rows(nq, nsw, ngate, cos, sin, gq, gks, gkw, bm):
    b, t, _ = nq.shape
    tm = min(t, 512)
    tok = lambda w: pl.BlockSpec((1, tm, w), lambda i, j: (i, j, 0))
    const = lambda shape: pl.BlockSpec(shape, lambda i, j: (0,) * len(shape))
    kv_shape = jax.ShapeDtypeStruct((b, t, LANES), BF16)
    return pl.pallas_call(
        _rows_kernel,
        grid=(b, t // tm),
        in_specs=[tok(NQ_W), tok(NSW_W), tok(NGATE_W),
                  pl.BlockSpec((tm, LANES), lambda i, j: (j, 0)),
                  pl.BlockSpec((tm, LANES), lambda i, j: (j, 0)),
                  const((1, NQ_W)), const((1, LANES)), const((1, LANES)), const((LANES, LANES))],
        out_specs=[pl.BlockSpec((1, N_HEADS, tm, LANES), lambda i, j: (i, 0, j, 0)),
                   tok(LANES), tok(LANES), tok(LANES), tok(LANES), tok(256), tok(256),
                   pl.BlockSpec((1, N_KV, tm, LANES), lambda i, j: (i, 0, j, 0))],
        out_shape=[jax.ShapeDtypeStruct((b, N_HEADS, t, LANES), BF16), kv_shape, kv_shape, kv_shape, kv_shape,
                   jax.ShapeDtypeStruct((b, t, 256), F32), jax.ShapeDtypeStruct((b, t, 256), F32),
                   jax.ShapeDtypeStruct((b, N_KV, t, LANES), F32)],
        compiler_params=_cparams(("parallel", "parallel")),
        name="nsa_rows",
    )(nq, nsw, ngate, cos, sin, gq, gks, gkw, bm)


def _block_mlp(x_bf, w1, pe, w2, prev_first):
    p = _dot(x_bf, w1)
    bias = _dot(pe.astype(BF16), w1)
    pa = p[:, :256] + bias[0:1, :256]
    pb = p[:, 256:] + bias[1:2, 256:]
    pa_prev = jnp.where(_iota(pa.shape, 0) == 0, prev_first, pltpu.roll(pa, 1, axis=0))
    pre = pa_prev + pb
    hid = (pre * _sigmoid(pre)).astype(BF16)
    return _dot(hid, w2), pa[pa.shape[0] - 1:, :]


def _compress_kernel(u_ref, w1_ref, pe_ref, w2_ref, gkc_ref, cos_ref, sin_ref, bm_ref, kc_ref, vc_ref):
    outs = []
    for kv in range(2):
        x = jnp.concatenate(
            [u_ref[0, :, j * 256 + kv * LANES:j * 256 + (kv + 1) * LANES] for j in range(CMP_STRIDE)],
            axis=1).astype(BF16)
        outs.append(_block_mlp(x, w1_ref[kv], pe_ref[kv], w2_ref[kv], jnp.zeros((1, 256), F32))[0])
    kc_ref[0] = _norm_rope_slab(outs[0], gkc_ref[...], cos_ref[...], sin_ref[...], bm_ref[...]).astype(BF16)
    vc_ref[0] = outs[1].astype(BF16)


CMP_PAGES = 32
PAGE_UNITS = PAGE_SIZE // CMP_STRIDE


def _compress_paged_kernel(pt_ref, *refs):
    pages = refs[:CMP_PAGES]
    (w1_ref, pe_ref, w2_ref, gkc_ref, cos_ref, sin_ref, bm_ref, kc_ref, vc_ref, t_scr, x_scr, carry_scr) = refs[CMP_PAGES:]

    @pl.when(pl.program_id(1) == 0)
    def _():
        carry_scr[...] = jnp.zeros(carry_scr.shape, F32)

    outs = []
    for kv in range(2):
        for i in range(CMP_PAGES):
            t_scr[i] = pages[i][0, 0, kv].reshape(LANES, PAGE_SIZE).T
            for j in range(CMP_STRIDE):
                x_scr[i * PAGE_UNITS:(i + 1) * PAGE_UNITS, j * LANES:(j + 1) * LANES] = \
                    t_scr[i, pl.ds(j, PAGE_UNITS, stride=CMP_STRIDE), :]
        out, last = _block_mlp(x_scr[...].astype(BF16), w1_ref[kv], pe_ref[kv], w2_ref[kv], carry_scr[kv, 0:1, :])
        carry_scr[kv, 0:1, :] = last
        outs.append(out)
    kc_ref[0] = _norm_rope_slab(outs[0], gkc_ref[...], cos_ref[...], sin_ref[...], bm_ref[...]).astype(BF16)
    vc_ref[0] = outs[1].astype(BF16)


def _compress_paged(page_table, cache_t, layer, w1r, pe_r, w2r, gkc, cos_c, sin_c, bm):
    db, n_pages = page_table.shape
    step_units = CMP_PAGES * PAGE_UNITS
    nu = n_pages * PAGE_UNITS

    def page_spec(i):
        return pl.BlockSpec((1, 1, 2, N_KV, N_HD, PAGE_SIZE),
                            lambda b, c, pt: (layer, pt[b, c * CMP_PAGES + i], 0, 0, 0, 0))

    const = lambda shape: pl.BlockSpec(shape, lambda b, c, pt: (0,) * len(shape))
    tab = pl.BlockSpec((step_units, LANES), lambda b, c, pt: (c, 0))
    out = jax.ShapeDtypeStruct((db, nu, LANES), BF16)
    ospec = pl.BlockSpec((1, step_units, LANES), lambda b, c, pt: (b, c, 0))
    return pl.pallas_call(
        _compress_paged_kernel,
        grid_spec=pltpu.PrefetchScalarGridSpec(
            num_scalar_prefetch=1,
            grid=(db, n_pages // CMP_PAGES),
            in_specs=[page_spec(i) for i in range(CMP_PAGES)] + [
                const((2, 2048, 512)), const((2, 8, 2048)), const((2, 256, LANES)),
                const((1, LANES)), tab, tab, const((LANES, LANES))],
            out_specs=[ospec, ospec],
            scratch_shapes=[pltpu.VMEM((CMP_PAGES, PAGE_SIZE, LANES), F32),
                            pltpu.VMEM((step_units, CMP_STRIDE * LANES), F32),
                            pltpu.VMEM((2, 8, 256), F32)]),
        out_shape=[out, out],
        compiler_params=_cparams(("parallel", "arbitrary")),
        name="nsa_compress_paged",
    )(page_table, *([cache_t] * CMP_PAGES), w1r, pe_r, w2r, gkc, cos_c, sin_c, bm)


def _compress(units, w1r, pe_r, w2r, gkc, cos_c, sin_c, bm):
    b, nu, _ = units.shape
    const = lambda shape: pl.BlockSpec(shape, lambda i: (0,) * len(shape))
    out = jax.ShapeDtypeStruct((b, nu, LANES), BF16)
    return pl.pallas_call(
        _compress_kernel,
        grid=(b,),
        in_specs=[pl.BlockSpec((1, nu, 4096), lambda i: (i, 0, 0)),
                  const((2, 2048, 512)), const((2, 8, 2048)), const((2, 256, LANES)),
                  const((1, LANES)), const((nu, LANES)), const((nu, LANES)), const((LANES, LANES))],
        out_specs=[pl.BlockSpec((1, nu, LANES), lambda i: (i, 0, 0))] * 2,
        out_shape=[out, out],
        compiler_params=_cparams(("parallel",)),
        name="nsa_compress",
    )(units, w1r, pe_r, w2r, gkc, cos_c, sin_c, bm)


def _cmp_kernel(q_ref, kc_ref, vc_ref, c2s_ref, oc_ref, sel_ref, *, tq, pos0, n_cmp, nsel):
    qi = pl.program_id(1)
    nc_pad = kc_ref.shape[1]
    q = q_ref[0].reshape(N_HEADS * tq, LANES)
    s = _dot_nt(q, kc_ref[0]).reshape(N_HEADS, tq, nc_pad)
    qpos = pos0 + qi * tq + _iota((tq, 1), 0)
    c_idx = _iota((tq, nc_pad), 1)
    c_real = jnp.where(c_idx >= 1, c_idx, nc_pad + n_cmp) <= n_cmp
    c_ok = jnp.where(c_real, c_idx * CMP_STRIDE + (CMP_LEN - CMP_STRIDE - 1), jnp.int32(2 ** 30)) <= qpos
    s = s + jnp.where(c_ok, 0.0, NEG)[None]
    p = jnp.exp2(s - jnp.max(s, axis=-1, keepdims=True))
    p = p * ((qpos >= CMP_LEN - 1).astype(F32)[None] / jnp.sum(p, axis=-1, keepdims=True))
    oc_ref[0] = _dot(p.reshape(N_HEADS * tq, nc_pad).astype(BF16), vc_ref[0]).reshape(N_HEADS, tq, LANES)
    psum = jnp.sum(p.reshape(N_KV, N_REP, tq, nc_pad), axis=1).reshape(N_KV * tq, nc_pad)
    imp = _dot_split(psum, c2s_ref[...])
    work = _transpose2d(imp)
    ncol = N_KV * tq
    qpos_row = pos0 + qi * tq + _iota((1, ncol), 1) % tq
    blk = _iota((nsel, ncol), 0)
    cur = qpos_row // SLC_BLOCK
    forced = jnp.where(blk == 0, 1.0, jnp.where(blk == cur, 1.0, jnp.where(blk == cur - 1, 1.0, 0.0)))
    work = jnp.where(blk * SLC_BLOCK <= qpos_row, work + FORCE_BONUS * forced, NEG)
    blk_f = blk.astype(F32)
    sel = jnp.zeros((nsel, ncol), F32)
    for _ in range(SLC_TOPK):
        m = jnp.max(work, axis=0, keepdims=True)
        idx = jnp.min(jnp.where(work == m, blk_f, float(nsel)), axis=0, keepdims=True)
        hit = blk_f == idx
        sel = jnp.where(hit, 1.0, sel)
        work = jnp.where(hit, -jnp.inf, work)
    sel_ref[0] = _transpose2d(sel).reshape(N_KV, tq, nsel).astype(BF16)


def _cmp_attn(q_pad, kc, vc, c2s, *, tq, pos0, n_cmp):
    b, _, t, _ = q_pad.shape
    nc_pad = kc.shape[1]
    nsel = c2s.shape[1]
    kern = functools.partial(_cmp_kernel, tq=tq, pos0=pos0, n_cmp=n_cmp, nsel=nsel)
    return pl.pallas_call(
        kern,
        grid=(b, t // tq),
        in_specs=[pl.BlockSpec((1, N_HEADS, tq, LANES), lambda i, j: (i, 0, j, 0)),
                  pl.BlockSpec((1, nc_pad, LANES), lambda i, j: (i, 0, 0)),
                  pl.BlockSpec((1, nc_pad, LANES), lambda i, j: (i, 0, 0)),
                  pl.BlockSpec((nc_pad, nsel), lambda i, j: (0, 0))],
        out_specs=[pl.BlockSpec((1, N_HEADS, tq, LANES), lambda i, j: (i, 0, j, 0)),
                   pl.BlockSpec((1, N_KV, tq, nsel), lambda i, j: (i, 0, j, 0))],
        out_shape=[jax.ShapeDtypeStruct((b, N_HEADS, t, LANES), F32),
                   jax.ShapeDtypeStruct((b, N_KV, t, nsel), BF16)],
        compiler_params=_cparams(("parallel", "parallel")),
        name="nsa_cmp_topk",
    )(q_pad, kc, vc, c2s)


SLAB = 64


def _tile_scores(q_parts, k, s_scr, *, tq, width):
    for r in range(N_REP):
        s_scr[r * tq:(r + 1) * tq, 0:width] = _dot_nt(q_parts[r], k)


def _tile_update(v, bias_scr, s_scr, p_scr, m_scr, l_scr, alpha_scr, acc_scr, *, tq, width):
    nch = width // LANES

    def chunk(rows, i, c):
        x = s_scr[rows, c * LANES:(c + 1) * LANES]
        if bias_scr is not None:
            x = x + bias_scr[i * SLAB:(i + 1) * SLAB, c * LANES:(c + 1) * LANES]
        return x

    slabs = [(slice(r * tq + i * SLAB, r * tq + (i + 1) * SLAB), i) for r in range(N_REP) for i in range(tq // SLAB)]
    for rows, i in slabs:
        mx = chunk(rows, i, 0)
        for c in range(1, nch):
            mx = jnp.maximum(mx, chunk(rows, i, c))
        m_old = m_scr[rows, :]
        m_new = jnp.maximum(m_old, jnp.max(mx, axis=1, keepdims=True))
        alpha_scr[rows, :] = jnp.exp2(m_old - m_new)
        m_scr[rows, :] = m_new
    for rows, i in slabs:
        m_new = m_scr[rows, :]
        lsum = alpha_scr[rows, :] * l_scr[rows, :]
        for c in range(nch):
            p = jnp.exp2(chunk(rows, i, c) - m_new)
            lsum = lsum + p
            p_scr[rows, c * LANES:(c + 1) * LANES] = p.astype(BF16)
        l_scr[rows, :] = lsum
    for r in range(N_REP):
        rs = slice(r * tq, (r + 1) * tq)
        acc_scr[rs, :] = alpha_scr[rs, :] * acc_scr[rs, :] + _dot(p_scr[rs, 0:width], v)


def _slcwin_kernel(q_ref, ks_ref, vs_ref, kw_ref, vw_ref, sel_ref, et_ref, oc_ref, gates_ref, hn_ref,
                   s_scr, bias_scr, p_scr, m_scr, l_scr, alpha_scr, acc_scr, ow_scr, *, tq):
    g = pl.program_id(1)
    qi = pl.program_id(2)
    rows = N_REP * tq
    q0 = qi * tq
    qpos = q0 + _iota((tq, 1), 0)
    scr = dict(p_scr=p_scr, m_scr=m_scr, l_scr=l_scr, alpha_scr=alpha_scr, acc_scr=acc_scr, tq=tq)

    def reset():
        m_scr[...] = jnp.full((rows, LANES), NEG, F32)
        l_scr[...] = jnp.zeros((rows, LANES), F32)
        acc_scr[...] = jnp.zeros((rows, LANES), F32)

    def result():
        return acc_scr[...] / jnp.sum(l_scr[...], axis=1, keepdims=True)

    wlen = WINDOW + tq
    start = pl.multiple_of(jnp.maximum(q0 - WINDOW, 0), tq)
    reset()
    dist = qpos - (start + _iota((tq, wlen), 1))
    bias_scr[:, 0:wlen] = jnp.where(jnp.where(dist >= 0, dist, WINDOW) < WINDOW, 0.0, NEG)
    _tile_scores([q_ref[0, r] for r in range(N_REP)], kw_ref[0, pl.ds(start, wlen), :], s_scr, tq=tq, width=wlen)
    _tile_update(vw_ref[0, pl.ds(start, wlen), :], bias_scr, s_scr, width=wlen, **scr)
    ow_scr[...] = result()

    reset()
    sel_m = ((sel_ref[0, 0].astype(F32) - 1.0) * (-NEG)).astype(BF16)
    q_aug = [jnp.concatenate([q_ref[0, r], sel_m], axis=1) for r in range(N_REP)]
    n_kv = (q0 + tq + KV_TILE - 1) // KV_TILE

    def scores(j, dst):
        k0 = pl.multiple_of(j * KV_TILE, KV_TILE)
        k_aug = jnp.concatenate([ks_ref[0, pl.ds(k0, KV_TILE), :], et_ref[pl.ds(k0, KV_TILE), :]], axis=1)
        _tile_scores(q_aug, k_aug, dst, tq=tq, width=KV_TILE)

    def values(j):
        return vs_ref[0, pl.ds(pl.multiple_of(j * KV_TILE, KV_TILE), KV_TILE), :]

    n_plain = n_kv - 1

    def body(j, carry):
        scores(j, s_scr)
        _tile_update(values(j), None, s_scr, width=KV_TILE, **scr)
        return carry

    lax.fori_loop(0, n_plain, body, 0)
    bias_scr[:, 0:KV_TILE] = jnp.where(n_plain * KV_TILE + _iota((tq, KV_TILE), 1) <= qpos, 0.0, NEG)
    scores(n_plain, s_scr)
    _tile_update(values(n_plain), bias_scr, s_scr, width=KV_TILE, **scr)
    o_s_all = result()

    gt = gates_ref[0, 0]
    low = _iota((1, LANES), 1) < N_HD
    vals = []
    for r in range(N_REP):
        rs = slice(r * tq, (r + 1) * tq)
        vals.append(gt[:, 3 * r:3 * r + 1] * oc_ref[0, r] + gt[:, 3 * r + 1:3 * r + 2] * o_s_all[rs]
                    + gt[:, 3 * r + 2:3 * r + 3] * ow_scr[rs, :])
    for pair in range(N_REP // 2):
        a, b = vals[2 * pair], vals[2 * pair + 1]
        a_sw, b_sw = pltpu.roll(a, N_HD, axis=1), pltpu.roll(b, N_HD, axis=1)
        lo = jnp.where(g == 0, a, a_sw)
        hi = jnp.where(g == 0, b_sw, b)
        hn_ref[0, :, pair * LANES:(pair + 1) * LANES] = jnp.where(low, lo, hi).astype(BF16)


def _slcwin(q_pad, ks, vs, kw, vw, sel, e_t, oc, gates, *, tq):
    b, _, t, _ = q_pad.shape
    nsel = sel.shape[3]
    assert nsel == LANES and e_t.shape == (t, LANES)
    rows = N_REP * tq
    wlen = WINDOW + tq
    kern = functools.partial(_slcwin_kernel, tq=tq)
    qspec = pl.BlockSpec((1, N_REP, tq, LANES), lambda i, g, j: (i, g, j, 0))
    full = pl.BlockSpec((1, t, LANES), lambda i, g, j: (i, 0, 0))
    return pl.pallas_call(
        kern,
        grid=(b, N_KV, t // tq),
        in_specs=[qspec, full, full, full, full,
                  pl.BlockSpec((1, 1, tq, nsel), lambda i, g, j: (i, g, j, 0)),
                  pl.BlockSpec((t, LANES), lambda i, g, j: (0, 0)),
                  qspec,
                  pl.BlockSpec((1, 1, tq, LANES), lambda i, g, j: (i, g, j, 0))],
        out_specs=pl.BlockSpec((1, tq, N_REP * N_HD), lambda i, g, j: (i, j, g)),
        out_shape=jax.ShapeDtypeStruct((b, t, N_WIDTH), BF16),
        scratch_shapes=[pltpu.VMEM((rows, wlen), F32), pltpu.VMEM((tq, wlen), F32), pltpu.VMEM((rows, wlen), BF16),
                        pltpu.VMEM((rows, LANES), F32), pltpu.VMEM((rows, LANES), F32), pltpu.VMEM((rows, LANES), F32),
                        pltpu.VMEM((rows, LANES), F32), pltpu.VMEM((rows, LANES), F32)],
        compiler_params=_cparams(("parallel", "parallel", "parallel")),
        name="nsa_slc_win",
    )(q_pad, ks, vs, kw, vw, sel, e_t, oc, gates)


SLC_PAGES = 32


def _heads_bias(b2, ts):
    w = b2.shape[1]
    return jnp.broadcast_to(b2.reshape(N_KV, 1, ts, w), (N_KV, N_REP, ts, w)).reshape(N_HEADS * ts, w)


def _slc_sample_kernel(pt_ref, *refs, ts):
    pages = refs[:SLC_PAGES]
    (q_ref, sel_ref, e_ref, enew_ref, newslc_ref, win_ref, newwin_ref, oc_ref, gates_ref,
     hn_ref, m_scr, l_scr, acc_scr) = refs[SLC_PAGES:]
    c = pl.program_id(1)
    rows = N_HEADS * ts
    q = q_ref[0].astype(F32)[:, 0:ts, :].reshape(rows, LANES).astype(BF16)
    sel2 = sel_ref[0].astype(F32)[:, 0:ts, :].reshape(N_KV * ts, sel_ref.shape[3]).astype(BF16)

    @pl.when(c == 0)
    def _():
        m_scr[...] = jnp.full((rows, 1), NEG, F32)
        l_scr[...] = jnp.zeros((rows, 1), F32)
        acc_scr[...] = jnp.zeros((rows, LANES), F32)

    def online(s, pv_fn):
        m_old = m_scr[...]
        m_new = jnp.maximum(m_old, jnp.max(s, axis=1, keepdims=True))
        p = jnp.exp2(s - m_new)
        alpha = jnp.exp2(m_old - m_new)
        l_scr[...] = alpha * l_scr[...] + jnp.sum(p, axis=1, keepdims=True)
        acc_scr[...] = alpha * acc_scr[...] + pv_fn(p.astype(BF16))
        m_scr[...] = m_new

    picked = _dot(sel2, e_ref[...])
    bias = _heads_bias(picked * (-NEG) + NEG, ts)
    k_t = jnp.concatenate([pg[0, 0, 0].reshape(LANES, PAGE_SIZE).astype(BF16) for pg in pages], axis=1)
    v_t = jnp.concatenate([pg[0, 0, 1].reshape(LANES, PAGE_SIZE).astype(BF16) for pg in pages], axis=1)
    online(_dot(q, k_t) + bias, lambda p: _dot_nt(p, v_t))

    @pl.when(c == pl.num_programs(1) - 1)
    def _():
        tok = _iota((N_KV * ts, 1), 0) % ts
        key = _iota((N_KV * ts, PAGE_SIZE), 1)
        new = newslc_ref[0].astype(BF16)
        picked_n = _dot(sel2, enew_ref[...])
        bias_n = _heads_bias(jnp.where(key <= tok, picked_n, 0.0) * (-NEG) + NEG, ts)
        online(_dot_nt(q, new[:, :LANES]) + bias_n, lambda p: _dot(p, new[:, LANES:]))
        kw_t = win_ref[0, 0, 0].reshape(LANES, WINDOW).astype(BF16)
        vw_t = win_ref[0, 0, 1].reshape(LANES, WINDOW).astype(BF16)
        nwin = newwin_ref[0].astype(BF16)
        wkey = _iota((N_KV * ts, WINDOW), 1)
        bias_w = jnp.concatenate([jnp.where(wkey > tok, 0.0, NEG), jnp.where(key <= tok, 0.0, NEG)], axis=1)
        s_w = jnp.concatenate([_dot(q, kw_t), _dot_nt(q, nwin[:, :LANES])], axis=1) + _heads_bias(bias_w, ts)
        p_w = jnp.exp2(s_w - jnp.max(s_w, axis=1, keepdims=True))
        l_w = jnp.sum(p_w, axis=1, keepdims=True)
        p_w = p_w.astype(BF16)
        o_w = (_dot_nt(p_w[:, :WINDOW], vw_t) + _dot(p_w[:, WINDOW:], nwin[:, LANES:])) / l_w
        o_s = acc_scr[...] / l_scr[...]
        low = _iota((1, LANES), 1) < N_HD
        vals = []
        for h in range(N_HEADS):
            g, r = divmod(h, N_REP)
            gt = gates_ref[0, g, 0:ts, :]
            rs = slice(h * ts, (h + 1) * ts)
            vals.append(gt[:, 3 * r:3 * r + 1] * oc_ref[0, h, 0:ts, :] + gt[:, 3 * r + 1:3 * r + 2] * o_s[rs]
                        + gt[:, 3 * r + 2:3 * r + 3] * o_w[rs])
        slabs = []
        for pair in range(N_HEADS // 2):
            a, b = vals[2 * pair], vals[2 * pair + 1]
            if pair < N_REP // 2:
                lo, hi = a, pltpu.roll(b, N_HD, axis=1)
            else:
                lo, hi = pltpu.roll(a, N_HD, axis=1), b
            slabs.append(jnp.where(low, lo, hi))
        out = jnp.concatenate(slabs, axis=1)
        pad = jnp.zeros((hn_ref.shape[1] - ts, N_WIDTH), F32)
        hn_ref[0] = jnp.concatenate([out, pad], axis=0).astype(BF16)


def _slc_sample(page_table, cache_t, layer, q_pad, sel, e_main, e_new, new_slc, win_t, new_win, oc, gates, *, ts):
    db, n_pages = page_table.shape
    tpad = q_pad.shape[2]
    nsel = sel.shape[3]
    n_steps = n_pages // SLC_PAGES
    step_keys = SLC_PAGES * PAGE_SIZE
    rows = N_HEADS * ts

    def page_spec(i):
        return pl.BlockSpec((1, 1, 2, N_KV, N_HD, PAGE_SIZE),
                            lambda b, c, pt: (layer, pt[b, c * SLC_PAGES + i], 0, 0, 0, 0))

    per_b = lambda shape: pl.BlockSpec((1,) + shape, lambda b, c, pt: (b,) + (0,) * len(shape))
    return pl.pallas_call(
        functools.partial(_slc_sample_kernel, ts=ts),
        grid_spec=pltpu.PrefetchScalarGridSpec(
            num_scalar_prefetch=1,
            grid=(db, n_steps),
            in_specs=[page_spec(i) for i in range(SLC_PAGES)] + [
                per_b((N_HEADS, tpad, LANES)), per_b((N_KV, tpad, nsel)),
                pl.BlockSpec((nsel, step_keys), lambda b, c, pt: (0, c)),
                pl.BlockSpec((nsel, PAGE_SIZE), lambda b, c, pt: (0, 0)),
                per_b((PAGE_SIZE, 256)),
                pl.BlockSpec((1, 1, 2, N_KV, N_HD, WINDOW), lambda b, c, pt: (layer, b, 0, 0, 0, 0)),
                per_b((PAGE_SIZE, 256)), per_b((N_HEADS, tpad, LANES)), per_b((N_KV, tpad, LANES))],
            out_specs=per_b((tpad, N_WIDTH)),
            scratch_shapes=[pltpu.VMEM((rows, 1), F32), pltpu.VMEM((rows, 1), F32), pltpu.VMEM((rows, LANES), F32)]),
        out_shape=jax.ShapeDtypeStruct((db, tpad, N_WIDTH), BF16),
        compiler_params=_cparams(("parallel", "arbitrary")),
        name="nsa_slc_sample",
    )(page_table, *([cache_t] * SLC_PAGES), q_pad, sel, e_main, e_new, new_slc, win_t, new_win, oc, gates)


def _rope_tables(pos):
    half = N_HD // 2
    inv = ROPE_THETA ** (-jnp.arange(half, dtype=F32) / half)
    ang = pos.astype(F32)[:, None] * inv[None, :]
    cos, sin = jnp.cos(ang), jnp.sin(ang)
    return jnp.tile(jnp.concatenate([cos, cos], axis=1), (1, 2)), jnp.tile(jnp.concatenate([-sin, sin], axis=1), (1, 2))


def _pad_cols(a, w):
    return jnp.pad(a, ((0, 0), (0, w - a.shape[1])))


def _prep_layer(l, w_norm_mix, w_in, b_mlstm_if, w_mlstm_conv, w_mlstm_hnorm, w_gla_gate2, b_gla_gate,
                w_gla_hnorm, w_qk_norm, w_cmp_pe, w_cmp_1, w_cmp_2, w_out, w_norm_ffn, w_ffn_up, w_ffn_down):
    wi = w_in[l]
    col = lambda n: wi[:, _OFF[n][0]:_OFF[n][0] + _OFF[n][1]]
    um = _pad_cols(jnp.concatenate([col('m_qk'), col('m_v'), col('m_o'), col('m_i'), col('m_f')], axis=1), UM_W)
    ug = _pad_cols(jnp.concatenate([col('g_q'), col('g_k'), col('g_v'), col('g_o'), col('g_lr')], axis=1), UG_W)
    ncmp = jnp.concatenate([col('n_kc'), col('n_vc')], axis=1)
    nsw = jnp.concatenate([col('n_ks'), col('n_vs'), col('n_kw'), col('n_vw')], axis=1)
    ngate = _pad_cols(col('n_gate'), NGATE_W)
    p = {}
    p['w_cat'] = jnp.concatenate([um, ug, col('n_q'), ncmp, nsw, ngate], axis=1).astype(BF16)
    p['g_mix'] = w_norm_mix[l][None, :]
    p['wconv'] = jnp.pad(w_mlstm_conv[l], ((0, 8 - M_CONV), (0, 0)))
    p['bif'] = _pad_cols(b_mlstm_if[l][None, :], LANES)
    p['gh_m'] = w_mlstm_hnorm[l][None, :]
    p['w2p'] = jnp.pad(w_gla_gate2[l], ((0, LANES - G_RANK), (0, 0)))
    p['bg'] = b_gla_gate[l][None, :]
    p['gh_g'] = w_gla_hnorm[l][None, :]
    gqk = w_qk_norm[l]
    p['gq'] = jnp.tile(gqk[0], N_HEADS)[None, :]
    p['gkc'] = jnp.tile(gqk[1], N_KV)[None, :]
    p['gks'] = jnp.tile(gqk[2], N_KV)[None, :]
    p['gkw'] = jnp.tile(gqk[3], N_KV)[None, :]
    w1 = w_cmp_1[l].reshape(2, 2, CMP_STRIDE, N_HD, CMP_HIDDEN)
    eye = jnp.eye(N_KV, dtype=F32)
    w1r = jnp.einsum('khjdc,gf->kjgdhfc', w1, eye).reshape(2, CMP_STRIDE * N_KV * N_HD, 2 * N_KV * CMP_HIDDEN)
    p['w1r'] = w1r.astype(BF16)
    pe = w_cmp_pe[l].reshape(2, 2, CMP_STRIDE, 1, N_HD)
    pe = jnp.broadcast_to(pe, (2, 2, CMP_STRIDE, N_KV, N_HD)).reshape(2, 2, 2048)
    p['pe_r'] = jnp.pad(pe, ((0, 0), (0, 6), (0, 0)))
    p['w2r'] = jnp.einsum('kcd,gf->kgcfd', w_cmp_2[l], eye).reshape(2, N_KV * CMP_HIDDEN, N_KV * N_HD).astype(BF16)
    p['w_out'] = w_out[l].astype(BF16)
    p['g_ffn'] = w_norm_ffn[l][None, :]
    nf = 2
    tf = D_FF // nf
    wu = w_ffn_up[l]
    p['w_up_r'] = jnp.stack([jnp.concatenate([wu[:, f * tf:(f + 1) * tf], wu[:, D_FF + f * tf:D_FF + (f + 1) * tf]],
                                             axis=1) for f in range(nf)]).astype(BF16)
    p['w_dn_r'] = w_ffn_down[l].reshape(nf, tf, D_MODEL).astype(BF16)
    return p


def _cmp2slc(n_cmp, nc_pad, nsel):
    m = np.zeros((nc_pad, nsel), np.float32)
    per = SLC_BLOCK // CMP_STRIDE
    for n in range(n_cmp):
        for u in range(CMP_LEN // CMP_STRIDE):
            m[n + 1, (n + u) // per] += 1.0
    return jnp.asarray(m, BF16)


def _expand_mat(nsel, kvlen):
    return jnp.asarray((np.arange(kvlen)[None, :] // SLC_BLOCK) == np.arange(nsel)[:, None], BF16)


def _group_mean_mat():
    idx = np.arange(LANES) // N_HD
    return jnp.asarray((idx[:, None] == idx[None, :]) / float(N_HD), BF16)


def _mixers(p, um, ug, mstate, gstate, *, L, n_valid):
    conv0, ct0, n0, m0 = mstate
    hm, conv_o, ct_o, n_o, m_o = _mlstm(um, conv0, ct0, n0, m0, p['wconv'], p['bif'], p['gh_m'], L=L, n_valid=n_valid)
    hg, s_o = _gla(ug, gstate, p['w2p'], p['bg'], p['gh_g'], L=L, n_valid=n_valid)
    return hm, hg, (conv_o, ct_o, n_o, m_o), s_o


def _mlstm_state_in(conv, c, n, m):
    b = conv.shape[0]
    conv0 = jnp.pad(conv.astype(F32), ((0, 0), (8 - (M_CONV - 1), 0), (0, 0)))
    ct0 = jnp.swapaxes(c.astype(F32), -1, -2).reshape(b, M_WIDTH, M_HD)
    n0 = jnp.broadcast_to(n.astype(F32).reshape(b, 1, M_WIDTH), (b, 8, M_WIDTH))
    m0 = jnp.broadcast_to(jnp.pad(m.astype(F32), ((0, 0), (0, 8 - M_HEADS)))[:, :, None], (b, 8, LANES))
    return conv0, ct0, n0, m0


def _mlstm_state_out(conv_o, ct_o, n_o, m_o):
    b = conv_o.shape[0]
    c = jnp.swapaxes(ct_o.reshape(b, M_HEADS, M_HD, M_HD), -1, -2)
    return c, n_o[:, 0].reshape(b, M_HEADS, M_HD), m_o[:, :M_HEADS, 0], conv_o[:, 8 - (M_CONV - 1):]


def kernel(x_prompt, x_sample, cache_cmp_kv, cache_slc_kv, state_win_kv, state_mlstm_C, state_mlstm_n,
           state_mlstm_m, state_mlstm_conv, state_gla_S, page_table, w_norm_mix, w_in, b_mlstm_if,
           w_mlstm_conv, w_mlstm_hnorm, w_gla_gate2, b_gla_gate, w_gla_hnorm, w_qk_norm, w_cmp_pe,
           w_cmp_1, w_cmp_2, w_out, w_norm_ffn, w_ffn_up, w_ffn_down):
    b, t, _ = x_prompt.shape
    db, td, _ = x_sample.shape
    depth = w_in.shape[0]
    n_pages = page_table.shape[1]
    past = n_pages * PAGE_SIZE
    win_buf = state_win_kv.shape[2]
    assert t % 512 == 0 and td < CMP_STRIDE and td <= SAMPLE_T and win_buf == WINDOW
    assert n_pages % SLC_PAGES == 0 and n_pages % CMP_PAGES == 0

    bm = _group_mean_mat()
    cos_p, sin_p = _rope_tables(jnp.arange(t))
    nu_p = t // CMP_STRIDE
    ncmp_p = (t - CMP_LEN) // CMP_STRIDE + 1
    cos_cp, sin_cp = _rope_tables(jnp.arange(nu_p) * CMP_STRIDE + CMP_LEN - CMP_STRIDE - 1)
    nsel_p = -(-t // SLC_BLOCK)
    nsel_p = -(-nsel_p // LANES) * LANES
    c2s_p = _cmp2slc(ncmp_p, nu_p, nsel_p)
    e_p = _expand_mat(nsel_p, t).T
    tq_p = 256
    cos_s, sin_s = _rope_tables(past + jnp.arange(SAMPLE_T))
    nu_s = past // CMP_STRIDE
    ncmp_s = (past + td - CMP_LEN) // CMP_STRIDE + 1
    cos_cs, sin_cs = _rope_tables(jnp.arange(nu_s) * CMP_STRIDE + CMP_LEN - CMP_STRIDE - 1)
    nsel_s = -(-(-(-(past + td) // SLC_BLOCK)) // LANES) * LANES
    c2s_s = _cmp2slc(ncmp_s, nu_s, nsel_s)
    e_s = _expand_mat(nsel_s, past + PAGE_SIZE)
    e_s_main, e_s_new = e_s[:, :past], e_s[:, past:]
    row_minor = lambda a: jnp.transpose(a, (0, 1, 3, 4, 5, 2))
    cmp_t, slc_t, win_t = row_minor(cache_cmp_kv), row_minor(cache_slc_kv), row_minor(state_win_kv)

    xp = x_prompt.reshape(b * t, D_MODEL)
    xs = jnp.pad(x_sample, ((0, 0), (0, SAMPLE_T - td), (0, 0))).reshape(db * SAMPLE_T, D_MODEL)

    zero_m = _mlstm_state_in(jnp.zeros((b, M_CONV - 1, 2 * M_WIDTH), F32), jnp.zeros((b, M_HEADS, M_HD, M_HD), F32),
                             jnp.zeros((b, M_HEADS, M_HD), F32), jnp.zeros((b, M_HEADS), F32))
    zero_g = jnp.zeros((b, LANES, G_DV), F32)

    pl_out = [[] for _ in range(8)]
    sl_out = [[] for _ in range(8)]
    for l in range(depth):
        p = _prep_layer(l, w_norm_mix, w_in, b_mlstm_if, w_mlstm_conv, w_mlstm_hnorm, w_gla_gate2, b_gla_gate,
                        w_gla_hnorm, w_qk_norm, w_cmp_pe, w_cmp_1, w_cmp_2, w_out, w_norm_ffn, w_ffn_up, w_ffn_down)
        um, ug, nq, ncmp, nsw, ngate = _pre(xp, p['g_mix'], p['w_cat'])
        r3 = lambda a, bb, tt: a.reshape(bb, tt, a.shape[-1])
        hm, hg, mst, gst = _mixers(p, r3(um, b, t), r3(ug, b, t), zero_m, zero_g, L=M_CHUNK, n_valid=M_CHUNK)
        q_hm, ks, vs, kw, vw, slc_f, win_f, gates = _rows(r3(nq, b, t), r3(nsw, b, t), r3(ngate, b, t), cos_p, sin_p,
                                                          p['gq'], p['gks'], p['gkw'], bm)
        kc, vc = _compress(ncmp.reshape(b, nu_p, 4096), p['w1r'], p['pe_r'], p['w2r'], p['gkc'], cos_cp, sin_cp, bm)
        oc, sel = _cmp_attn(q_hm, kc, vc, c2s_p, tq=tq_p, pos0=0, n_cmp=ncmp_p)
        hn = _slcwin(q_hm, ks, vs, kw, vw, sel, e_p, oc, gates, tq=tq_p)
        xp = _post(xp, hm.reshape(b * t, -1), hg.reshape(b * t, -1), hn.reshape(b * t, -1),
                   p['w_out'], p['g_ffn'], p['w_up_r'], p['w_dn_r'])
        c_o, n_o, m_o, conv_o = _mlstm_state_out(*mst)
        kv6 = lambda a: a.reshape(a.shape[0], a.shape[1], 2, N_KV, N_HD)
        for lst, val in zip(pl_out, (kv6(ncmp.reshape(b, t, 256)), kv6(slc_f), kv6(win_f[:, t - min(WINDOW, t):]),
                                     c_o, n_o, m_o, conv_o, gst.reshape(b, G_HEADS, G_DK, G_DV))):
            lst.append(val)

        um, ug, nq, ncmp, nsw, ngate = _pre(xs, p['g_mix'], p['w_cat'])
        mstate = _mlstm_state_in(state_mlstm_conv[l], state_mlstm_C[l], state_mlstm_n[l], state_mlstm_m[l])
        gstate = state_gla_S[l].astype(F32).reshape(db, LANES, G_DV)
        hm, hg, mst, gst = _mixers(p, r3(um, db, SAMPLE_T), r3(ug, db, SAMPLE_T), mstate, gstate,
                                   L=SAMPLE_T, n_valid=td)
        q_hm, ks_n, vs_n, kw_n, vw_n, slc_f, win_f, gates = _rows(
            r3(nq, db, SAMPLE_T), r3(nsw, db, SAMPLE_T), r3(ngate, db, SAMPLE_T), cos_s, sin_s,
            p['gq'], p['gks'], p['gkw'], bm)
        kc, vc = _compress_paged(page_table, cmp_t, l, p['w1r'], p['pe_r'], p['w2r'], p['gkc'], cos_cs, sin_cs, bm)
        oc, sel = _cmp_attn(q_hm, kc, vc, c2s_s, tq=SAMPLE_T, pos0=past, n_cmp=ncmp_s)
        pad_page = lambda a: jnp.pad(a, ((0, 0), (0, PAGE_SIZE - a.shape[1]), (0, 0)))
        hn = _slc_sample(page_table, slc_t, l, q_hm, sel, e_s_main, e_s_new, pad_page(slc_f), win_t,
                         pad_page(win_f), oc, gates, ts=-(-td // 8) * 8)
        new_win = jnp.concatenate([state_win_kv[l][:, td:].astype(F32), kv6(win_f[:, :td])], axis=1)
        xs = _post(xs, hm.reshape(db * SAMPLE_T, -1), hg.reshape(db * SAMPLE_T, -1), hn.reshape(db * SAMPLE_T, -1),
                   p['w_out'], p['g_ffn'], p['w_up_r'], p['w_dn_r'])
        c_o, n_o, m_o, conv_o = _mlstm_state_out(*mst)
        for lst, val in zip(sl_out, (kv6(ncmp.reshape(db, SAMPLE_T, 256)[:, :td]), kv6(slc_f[:, :td]),
                                     new_win,
                                     c_o, n_o, m_o, conv_o, gst.reshape(db, G_HEADS, G_DK, G_DV))):
            lst.append(val)

    outs_p = [jnp.stack(a) for a in pl_out]
    outs_s = [jnp.stack(a) for a in sl_out]
    y_p = xp.reshape(b, t, D_MODEL)
    y_s = xs.reshape(db, SAMPLE_T, D_MODEL)[:, :td]
    return (y_p, y_s, *outs_p, *outs_s)
```

```python
import functools
import math

import numpy as np
import jax
import jax.numpy as jnp
from jax import lax
from jax.experimental import pallas as pl
from jax.experimental.pallas import tpu as pltpu

F32 = jnp.float32
BF16 = jnp.bfloat16

D_MODEL = 1024
M_HEADS, M_HD, M_WIDTH, M_CONV, M_CHUNK = 4, 64, 256, 4, 64
G_HEADS, G_DK, G_DV, G_WIDTH, G_RANK, G_TAU, G_CHUNK = 4, 32, 64, 256, 16, 16.0, 64
N_HEADS, N_HD, N_KV, N_REP, N_WIDTH = 8, 64, 2, 4, 512
CMP_LEN, CMP_STRIDE, CMP_HIDDEN = 32, 16, 128
SLC_BLOCK, SLC_TOPK, WINDOW = 64, 16, 512
ROPE_THETA = 10000.0
D_FF = 2816
PAGE_SIZE = 128
NEG = -1e30
FORCE_BONUS = 1e4
EPS = 1e-6
LOG2E = 1.4426950408889634

LANES = 128
VMEM_LIMIT = 56 * 1024 * 1024
KV_TILE = 512
SAMPLE_T = 16

_OFF = {}
_o = 0
for _name, _w in (('m_qk', 512), ('m_v', 256), ('m_i', 4), ('m_f', 4), ('m_o', 256),
                  ('g_q', 128), ('g_k', 128), ('g_v', 256), ('g_lr', 16), ('g_o', 256),
                  ('n_q', 512), ('n_kc', 128), ('n_vc', 128), ('n_ks', 128),
                  ('n_vs', 128), ('n_kw', 128), ('n_vw', 128), ('n_gate', 24)):
    _OFF[_name] = (_o, _w)
    _o += _w
UM_W, UG_W, NQ_W, NCMP_W, NSW_W, NGATE_W = 1152, 896, 512, 256, 512, 128


def _cparams(sem):
    return pltpu.CompilerParams(dimension_semantics=sem, vmem_limit_bytes=VMEM_LIMIT)


def _dot(a, b):
    return jnp.dot(a, b, preferred_element_type=F32)


def _dot_nt(a, b):
    return lax.dot_general(a, b, (((1,), (1,)), ((), ())), preferred_element_type=F32)


def _dot_hi(a, b):
    return jnp.dot(a, b, preferred_element_type=F32, precision=lax.Precision.HIGHEST)


def _dot_split(a, b_bf):
    hi = a.astype(BF16)
    lo = (a - hi.astype(F32)).astype(BF16)
    return _dot(hi, b_bf) + _dot(lo, b_bf)


def _log_sigmoid(x):
    return jnp.minimum(x, 0.0) - jnp.log(1.0 + jnp.exp(-jnp.abs(x)))


def _sigmoid(x):
    return 1.0 / (1.0 + jnp.exp(-x))


def _transpose(x):
    r, c = x.shape
    if r < LANES:
        x = jnp.concatenate([x, jnp.zeros((LANES - r, c), x.dtype)], axis=0)
    parts = [x[:, i * LANES:(i + 1) * LANES].T[:, :r] for i in range(c // LANES)]
    return parts[0] if len(parts) == 1 else jnp.concatenate(parts, axis=0)


def _transpose2d(x):
    r, c = x.shape
    rp, cp = -(-r // LANES) * LANES, -(-c // LANES) * LANES
    if cp > c:
        x = jnp.concatenate([x, jnp.zeros((r, cp - c), x.dtype)], axis=1)
    if rp > r:
        x = jnp.concatenate([x, jnp.zeros((rp - r, cp), x.dtype)], axis=0)
    out_rows = []
    for j in range(cp // LANES):
        blocks = [x[i * LANES:(i + 1) * LANES, j * LANES:(j + 1) * LANES].T for i in range(rp // LANES)]
        out_rows.append(blocks[0] if len(blocks) == 1 else jnp.concatenate(blocks, axis=1))
    out = out_rows[0] if len(out_rows) == 1 else jnp.concatenate(out_rows, axis=0)
    return out[:c, :r]


def _iota(shape, dim):
    return lax.broadcasted_iota(jnp.int32, shape, dim)


def _pre_kernel(x_ref, g_ref, w_ref, um_ref, ug_ref, nq_ref, ncmp_ref, nsw_ref, ngate_ref):
    x = x_ref[...]
    h = x * lax.rsqrt(jnp.mean(x * x, axis=-1, keepdims=True) + EPS) * g_ref[...]
    u = _dot(h.astype(BF16), w_ref[...])
    o = 0
    for ref, w in ((um_ref, UM_W), (ug_ref, UG_W), (nq_ref, NQ_W), (ncmp_ref, NCMP_W),
                   (nsw_ref, NSW_W), (ngate_ref, NGATE_W)):
        ref[...] = u[:, o:o + w]
        o += w


def _pre(x2, g, w_cat):
    m = x2.shape[0]
    tm = min(m, 256)
    widths = (UM_W, UG_W, NQ_W, NCMP_W, NSW_W, NGATE_W)
    return pl.pallas_call(
        _pre_kernel,
        grid=(m // tm,),
        in_specs=[pl.BlockSpec((tm, D_MODEL), lambda i: (i, 0)),
                  pl.BlockSpec((1, D_MODEL), lambda i: (0, 0)),
                  pl.BlockSpec((D_MODEL, sum(widths)), lambda i: (0, 0))],
        out_specs=[pl.BlockSpec((tm, w), lambda i: (i, 0)) for w in widths],
        out_shape=[jax.ShapeDtypeStruct((m, w), F32) for w in widths],
        compiler_params=_cparams(("parallel",)),
        name="pre_proj",
    )(x2, g, w_cat)


def _post_kernel(x_ref, hm_ref, hg_ref, hn_ref, wout_ref, g_ref, wup_ref, wdn_ref, o_ref, h2_ref, *, tf):
    @pl.when(pl.program_id(1) == 0)
    def _():
        xn = x_ref[...]
        xn = xn + _dot(hm_ref[...], wout_ref[0:256, :])
        xn = xn + _dot(hg_ref[...], wout_ref[256:512, :])
        xn = xn + _dot(hn_ref[...], wout_ref[512:1024, :])
        o_ref[...] = xn
        h2 = xn * lax.rsqrt(jnp.mean(xn * xn, axis=-1, keepdims=True) + EPS) * g_ref[...]
        h2_ref[...] = h2.astype(BF16)

    au = _dot(h2_ref[...], wup_ref[0])
    a = au[:, :tf]
    act = (a * _sigmoid(a) * au[:, tf:]).astype(BF16)
    o_ref[...] += _dot(act, wdn_ref[0])


def _post(x2, hm, hg, hn, w_out, g, w_up_r, w_dn_r):
    m = x2.shape[0]
    tm = min(m, 512)
    nf, _, tf2 = w_up_r.shape
    tf = tf2 // 2
    return pl.pallas_call(
        functools.partial(_post_kernel, tf=tf),
        grid=(m // tm, nf),
        in_specs=[pl.BlockSpec((tm, D_MODEL), lambda i, f: (i, 0)),
                  pl.BlockSpec((tm, M_WIDTH), lambda i, f: (i, 0)),
                  pl.BlockSpec((tm, G_WIDTH), lambda i, f: (i, 0)),
                  pl.BlockSpec((tm, N_WIDTH), lambda i, f: (i, 0)),
                  pl.BlockSpec((D_MODEL, D_MODEL), lambda i, f: (0, 0)),
                  pl.BlockSpec((1, D_MODEL), lambda i, f: (0, 0)),
                  pl.BlockSpec((1, D_MODEL, tf2), lambda i, f: (f, 0, 0)),
                  pl.BlockSpec((1, tf, D_MODEL), lambda i, f: (f, 0, 0))],
        out_specs=pl.BlockSpec((tm, D_MODEL), lambda i, f: (i, 0)),
        out_shape=jax.ShapeDtypeStruct((m, D_MODEL), F32),
        scratch_shapes=[pltpu.VMEM((tm, D_MODEL), BF16)],
        compiler_params=_cparams(("parallel", "arbitrary")),
        name="post_ffn",
    )(x2, hm, hg, hn, w_out, g, w_up_r, w_dn_r)


CHUNK_UNROLL = 4


def _mlstm_kernel(um_ref, conv0_ref, ct0_ref, n0_ref, m0_ref, wconv_ref, bif_ref, gh_ref,
                  hm_ref, conv_out_ref, ct_out_ref, n_out_ref, m_out_ref,
                  xpad, q_s, k_s, ct_s, n_s, m_s, *, L, n_valid, blk):
    j = pl.program_id(1)

    @pl.when(j == 0)
    def _():
        xpad[0:8, :] = conv0_ref[0]
        ct_s[...] = ct0_ref[0]
        n_s[...] = n0_ref[0]
        m_s[...] = m0_ref[0]

    qk_pre = um_ref[0, :, 0:2 * M_WIDTH]
    xpad[8:8 + blk, :] = qk_pre
    wc = wconv_ref[...]
    acc = (xpad[5:5 + blk, :] * wc[0:1] + xpad[6:6 + blk, :] * wc[1:2]
           + xpad[7:7 + blk, :] * wc[2:3] + qk_pre * wc[3:4])
    qk = acc * _sigmoid(acc)
    q_s[...] = qk[:, :M_WIDTH]
    k_s[...] = qk[:, M_WIDTH:] * (M_HD ** -0.5)
    last = n_valid if blk == L else blk
    conv_out_ref[0] = xpad[last:last + 8, :]
    xpad[0:8, :] = xpad[blk:blk + 8, :]

    row = _iota((L, L), 0)
    col = _iota((L, L), 1)
    causal = row >= col
    tril = causal.astype(F32)
    lane_w = _iota((1, M_WIDTH), 1) // M_HD
    row_w = _iota((M_WIDTH, 1), 0) // M_HD
    valid_col = _iota((L, 1), 0) < n_valid

    row8 = _iota((8, LANES), 0)

    n_chunks = blk // L
    unroll = CHUNK_UNROLL if n_chunks % CHUNK_UNROLL == 0 else 1
    H = range(M_HEADS)
    U = range(unroll)
    hmask = [lane_w == h for h in H]
    lst = slice(n_valid - 1, n_valid)

    def group(i, state):
        ct, n_row, m_tile = state
        r0 = [pl.multiple_of((i * unroll + u) * L, L) for u in U]
        qc = [q_s[pl.ds(r, L), :] for r in r0]
        kc = [k_s[pl.ds(r, L), :] for r in r0]
        act = [um_ref[0, pl.ds(r, L), 1024:1152] + bif_ref[...] for r in r0]
        bcum = [_dot_hi(tril, _log_sigmoid(x)) for x in act]
        act_t = [_transpose(x) for x in act]
        bcum_t = [_transpose(x) for x in bcum]
        k_bf = [x.astype(BF16) for x in kc]
        kt_bf = [_transpose(x).astype(BF16) for x in kc]
        bcol = [[bcum[u][:, 4 + h:5 + h] for h in H] for u in U]
        dmat = [[jnp.where(causal, bcol[u][h] - bcum_t[u][4 + h:5 + h, :] + act_t[u][h:h + 1, :], NEG) for h in H]
                for u in U]
        m_loc = [[jnp.max(dmat[u][h], axis=1, keepdims=True) for h in H] for u in U]
        q_h = [[jnp.where(hmask[h], qc[u], 0.0) for h in H] for u in U]
        q_bf = [[q_h[u][h].astype(BF16) for h in H] for u in U]
        s = [[_dot_nt(q_bf[u][h], k_bf[u]) * jnp.exp(dmat[u][h] - m_loc[u][h]) for h in H] for u in U]
        v_h = [[um_ref[0, pl.ds(r, L), 512 + h * M_HD:512 + (h + 1) * M_HD] for h in H] for r in r0]
        sv = [[_dot(s[u][h].astype(BF16), v_h[u][h].astype(BF16)) for h in H] for u in U]
        ssum = [[jnp.sum(s[u][h], axis=1, keepdims=True) for h in H] for u in U]
        w_l = [[jnp.where(valid_col, jnp.exp(bcol[u][h][lst] - bcol[u][h] + act[u][:, h:h + 1] - m_loc[u][h][lst]), 0.0)
                for h in H] for u in U]
        upd = [[_dot(kt_bf[u], (v_h[u][h] * w_l[u][h]).astype(BF16)) for h in H] for u in U]
        ksum = [[jnp.sum(kc[u] * w_l[u][h], axis=0, keepdims=True) for h in H] for u in U]
        gate = [[_sigmoid(um_ref[0, pl.ds(r, L), 768 + h * M_HD:768 + (h + 1) * M_HD]) for h in H] for r in r0]
        for u in U:
            ct_bf = ct.astype(BF16)
            m_inter = [m_tile[h:h + 1, 0:1] + bcol[u][h] for h in H]
            m_new = [jnp.maximum(m_inter[h], m_loc[u][h]) for h in H]
            f = [jnp.exp(m_loc[u][h] - m_new[h]) for h in H]
            a_inter = [jnp.exp(m_inter[h] - m_new[h]) for h in H]
            num = [a_inter[h] * _dot(q_bf[u][h], ct_bf) + f[h] * sv[u][h] for h in H]
            qn = [jnp.sum(q_h[u][h] * n_row, axis=1, keepdims=True) for h in H]
            den = [a_inter[h] * qn[h] + f[h] * ssum[u][h] for h in H]
            hh = [gate[u][h] * (num[h] / jnp.maximum(jnp.abs(den[h]), jnp.exp(-m_new[h]))) for h in H]
            ms = [jnp.mean(x * x, axis=1, keepdims=True) for x in hh]
            outs = [hh[h] * lax.rsqrt(ms[h] + EPS) * gh_ref[:, h * M_HD:(h + 1) * M_HD] for h in H]
            hm_ref[0, pl.ds(r0[u], L), :] = jnp.concatenate(outs, axis=1).astype(BF16)
            ct_old, n_old = ct, n_row
            for h in H:
                ct = jnp.where(row_w == h, a_inter[h][lst] * ct_old + f[h][lst] * upd[u][h], ct)
                n_row = jnp.where(hmask[h], a_inter[h][lst] * n_old + f[h][lst] * ksum[u][h], n_row)
                m_tile = jnp.where(row8 == h, m_new[h][lst], m_tile)
        return ct, n_row, m_tile

    ct, n_row, m_tile = lax.fori_loop(0, n_chunks // unroll, group, (ct_s[...], n_s[0:1, :], m_s[...]))
    ct_s[...] = ct
    n_s[...] = jnp.broadcast_to(n_row, n_s.shape)
    m_s[...] = m_tile

    @pl.when(j == pl.num_programs(1) - 1)
    def _():
        ct_out_ref[0] = ct
        n_out_ref[0] = jnp.broadcast_to(n_row, n_s.shape)
        m_out_ref[0] = m_tile


def _mlstm(um, conv0, ct0, n0, m0, wconv, bif, gh, *, L, n_valid):
    b, t, _ = um.shape
    blk = min(t, 512)
    kern = functools.partial(_mlstm_kernel, L=L, n_valid=n_valid, blk=blk)
    per_b = lambda shape: pl.BlockSpec((1,) + shape, lambda i, j: (i,) + (0,) * len(shape))
    const = lambda shape: pl.BlockSpec(shape, lambda i, j: (0,) * len(shape))
    return pl.pallas_call(
        kern,
        grid=(b, t // blk),
        in_specs=[pl.BlockSpec((1, blk, UM_W), lambda i, j: (i, j, 0)),
                  per_b((8, 2 * M_WIDTH)), per_b((M_WIDTH, M_HD)), per_b((8, M_WIDTH)), per_b((8, LANES)),
                  const((8, 2 * M_WIDTH)), const((1, LANES)), const((1, M_WIDTH))],
        out_specs=[pl.BlockSpec((1, blk, M_WIDTH), lambda i, j: (i, j, 0)),
                   per_b((8, 2 * M_WIDTH)), per_b((M_WIDTH, M_HD)), per_b((8, M_WIDTH)), per_b((8, LANES))],
        out_shape=[jax.ShapeDtypeStruct((b, t, M_WIDTH), BF16),
                   jax.ShapeDtypeStruct((b, 8, 2 * M_WIDTH), F32),
                   jax.ShapeDtypeStruct((b, M_WIDTH, M_HD), F32),
                   jax.ShapeDtypeStruct((b, 8, M_WIDTH), F32),
                   jax.ShapeDtypeStruct((b, 8, LANES), F32)],
        scratch_shapes=[pltpu.VMEM((blk + 8, 2 * M_WIDTH), F32),
                        pltpu.VMEM((blk, M_WIDTH), F32), pltpu.VMEM((blk, M_WIDTH), F32),
                        pltpu.VMEM((M_WIDTH, M_HD), F32), pltpu.VMEM((8, M_WIDTH), F32),
                        pltpu.VMEM((8, LANES), F32)],
        compiler_params=_cparams(("parallel", "arbitrary")),
        name="mlstm",
    )(um, conv0, ct0, n0, m0, wconv, bif, gh)


def _gla_kernel(ug_ref, s0_ref, w2_ref, bg_ref, gh_ref, hg_ref, s_out_ref, s_s, *, L, n_valid, blk):
    j = pl.program_id(1)

    @pl.when(j == 0)
    def _():
        s_s[...] = s0_ref[0]

    row = _iota((L, L), 0)
    col = _iota((L, L), 1)
    causal = row >= col
    tril = causal.astype(F32)
    lane_k = _iota((1, LANES), 1) // G_DK
    row_k = _iota((LANES, 1), 0) // G_DK
    valid_col = _iota((L, 1), 0) < n_valid
    mid = max(n_valid // 2, 1)

    n_chunks = blk // L
    unroll = CHUNK_UNROLL if n_chunks % CHUNK_UNROLL == 0 else 1
    H = range(G_HEADS)
    U = range(unroll)
    hmask = [lane_k == h for h in H]

    def group(i, s_all):
        r0 = [pl.multiple_of((i * unroll + u) * L, L) for u in U]
        q = [ug_ref[0, pl.ds(r, L), 0:128] * (G_DK ** -0.5) for r in r0]
        k = [ug_ref[0, pl.ds(r, L), 128:256] for r in r0]
        z = [_dot_hi(ug_ref[0, pl.ds(r, L), 768:896], w2_ref[...]) + bg_ref[...] for r in r0]
        g = [_log_sigmoid(x) * (1.0 / G_TAU) for x in z]
        bc = [_dot_hi(tril, x) for x in g]
        c_ref = [x[mid - 1:mid] for x in bc]
        last = [x[n_valid - 1:n_valid] for x in bc]
        qe = [q[u] * jnp.exp(bc[u] - c_ref[u]) for u in U]
        ke = [(k[u] * jnp.exp(c_ref[u] - bc[u])).astype(BF16) for u in U]
        qin = [q[u] * jnp.exp(bc[u]) for u in U]
        kd_t = [_transpose(jnp.where(valid_col, k[u] * jnp.exp(last[u] - bc[u]), 0.0)).astype(BF16) for u in U]
        decay = [jnp.exp(_transpose(jnp.broadcast_to(x, (8, LANES)))[:, 0:1]) for x in last]
        v_h = [[ug_ref[0, pl.ds(r, L), 256 + h * G_DV:256 + (h + 1) * G_DV].astype(BF16) for h in H] for r in r0]
        a = [[jnp.where(causal, _dot_nt(jnp.where(hmask[h], qe[u], 0.0).astype(BF16), ke[u]), 0.0) for h in H]
             for u in U]
        intra = [[_dot(a[u][h].astype(BF16), v_h[u][h]) for h in H] for u in U]
        q_in = [[jnp.where(hmask[h], qin[u], 0.0).astype(BF16) for h in H] for u in U]
        upd = [[_dot(kd_t[u], v_h[u][h]) for h in H] for u in U]
        gate = [[ug_ref[0, pl.ds(r, L), 512 + h * G_DV:512 + (h + 1) * G_DV] for h in H] for r in r0]
        gate = [[x * _sigmoid(x) * gh_ref[:, h * G_DV:(h + 1) * G_DV] for h, x in enumerate(gs)] for gs in gate]
        for u in U:
            s_bf = s_all.astype(BF16)
            o = [_dot(q_in[u][h], s_bf) + intra[u][h] for h in H]
            ms = [jnp.mean(x * x, axis=1, keepdims=True) for x in o]
            outs = [o[h] * lax.rsqrt(ms[h] + EPS) * gate[u][h] for h in H]
            hg_ref[0, pl.ds(r0[u], L), :] = jnp.concatenate(outs, axis=1).astype(BF16)
            s_all = decay[u] * s_all
            for h in H:
                s_all = s_all + jnp.where(row_k == h, upd[u][h], 0.0)
        return s_all

    s_fin = lax.fori_loop(0, n_chunks // unroll, group, s_s[...])
    s_s[...] = s_fin

    @pl.when(j == pl.num_programs(1) - 1)
    def _():
        s_out_ref[0] = s_fin


def _gla(ug, s0, w2p, bg, gh, *, L, n_valid):
    b, t, _ = ug.shape
    blk = min(t, 512)
    kern = functools.partial(_gla_kernel, L=L, n_valid=n_valid, blk=blk)
    return pl.pallas_call(
        kern,
        grid=(b, t // blk),
        in_specs=[pl.BlockSpec((1, blk, UG_W), lambda i, j: (i, j, 0)),
                  pl.BlockSpec((1, LANES, G_DV), lambda i, j: (i, 0, 0)),
                  pl.BlockSpec((LANES, LANES), lambda i, j: (0, 0)),
                  pl.BlockSpec((1, LANES), lambda i, j: (0, 0)),
                  pl.BlockSpec((1, G_WIDTH), lambda i, j: (0, 0))],
        out_specs=[pl.BlockSpec((1, blk, G_WIDTH), lambda i, j: (i, j, 0)),
                   pl.BlockSpec((1, LANES, G_DV), lambda i, j: (i, 0, 0))],
        out_shape=[jax.ShapeDtypeStruct((b, t, G_WIDTH), BF16),
                   jax.ShapeDtypeStruct((b, LANES, G_DV), F32)],
        scratch_shapes=[pltpu.VMEM((LANES, G_DV), F32)],
        compiler_params=_cparams(("parallel", "arbitrary")),
        name="gla",
    )(ug, s0, w2p, bg, gh)


def _group_mean_sq(x, bm_bf):
    return _dot_split(x * x, bm_bf)


def _rope_slab(y, cos, sin_signed):
    lane = _iota(y.shape, 1)
    rot = jnp.where((lane % N_HD) < (N_HD // 2), pltpu.roll(y, 96, axis=1), pltpu.roll(y, 32, axis=1))
    return y * cos + rot * sin_signed


def _norm_rope_slab(x, g, cos, sin_signed, bm_bf):
    y = x * lax.rsqrt(_group_mean_sq(x, bm_bf) + EPS) * g
    return _rope_slab(y, cos, sin_signed)


def _rows_kernel(nq_ref, nsw_ref, ngate_ref, cos_ref, sin_ref, gq_ref, gks_ref, gkw_ref, bm_ref,
                 q_ref, ks_ref, vs_ref, kw_ref, vw_ref, slc_ref, win_ref, gates_ref):
    cos = cos_ref[...]
    sin = sin_ref[...]
    bm = bm_ref[...]
    low = _iota((1, LANES), 1) < N_HD
    for sl in range(4):
        x = nq_ref[0, :, sl * LANES:(sl + 1) * LANES]
        y = _norm_rope_slab(x, gq_ref[:, sl * LANES:(sl + 1) * LANES], cos, sin, bm) * (N_HD ** -0.5 * LOG2E)
        y_sw = pltpu.roll(y, N_HD, axis=1)
        if sl < 2:
            even, odd = jnp.where(low, y, 0.0), jnp.where(low, y_sw, 0.0)
        else:
            even, odd = jnp.where(low, 0.0, y_sw), jnp.where(low, 0.0, y)
        q_ref[0, 2 * sl] = even.astype(BF16)
        q_ref[0, 2 * sl + 1] = odd.astype(BF16)
    ks = _norm_rope_slab(nsw_ref[0, :, 0:128], gks_ref[...], cos, sin, bm)
    vs = nsw_ref[0, :, 128:256]
    kw = _norm_rope_slab(nsw_ref[0, :, 256:384], gkw_ref[...], cos, sin, bm)
    vw = nsw_ref[0, :, 384:512]
    slc_ref[0, :, 0:128] = ks
    slc_ref[0, :, 128:256] = vs
    win_ref[0, :, 0:128] = kw
    win_ref[0, :, 128:256] = vw
    for ref, val in ((ks_ref, ks), (vs_ref, vs), (kw_ref, kw), (vw_ref, vw)):
        ref[0] = val.astype(BF16)
    gt = _sigmoid(ngate_ref[0])
    gates_ref[0, 0] = gt
    gates_ref[0, 1] = pltpu.roll(gt, LANES - 3 * N_REP, axis=1)


def _rows(nq, nsw, ngate, cos, sin, gq, gks, gkw, bm):
    b, t, _ = nq.shape
    tm = min(t, 512)
    tok = lambda w: pl.BlockSpec((1, tm, w), lambda i, j: (i, j, 0))
    const = lambda shape: pl.BlockSpec(shape, lambda i, j: (0,) * len(shape))
    kv_shape = jax.ShapeDtypeStruct((b, t, LANES), BF16)
    return pl.pallas_call(
        _rows_kernel,
        grid=(b, t // tm),
        in_specs=[tok(NQ_W), tok(NSW_W), tok(NGATE_W),
                  pl.BlockSpec((tm, LANES), lambda i, j: (j, 0)),
                  pl.BlockSpec((tm, LANES), lambda i, j: (j, 0)),
                  const((1, NQ_W)), const((1, LANES)), const((1, LANES)), const((LANES, LANES))],
        out_specs=[pl.BlockSpec((1, N_HEADS, tm, LANES), lambda i, j: (i, 0, j, 0)),
                   tok(LANES), tok(LANES), tok(LANES), tok(LANES), tok(256), tok(256),
                   pl.BlockSpec((1, N_KV, tm, LANES), lambda i, j: (i, 0, j, 0))],
        out_shape=[jax.ShapeDtypeStruct((b, N_HEADS, t, LANES), BF16), kv_shape, kv_shape, kv_shape, kv_shape,
                   jax.ShapeDtypeStruct((b, t, 256), F32), jax.ShapeDtypeStruct((b, t, 256), F32),
                   jax.ShapeDtypeStruct((b, N_KV, t, LANES), F32)],
        compiler_params=_cparams(("parallel", "parallel")),
        name="nsa_rows",
    )(nq, nsw, ngate, cos, sin, gq, gks, gkw, bm)


def _block_mlp(x_bf, w1, pe, w2, prev_first):
    p = _dot(x_bf, w1)
    bias = _dot(pe.astype(BF16), w1)
    pa = p[:, :256] + bias[0:1, :256]
    pb = p[:, 256:] + bias[1:2, 256:]
    pa_prev = jnp.where(_iota(pa.shape, 0) == 0, prev_first, pltpu.roll(pa, 1, axis=0))
    pre = pa_prev + pb
    hid = (pre * _sigmoid(pre)).astype(BF16)
    return _dot(hid, w2), pa[pa.shape[0] - 1:, :]


def _compress_kernel(u_ref, w1_ref, pe_ref, w2_ref, gkc_ref, cos_ref, sin_ref, bm_ref, kc_ref, vc_ref):
    outs = []
    for kv in range(2):
        x = jnp.concatenate(
            [u_ref[0, :, j * 256 + kv * LANES:j * 256 + (kv + 1) * LANES] for j in range(CMP_STRIDE)],
            axis=1).astype(BF16)
        outs.append(_block_mlp(x, w1_ref[kv], pe_ref[kv], w2_ref[kv], jnp.zeros((1, 256), F32))[0])
    kc_ref[0] = _norm_rope_slab(outs[0], gkc_ref[...], cos_ref[...], sin_ref[...], bm_ref[...]).astype(BF16)
    vc_ref[0] = outs[1].astype(BF16)


CMP_PAGES = 32
PAGE_UNITS = PAGE_SIZE // CMP_STRIDE


def _compress_paged_kernel(pt_ref, *refs):
    pages = refs[:CMP_PAGES]
    (w1_ref, pe_ref, w2_ref, gkc_ref, cos_ref, sin_ref, bm_ref, kc_ref, vc_ref, t_scr, x_scr, carry_scr) = refs[CMP_PAGES:]

    @pl.when(pl.program_id(1) == 0)
    def _():
        carry_scr[...] = jnp.zeros(carry_scr.shape, F32)

    outs = []
    for kv in range(2):
        for i in range(CMP_PAGES):
            t_scr[i] = pages[i][0, 0, kv].reshape(LANES, PAGE_SIZE).T
            for j in range(CMP_STRIDE):
                x_scr[i * PAGE_UNITS:(i + 1) * PAGE_UNITS, j * LANES:(j + 1) * LANES] = \
                    t_scr[i, pl.ds(j, PAGE_UNITS, stride=CMP_STRIDE), :]
        out, last = _block_mlp(x_scr[...].astype(BF16), w1_ref[kv], pe_ref[kv], w2_ref[kv], carry_scr[kv, 0:1, :])
        carry_scr[kv, 0:1, :] = last
        outs.append(out)
    kc_ref[0] = _norm_rope_slab(outs[0], gkc_ref[...], cos_ref[...], sin_ref[...], bm_ref[...]).astype(BF16)
    vc_ref[0] = outs[1].astype(BF16)


def _compress_paged(page_table, cache_t, layer, w1r, pe_r, w2r, gkc, cos_c, sin_c, bm):
    db, n_pages = page_table.shape
    step_units = CMP_PAGES * PAGE_UNITS
    nu = n_pages * PAGE_UNITS

    def page_spec(i):
        return pl.BlockSpec((1, 1, 2, N_KV, N_HD, PAGE_SIZE),
                            lambda b, c, pt: (layer, pt[b, c * CMP_PAGES + i], 0, 0, 0, 0))

    const = lambda shape: pl.BlockSpec(shape, lambda b, c, pt: (0,) * len(shape))
    tab = pl.BlockSpec((step_units, LANES), lambda b, c, pt: (c, 0))
    out = jax.ShapeDtypeStruct((db, nu, LANES), BF16)
    ospec = pl.BlockSpec((1, step_units, LANES), lambda b, c, pt: (b, c, 0))
    return pl.pallas_call(
        _compress_paged_kernel,
        grid_spec=pltpu.PrefetchScalarGridSpec(
            num_scalar_prefetch=1,
            grid=(db, n_pages // CMP_PAGES),
            in_specs=[page_spec(i) for i in range(CMP_PAGES)] + [
                const((2, 2048, 512)), const((2, 8, 2048)), const((2, 256, LANES)),
                const((1, LANES)), tab, tab, const((LANES, LANES))],
            out_specs=[ospec, ospec],
            scratch_shapes=[pltpu.VMEM((CMP_PAGES, PAGE_SIZE, LANES), F32),
                            pltpu.VMEM((step_units, CMP_STRIDE * LANES), F32),
                            pltpu.VMEM((2, 8, 256), F32)]),
        out_shape=[out, out],
        compiler_params=_cparams(("parallel", "arbitrary")),
        name="nsa_compress_paged",
    )(page_table, *([cache_t] * CMP_PAGES), w1r, pe_r, w2r, gkc, cos_c, sin_c, bm)


def _compress(units, w1r, pe_r, w2r, gkc, cos_c, sin_c, bm):
    b, nu, _ = units.shape
    const = lambda shape: pl.BlockSpec(shape, lambda i: (0,) * len(shape))
    out = jax.ShapeDtypeStruct((b, nu, LANES), BF16)
    return pl.pallas_call(
        _compress_kernel,
        grid=(b,),
        in_specs=[pl.BlockSpec((1, nu, 4096), lambda i: (i, 0, 0)),
                  const((2, 2048, 512)), const((2, 8, 2048)), const((2, 256, LANES)),
                  const((1, LANES)), const((nu, LANES)), const((nu, LANES)), const((LANES, LANES))],
        out_specs=[pl.BlockSpec((1, nu, LANES), lambda i: (i, 0, 0))] * 2,
        out_shape=[out, out],
        compiler_params=_cparams(("parallel",)),
        name="nsa_compress",
    )(units, w1r, pe_r, w2r, gkc, cos_c, sin_c, bm)


def _cmp_kernel(q_ref, kc_ref, vc_ref, c2s_ref, oc_ref, sel_ref, *, tq, pos0, n_cmp, nsel):
    qi = pl.program_id(1)
    nc_pad = kc_ref.shape[1]
    q = q_ref[0].reshape(N_HEADS * tq, LANES)
    s = _dot_nt(q, kc_ref[0]).reshape(N_HEADS, tq, nc_pad)
    qpos = pos0 + qi * tq + _iota((tq, 1), 0)
    c_idx = _iota((tq, nc_pad), 1)
    c_real = jnp.where(c_idx >= 1, c_idx, nc_pad + n_cmp) <= n_cmp
    c_ok = jnp.where(c_real, c_idx * CMP_STRIDE + (CMP_LEN - CMP_STRIDE - 1), jnp.int32(2 ** 30)) <= qpos
    s = s + jnp.where(c_ok, 0.0, NEG)[None]
    p = jnp.exp2(s - jnp.max(s, axis=-1, keepdims=True))
    p = p * ((qpos >= CMP_LEN - 1).astype(F32)[None] / jnp.sum(p, axis=-1, keepdims=True))
    oc_ref[0] = _dot(p.reshape(N_HEADS * tq, nc_pad).astype(BF16), vc_ref[0]).reshape(N_HEADS, tq, LANES)
    psum = jnp.sum(p.reshape(N_KV, N_REP, tq, nc_pad), axis=1).reshape(N_KV * tq, nc_pad)
    imp = _dot_split(psum, c2s_ref[...])
    work = _transpose2d(imp)
    ncol = N_KV * tq
    qpos_row = pos0 + qi * tq + _iota((1, ncol), 1) % tq
    blk = _iota((nsel, ncol), 0)
    cur = qpos_row // SLC_BLOCK
    forced = jnp.where(blk == 0, 1.0, jnp.where(blk == cur, 1.0, jnp.where(blk == cur - 1, 1.0, 0.0)))
    work = jnp.where(blk * SLC_BLOCK <= qpos_row, work + FORCE_BONUS * forced, NEG)
    blk_f = blk.astype(F32)
    for _ in range(SLC_TOPK):
        m = jnp.max(work, axis=0, keepdims=True)
        idx = jnp.min(jnp.where(work == m, blk_f, float(nsel)), axis=0, keepdims=True)
        work = jnp.where(blk_f == idx, -jnp.inf, work)
    sel = jnp.where(work == -jnp.inf, 1.0, 0.0)
    sel_ref[0] = _transpose2d(sel).reshape(N_KV, tq, nsel).astype(BF16)


def _cmp_attn(q_pad, kc, vc, c2s, *, tq, pos0, n_cmp):
    b, _, t, _ = q_pad.shape
    nc_pad = kc.shape[1]
    nsel = c2s.shape[1]
    kern = functools.partial(_cmp_kernel, tq=tq, pos0=pos0, n_cmp=n_cmp, nsel=nsel)
    return pl.pallas_call(
        kern,
        grid=(b, t // tq),
        in_specs=[pl.BlockSpec((1, N_HEADS, tq, LANES), lambda i, j: (i, 0, j, 0)),
                  pl.BlockSpec((1, nc_pad, LANES), lambda i, j: (i, 0, 0)),
                  pl.BlockSpec((1, nc_pad, LANES), lambda i, j: (i, 0, 0)),
                  pl.BlockSpec((nc_pad, nsel), lambda i, j: (0, 0))],
        out_specs=[pl.BlockSpec((1, N_HEADS, tq, LANES), lambda i, j: (i, 0, j, 0)),
                   pl.BlockSpec((1, N_KV, tq, nsel), lambda i, j: (i, 0, j, 0))],
        out_shape=[jax.ShapeDtypeStruct((b, N_HEADS, t, LANES), F32),
                   jax.ShapeDtypeStruct((b, N_KV, t, nsel), BF16)],
        compiler_params=_cparams(("parallel", "parallel")),
        name="nsa_cmp_topk",
    )(q_pad, kc, vc, c2s)


SCORE_DTYPE = BF16
SLAB = 64


def _tile_scores(q_parts, k, s_scr, *, tq, width):
    for r in range(N_REP):
        s_scr[r * tq:(r + 1) * tq, 0:width] = _dot_nt(q_parts[r], k).astype(s_scr.dtype)


def _tile_update(v, bias_scr, s_scr, p_scr, m_scr, l_scr, alpha_scr, acc_scr, *, tq, width):
    nch = width // LANES
    sd = s_scr.dtype

    def chunk(rows, i, c):
        x = s_scr[rows, c * LANES:(c + 1) * LANES]
        if bias_scr is not None:
            x = x + bias_scr[i * SLAB:(i + 1) * SLAB, c * LANES:(c + 1) * LANES]
        return x

    slabs = [(slice(r * tq + i * SLAB, r * tq + (i + 1) * SLAB), i) for r in range(N_REP) for i in range(tq // SLAB)]
    for rows, i in slabs:
        mx = chunk(rows, i, 0)
        for c in range(1, nch):
            mx = jnp.maximum(mx, chunk(rows, i, c))
        m_old = m_scr[rows, :]
        m_new = jnp.maximum(m_old, jnp.max(mx.astype(F32), axis=1, keepdims=True)).astype(sd).astype(F32)
        alpha_scr[rows, :] = jnp.exp2(m_old - m_new)
        m_scr[rows, :] = m_new
    for rows, i in slabs:
        m_new = m_scr[rows, :].astype(sd)
        psum = None
        for c in range(nch):
            p = jnp.exp2(chunk(rows, i, c) - m_new)
            psum = p if psum is None else psum + p
            p_scr[rows, c * LANES:(c + 1) * LANES] = p.astype(BF16)
        l_scr[rows, :] = alpha_scr[rows, :] * l_scr[rows, :] + psum.astype(F32)
    for r in range(N_REP):
        rs = slice(r * tq, (r + 1) * tq)
        acc_scr[rs, :] = alpha_scr[rs, :] * acc_scr[rs, :] + _dot(p_scr[rs, 0:width], v)


def _slcwin_kernel(q_ref, ks_ref, vs_ref, kw_ref, vw_ref, sel_ref, et_ref, oc_ref, gates_ref, hn_ref,
                   s_scr, bias_scr, p_scr, m_scr, l_scr, alpha_scr, acc_scr, ow_scr, *, tq):
    g = pl.program_id(1)
    qi = pl.program_id(2)
    rows = N_REP * tq
    q0 = qi * tq
    qpos = q0 + _iota((tq, 1), 0)
    scr = dict(p_scr=p_scr, m_scr=m_scr, l_scr=l_scr, alpha_scr=alpha_scr, acc_scr=acc_scr, tq=tq)

    def reset():
        m_scr[...] = jnp.full((rows, LANES), NEG, F32)
        l_scr[...] = jnp.zeros((rows, LANES), F32)
        acc_scr[...] = jnp.zeros((rows, LANES), F32)

    def result():
        return acc_scr[...] / jnp.sum(l_scr[...], axis=1, keepdims=True)

    wlen = WINDOW + tq
    start = pl.multiple_of(jnp.maximum(q0 - WINDOW, 0), tq)
    reset()
    dist = qpos - (start + _iota((tq, wlen), 1))
    bias_scr[:, 0:wlen] = jnp.where(jnp.where(dist >= 0, dist, WINDOW) < WINDOW, 0.0, NEG).astype(bias_scr.dtype)
    _tile_scores([q_ref[0, r] for r in range(N_REP)], kw_ref[0, pl.ds(start, wlen), :], s_scr, tq=tq, width=wlen)
    _tile_update(vw_ref[0, pl.ds(start, wlen), :], bias_scr, s_scr, width=wlen, **scr)
    ow_scr[...] = result()

    reset()
    sel_m = ((sel_ref[0, 0].astype(F32) - 1.0) * (-NEG)).astype(BF16)
    q_aug = [jnp.concatenate([q_ref[0, r], sel_m], axis=1) for r in range(N_REP)]
    n_kv = (q0 + tq + KV_TILE - 1) // KV_TILE

    def scores(j, dst):
        k0 = pl.multiple_of(j * KV_TILE, KV_TILE)
        k_aug = jnp.concatenate([ks_ref[0, pl.ds(k0, KV_TILE), :], et_ref[pl.ds(k0, KV_TILE), :]], axis=1)
        _tile_scores(q_aug, k_aug, dst, tq=tq, width=KV_TILE)

    def values(j):
        return vs_ref[0, pl.ds(pl.multiple_of(j * KV_TILE, KV_TILE), KV_TILE), :]

    n_plain = n_kv - 1

    def body(j, carry):
        scores(j, s_scr)
        _tile_update(values(j), None, s_scr, width=KV_TILE, **scr)
        return carry

    lax.fori_loop(0, n_plain, body, 0)
    bias_scr[:, 0:KV_TILE] = jnp.where(n_plain * KV_TILE + _iota((tq, KV_TILE), 1) <= qpos, 0.0, NEG).astype(
        bias_scr.dtype)
    scores(n_plain, s_scr)
    _tile_update(values(n_plain), bias_scr, s_scr, width=KV_TILE, **scr)
    o_s_all = result()

    gt = gates_ref[0, 0]
    low = _iota((1, LANES), 1) < N_HD
    vals = []
    for r in range(N_REP):
        rs = slice(r * tq, (r + 1) * tq)
        vals.append(gt[:, 3 * r:3 * r + 1] * oc_ref[0, r] + gt[:, 3 * r + 1:3 * r + 2] * o_s_all[rs]
                    + gt[:, 3 * r + 2:3 * r + 3] * ow_scr[rs, :])
    for pair in range(N_REP // 2):
        a, b = vals[2 * pair], vals[2 * pair + 1]
        a_sw, b_sw = pltpu.roll(a, N_HD, axis=1), pltpu.roll(b, N_HD, axis=1)
        lo = jnp.where(g == 0, a, a_sw)
        hi = jnp.where(g == 0, b_sw, b)
        hn_ref[0, :, pair * LANES:(pair + 1) * LANES] = jnp.where(low, lo, hi).astype(BF16)


def _slcwin(q_pad, ks, vs, kw, vw, sel, e_t, oc, gates, *, tq):
    b, _, t, _ = q_pad.shape
    nsel = sel.shape[3]
    assert nsel == LANES and e_t.shape == (t, LANES)
    rows = N_REP * tq
    wlen = WINDOW + tq
    kern = functools.partial(_slcwin_kernel, tq=tq)
    qspec = pl.BlockSpec((1, N_REP, tq, LANES), lambda i, g, j: (i, g, j, 0))
    full = pl.BlockSpec((1, t, LANES), lambda i, g, j: (i, 0, 0))
    return pl.pallas_call(
        kern,
        grid=(b, N_KV, t // tq),
        in_specs=[qspec, full, full, full, full,
                  pl.BlockSpec((1, 1, tq, nsel), lambda i, g, j: (i, g, j, 0)),
                  pl.BlockSpec((t, LANES), lambda i, g, j: (0, 0)),
                  qspec,
                  pl.BlockSpec((1, 1, tq, LANES), lambda i, g, j: (i, g, j, 0))],
        out_specs=pl.BlockSpec((1, tq, N_REP * N_HD), lambda i, g, j: (i, j, g)),
        out_shape=jax.ShapeDtypeStruct((b, t, N_WIDTH), BF16),
        scratch_shapes=[pltpu.VMEM((rows, wlen), SCORE_DTYPE), pltpu.VMEM((tq, wlen), SCORE_DTYPE),
                        pltpu.VMEM((rows, wlen), BF16),
                        pltpu.VMEM((rows, LANES), F32), pltpu.VMEM((rows, LANES), F32), pltpu.VMEM((rows, LANES), F32),
                        pltpu.VMEM((rows, LANES), F32), pltpu.VMEM((rows, LANES), F32)],
        compiler_params=_cparams(("parallel", "parallel", "parallel")),
        name="nsa_slc_win",
    )(q_pad, ks, vs, kw, vw, sel, e_t, oc, gates)


SLC_PAGES = 32


def _heads_bias(b2, ts):
    w = b2.shape[1]
    return jnp.broadcast_to(b2.reshape(N_KV, 1, ts, w), (N_KV, N_REP, ts, w)).reshape(N_HEADS * ts, w)


def _slc_sample_kernel(pt_ref, *refs, ts):
    pages = refs[:SLC_PAGES]
    (q_ref, sel_ref, e_ref, enew_ref, newslc_ref, win_ref, newwin_ref, oc_ref, gates_ref,
     hn_ref, m_scr, l_scr, acc_scr) = refs[SLC_PAGES:]
    c = pl.program_id(1)
    rows = N_HEADS * ts
    q = q_ref[0].astype(F32)[:, 0:ts, :].reshape(rows, LANES).astype(BF16)
    sel2 = sel_ref[0].astype(F32)[:, 0:ts, :].reshape(N_KV * ts, sel_ref.shape[3]).astype(BF16)

    @pl.when(c == 0)
    def _():
        m_scr[...] = jnp.full((rows, 1), NEG, F32)
        l_scr[...] = jnp.zeros((rows, 1), F32)
        acc_scr[...] = jnp.zeros((rows, LANES), F32)

    def online(s, pv_fn):
        m_old = m_scr[...]
        m_new = jnp.maximum(m_old, jnp.max(s, axis=1, keepdims=True))
        p = jnp.exp2(s - m_new)
        alpha = jnp.exp2(m_old - m_new)
        l_scr[...] = alpha * l_scr[...] + jnp.sum(p, axis=1, keepdims=True)
        acc_scr[...] = alpha * acc_scr[...] + pv_fn(p.astype(BF16))
        m_scr[...] = m_new

    picked = _dot(sel2, e_ref[...])
    bias = _heads_bias(picked * (-NEG) + NEG, ts)
    k_t = jnp.concatenate([pg[0, 0, 0].reshape(LANES, PAGE_SIZE).astype(BF16) for pg in pages], axis=1)
    v_t = jnp.concatenate([pg[0, 0, 1].reshape(LANES, PAGE_SIZE).astype(BF16) for pg in pages], axis=1)
    online(_dot(q, k_t) + bias, lambda p: _dot_nt(p, v_t))

    @pl.when(c == pl.num_programs(1) - 1)
    def _():
        tok = _iota((N_KV * ts, 1), 0) % ts
        key = _iota((N_KV * ts, PAGE_SIZE), 1)
        new = newslc_ref[0].astype(BF16)
        picked_n = _dot(sel2, enew_ref[...])
        bias_n = _heads_bias(jnp.where(key <= tok, picked_n, 0.0) * (-NEG) + NEG, ts)
        online(_dot_nt(q, new[:, :LANES]) + bias_n, lambda p: _dot(p, new[:, LANES:]))
        kw_t = win_ref[0, 0, 0].reshape(LANES, WINDOW).astype(BF16)
        vw_t = win_ref[0, 0, 1].reshape(LANES, WINDOW).astype(BF16)
        nwin = newwin_ref[0].astype(BF16)
        wkey = _iota((N_KV * ts, WINDOW), 1)
        bias_w = jnp.concatenate([jnp.where(wkey > tok, 0.0, NEG), jnp.where(key <= tok, 0.0, NEG)], axis=1)
        s_w = jnp.concatenate([_dot(q, kw_t), _dot_nt(q, nwin[:, :LANES])], axis=1) + _heads_bias(bias_w, ts)
        p_w = jnp.exp2(s_w - jnp.max(s_w, axis=1, keepdims=True))
        l_w = jnp.sum(p_w, axis=1, keepdims=True)
        p_w = p_w.astype(BF16)
        o_w = (_dot_nt(p_w[:, :WINDOW], vw_t) + _dot(p_w[:, WINDOW:], nwin[:, LANES:])) / l_w
        o_s = acc_scr[...] / l_scr[...]
        low = _iota((1, LANES), 1) < N_HD
        vals = []
        for h in range(N_HEADS):
            g, r = divmod(h, N_REP)
            gt = gates_ref[0, g, 0:ts, :]
            rs = slice(h * ts, (h + 1) * ts)
            vals.append(gt[:, 3 * r:3 * r + 1] * oc_ref[0, h, 0:ts, :] + gt[:, 3 * r + 1:3 * r + 2] * o_s[rs]
                        + gt[:, 3 * r + 2:3 * r + 3] * o_w[rs])
        slabs = []
        for pair in range(N_HEADS // 2):
            a, b = vals[2 * pair], vals[2 * pair + 1]
            if pair < N_REP // 2:
                lo, hi = a, pltpu.roll(b, N_HD, axis=1)
            else:
                lo, hi = pltpu.roll(a, N_HD, axis=1), b
            slabs.append(jnp.where(low, lo, hi))
        out = jnp.concatenate(slabs, axis=1)
        pad = jnp.zeros((hn_ref.shape[1] - ts, N_WIDTH), F32)
        hn_ref[0] = jnp.concatenate([out, pad], axis=0).astype(BF16)


def _slc_sample(page_table, cache_t, layer, q_pad, sel, e_main, e_new, new_slc, win_t, new_win, oc, gates, *, ts):
    db, n_pages = page_table.shape
    tpad = q_pad.shape[2]
    nsel = sel.shape[3]
    n_steps = n_pages // SLC_PAGES
    step_keys = SLC_PAGES * PAGE_SIZE
    rows = N_HEADS * ts

    def page_spec(i):
        return pl.BlockSpec((1, 1, 2, N_KV, N_HD, PAGE_SIZE),
                            lambda b, c, pt: (layer, pt[b, c * SLC_PAGES + i], 0, 0, 0, 0))

    per_b = lambda shape: pl.BlockSpec((1,) + shape, lambda b, c, pt: (b,) + (0,) * len(shape))
    return pl.pallas_call(
        functools.partial(_slc_sample_kernel, ts=ts),
        grid_spec=pltpu.PrefetchScalarGridSpec(
            num_scalar_prefetch=1,
            grid=(db, n_steps),
            in_specs=[page_spec(i) for i in range(SLC_PAGES)] + [
                per_b((N_HEADS, tpad, LANES)), per_b((N_KV, tpad, nsel)),
                pl.BlockSpec((nsel, step_keys), lambda b, c, pt: (0, c)),
                pl.BlockSpec((nsel, PAGE_SIZE), lambda b, c, pt: (0, 0)),
                per_b((PAGE_SIZE, 256)),
                pl.BlockSpec((1, 1, 2, N_KV, N_HD, WINDOW), lambda b, c, pt: (layer, b, 0, 0, 0, 0)),
                per_b((PAGE_SIZE, 256)), per_b((N_HEADS, tpad, LANES)), per_b((N_KV, tpad, LANES))],
            out_specs=per_b((tpad, N_WIDTH)),
            scratch_shapes=[pltpu.VMEM((rows, 1), F32), pltpu.VMEM((rows, 1), F32), pltpu.VMEM((rows, LANES), F32)]),
        out_shape=jax.ShapeDtypeStruct((db, tpad, N_WIDTH), BF16),
        compiler_params=_cparams(("parallel", "arbitrary")),
        name="nsa_slc_sample",
    )(page_table, *([cache_t] * SLC_PAGES), q_pad, sel, e_main, e_new, new_slc, win_t, new_win, oc, gates)


def _rope_tables(pos):
    half = N_HD // 2
    inv = ROPE_THETA ** (-jnp.arange(half, dtype=F32) / half)
    ang = pos.astype(F32)[:, None] * inv[None, :]
    cos, sin = jnp.cos(ang), jnp.sin(ang)
    return jnp.tile(jnp.concatenate([cos, cos], axis=1), (1, 2)), jnp.tile(jnp.concatenate([-sin, sin], axis=1), (1, 2))


def _pad_cols(a, w):
    return jnp.pad(a, ((0, 0), (0, w - a.shape[1])))


def _prep_layer(l, w_norm_mix, w_in, b_mlstm_if, w_mlstm_conv, w_mlstm_hnorm, w_gla_gate2, b_gla_gate,
                w_gla_hnorm, w_qk_norm, w_cmp_pe, w_cmp_1, w_cmp_2, w_out, w_norm_ffn, w_ffn_up, w_ffn_down):
    wi = w_in[l]
    col = lambda n: wi[:, _OFF[n][0]:_OFF[n][0] + _OFF[n][1]]
    um = _pad_cols(jnp.concatenate([col('m_qk'), col('m_v'), col('m_o'), col('m_i'), col('m_f')], axis=1), UM_W)
    ug = _pad_cols(jnp.concatenate([col('g_q'), col('g_k'), col('g_v'), col('g_o'), col('g_lr')], axis=1), UG_W)
    ncmp = jnp.concatenate([col('n_kc'), col('n_vc')], axis=1)
    nsw = jnp.concatenate([col('n_ks'), col('n_vs'), col('n_kw'), col('n_vw')], axis=1)
    ngate = _pad_cols(col('n_gate'), NGATE_W)
    p = {}
    p['w_cat'] = jnp.concatenate([um, ug, col('n_q'), ncmp, nsw, ngate], axis=1).astype(BF16)
    p['g_mix'] = w_norm_mix[l][None, :]
    p['wconv'] = jnp.pad(w_mlstm_conv[l], ((0, 8 - M_CONV), (0, 0)))
    p['bif'] = _pad_cols(b_mlstm_if[l][None, :], LANES)
    p['gh_m'] = w_mlstm_hnorm[l][None, :]
    p['w2p'] = jnp.pad(w_gla_gate2[l], ((0, LANES - G_RANK), (0, 0)))
    p['bg'] = b_gla_gate[l][None, :]
    p['gh_g'] = w_gla_hnorm[l][None, :]
    gqk = w_qk_norm[l]
    p['gq'] = jnp.tile(gqk[0], N_HEADS)[None, :]
    p['gkc'] = jnp.tile(gqk[1], N_KV)[None, :]
    p['gks'] = jnp.tile(gqk[2], N_KV)[None, :]
    p['gkw'] = jnp.tile(gqk[3], N_KV)[None, :]
    w1 = w_cmp_1[l].reshape(2, 2, CMP_STRIDE, N_HD, CMP_HIDDEN)
    eye = jnp.eye(N_KV, dtype=F32)
    w1r = jnp.einsum('khjdc,gf->kjgdhfc', w1, eye).reshape(2, CMP_STRIDE * N_KV * N_HD, 2 * N_KV * CMP_HIDDEN)
    p['w1r'] = w1r.astype(BF16)
    pe = w_cmp_pe[l].reshape(2, 2, CMP_STRIDE, 1, N_HD)
    pe = jnp.broadcast_to(pe, (2, 2, CMP_STRIDE, N_KV, N_HD)).reshape(2, 2, 2048)
    p['pe_r'] = jnp.pad(pe, ((0, 0), (0, 6), (0, 0)))
    p['w2r'] = jnp.einsum('kcd,gf->kgcfd', w_cmp_2[l], eye).reshape(2, N_KV * CMP_HIDDEN, N_KV * N_HD).astype(BF16)
    p['w_out'] = w_out[l].astype(BF16)
    p['g_ffn'] = w_norm_ffn[l][None, :]
    nf = 2
    tf = D_FF // nf
    wu = w_ffn_up[l]
    p['w_up_r'] = jnp.stack([jnp.concatenate([wu[:, f * tf:(f + 1) * tf], wu[:, D_FF + f * tf:D_FF + (f + 1) * tf]],
                                             axis=1) for f in range(nf)]).astype(BF16)
    p['w_dn_r'] = w_ffn_down[l].reshape(nf, tf, D_MODEL).astype(BF16)
    return p


def _cmp2slc(n_cmp, nc_pad, nsel):
    m = np.zeros((nc_pad, nsel), np.float32)
    per = SLC_BLOCK // CMP_STRIDE
    for n in range(n_cmp):
        for u in range(CMP_LEN // CMP_STRIDE):
            m[n + 1, (n + u) // per] += 1.0
    return jnp.asarray(m, BF16)


def _expand_mat(nsel, kvlen):
    return jnp.asarray((np.arange(kvlen)[None, :] // SLC_BLOCK) == np.arange(nsel)[:, None], BF16)


def _group_mean_mat():
    idx = np.arange(LANES) // N_HD
    return jnp.asarray((idx[:, None] == idx[None, :]) / float(N_HD), BF16)


def _mixers(p, um, ug, mstate, gstate, *, L, n_valid):
    conv0, ct0, n0, m0 = mstate
    hm, conv_o, ct_o, n_o, m_o = _mlstm(um, conv0, ct0, n0, m0, p['wconv'], p['bif'], p['gh_m'], L=L, n_valid=n_valid)
    hg, s_o = _gla(ug, gstate, p['w2p'], p['bg'], p['gh_g'], L=L, n_valid=n_valid)
    return hm, hg, (conv_o, ct_o, n_o, m_o), s_o


def _mlstm_state_in(conv, c, n, m):
    b = conv.shape[0]
    conv0 = jnp.pad(conv.astype(F32), ((0, 0), (8 - (M_CONV - 1), 0), (0, 0)))
    ct0 = jnp.swapaxes(c.astype(F32), -1, -2).reshape(b, M_WIDTH, M_HD)
    n0 = jnp.broadcast_to(n.astype(F32).reshape(b, 1, M_WIDTH), (b, 8, M_WIDTH))
    m0 = jnp.broadcast_to(jnp.pad(m.astype(F32), ((0, 0), (0, 8 - M_HEADS)))[:, :, None], (b, 8, LANES))
    return conv0, ct0, n0, m0


def _mlstm_state_out(conv_o, ct_o, n_o, m_o):
    b = conv_o.shape[0]
    c = jnp.swapaxes(ct_o.reshape(b, M_HEADS, M_HD, M_HD), -1, -2)
    return c, n_o[:, 0].reshape(b, M_HEADS, M_HD), m_o[:, :M_HEADS, 0], conv_o[:, 8 - (M_CONV - 1):]


def kernel(x_prompt, x_sample, cache_cmp_kv, cache_slc_kv, state_win_kv, state_mlstm_C, state_mlstm_n,
           state_mlstm_m, state_mlstm_conv, state_gla_S, page_table, w_norm_mix, w_in, b_mlstm_if,
           w_mlstm_conv, w_mlstm_hnorm, w_gla_gate2, b_gla_gate, w_gla_hnorm, w_qk_norm, w_cmp_pe,
           w_cmp_1, w_cmp_2, w_out, w_norm_ffn, w_ffn_up, w_ffn_down):
    b, t, _ = x_prompt.shape
    db, td, _ = x_sample.shape
    depth = w_in.shape[0]
    n_pages = page_table.shape[1]
    past = n_pages * PAGE_SIZE
    win_buf = state_win_kv.shape[2]
    assert t % 512 == 0 and td < CMP_STRIDE and td <= SAMPLE_T and win_buf == WINDOW
    assert n_pages % SLC_PAGES == 0 and n_pages % CMP_PAGES == 0

    bm = _group_mean_mat()
    cos_p, sin_p = _rope_tables(jnp.arange(t))
    nu_p = t // CMP_STRIDE
    ncmp_p = (t - CMP_LEN) // CMP_STRIDE + 1
    cos_cp, sin_cp = _rope_tables(jnp.arange(nu_p) * CMP_STRIDE + CMP_LEN - CMP_STRIDE - 1)
    nsel_p = -(-t // SLC_BLOCK)
    nsel_p = -(-nsel_p // LANES) * LANES
    c2s_p = _cmp2slc(ncmp_p, nu_p, nsel_p)
    e_p = _expand_mat(nsel_p, t).T
    tq_p = 256
    cos_s, sin_s = _rope_tables(past + jnp.arange(SAMPLE_T))
    nu_s = past // CMP_STRIDE
    ncmp_s = (past + td - CMP_LEN) // CMP_STRIDE + 1
    cos_cs, sin_cs = _rope_tables(jnp.arange(nu_s) * CMP_STRIDE + CMP_LEN - CMP_STRIDE - 1)
    nsel_s = -(-(-(-(past + td) // SLC_BLOCK)) // LANES) * LANES
    c2s_s = _cmp2slc(ncmp_s, nu_s, nsel_s)
    e_s = _expand_mat(nsel_s, past + PAGE_SIZE)
    e_s_main, e_s_new = e_s[:, :past], e_s[:, past:]
    row_minor = lambda a: jnp.transpose(a, (0, 1, 3, 4, 5, 2))
    cmp_t, slc_t, win_t = row_minor(cache_cmp_kv), row_minor(cache_slc_kv), row_minor(state_win_kv)

    xp = x_prompt.reshape(b * t, D_MODEL)
    xs = jnp.pad(x_sample, ((0, 0), (0, SAMPLE_T - td), (0, 0))).reshape(db * SAMPLE_T, D_MODEL)

    zero_m = _mlstm_state_in(jnp.zeros((b, M_CONV - 1, 2 * M_WIDTH), F32), jnp.zeros((b, M_HEADS, M_HD, M_HD), F32),
                             jnp.zeros((b, M_HEADS, M_HD), F32), jnp.zeros((b, M_HEADS), F32))
    zero_g = jnp.zeros((b, LANES, G_DV), F32)

    pl_out = [[] for _ in range(8)]
    sl_out = [[] for _ in range(8)]
    for l in range(depth):
        p = _prep_layer(l, w_norm_mix, w_in, b_mlstm_if, w_mlstm_conv, w_mlstm_hnorm, w_gla_gate2, b_gla_gate,
                        w_gla_hnorm, w_qk_norm, w_cmp_pe, w_cmp_1, w_cmp_2, w_out, w_norm_ffn, w_ffn_up, w_ffn_down)
        um, ug, nq, ncmp, nsw, ngate = _pre(xp, p['g_mix'], p['w_cat'])
        r3 = lambda a, bb, tt: a.reshape(bb, tt, a.shape[-1])
        hm, hg, mst, gst = _mixers(p, r3(um, b, t), r3(ug, b, t), zero_m, zero_g, L=M_CHUNK, n_valid=M_CHUNK)
        q_hm, ks, vs, kw, vw, slc_f, win_f, gates = _rows(r3(nq, b, t), r3(nsw, b, t), r3(ngate, b, t), cos_p, sin_p,
                                                          p['gq'], p['gks'], p['gkw'], bm)
        kc, vc = _compress(ncmp.reshape(b, nu_p, 4096), p['w1r'], p['pe_r'], p['w2r'], p['gkc'], cos_cp, sin_cp, bm)
        oc, sel = _cmp_attn(q_hm, kc, vc, c2s_p, tq=tq_p, pos0=0, n_cmp=ncmp_p)
        hn = _slcwin(q_hm, ks, vs, kw, vw, sel, e_p, oc, gates, tq=tq_p)
        xp = _post(xp, hm.reshape(b * t, -1), hg.reshape(b * t, -1), hn.reshape(b * t, -1),
                   p['w_out'], p['g_ffn'], p['w_up_r'], p['w_dn_r'])
        c_o, n_o, m_o, conv_o = _mlstm_state_out(*mst)
        kv6 = lambda a: a.reshape(a.shape[0], a.shape[1], 2, N_KV, N_HD)
        for lst, val in zip(pl_out, (kv6(ncmp.reshape(b, t, 256)), kv6(slc_f), kv6(win_f[:, t - min(WINDOW, t):]),
                                     c_o, n_o, m_o, conv_o, gst.reshape(b, G_HEADS, G_DK, G_DV))):
            lst.append(val)

        um, ug, nq, ncmp, nsw, ngate = _pre(xs, p['g_mix'], p['w_cat'])
        mstate = _mlstm_state_in(state_mlstm_conv[l], state_mlstm_C[l], state_mlstm_n[l], state_mlstm_m[l])
        gstate = state_gla_S[l].astype(F32).reshape(db, LANES, G_DV)
        hm, hg, mst, gst = _mixers(p, r3(um, db, SAMPLE_T), r3(ug, db, SAMPLE_T), mstate, gstate,
                                   L=SAMPLE_T, n_valid=td)
        q_hm, ks_n, vs_n, kw_n, vw_n, slc_f, win_f, gates = _rows(
            r3(nq, db, SAMPLE_T), r3(nsw, db, SAMPLE_T), r3(ngate, db, SAMPLE_T), cos_s, sin_s,
            p['gq'], p['gks'], p['gkw'], bm)
        kc, vc = _compress_paged(page_table, cmp_t, l, p['w1r'], p['pe_r'], p['w2r'], p['gkc'], cos_cs, sin_cs, bm)
        oc, sel = _cmp_attn(q_hm, kc, vc, c2s_s, tq=SAMPLE_T, pos0=past, n_cmp=ncmp_s)
        pad_page = lambda a: jnp.pad(a, ((0, 0), (0, PAGE_SIZE - a.shape[1]), (0, 0)))
        hn = _slc_sample(page_table, slc_t, l, q_hm, sel, e_s_main, e_s_new, pad_page(slc_f), win_t,
                         pad_page(win_f), oc, gates, ts=-(-td // 8) * 8)
        new_win = jnp.concatenate([state_win_kv[l][:, td:].astype(F32), kv6(win_f[:, :td])], axis=1)
        xs = _post(xs, hm.reshape(db * SAMPLE_T, -1), hg.reshape(db * SAMPLE_T, -1), hn.reshape(db * SAMPLE_T, -1),
                   p['w_out'], p['g_ffn'], p['w_up_r'], p['w_dn_r'])
        c_o, n_o, m_o, conv_o = _mlstm_state_out(*mst)
        for lst, val in zip(sl_out, (kv6(ncmp.reshape(db, SAMPLE_T, 256)[:, :td]), kv6(slc_f[:, :td]),
                                     new_win,
                                     c_o, n_o, m_o, conv_o, gst.reshape(db, G_HEADS, G_DK, G_DV))):
            lst.append(val)

    outs_p = [jnp.stack(a) for a in pl_out]
    outs_s = [jnp.stack(a) for a in sl_out]
    y_p = xp.reshape(b, t, D_MODEL)
    y_s = xs.reshape(db, SAMPLE_T, D_MODEL)[:, :td]
    return (y_p, y_s, *outs_p, *outs_s)
```

```python
import functools
import math

import numpy as np
import jax
import jax.numpy as jnp
from jax import lax
from jax.experimental import pallas as pl
from jax.experimental.pallas import tpu as pltpu

F32 = jnp.float32
BF16 = jnp.bfloat16

D_MODEL = 1024
M_HEADS, M_HD, M_WIDTH, M_CONV, M_CHUNK = 4, 64, 256, 4, 64
G_HEADS, G_DK, G_DV, G_WIDTH, G_RANK, G_TAU, G_CHUNK = 4, 32, 64, 256, 16, 16.0, 64
N_HEADS, N_HD, N_KV, N_REP, N_WIDTH = 8, 64, 2, 4, 512
CMP_LEN, CMP_STRIDE, CMP_HIDDEN = 32, 16, 128
SLC_BLOCK, SLC_TOPK, WINDOW = 64, 16, 512
ROPE_THETA = 10000.0
D_FF = 2816
PAGE_SIZE = 128
NEG = -1e30
FORCE_BONUS = 1e4
EPS = 1e-6
LOG2E = 1.4426950408889634

LANES = 128
VMEM_LIMIT = 56 * 1024 * 1024
KV_TILE = 512
SAMPLE_T = 16

_OFF = {}
_o = 0
for _name, _w in (('m_qk', 512), ('m_v', 256), ('m_i', 4), ('m_f', 4), ('m_o', 256),
                  ('g_q', 128), ('g_k', 128), ('g_v', 256), ('g_lr', 16), ('g_o', 256),
                  ('n_q', 512), ('n_kc', 128), ('n_vc', 128), ('n_ks', 128),
                  ('n_vs', 128), ('n_kw', 128), ('n_vw', 128), ('n_gate', 24)):
    _OFF[_name] = (_o, _w)
    _o += _w
UM_W, UG_W, NQ_W, NCMP_W, NSW_W, NGATE_W = 1152, 896, 512, 256, 512, 128


def _cparams(sem):
    return pltpu.CompilerParams(dimension_semantics=sem, vmem_limit_bytes=VMEM_LIMIT)


def _dot(a, b):
    return jnp.dot(a, b, preferred_element_type=F32)


def _dot_nt(a, b):
    return lax.dot_general(a, b, (((1,), (1,)), ((), ())), preferred_element_type=F32)


def _dot_hi(a, b):
    return jnp.dot(a, b, preferred_element_type=F32, precision=lax.Precision.HIGHEST)


def _dot_split(a, b_bf):
    hi = a.astype(BF16)
    lo = (a - hi.astype(F32)).astype(BF16)
    return _dot(hi, b_bf) + _dot(lo, b_bf)


def _log_sigmoid(x):
    return jnp.minimum(x, 0.0) - jnp.log(1.0 + jnp.exp(-jnp.abs(x)))


def _sigmoid(x):
    return 1.0 / (1.0 + jnp.exp(-x))


def _transpose(x):
    r, c = x.shape
    if r < LANES:
        x = jnp.concatenate([x, jnp.zeros((LANES - r, c), x.dtype)], axis=0)
    parts = [x[:, i * LANES:(i + 1) * LANES].T[:, :r] for i in range(c // LANES)]
    return parts[0] if len(parts) == 1 else jnp.concatenate(parts, axis=0)


def _transpose2d(x):
    r, c = x.shape
    rp, cp = -(-r // LANES) * LANES, -(-c // LANES) * LANES
    if cp > c:
        x = jnp.concatenate([x, jnp.zeros((r, cp - c), x.dtype)], axis=1)
    if rp > r:
        x = jnp.concatenate([x, jnp.zeros((rp - r, cp), x.dtype)], axis=0)
    out_rows = []
    for j in range(cp // LANES):
        blocks = [x[i * LANES:(i + 1) * LANES, j * LANES:(j + 1) * LANES].T for i in range(rp // LANES)]
        out_rows.append(blocks[0] if len(blocks) == 1 else jnp.concatenate(blocks, axis=1))
    out = out_rows[0] if len(out_rows) == 1 else jnp.concatenate(out_rows, axis=0)
    return out[:c, :r]


def _iota(shape, dim):
    return lax.broadcasted_iota(jnp.int32, shape, dim)


def _pre_kernel(x_ref, g_ref, w_ref, um_ref, ug_ref, nq_ref, ncmp_ref, nsw_ref, ngate_ref):
    x = x_ref[...]
    h = x * lax.rsqrt(jnp.mean(x * x, axis=-1, keepdims=True) + EPS) * g_ref[...]
    u = _dot(h.astype(BF16), w_ref[...])
    o = 0
    for ref, w in ((um_ref, UM_W), (ug_ref, UG_W), (nq_ref, NQ_W), (ncmp_ref, NCMP_W),
                   (nsw_ref, NSW_W), (ngate_ref, NGATE_W)):
        ref[...] = u[:, o:o + w]
        o += w


def _pre(x2, g, w_cat):
    m = x2.shape[0]
    tm = min(m, 256)
    widths = (UM_W, UG_W, NQ_W, NCMP_W, NSW_W, NGATE_W)
    return pl.pallas_call(
        _pre_kernel,
        grid=(m // tm,),
        in_specs=[pl.BlockSpec((tm, D_MODEL), lambda i: (i, 0)),
                  pl.BlockSpec((1, D_MODEL), lambda i: (0, 0)),
                  pl.BlockSpec((D_MODEL, sum(widths)), lambda i: (0, 0))],
        out_specs=[pl.BlockSpec((tm, w), lambda i: (i, 0)) for w in widths],
        out_shape=[jax.ShapeDtypeStruct((m, w), F32) for w in widths],
        compiler_params=_cparams(("parallel",)),
        name="pre_proj",
    )(x2, g, w_cat)


def _post_kernel(x_ref, hm_ref, hg_ref, hn_ref, wout_ref, g_ref, wup_ref, wdn_ref, o_ref, h2_ref, *, tf):
    @pl.when(pl.program_id(1) == 0)
    def _():
        xn = x_ref[...]
        xn = xn + _dot(hm_ref[...], wout_ref[0:256, :])
        xn = xn + _dot(hg_ref[...], wout_ref[256:512, :])
        xn = xn + _dot(hn_ref[...], wout_ref[512:1024, :])
        o_ref[...] = xn
        h2 = xn * lax.rsqrt(jnp.mean(xn * xn, axis=-1, keepdims=True) + EPS) * g_ref[...]
        h2_ref[...] = h2.astype(BF16)

    au = _dot(h2_ref[...], wup_ref[0])
    a = au[:, :tf]
    act = (a * _sigmoid(a) * au[:, tf:]).astype(BF16)
    o_ref[...] += _dot(act, wdn_ref[0])


def _post(x2, hm, hg, hn, w_out, g, w_up_r, w_dn_r):
    m = x2.shape[0]
    tm = min(m, 512)
    nf, _, tf2 = w_up_r.shape
    tf = tf2 // 2
    return pl.pallas_call(
        functools.partial(_post_kernel, tf=tf),
        grid=(m // tm, nf),
        in_specs=[pl.BlockSpec((tm, D_MODEL), lambda i, f: (i, 0)),
                  pl.BlockSpec((tm, M_WIDTH), lambda i, f: (i, 0)),
                  pl.BlockSpec((tm, G_WIDTH), lambda i, f: (i, 0)),
                  pl.BlockSpec((tm, N_WIDTH), lambda i, f: (i, 0)),
                  pl.BlockSpec((D_MODEL, D_MODEL), lambda i, f: (0, 0)),
                  pl.BlockSpec((1, D_MODEL), lambda i, f: (0, 0)),
                  pl.BlockSpec((1, D_MODEL, tf2), lambda i, f: (f, 0, 0)),
                  pl.BlockSpec((1, tf, D_MODEL), lambda i, f: (f, 0, 0))],
        out_specs=pl.BlockSpec((tm, D_MODEL), lambda i, f: (i, 0)),
        out_shape=jax.ShapeDtypeStruct((m, D_MODEL), F32),
        scratch_shapes=[pltpu.VMEM((tm, D_MODEL), BF16)],
        compiler_params=_cparams(("parallel", "arbitrary")),
        name="post_ffn",
    )(x2, hm, hg, hn, w_out, g, w_up_r, w_dn_r)


CHUNK_UNROLL = 4


def _mlstm_kernel(um_ref, conv0_ref, ct0_ref, n0_ref, m0_ref, wconv_ref, bif_ref, gh_ref,
                  hm_ref, conv_out_ref, ct_out_ref, n_out_ref, m_out_ref,
                  xpad, q_s, k_s, ct_s, n_s, m_s, *, L, n_valid, blk):
    j = pl.program_id(1)

    @pl.when(j == 0)
    def _():
        xpad[0:8, :] = conv0_ref[0]
        ct_s[...] = ct0_ref[0]
        n_s[...] = n0_ref[0]
        m_s[...] = m0_ref[0]

    qk_pre = um_ref[0, :, 0:2 * M_WIDTH]
    xpad[8:8 + blk, :] = qk_pre
    wc = wconv_ref[...]
    acc = (xpad[5:5 + blk, :] * wc[0:1] + xpad[6:6 + blk, :] * wc[1:2]
           + xpad[7:7 + blk, :] * wc[2:3] + qk_pre * wc[3:4])
    qk = acc * _sigmoid(acc)
    q_s[...] = qk[:, :M_WIDTH]
    k_s[...] = qk[:, M_WIDTH:] * (M_HD ** -0.5)
    last = n_valid if blk == L else blk
    conv_out_ref[0] = xpad[last:last + 8, :]
    xpad[0:8, :] = xpad[blk:blk + 8, :]

    row = _iota((L, L), 0)
    col = _iota((L, L), 1)
    causal = row >= col
    tril = causal.astype(F32)
    lane_w = _iota((1, M_WIDTH), 1) // M_HD
    row_w = _iota((M_WIDTH, 1), 0) // M_HD
    valid_col = _iota((L, 1), 0) < n_valid

    row8 = _iota((8, LANES), 0)

    n_chunks = blk // L
    unroll = CHUNK_UNROLL if n_chunks % CHUNK_UNROLL == 0 else 1
    H = range(M_HEADS)
    U = range(unroll)
    hmask = [lane_w == h for h in H]
    lst = slice(n_valid - 1, n_valid)

    def group(i, state):
        ct, n_row, m_tile = state
        r0 = [pl.multiple_of((i * unroll + u) * L, L) for u in U]
        qc = [q_s[pl.ds(r, L), :] for r in r0]
        kc = [k_s[pl.ds(r, L), :] for r in r0]
        act = [um_ref[0, pl.ds(r, L), 1024:1152] + bif_ref[...] for r in r0]
        bcum = [_dot_hi(tril, _log_sigmoid(x)) for x in act]
        act_t = [_transpose(x) for x in act]
        bcum_t = [_transpose(x) for x in bcum]
        k_bf = [x.astype(BF16) for x in kc]
        kt_bf = [_transpose(x).astype(BF16) for x in kc]
        bcol = [[bcum[u][:, 4 + h:5 + h] for h in H] for u in U]
        dmat = [[jnp.where(causal, bcol[u][h] - bcum_t[u][4 + h:5 + h, :] + act_t[u][h:h + 1, :], NEG) for h in H]
                for u in U]
        m_loc = [[jnp.max(dmat[u][h], axis=1, keepdims=True) for h in H] for u in U]
        q_h = [[jnp.where(hmask[h], qc[u], 0.0) for h in H] for u in U]
        q_bf = [[q_h[u][h].astype(BF16) for h in H] for u in U]
        s = [[_dot_nt(q_bf[u][h], k_bf[u]) * jnp.exp(dmat[u][h] - m_loc[u][h]) for h in H] for u in U]
        v_h = [[um_ref[0, pl.ds(r, L), 512 + h * M_HD:512 + (h + 1) * M_HD] for h in H] for r in r0]
        sv = [[_dot(s[u][h].astype(BF16), v_h[u][h].astype(BF16)) for h in H] for u in U]
        ssum = [[jnp.sum(s[u][h], axis=1, keepdims=True) for h in H] for u in U]
        w_l = [[jnp.where(valid_col, jnp.exp(bcol[u][h][lst] - bcol[u][h] + act[u][:, h:h + 1] - m_loc[u][h][lst]), 0.0)
                for h in H] for u in U]
        upd = [[_dot(kt_bf[u], (v_h[u][h] * w_l[u][h]).astype(BF16)) for h in H] for u in U]
        ksum = [[jnp.sum(kc[u] * w_l[u][h], axis=0, keepdims=True) for h in H] for u in U]
        gate = [[_sigmoid(um_ref[0, pl.ds(r, L), 768 + h * M_HD:768 + (h + 1) * M_HD]) for h in H] for r in r0]
        for u in U:
            ct_bf = ct.astype(BF16)
            m_inter = [m_tile[h:h + 1, 0:1] + bcol[u][h] for h in H]
            m_new = [jnp.maximum(m_inter[h], m_loc[u][h]) for h in H]
            f = [jnp.exp(m_loc[u][h] - m_new[h]) for h in H]
            a_inter = [jnp.exp(m_inter[h] - m_new[h]) for h in H]
            num = [a_inter[h] * _dot(q_bf[u][h], ct_bf) + f[h] * sv[u][h] for h in H]
            qn = [jnp.sum(q_h[u][h] * n_row, axis=1, keepdims=True) for h in H]
            den = [a_inter[h] * qn[h] + f[h] * ssum[u][h] for h in H]
            hh = [gate[u][h] * (num[h] / jnp.maximum(jnp.abs(den[h]), jnp.exp(-m_new[h]))) for h in H]
            ms = [jnp.mean(x * x, axis=1, keepdims=True) for x in hh]
            outs = [hh[h] * lax.rsqrt(ms[h] + EPS) * gh_ref[:, h * M_HD:(h + 1) * M_HD] for h in H]
            hm_ref[0, pl.ds(r0[u], L), :] = jnp.concatenate(outs, axis=1).astype(BF16)
            ct_old, n_old = ct, n_row
            for h in H:
                ct = jnp.where(row_w == h, a_inter[h][lst] * ct_old + f[h][lst] * upd[u][h], ct)
                n_row = jnp.where(hmask[h], a_inter[h][lst] * n_old + f[h][lst] * ksum[u][h], n_row)
                m_tile = jnp.where(row8 == h, m_new[h][lst], m_tile)
        return ct, n_row, m_tile

    ct, n_row, m_tile = lax.fori_loop(0, n_chunks // unroll, group, (ct_s[...], n_s[0:1, :], m_s[...]))
    ct_s[...] = ct
    n_s[...] = jnp.broadcast_to(n_row, n_s.shape)
    m_s[...] = m_tile

    @pl.when(j == pl.num_programs(1) - 1)
    def _():
        ct_out_ref[0] = ct
        n_out_ref[0] = jnp.broadcast_to(n_row, n_s.shape)
        m_out_ref[0] = m_tile


def _mlstm(um, conv0, ct0, n0, m0, wconv, bif, gh, *, L, n_valid):
    b, t, _ = um.shape
    blk = min(t, 512)
    kern = functools.partial(_mlstm_kernel, L=L, n_valid=n_valid, blk=blk)
    per_b = lambda shape: pl.BlockSpec((1,) + shape, lambda i, j: (i,) + (0,) * len(shape))
    const = lambda shape: pl.BlockSpec(shape, lambda i, j: (0,) * len(shape))
    return pl.pallas_call(
        kern,
        grid=(b, t // blk),
        in_specs=[pl.BlockSpec((1, blk, UM_W), lambda i, j: (i, j, 0)),
                  per_b((8, 2 * M_WIDTH)), per_b((M_WIDTH, M_HD)), per_b((8, M_WIDTH)), per_b((8, LANES)),
                  const((8, 2 * M_WIDTH)), const((1, LANES)), const((1, M_WIDTH))],
        out_specs=[pl.BlockSpec((1, blk, M_WIDTH), lambda i, j: (i, j, 0)),
                   per_b((8, 2 * M_WIDTH)), per_b((M_WIDTH, M_HD)), per_b((8, M_WIDTH)), per_b((8, LANES))],
        out_shape=[jax.ShapeDtypeStruct((b, t, M_WIDTH), BF16),
                   jax.ShapeDtypeStruct((b, 8, 2 * M_WIDTH), F32),
                   jax.ShapeDtypeStruct((b, M_WIDTH, M_HD), F32),
                   jax.ShapeDtypeStruct((b, 8, M_WIDTH), F32),
                   jax.ShapeDtypeStruct((b, 8, LANES), F32)],
        scratch_shapes=[pltpu.VMEM((blk + 8, 2 * M_WIDTH), F32),
                        pltpu.VMEM((blk, M_WIDTH), F32), pltpu.VMEM((blk, M_WIDTH), F32),
                        pltpu.VMEM((M_WIDTH, M_HD), F32), pltpu.VMEM((8, M_WIDTH), F32),
                        pltpu.VMEM((8, LANES), F32)],
        compiler_params=_cparams(("parallel", "arbitrary")),
        name="mlstm",
    )(um, conv0, ct0, n0, m0, wconv, bif, gh)


def _gla_kernel(ug_ref, s0_ref, w2_ref, bg_ref, gh_ref, hg_ref, s_out_ref, s_s, *, L, n_valid, blk):
    j = pl.program_id(1)

    @pl.when(j == 0)
    def _():
        s_s[...] = s0_ref[0]

    row = _iota((L, L), 0)
    col = _iota((L, L), 1)
    causal = row >= col
    tril = causal.astype(F32)
    lane_k = _iota((1, LANES), 1) // G_DK
    row_k = _iota((LANES, 1), 0) // G_DK
    valid_col = _iota((L, 1), 0) < n_valid
    mid = max(n_valid // 2, 1)

    n_chunks = blk // L
    unroll = CHUNK_UNROLL if n_chunks % CHUNK_UNROLL == 0 else 1
    H = range(G_HEADS)
    U = range(unroll)
    hmask = [lane_k == h for h in H]

    def group(i, s_all):
        r0 = [pl.multiple_of((i * unroll + u) * L, L) for u in U]
        q = [ug_ref[0, pl.ds(r, L), 0:128] * (G_DK ** -0.5) for r in r0]
        k = [ug_ref[0, pl.ds(r, L), 128:256] for r in r0]
        z = [_dot_hi(ug_ref[0, pl.ds(r, L), 768:896], w2_ref[...]) + bg_ref[...] for r in r0]
        g = [_log_sigmoid(x) * (1.0 / G_TAU) for x in z]
        bc = [_dot_hi(tril, x) for x in g]
        c_ref = [x[mid - 1:mid] for x in bc]
        last = [x[n_valid - 1:n_valid] for x in bc]
        qe = [q[u] * jnp.exp(bc[u] - c_ref[u]) for u in U]
        ke = [(k[u] * jnp.exp(c_ref[u] - bc[u])).astype(BF16) for u in U]
        qin = [q[u] * jnp.exp(bc[u]) for u in U]
        kd_t = [_transpose(jnp.where(valid_col, k[u] * jnp.exp(last[u] - bc[u]), 0.0)).astype(BF16) for u in U]
        decay = [jnp.exp(_transpose(jnp.broadcast_to(x, (8, LANES)))[:, 0:1]) for x in last]
        v_h = [[ug_ref[0, pl.ds(r, L), 256 + h * G_DV:256 + (h + 1) * G_DV].astype(BF16) for h in H] for r in r0]
        a = [[jnp.where(causal, _dot_nt(jnp.where(hmask[h], qe[u], 0.0).astype(BF16), ke[u]), 0.0) for h in H]
             for u in U]
        intra = [[_dot(a[u][h].astype(BF16), v_h[u][h]) for h in H] for u in U]
        q_in = [[jnp.where(hmask[h], qin[u], 0.0).astype(BF16) for h in H] for u in U]
        upd = [[_dot(kd_t[u], v_h[u][h]) for h in H] for u in U]
        gate = [[ug_ref[0, pl.ds(r, L), 512 + h * G_DV:512 + (h + 1) * G_DV] for h in H] for r in r0]
        gate = [[x * _sigmoid(x) * gh_ref[:, h * G_DV:(h + 1) * G_DV] for h, x in enumerate(gs)] for gs in gate]
        for u in U:
            s_bf = s_all.astype(BF16)
            o = [_dot(q_in[u][h], s_bf) + intra[u][h] for h in H]
            ms = [jnp.mean(x * x, axis=1, keepdims=True) for x in o]
            outs = [o[h] * lax.rsqrt(ms[h] + EPS) * gate[u][h] for h in H]
            hg_ref[0, pl.ds(r0[u], L), :] = jnp.concatenate(outs, axis=1).astype(BF16)
            s_all = decay[u] * s_all
            for h in H:
                s_all = s_all + jnp.where(row_k == h, upd[u][h], 0.0)
        return s_all

    s_fin = lax.fori_loop(0, n_chunks // unroll, group, s_s[...])
    s_s[...] = s_fin

    @pl.when(j == pl.num_programs(1) - 1)
    def _():
        s_out_ref[0] = s_fin


def _gla(ug, s0, w2p, bg, gh, *, L, n_valid):
    b, t, _ = ug.shape
    blk = min(t, 512)
    kern = functools.partial(_gla_kernel, L=L, n_valid=n_valid, blk=blk)
    return pl.pallas_call(
        kern,
        grid=(b, t // blk),
        in_specs=[pl.BlockSpec((1, blk, UG_W), lambda i, j: (i, j, 0)),
                  pl.BlockSpec((1, LANES, G_DV), lambda i, j: (i, 0, 0)),
                  pl.BlockSpec((LANES, LANES), lambda i, j: (0, 0)),
                  pl.BlockSpec((1, LANES), lambda i, j: (0, 0)),
                  pl.BlockSpec((1, G_WIDTH), lambda i, j: (0, 0))],
        out_specs=[pl.BlockSpec((1, blk, G_WIDTH), lambda i, j: (i, j, 0)),
                   pl.BlockSpec((1, LANES, G_DV), lambda i, j: (i, 0, 0))],
        out_shape=[jax.ShapeDtypeStruct((b, t, G_WIDTH), BF16),
                   jax.ShapeDtypeStruct((b, LANES, G_DV), F32)],
        scratch_shapes=[pltpu.VMEM((LANES, G_DV), F32)],
        compiler_params=_cparams(("parallel", "arbitrary")),
        name="gla",
    )(ug, s0, w2p, bg, gh)


def _group_mean_sq(x, bm_bf):
    return _dot_split(x * x, bm_bf)


def _rope_slab(y, cos, sin_signed):
    lane = _iota(y.shape, 1)
    rot = jnp.where((lane % N_HD) < (N_HD // 2), pltpu.roll(y, 96, axis=1), pltpu.roll(y, 32, axis=1))
    return y * cos + rot * sin_signed


def _norm_rope_slab(x, g, cos, sin_signed, bm_bf):
    y = x * lax.rsqrt(_group_mean_sq(x, bm_bf) + EPS) * g
    return _rope_slab(y, cos, sin_signed)


def _rows_kernel(nq_ref, nsw_ref, ngate_ref, cos_ref, sin_ref, gq_ref, gks_ref, gkw_ref, bm_ref,
                 q_ref, ks_ref, vs_ref, kw_ref, vw_ref, slc_ref, win_ref, gates_ref):
    cos = cos_ref[...]
    sin = sin_ref[...]
    bm = bm_ref[...]
    low = _iota((1, LANES), 1) < N_HD
    for sl in range(4):
        x = nq_ref[0, :, sl * LANES:(sl + 1) * LANES]
        y = _norm_rope_slab(x, gq_ref[:, sl * LANES:(sl + 1) * LANES], cos, sin, bm) * (N_HD ** -0.5 * LOG2E)
        y_sw = pltpu.roll(y, N_HD, axis=1)
        if sl < 2:
            even, odd = jnp.where(low, y, 0.0), jnp.where(low, y_sw, 0.0)
        else:
            even, odd = jnp.where(low, 0.0, y_sw), jnp.where(low, 0.0, y)
        q_ref[0, 2 * sl] = even.astype(BF16)
        q_ref[0, 2 * sl + 1] = odd.astype(BF16)
    ks = _norm_rope_slab(nsw_ref[0, :, 0:128], gks_ref[...], cos, sin, bm)
    vs = nsw_ref[0, :, 128:256]
    kw = _norm_rope_slab(nsw_ref[0, :, 256:384], gkw_ref[...], cos, sin, bm)
    vw = nsw_ref[0, :, 384:512]
    slc_ref[0, :, 0:128] = ks
    slc_ref[0, :, 128:256] = vs
    win_ref[0, :, 0:128] = kw
    win_ref[0, :, 128:256] = vw
    ks_ref[0] = ks.astype(BF16)
    kw_ref[0] = kw.astype(BF16)
    for ref, val in ((vs_ref, vs), (vw_ref, vw)):
        ref[0, 0] = jnp.where(low, val, 1.0).astype(BF16)
        ref[0, 1] = jnp.where(low, 1.0, val).astype(BF16)
    gt = _sigmoid(ngate_ref[0])
    gates_ref[0, 0] = gt
    gates_ref[0, 1] = pltpu.roll(gt, LANES - 3 * N_REP, axis=1)


def _rows(nq, nsw, ngate, cos, sin, gq, gks, gkw, bm):
    b, t, _ = nq.shape
    tm = min(t, 512)
    tok = lambda w: pl.BlockSpec((1, tm, w), lambda i, j: (i, j, 0))
    const = lambda shape: pl.BlockSpec(shape, lambda i, j: (0,) * len(shape))
    k_shape = jax.ShapeDtypeStruct((b, t, LANES), BF16)
    v_shape = jax.ShapeDtypeStruct((b, N_KV, t, LANES), BF16)
    per_group = pl.BlockSpec((1, N_KV, tm, LANES), lambda i, j: (i, 0, j, 0))
    return pl.pallas_call(
        _rows_kernel,
        grid=(b, t // tm),
        in_specs=[tok(NQ_W), tok(NSW_W), tok(NGATE_W),
                  pl.BlockSpec((tm, LANES), lambda i, j: (j, 0)),
                  pl.BlockSpec((tm, LANES), lambda i, j: (j, 0)),
                  const((1, NQ_W)), const((1, LANES)), const((1, LANES)), const((LANES, LANES))],
        out_specs=[pl.BlockSpec((1, N_HEADS, tm, LANES), lambda i, j: (i, 0, j, 0)),
                   tok(LANES), per_group, tok(LANES), per_group, tok(256), tok(256), per_group],
        out_shape=[jax.ShapeDtypeStruct((b, N_HEADS, t, LANES), BF16), k_shape, v_shape, k_shape, v_shape,
                   jax.ShapeDtypeStruct((b, t, 256), F32), jax.ShapeDtypeStruct((b, t, 256), F32),
                   jax.ShapeDtypeStruct((b, N_KV, t, LANES), F32)],
        compiler_params=_cparams(("parallel", "parallel")),
        name="nsa_rows",
    )(nq, nsw, ngate, cos, sin, gq, gks, gkw, bm)


def _block_mlp(x_bf, w1, pe, w2, prev_first):
    p = _dot(x_bf, w1)
    bias = _dot(pe.astype(BF16), w1)
    pa = p[:, :256] + bias[0:1, :256]
    pb = p[:, 256:] + bias[1:2, 256:]
    pa_prev = jnp.where(_iota(pa.shape, 0) == 0, prev_first, pltpu.roll(pa, 1, axis=0))
    pre = pa_prev + pb
    hid = (pre * _sigmoid(pre)).astype(BF16)
    return _dot(hid, w2), pa[pa.shape[0] - 1:, :]


def _compress_kernel(u_ref, w1_ref, pe_ref, w2_ref, gkc_ref, cos_ref, sin_ref, bm_ref, kc_ref, vc_ref):
    outs = []
    for kv in range(2):
        x = jnp.concatenate(
            [u_ref[0, :, j * 256 + kv * LANES:j * 256 + (kv + 1) * LANES] for j in range(CMP_STRIDE)],
            axis=1).astype(BF16)
        outs.append(_block_mlp(x, w1_ref[kv], pe_ref[kv], w2_ref[kv], jnp.zeros((1, 256), F32))[0])
    kc_ref[0] = _norm_rope_slab(outs[0], gkc_ref[...], cos_ref[...], sin_ref[...], bm_ref[...]).astype(BF16)
    vc_ref[0] = outs[1].astype(BF16)


CMP_PAGES = 32
PAGE_UNITS = PAGE_SIZE // CMP_STRIDE


def _compress_paged_kernel(pt_ref, *refs):
    pages = refs[:CMP_PAGES]
    (w1_ref, pe_ref, w2_ref, gkc_ref, cos_ref, sin_ref, bm_ref, kc_ref, vc_ref, t_scr, x_scr, carry_scr) = refs[CMP_PAGES:]

    @pl.when(pl.program_id(1) == 0)
    def _():
        carry_scr[...] = jnp.zeros(carry_scr.shape, F32)

    outs = []
    for kv in range(2):
        for i in range(CMP_PAGES):
            t_scr[i] = pages[i][0, 0, kv].reshape(LANES, PAGE_SIZE).T
            for j in range(CMP_STRIDE):
                x_scr[i * PAGE_UNITS:(i + 1) * PAGE_UNITS, j * LANES:(j + 1) * LANES] = \
                    t_scr[i, pl.ds(j, PAGE_UNITS, stride=CMP_STRIDE), :]
        out, last = _block_mlp(x_scr[...].astype(BF16), w1_ref[kv], pe_ref[kv], w2_ref[kv], carry_scr[kv, 0:1, :])
        carry_scr[kv, 0:1, :] = last
        outs.append(out)
    kc_ref[0] = _norm_rope_slab(outs[0], gkc_ref[...], cos_ref[...], sin_ref[...], bm_ref[...]).astype(BF16)
    vc_ref[0] = outs[1].astype(BF16)


def _compress_paged(page_table, cache_t, layer, w1r, pe_r, w2r, gkc, cos_c, sin_c, bm):
    db, n_pages = page_table.shape
    step_units = CMP_PAGES * PAGE_UNITS
    nu = n_pages * PAGE_UNITS

    def page_spec(i):
        return pl.BlockSpec((1, 1, 2, N_KV, N_HD, PAGE_SIZE),
                            lambda b, c, pt: (layer, pt[b, c * CMP_PAGES + i], 0, 0, 0, 0))

    const = lambda shape: pl.BlockSpec(shape, lambda b, c, pt: (0,) * len(shape))
    tab = pl.BlockSpec((step_units, LANES), lambda b, c, pt: (c, 0))
    out = jax.ShapeDtypeStruct((db, nu, LANES), BF16)
    ospec = pl.BlockSpec((1, step_units, LANES), lambda b, c, pt: (b, c, 0))
    return pl.pallas_call(
        _compress_paged_kernel,
        grid_spec=pltpu.PrefetchScalarGridSpec(
            num_scalar_prefetch=1,
            grid=(db, n_pages // CMP_PAGES),
            in_specs=[page_spec(i) for i in range(CMP_PAGES)] + [
                const((2, 2048, 512)), const((2, 8, 2048)), const((2, 256, LANES)),
                const((1, LANES)), tab, tab, const((LANES, LANES))],
            out_specs=[ospec, ospec],
            scratch_shapes=[pltpu.VMEM((CMP_PAGES, PAGE_SIZE, LANES), F32),
                            pltpu.VMEM((step_units, CMP_STRIDE * LANES), F32),
                            pltpu.VMEM((2, 8, 256), F32)]),
        out_shape=[out, out],
        compiler_params=_cparams(("parallel", "arbitrary")),
        name="nsa_compress_paged",
    )(page_table, *([cache_t] * CMP_PAGES), w1r, pe_r, w2r, gkc, cos_c, sin_c, bm)


def _compress(units, w1r, pe_r, w2r, gkc, cos_c, sin_c, bm):
    b, nu, _ = units.shape
    const = lambda shape: pl.BlockSpec(shape, lambda i: (0,) * len(shape))
    out = jax.ShapeDtypeStruct((b, nu, LANES), BF16)
    return pl.pallas_call(
        _compress_kernel,
        grid=(b,),
        in_specs=[pl.BlockSpec((1, nu, 4096), lambda i: (i, 0, 0)),
                  const((2, 2048, 512)), const((2, 8, 2048)), const((2, 256, LANES)),
                  const((1, LANES)), const((nu, LANES)), const((nu, LANES)), const((LANES, LANES))],
        out_specs=[pl.BlockSpec((1, nu, LANES), lambda i: (i, 0, 0))] * 2,
        out_shape=[out, out],
        compiler_params=_cparams(("parallel",)),
        name="nsa_compress",
    )(units, w1r, pe_r, w2r, gkc, cos_c, sin_c, bm)


def _cmp_kernel(q_ref, kc_ref, vc_ref, c2s_ref, oc_ref, sel_ref, *, tq, pos0, n_cmp, nsel):
    qi = pl.program_id(1)
    nc_pad = kc_ref.shape[1]
    q = q_ref[0].reshape(N_HEADS * tq, LANES)
    s = _dot_nt(q, kc_ref[0]).reshape(N_HEADS, tq, nc_pad)
    qpos = pos0 + qi * tq + _iota((tq, 1), 0)
    c_idx = _iota((tq, nc_pad), 1)
    c_real = jnp.where(c_idx >= 1, c_idx, nc_pad + n_cmp) <= n_cmp
    c_ok = jnp.where(c_real, c_idx * CMP_STRIDE + (CMP_LEN - CMP_STRIDE - 1), jnp.int32(2 ** 30)) <= qpos
    s = s + jnp.where(c_ok, 0.0, NEG)[None]
    p = jnp.exp2(s - jnp.max(s, axis=-1, keepdims=True))
    p = p * ((qpos >= CMP_LEN - 1).astype(F32)[None] / jnp.sum(p, axis=-1, keepdims=True))
    oc_ref[0] = _dot(p.reshape(N_HEADS * tq, nc_pad).astype(BF16), vc_ref[0]).reshape(N_HEADS, tq, LANES)
    psum = jnp.sum(p.reshape(N_KV, N_REP, tq, nc_pad), axis=1).reshape(N_KV * tq, nc_pad)
    imp = _dot_split(psum, c2s_ref[...])
    work = _transpose2d(imp)
    ncol = N_KV * tq
    qpos_row = pos0 + qi * tq + _iota((1, ncol), 1) % tq
    blk = _iota((nsel, ncol), 0)
    cur = qpos_row // SLC_BLOCK
    forced = jnp.where(blk == 0, 1.0, jnp.where(blk == cur, 1.0, jnp.where(blk == cur - 1, 1.0, 0.0)))
    work = jnp.where(blk * SLC_BLOCK <= qpos_row, work + FORCE_BONUS * forced, NEG)
    blk_f = blk.astype(F32)
    for _ in range(SLC_TOPK):
        m = jnp.max(work, axis=0, keepdims=True)
        idx = jnp.min(jnp.where(work == m, blk_f, float(nsel)), axis=0, keepdims=True)
        work = jnp.where(blk_f == idx, -jnp.inf, work)
    sel = jnp.where(work == -jnp.inf, 1.0, 0.0)
    sel_ref[0] = _transpose2d(sel).reshape(N_KV, tq, nsel).astype(BF16)


def _cmp_attn(q_pad, kc, vc, c2s, *, tq, pos0, n_cmp):
    b, _, t, _ = q_pad.shape
    nc_pad = kc.shape[1]
    nsel = c2s.shape[1]
    kern = functools.partial(_cmp_kernel, tq=tq, pos0=pos0, n_cmp=n_cmp, nsel=nsel)
    return pl.pallas_call(
        kern,
        grid=(b, t // tq),
        in_specs=[pl.BlockSpec((1, N_HEADS, tq, LANES), lambda i, j: (i, 0, j, 0)),
                  pl.BlockSpec((1, nc_pad, LANES), lambda i, j: (i, 0, 0)),
                  pl.BlockSpec((1, nc_pad, LANES), lambda i, j: (i, 0, 0)),
                  pl.BlockSpec((nc_pad, nsel), lambda i, j: (0, 0))],
        out_specs=[pl.BlockSpec((1, N_HEADS, tq, LANES), lambda i, j: (i, 0, j, 0)),
                   pl.BlockSpec((1, N_KV, tq, nsel), lambda i, j: (i, 0, j, 0))],
        out_shape=[jax.ShapeDtypeStruct((b, N_HEADS, t, LANES), F32),
                   jax.ShapeDtypeStruct((b, N_KV, t, nsel), BF16)],
        compiler_params=_cparams(("parallel", "parallel")),
        name="nsa_cmp_topk",
    )(q_pad, kc, vc, c2s)


SCORE_DTYPE = F32
SLAB = 64


def _tile_scores(q_parts, k, s_scr, *, tq, width):
    for r in range(N_REP):
        s_scr[r * tq:(r + 1) * tq, 0:width] = _dot_nt(q_parts[r], k).astype(s_scr.dtype)


def _tile_update(v, bias_scr, s_scr, p_scr, m_scr, alpha_scr, acc_scr, *, tq, width):
    nch = width // LANES
    sd = s_scr.dtype

    def chunk(rows, i, c):
        x = s_scr[rows, c * LANES:(c + 1) * LANES]
        if bias_scr is not None:
            x = x + bias_scr[i * SLAB:(i + 1) * SLAB, c * LANES:(c + 1) * LANES]
        return x

    slabs = [(slice(r * tq + i * SLAB, r * tq + (i + 1) * SLAB), i) for r in range(N_REP) for i in range(tq // SLAB)]
    for rows, i in slabs:
        mx = chunk(rows, i, 0)
        for c in range(1, nch):
            mx = jnp.maximum(mx, chunk(rows, i, c))
        m_old = m_scr[rows, :]
        m_new = jnp.maximum(m_old, jnp.max(mx.astype(F32), axis=1, keepdims=True)).astype(sd).astype(F32)
        alpha_scr[rows, :] = jnp.exp2(m_old - m_new)
        m_scr[rows, :] = m_new
    for rows, i in slabs:
        m_new = m_scr[rows, :].astype(sd)
        for c in range(nch):
            p_scr[rows, c * LANES:(c + 1) * LANES] = jnp.exp2(chunk(rows, i, c) - m_new).astype(BF16)
    for r in range(N_REP):
        rs = slice(r * tq, (r + 1) * tq)
        acc_scr[rs, :] = alpha_scr[rs, :] * acc_scr[rs, :] + _dot(p_scr[rs, 0:width], v)


def _slcwin_kernel(q_ref, ks_ref, vs_ref, kw_ref, vw_ref, sel_ref, et_ref, oc_ref, gates_ref, hn_ref,
                   s_scr, bias_scr, p_scr, m_scr, alpha_scr, acc_scr, ow_scr, *, tq):
    g = pl.program_id(1)
    qi = pl.program_id(2)
    rows = N_REP * tq
    q0 = qi * tq
    qpos = q0 + _iota((tq, 1), 0)
    scr = dict(p_scr=p_scr, m_scr=m_scr, alpha_scr=alpha_scr, acc_scr=acc_scr, tq=tq)

    def reset():
        m_scr[...] = jnp.full((rows, LANES), NEG, F32)
        acc_scr[...] = jnp.zeros((rows, LANES), F32)

    def result():
        acc = acc_scr[...]
        return acc / pltpu.roll(acc, N_HD, axis=1)

    wlen = WINDOW + tq
    start = pl.multiple_of(jnp.maximum(q0 - WINDOW, 0), tq)
    reset()
    dist = qpos - (start + _iota((tq, wlen), 1))
    bias_scr[:, 0:wlen] = jnp.where(jnp.where(dist >= 0, dist, WINDOW) < WINDOW, 0.0, NEG).astype(bias_scr.dtype)
    _tile_scores([q_ref[0, r] for r in range(N_REP)], kw_ref[0, pl.ds(start, wlen), :], s_scr, tq=tq, width=wlen)
    _tile_update(vw_ref[0, 0, pl.ds(start, wlen), :], bias_scr, s_scr, width=wlen, **scr)
    ow_scr[...] = result()

    reset()
    sel_m = ((sel_ref[0, 0].astype(F32) - 1.0) * (-NEG)).astype(BF16)
    q_aug = [jnp.concatenate([q_ref[0, r], sel_m], axis=1) for r in range(N_REP)]
    n_kv = (q0 + tq + KV_TILE - 1) // KV_TILE

    def scores(j, dst):
        k0 = pl.multiple_of(j * KV_TILE, KV_TILE)
        k_aug = jnp.concatenate([ks_ref[0, pl.ds(k0, KV_TILE), :], et_ref[pl.ds(k0, KV_TILE), :]], axis=1)
        _tile_scores(q_aug, k_aug, dst, tq=tq, width=KV_TILE)

    def values(j):
        return vs_ref[0, 0, pl.ds(pl.multiple_of(j * KV_TILE, KV_TILE), KV_TILE), :]

    n_plain = n_kv - 1

    def body(j, carry):
        scores(j, s_scr)
        _tile_update(values(j), None, s_scr, width=KV_TILE, **scr)
        return carry

    lax.fori_loop(0, n_plain, body, 0)
    bias_scr[:, 0:KV_TILE] = jnp.where(n_plain * KV_TILE + _iota((tq, KV_TILE), 1) <= qpos, 0.0, NEG).astype(
        bias_scr.dtype)
    scores(n_plain, s_scr)
    _tile_update(values(n_plain), bias_scr, s_scr, width=KV_TILE, **scr)
    o_s_all = result()

    gt = gates_ref[0, 0]
    low = _iota((1, LANES), 1) < N_HD
    vals = []
    for r in range(N_REP):
        rs = slice(r * tq, (r + 1) * tq)
        vals.append(gt[:, 3 * r:3 * r + 1] * oc_ref[0, r] + gt[:, 3 * r + 1:3 * r + 2] * o_s_all[rs]
                    + gt[:, 3 * r + 2:3 * r + 3] * ow_scr[rs, :])
    for pair in range(N_REP // 2):
        a, b = vals[2 * pair], vals[2 * pair + 1]
        a_sw, b_sw = pltpu.roll(a, N_HD, axis=1), pltpu.roll(b, N_HD, axis=1)
        lo = jnp.where(g == 0, a, a_sw)
        hi = jnp.where(g == 0, b_sw, b)
        hn_ref[0, :, pair * LANES:(pair + 1) * LANES] = jnp.where(low, lo, hi).astype(BF16)


def _slcwin(q_pad, ks, vs, kw, vw, sel, e_t, oc, gates, *, tq):
    b, _, t, _ = q_pad.shape
    nsel = sel.shape[3]
    assert nsel == LANES and e_t.shape == (t, LANES)
    rows = N_REP * tq
    wlen = WINDOW + tq
    kern = functools.partial(_slcwin_kernel, tq=tq)
    qspec = pl.BlockSpec((1, N_REP, tq, LANES), lambda i, g, j: (i, g, j, 0))
    keys = pl.BlockSpec((1, t, LANES), lambda i, g, j: (i, 0, 0))
    vals = pl.BlockSpec((1, 1, t, LANES), lambda i, g, j: (i, g, 0, 0))
    return pl.pallas_call(
        kern,
        grid=(b, N_KV, t // tq),
        in_specs=[qspec, keys, vals, keys, vals,
                  pl.BlockSpec((1, 1, tq, nsel), lambda i, g, j: (i, g, j, 0)),
                  pl.BlockSpec((t, LANES), lambda i, g, j: (0, 0)),
                  qspec,
                  pl.BlockSpec((1, 1, tq, LANES), lambda i, g, j: (i, g, j, 0))],
        out_specs=pl.BlockSpec((1, tq, N_REP * N_HD), lambda i, g, j: (i, j, g)),
        out_shape=jax.ShapeDtypeStruct((b, t, N_WIDTH), BF16),
        scratch_shapes=[pltpu.VMEM((rows, wlen), SCORE_DTYPE), pltpu.VMEM((tq, wlen), SCORE_DTYPE),
                        pltpu.VMEM((rows, wlen), BF16),
                        pltpu.VMEM((rows, LANES), F32), pltpu.VMEM((rows, LANES), F32),
                        pltpu.VMEM((rows, LANES), F32), pltpu.VMEM((rows, LANES), F32)],
        compiler_params=_cparams(("parallel", "parallel", "parallel")),
        name="nsa_slc_win",
    )(q_pad, ks, vs, kw, vw, sel, e_t, oc, gates)


SLC_PAGES = 32


def _heads_bias(b2, ts):
    w = b2.shape[1]
    return jnp.broadcast_to(b2.reshape(N_KV, 1, ts, w), (N_KV, N_REP, ts, w)).reshape(N_HEADS * ts, w)


def _slc_sample_kernel(pt_ref, *refs, ts):
    pages = refs[:SLC_PAGES]
    (q_ref, sel_ref, e_ref, enew_ref, newslc_ref, win_ref, newwin_ref, oc_ref, gates_ref,
     hn_ref, m_scr, l_scr, acc_scr) = refs[SLC_PAGES:]
    c = pl.program_id(1)
    rows = N_HEADS * ts
    q = q_ref[0].astype(F32)[:, 0:ts, :].reshape(rows, LANES).astype(BF16)
    sel2 = sel_ref[0].astype(F32)[:, 0:ts, :].reshape(N_KV * ts, sel_ref.shape[3]).astype(BF16)

    @pl.when(c == 0)
    def _():
        m_scr[...] = jnp.full((rows, 1), NEG, F32)
        l_scr[...] = jnp.zeros((rows, 1), F32)
        acc_scr[...] = jnp.zeros((rows, LANES), F32)

    def online(s, pv_fn):
        m_old = m_scr[...]
        m_new = jnp.maximum(m_old, jnp.max(s, axis=1, keepdims=True))
        p = jnp.exp2(s - m_new)
        alpha = jnp.exp2(m_old - m_new)
        l_scr[...] = alpha * l_scr[...] + jnp.sum(p, axis=1, keepdims=True)
        acc_scr[...] = alpha * acc_scr[...] + pv_fn(p.astype(BF16))
        m_scr[...] = m_new

    picked = _dot(sel2, e_ref[...])
    bias = _heads_bias(picked * (-NEG) + NEG, ts)
    k_t = jnp.concatenate([pg[0, 0, 0].reshape(LANES, PAGE_SIZE).astype(BF16) for pg in pages], axis=1)
    v_t = jnp.concatenate([pg[0, 0, 1].reshape(LANES, PAGE_SIZE).astype(BF16) for pg in pages], axis=1)
    online(_dot(q, k_t) + bias, lambda p: _dot_nt(p, v_t))

    @pl.when(c == pl.num_programs(1) - 1)
    def _():
        tok = _iota((N_KV * ts, 1), 0) % ts
        key = _iota((N_KV * ts, PAGE_SIZE), 1)
        new = newslc_ref[0].astype(BF16)
        picked_n = _dot(sel2, enew_ref[...])
        bias_n = _heads_bias(jnp.where(key <= tok, picked_n, 0.0) * (-NEG) + NEG, ts)
        online(_dot_nt(q, new[:, :LANES]) + bias_n, lambda p: _dot(p, new[:, LANES:]))
        kw_t = win_ref[0, 0, 0].reshape(LANES, WINDOW).astype(BF16)
        vw_t = win_ref[0, 0, 1].reshape(LANES, WINDOW).astype(BF16)
        nwin = newwin_ref[0].astype(BF16)
        wkey = _iota((N_KV * ts, WINDOW), 1)
        bias_w = jnp.concatenate([jnp.where(wkey > tok, 0.0, NEG), jnp.where(key <= tok, 0.0, NEG)], axis=1)
        s_w = jnp.concatenate([_dot(q, kw_t), _dot_nt(q, nwin[:, :LANES])], axis=1) + _heads_bias(bias_w, ts)
        p_w = jnp.exp2(s_w - jnp.max(s_w, axis=1, keepdims=True))
        l_w = jnp.sum(p_w, axis=1, keepdims=True)
        p_w = p_w.astype(BF16)
        o_w = (_dot_nt(p_w[:, :WINDOW], vw_t) + _dot(p_w[:, WINDOW:], nwin[:, LANES:])) / l_w
        o_s = acc_scr[...] / l_scr[...]
        low = _iota((1, LANES), 1) < N_HD
        vals = []
        for h in range(N_HEADS):
            g, r = divmod(h, N_REP)
            gt = gates_ref[0, g, 0:ts, :]
            rs = slice(h * ts, (h + 1) * ts)
            vals.append(gt[:, 3 * r:3 * r + 1] * oc_ref[0, h, 0:ts, :] + gt[:, 3 * r + 1:3 * r + 2] * o_s[rs]
                        + gt[:, 3 * r + 2:3 * r + 3] * o_w[rs])
        slabs = []
        for pair in range(N_HEADS // 2):
            a, b = vals[2 * pair], vals[2 * pair + 1]
            if pair < N_REP // 2:
                lo, hi = a, pltpu.roll(b, N_HD, axis=1)
            else:
                lo, hi = pltpu.roll(a, N_HD, axis=1), b
            slabs.append(jnp.where(low, lo, hi))
        out = jnp.concatenate(slabs, axis=1)
        pad = jnp.zeros((hn_ref.shape[1] - ts, N_WIDTH), F32)
        hn_ref[0] = jnp.concatenate([out, pad], axis=0).astype(BF16)


def _slc_sample(page_table, cache_t, layer, q_pad, sel, e_main, e_new, new_slc, win_t, new_win, oc, gates, *, ts):
    db, n_pages = page_table.shape
    tpad = q_pad.shape[2]
    nsel = sel.shape[3]
    n_steps = n_pages // SLC_PAGES
    step_keys = SLC_PAGES * PAGE_SIZE
    rows = N_HEADS * ts

    def page_spec(i):
        return pl.BlockSpec((1, 1, 2, N_KV, N_HD, PAGE_SIZE),
                            lambda b, c, pt: (layer, pt[b, c * SLC_PAGES + i], 0, 0, 0, 0))

    per_b = lambda shape: pl.BlockSpec((1,) + shape, lambda b, c, pt: (b,) + (0,) * len(shape))
    return pl.pallas_call(
        functools.partial(_slc_sample_kernel, ts=ts),
        grid_spec=pltpu.PrefetchScalarGridSpec(
            num_scalar_prefetch=1,
            grid=(db, n_steps),
            in_specs=[page_spec(i) for i in range(SLC_PAGES)] + [
                per_b((N_HEADS, tpad, LANES)), per_b((N_KV, tpad, nsel)),
                pl.BlockSpec((nsel, step_keys), lambda b, c, pt: (0, c)),
                pl.BlockSpec((nsel, PAGE_SIZE), lambda b, c, pt: (0, 0)),
                per_b((PAGE_SIZE, 256)),
                pl.BlockSpec((1, 1, 2, N_KV, N_HD, WINDOW), lambda b, c, pt: (layer, b, 0, 0, 0, 0)),
                per_b((PAGE_SIZE, 256)), per_b((N_HEADS, tpad, LANES)), per_b((N_KV, tpad, LANES))],
            out_specs=per_b((tpad, N_WIDTH)),
            scratch_shapes=[pltpu.VMEM((rows, 1), F32), pltpu.VMEM((rows, 1), F32), pltpu.VMEM((rows, LANES), F32)]),
        out_shape=jax.ShapeDtypeStruct((db, tpad, N_WIDTH), BF16),
        compiler_params=_cparams(("parallel", "arbitrary")),
        name="nsa_slc_sample",
    )(page_table, *([cache_t] * SLC_PAGES), q_pad, sel, e_main, e_new, new_slc, win_t, new_win, oc, gates)


def _rope_tables(pos):
    half = N_HD // 2
    inv = ROPE_THETA ** (-jnp.arange(half, dtype=F32) / half)
    ang = pos.astype(F32)[:, None] * inv[None, :]
    cos, sin = jnp.cos(ang), jnp.sin(ang)
    return jnp.tile(jnp.concatenate([cos, cos], axis=1), (1, 2)), jnp.tile(jnp.concatenate([-sin, sin], axis=1), (1, 2))


def _pad_cols(a, w):
    return jnp.pad(a, ((0, 0), (0, w - a.shape[1])))


def _prep_layer(l, w_norm_mix, w_in, b_mlstm_if, w_mlstm_conv, w_mlstm_hnorm, w_gla_gate2, b_gla_gate,
                w_gla_hnorm, w_qk_norm, w_cmp_pe, w_cmp_1, w_cmp_2, w_out, w_norm_ffn, w_ffn_up, w_ffn_down):
    wi = w_in[l]
    col = lambda n: wi[:, _OFF[n][0]:_OFF[n][0] + _OFF[n][1]]
    um = _pad_cols(jnp.concatenate([col('m_qk'), col('m_v'), col('m_o'), col('m_i'), col('m_f')], axis=1), UM_W)
    ug = _pad_cols(jnp.concatenate([col('g_q'), col('g_k'), col('g_v'), col('g_o'), col('g_lr')], axis=1), UG_W)
    ncmp = jnp.concatenate([col('n_kc'), col('n_vc')], axis=1)
    nsw = jnp.concatenate([col('n_ks'), col('n_vs'), col('n_kw'), col('n_vw')], axis=1)
    ngate = _pad_cols(col('n_gate'), NGATE_W)
    p = {}
    p['w_cat'] = jnp.concatenate([um, ug, col('n_q'), ncmp, nsw, ngate], axis=1).astype(BF16)
    p['g_mix'] = w_norm_mix[l][None, :]
    p['wconv'] = jnp.pad(w_mlstm_conv[l], ((0, 8 - M_CONV), (0, 0)))
    p['bif'] = _pad_cols(b_mlstm_if[l][None, :], LANES)
    p['gh_m'] = w_mlstm_hnorm[l][None, :]
    p['w2p'] = jnp.pad(w_gla_gate2[l], ((0, LANES - G_RANK), (0, 0)))
    p['bg'] = b_gla_gate[l][None, :]
    p['gh_g'] = w_gla_hnorm[l][None, :]
    gqk = w_qk_norm[l]
    p['gq'] = jnp.tile(gqk[0], N_HEADS)[None, :]
    p['gkc'] = jnp.tile(gqk[1], N_KV)[None, :]
    p['gks'] = jnp.tile(gqk[2], N_KV)[None, :]
    p['gkw'] = jnp.tile(gqk[3], N_KV)[None, :]
    w1 = w_cmp_1[l].reshape(2, 2, CMP_STRIDE, N_HD, CMP_HIDDEN)
    eye = jnp.eye(N_KV, dtype=F32)
    w1r = jnp.einsum('khjdc,gf->kjgdhfc', w1, eye).reshape(2, CMP_STRIDE * N_KV * N_HD, 2 * N_KV * CMP_HIDDEN)
    p['w1r'] = w1r.astype(BF16)
    pe = w_cmp_pe[l].reshape(2, 2, CMP_STRIDE, 1, N_HD)
    pe = jnp.broadcast_to(pe, (2, 2, CMP_STRIDE, N_KV, N_HD)).reshape(2, 2, 2048)
    p['pe_r'] = jnp.pad(pe, ((0, 0), (0, 6), (0, 0)))
    p['w2r'] = jnp.einsum('kcd,gf->kgcfd', w_cmp_2[l], eye).reshape(2, N_KV * CMP_HIDDEN, N_KV * N_HD).astype(BF16)
    p['w_out'] = w_out[l].astype(BF16)
    p['g_ffn'] = w_norm_ffn[l][None, :]
    nf = 2
    tf = D_FF // nf
    wu = w_ffn_up[l]
    p['w_up_r'] = jnp.stack([jnp.concatenate([wu[:, f * tf:(f + 1) * tf], wu[:, D_FF + f * tf:D_FF + (f + 1) * tf]],
                                             axis=1) for f in range(nf)]).astype(BF16)
    p['w_dn_r'] = w_ffn_down[l].reshape(nf, tf, D_MODEL).astype(BF16)
    return p


def _cmp2slc(n_cmp, nc_pad, nsel):
    m = np.zeros((nc_pad, nsel), np.float32)
    per = SLC_BLOCK // CMP_STRIDE
    for n in range(n_cmp):
        for u in range(CMP_LEN // CMP_STRIDE):
            m[n + 1, (n + u) // per] += 1.0
    return jnp.asarray(m, BF16)


def _expand_mat(nsel, kvlen):
    return jnp.asarray((np.arange(kvlen)[None, :] // SLC_BLOCK) == np.arange(nsel)[:, None], BF16)


def _group_mean_mat():
    idx = np.arange(LANES) // N_HD
    return jnp.asarray((idx[:, None] == idx[None, :]) / float(N_HD), BF16)


def _mixers(p, um, ug, mstate, gstate, *, L, n_valid):
    conv0, ct0, n0, m0 = mstate
    hm, conv_o, ct_o, n_o, m_o = _mlstm(um, conv0, ct0, n0, m0, p['wconv'], p['bif'], p['gh_m'], L=L, n_valid=n_valid)
    hg, s_o = _gla(ug, gstate, p['w2p'], p['bg'], p['gh_g'], L=L, n_valid=n_valid)
    return hm, hg, (conv_o, ct_o, n_o, m_o), s_o


def _mlstm_state_in(conv, c, n, m):
    b = conv.shape[0]
    conv0 = jnp.pad(conv.astype(F32), ((0, 0), (8 - (M_CONV - 1), 0), (0, 0)))
    ct0 = jnp.swapaxes(c.astype(F32), -1, -2).reshape(b, M_WIDTH, M_HD)
    n0 = jnp.broadcast_to(n.astype(F32).reshape(b, 1, M_WIDTH), (b, 8, M_WIDTH))
    m0 = jnp.broadcast_to(jnp.pad(m.astype(F32), ((0, 0), (0, 8 - M_HEADS)))[:, :, None], (b, 8, LANES))
    return conv0, ct0, n0, m0


def _mlstm_state_out(conv_o, ct_o, n_o, m_o):
    b = conv_o.shape[0]
    c = jnp.swapaxes(ct_o.reshape(b, M_HEADS, M_HD, M_HD), -1, -2)
    return c, n_o[:, 0].reshape(b, M_HEADS, M_HD), m_o[:, :M_HEADS, 0], conv_o[:, 8 - (M_CONV - 1):]


def kernel(x_prompt, x_sample, cache_cmp_kv, cache_slc_kv, state_win_kv, state_mlstm_C, state_mlstm_n,
           state_mlstm_m, state_mlstm_conv, state_gla_S, page_table, w_norm_mix, w_in, b_mlstm_if,
           w_mlstm_conv, w_mlstm_hnorm, w_gla_gate2, b_gla_gate, w_gla_hnorm, w_qk_norm, w_cmp_pe,
           w_cmp_1, w_cmp_2, w_out, w_norm_ffn, w_ffn_up, w_ffn_down):
    b, t, _ = x_prompt.shape
    db, td, _ = x_sample.shape
    depth = w_in.shape[0]
    n_pages = page_table.shape[1]
    past = n_pages * PAGE_SIZE
    win_buf = state_win_kv.shape[2]
    assert t % 512 == 0 and td < CMP_STRIDE and td <= SAMPLE_T and win_buf == WINDOW
    assert n_pages % SLC_PAGES == 0 and n_pages % CMP_PAGES == 0

    bm = _group_mean_mat()
    cos_p, sin_p = _rope_tables(jnp.arange(t))
    nu_p = t // CMP_STRIDE
    ncmp_p = (t - CMP_LEN) // CMP_STRIDE + 1
    cos_cp, sin_cp = _rope_tables(jnp.arange(nu_p) * CMP_STRIDE + CMP_LEN - CMP_STRIDE - 1)
    nsel_p = -(-t // SLC_BLOCK)
    nsel_p = -(-nsel_p // LANES) * LANES
    c2s_p = _cmp2slc(ncmp_p, nu_p, nsel_p)
    e_p = _expand_mat(nsel_p, t).T
    tq_p = 256
    cos_s, sin_s = _rope_tables(past + jnp.arange(SAMPLE_T))
    nu_s = past // CMP_STRIDE
    ncmp_s = (past + td - CMP_LEN) // CMP_STRIDE + 1
    cos_cs, sin_cs = _rope_tables(jnp.arange(nu_s) * CMP_STRIDE + CMP_LEN - CMP_STRIDE - 1)
    nsel_s = -(-(-(-(past + td) // SLC_BLOCK)) // LANES) * LANES
    c2s_s = _cmp2slc(ncmp_s, nu_s, nsel_s)
    e_s = _expand_mat(nsel_s, past + PAGE_SIZE)
    e_s_main, e_s_new = e_s[:, :past], e_s[:, past:]
    row_minor = lambda a: jnp.transpose(a, (0, 1, 3, 4, 5, 2))
    cmp_t, slc_t, win_t = row_minor(cache_cmp_kv), row_minor(cache_slc_kv), row_minor(state_win_kv)

    xp = x_prompt.reshape(b * t, D_MODEL)
    xs = jnp.pad(x_sample, ((0, 0), (0, SAMPLE_T - td), (0, 0))).reshape(db * SAMPLE_T, D_MODEL)

    zero_m = _mlstm_state_in(jnp.zeros((b, M_CONV - 1, 2 * M_WIDTH), F32), jnp.zeros((b, M_HEADS, M_HD, M_HD), F32),
                             jnp.zeros((b, M_HEADS, M_HD), F32), jnp.zeros((b, M_HEADS), F32))
    zero_g = jnp.zeros((b, LANES, G_DV), F32)

    pl_out = [[] for _ in range(8)]
    sl_out = [[] for _ in range(8)]
    for l in range(depth):
        p = _prep_layer(l, w_norm_mix, w_in, b_mlstm_if, w_mlstm_conv, w_mlstm_hnorm, w_gla_gate2, b_gla_gate,
                        w_gla_hnorm, w_qk_norm, w_cmp_pe, w_cmp_1, w_cmp_2, w_out, w_norm_ffn, w_ffn_up, w_ffn_down)
        um, ug, nq, ncmp, nsw, ngate = _pre(xp, p['g_mix'], p['w_cat'])
        r3 = lambda a, bb, tt: a.reshape(bb, tt, a.shape[-1])
        hm, hg, mst, gst = _mixers(p, r3(um, b, t), r3(ug, b, t), zero_m, zero_g, L=M_CHUNK, n_valid=M_CHUNK)
        q_hm, ks, vs, kw, vw, slc_f, win_f, gates = _rows(r3(nq, b, t), r3(nsw, b, t), r3(ngate, b, t), cos_p, sin_p,
                                                          p['gq'], p['gks'], p['gkw'], bm)
        kc, vc = _compress(ncmp.reshape(b, nu_p, 4096), p['w1r'], p['pe_r'], p['w2r'], p['gkc'], cos_cp, sin_cp, bm)
        oc, sel = _cmp_attn(q_hm, kc, vc, c2s_p, tq=tq_p, pos0=0, n_cmp=ncmp_p)
        hn = _slcwin(q_hm, ks, vs, kw, vw, sel, e_p, oc, gates, tq=tq_p)
        xp = _post(xp, hm.reshape(b * t, -1), hg.reshape(b * t, -1), hn.reshape(b * t, -1),
                   p['w_out'], p['g_ffn'], p['w_up_r'], p['w_dn_r'])
        c_o, n_o, m_o, conv_o = _mlstm_state_out(*mst)
        kv6 = lambda a: a.reshape(a.shape[0], a.shape[1], 2, N_KV, N_HD)
        for lst, val in zip(pl_out, (kv6(ncmp.reshape(b, t, 256)), kv6(slc_f), kv6(win_f[:, t - min(WINDOW, t):]),
                                     c_o, n_o, m_o, conv_o, gst.reshape(b, G_HEADS, G_DK, G_DV))):
            lst.append(val)

        um, ug, nq, ncmp, nsw, ngate = _pre(xs, p['g_mix'], p['w_cat'])
        mstate = _mlstm_state_in(state_mlstm_conv[l], state_mlstm_C[l], state_mlstm_n[l], state_mlstm_m[l])
        gstate = state_gla_S[l].astype(F32).reshape(db, LANES, G_DV)
        hm, hg, mst, gst = _mixers(p, r3(um, db, SAMPLE_T), r3(ug, db, SAMPLE_T), mstate, gstate,
                                   L=SAMPLE_T, n_valid=td)
        q_hm, ks_n, vs_n, kw_n, vw_n, slc_f, win_f, gates = _rows(
            r3(nq, db, SAMPLE_T), r3(nsw, db, SAMPLE_T), r3(ngate, db, SAMPLE_T), cos_s, sin_s,
            p['gq'], p['gks'], p['gkw'], bm)
        kc, vc = _compress_paged(page_table, cmp_t, l, p['w1r'], p['pe_r'], p['w2r'], p['gkc'], cos_cs, sin_cs, bm)
        oc, sel = _cmp_attn(q_hm, kc, vc, c2s_s, tq=SAMPLE_T, pos0=past, n_cmp=ncmp_s)
        pad_page = lambda a: jnp.pad(a, ((0, 0), (0, PAGE_SIZE - a.shape[1]), (0, 0)))
        hn = _slc_sample(page_table, slc_t, l, q_hm, sel, e_s_main, e_s_new, pad_page(slc_f), win_t,
                         pad_page(win_f), oc, gates, ts=-(-td // 8) * 8)
        new_win = jnp.concatenate([state_win_kv[l][:, td:].astype(F32), kv6(win_f[:, :td])], axis=1)
        xs = _post(xs, hm.reshape(db * SAMPLE_T, -1), hg.reshape(db * SAMPLE_T, -1), hn.reshape(db * SAMPLE_T, -1),
                   p['w_out'], p['g_ffn'], p['w_up_r'], p['w_dn_r'])
        c_o, n_o, m_o, conv_o = _mlstm_state_out(*mst)
        for lst, val in zip(sl_out, (kv6(ncmp.reshape(db, SAMPLE_T, 256)[:, :td]), kv6(slc_f[:, :td]),
                                     new_win,
                                     c_o, n_o, m_o, conv_o, gst.reshape(db, G_HEADS, G_DK, G_DV))):
            lst.append(val)

    outs_p = [jnp.stack(a) for a in pl_out]
    outs_s = [jnp.stack(a) for a in sl_out]
    y_p = xp.reshape(b, t, D_MODEL)
    y_s = xs.reshape(db, SAMPLE_T, D_MODEL)[:, :td]
    return (y_p, y_s, *outs_p, *outs_s)
```

```python
import functools

import numpy as np
import jax
import jax.numpy as jnp
from jax import lax
from jax.experimental import pallas as pl
from jax.experimental.pallas import tpu as pltpu

F32 = jnp.float32
BF16 = jnp.bfloat16

D_MODEL = 1024
M_HEADS, M_HD, M_WIDTH, M_CONV, M_CHUNK = 4, 64, 256, 4, 64
G_HEADS, G_DK, G_DV, G_WIDTH, G_RANK, G_TAU, G_CHUNK = 4, 32, 64, 256, 16, 16.0, 64
N_HEADS, N_HD, N_KV, N_REP, N_WIDTH = 8, 64, 2, 4, 512
CMP_LEN, CMP_STRIDE, CMP_HIDDEN = 32, 16, 128
SLC_BLOCK, SLC_TOPK, WINDOW = 64, 16, 512
ROPE_THETA = 10000.0
D_FF = 2816
PAGE_SIZE = 128
NEG = -1e30
FORCE_BONUS = 1e4
EPS = 1e-6
LOG2E = 1.4426950408889634

LANES = 128
VMEM_LIMIT = 56 * 1024 * 1024
KV_TILE = 512
SAMPLE_T = 16

_OFF = {}
_o = 0
for _name, _w in (('m_qk', 512), ('m_v', 256), ('m_i', 4), ('m_f', 4), ('m_o', 256),
                  ('g_q', 128), ('g_k', 128), ('g_v', 256), ('g_lr', 16), ('g_o', 256),
                  ('n_q', 512), ('n_kc', 128), ('n_vc', 128), ('n_ks', 128),
                  ('n_vs', 128), ('n_kw', 128), ('n_vw', 128), ('n_gate', 24)):
    _OFF[_name] = (_o, _w)
    _o += _w
UM_W, UG_W, NQ_W, NCMP_W, NSW_W, NGATE_W = 1152, 896, 512, 256, 512, 128


def _cparams(sem):
    return pltpu.CompilerParams(dimension_semantics=sem, vmem_limit_bytes=VMEM_LIMIT)


def _dot(a, b):
    return jnp.dot(a, b, preferred_element_type=F32)


def _dot_nt(a, b):
    return lax.dot_general(a, b, (((1,), (1,)), ((), ())), preferred_element_type=F32)


def _dot_hi(a, b):
    return jnp.dot(a, b, preferred_element_type=F32, precision=lax.Precision.HIGHEST)


def _dot_split(a, b_bf):
    hi = a.astype(BF16)
    lo = (a - hi.astype(F32)).astype(BF16)
    return _dot(hi, b_bf) + _dot(lo, b_bf)


def _log_sigmoid(x):
    return jnp.minimum(x, 0.0) - jnp.log(1.0 + jnp.exp(-jnp.abs(x)))


def _sigmoid(x):
    return 1.0 / (1.0 + jnp.exp(-x))


def _transpose(x):
    r, c = x.shape
    if r < LANES:
        x = jnp.concatenate([x, jnp.zeros((LANES - r, c), x.dtype)], axis=0)
    parts = [x[:, i * LANES:(i + 1) * LANES].T[:, :r] for i in range(c // LANES)]
    return parts[0] if len(parts) == 1 else jnp.concatenate(parts, axis=0)


def _transpose2d(x):
    r, c = x.shape
    rp, cp = -(-r // LANES) * LANES, -(-c // LANES) * LANES
    if cp > c:
        x = jnp.concatenate([x, jnp.zeros((r, cp - c), x.dtype)], axis=1)
    if rp > r:
        x = jnp.concatenate([x, jnp.zeros((rp - r, cp), x.dtype)], axis=0)
    out_rows = []
    for j in range(cp // LANES):
        blocks = [x[i * LANES:(i + 1) * LANES, j * LANES:(j + 1) * LANES].T for i in range(rp // LANES)]
        out_rows.append(blocks[0] if len(blocks) == 1 else jnp.concatenate(blocks, axis=1))
    out = out_rows[0] if len(out_rows) == 1 else jnp.concatenate(out_rows, axis=0)
    return out[:c, :r]


def _iota(shape, dim):
    return lax.broadcasted_iota(jnp.int32, shape, dim)


def _pre_kernel(x_ref, g_ref, w_ref, um_ref, ug_ref, nq_ref, ncmp_ref, nsw_ref, ngate_ref):
    x = x_ref[...]
    h = x * lax.rsqrt(jnp.mean(x * x, axis=-1, keepdims=True) + EPS) * g_ref[...]
    u = _dot(h.astype(BF16), w_ref[...])
    o = 0
    for ref, w in ((um_ref, UM_W), (ug_ref, UG_W), (nq_ref, NQ_W), (ncmp_ref, NCMP_W),
                   (nsw_ref, NSW_W), (ngate_ref, NGATE_W)):
        ref[...] = u[:, o:o + w]
        o += w


def _pre(x2, g, w_cat):
    m = x2.shape[0]
    tm = min(m, 256)
    widths = (UM_W, UG_W, NQ_W, NCMP_W, NSW_W, NGATE_W)
    return pl.pallas_call(
        _pre_kernel,
        grid=(m // tm,),
        in_specs=[pl.BlockSpec((tm, D_MODEL), lambda i: (i, 0)),
                  pl.BlockSpec((1, D_MODEL), lambda i: (0, 0)),
                  pl.BlockSpec((D_MODEL, sum(widths)), lambda i: (0, 0))],
        out_specs=[pl.BlockSpec((tm, w), lambda i: (i, 0)) for w in widths],
        out_shape=[jax.ShapeDtypeStruct((m, w), F32) for w in widths],
        compiler_params=_cparams(("parallel",)),
        name="pre_proj",
    )(x2, g, w_cat)


def _post_kernel(x_ref, hm_ref, hg_ref, hn_ref, wout_ref, g_ref, wup_ref, wdn_ref, o_ref, h2_ref, *, tf):
    @pl.when(pl.program_id(1) == 0)
    def _():
        xn = x_ref[...]
        xn = xn + _dot(hm_ref[...], wout_ref[0:256, :])
        xn = xn + _dot(hg_ref[...], wout_ref[256:512, :])
        xn = xn + _dot(hn_ref[...], wout_ref[512:1024, :])
        o_ref[...] = xn
        h2 = xn * lax.rsqrt(jnp.mean(xn * xn, axis=-1, keepdims=True) + EPS) * g_ref[...]
        h2_ref[...] = h2.astype(BF16)

    au = _dot(h2_ref[...], wup_ref[0])
    a = au[:, :tf]
    act = (a * _sigmoid(a) * au[:, tf:]).astype(BF16)
    o_ref[...] += _dot(act, wdn_ref[0])


def _post(x2, hm, hg, hn, w_out, g, w_up_r, w_dn_r):
    m = x2.shape[0]
    tm = min(m, 512)
    nf, _, tf2 = w_up_r.shape
    tf = tf2 // 2
    return pl.pallas_call(
        functools.partial(_post_kernel, tf=tf),
        grid=(m // tm, nf),
        in_specs=[pl.BlockSpec((tm, D_MODEL), lambda i, f: (i, 0)),
                  pl.BlockSpec((tm, M_WIDTH), lambda i, f: (i, 0)),
                  pl.BlockSpec((tm, G_WIDTH), lambda i, f: (i, 0)),
                  pl.BlockSpec((tm, N_WIDTH), lambda i, f: (i, 0)),
                  pl.BlockSpec((D_MODEL, D_MODEL), lambda i, f: (0, 0)),
                  pl.BlockSpec((1, D_MODEL), lambda i, f: (0, 0)),
                  pl.BlockSpec((1, D_MODEL, tf2), lambda i, f: (f, 0, 0)),
                  pl.BlockSpec((1, tf, D_MODEL), lambda i, f: (f, 0, 0))],
        out_specs=pl.BlockSpec((tm, D_MODEL), lambda i, f: (i, 0)),
        out_shape=jax.ShapeDtypeStruct((m, D_MODEL), F32),
        scratch_shapes=[pltpu.VMEM((tm, D_MODEL), BF16)],
        compiler_params=_cparams(("parallel", "arbitrary")),
        name="post_ffn",
    )(x2, hm, hg, hn, w_out, g, w_up_r, w_dn_r)


CHUNK_UNROLL = 8


def _mlstm_kernel(um_ref, conv0_ref, ct0_ref, n0_ref, m0_ref, wconv_ref, bif_ref, gh_ref,
                  hm_ref, conv_out_ref, ct_out_ref, n_out_ref, m_out_ref,
                  xpad, q_s, k_s, ct_s, n_s, m_s, *, L, n_valid, blk):
    j = pl.program_id(1)

    @pl.when(j == 0)
    def _():
        xpad[0:8, :] = conv0_ref[0]
        ct_s[...] = ct0_ref[0]
        n_s[...] = n0_ref[0]
        m_s[...] = m0_ref[0]

    qk_pre = um_ref[0, :, 0:2 * M_WIDTH]
    xpad[8:8 + blk, :] = qk_pre
    wc = wconv_ref[...]
    acc = (xpad[5:5 + blk, :] * wc[0:1] + xpad[6:6 + blk, :] * wc[1:2]
           + xpad[7:7 + blk, :] * wc[2:3] + qk_pre * wc[3:4])
    qk = acc * _sigmoid(acc)
    q_s[...] = qk[:, :M_WIDTH]
    k_s[...] = qk[:, M_WIDTH:] * (M_HD ** -0.5)
    last = n_valid if blk == L else blk
    conv_out_ref[0] = xpad[last:last + 8, :]
    xpad[0:8, :] = xpad[blk:blk + 8, :]

    row = _iota((L, L), 0)
    col = _iota((L, L), 1)
    causal = row >= col
    tril = causal.astype(F32)
    lane_w = _iota((1, M_WIDTH), 1) // M_HD
    row_w = _iota((M_WIDTH, 1), 0) // M_HD
    valid_col = _iota((L, 1), 0) < n_valid

    row8 = _iota((8, LANES), 0)

    n_chunks = blk // L
    unroll = CHUNK_UNROLL if n_chunks % CHUNK_UNROLL == 0 else 1
    H = range(M_HEADS)
    U = range(unroll)
    hmask = [lane_w == h for h in H]
    lst = slice(n_valid - 1, n_valid)

    def group(i, state):
        ct, n_row, m_tile = state
        r0 = [pl.multiple_of((i * unroll + u) * L, L) for u in U]
        qc = [q_s[pl.ds(r, L), :] for r in r0]
        kc = [k_s[pl.ds(r, L), :] for r in r0]
        act = [um_ref[0, pl.ds(r, L), 1024:1152] + bif_ref[...] for r in r0]
        bcum = [_dot_hi(tril, _log_sigmoid(x)) for x in act]
        act_t = [_transpose(x) for x in act]
        bcum_t = [_transpose(x) for x in bcum]
        k_bf = [x.astype(BF16) for x in kc]
        kt_bf = [_transpose(x).astype(BF16) for x in kc]
        bcol = [[bcum[u][:, 4 + h:5 + h] for h in H] for u in U]
        dmat = [[jnp.where(causal, bcol[u][h] - bcum_t[u][4 + h:5 + h, :] + act_t[u][h:h + 1, :], NEG) for h in H]
                for u in U]
        m_loc = [[jnp.max(dmat[u][h], axis=1, keepdims=True) for h in H] for u in U]
        q_h = [[jnp.where(hmask[h], qc[u], 0.0) for h in H] for u in U]
        q_bf = [[q_h[u][h].astype(BF16) for h in H] for u in U]
        s = [[_dot_nt(q_bf[u][h], k_bf[u]) * jnp.exp(dmat[u][h] - m_loc[u][h]) for h in H] for u in U]
        v_h = [[um_ref[0, pl.ds(r, L), 512 + h * M_HD:512 + (h + 1) * M_HD] for h in H] for r in r0]
        sv = [[_dot(s[u][h].astype(BF16), v_h[u][h].astype(BF16)) for h in H] for u in U]
        ssum = [[jnp.sum(s[u][h], axis=1, keepdims=True) for h in H] for u in U]
        w_l = [[jnp.where(valid_col, jnp.exp(bcol[u][h][lst] - bcol[u][h] + act[u][:, h:h + 1] - m_loc[u][h][lst]), 0.0)
                for h in H] for u in U]
        upd = [[_dot(kt_bf[u], (v_h[u][h] * w_l[u][h]).astype(BF16)) for h in H] for u in U]
        ksum = [[jnp.sum(kc[u] * w_l[u][h], axis=0, keepdims=True) for h in H] for u in U]
        gate = [[_sigmoid(um_ref[0, pl.ds(r, L), 768 + h * M_HD:768 + (h + 1) * M_HD]) for h in H] for r in r0]
        for u in U:
            ct_bf = ct.astype(BF16)
            m_inter = [m_tile[h:h + 1, 0:1] + bcol[u][h] for h in H]
            m_new = [jnp.maximum(m_inter[h], m_loc[u][h]) for h in H]
            f = [jnp.exp(m_loc[u][h] - m_new[h]) for h in H]
            a_inter = [jnp.exp(m_inter[h] - m_new[h]) for h in H]
            num = [a_inter[h] * _dot(q_bf[u][h], ct_bf) + f[h] * sv[u][h] for h in H]
            qn = [jnp.sum(q_h[u][h] * n_row, axis=1, keepdims=True) for h in H]
            den = [a_inter[h] * qn[h] + f[h] * ssum[u][h] for h in H]
            hh = [gate[u][h] * (num[h] / jnp.maximum(jnp.abs(den[h]), jnp.exp(-m_new[h]))) for h in H]
            ms = [jnp.mean(x * x, axis=1, keepdims=True) for x in hh]
            outs = [hh[h] * lax.rsqrt(ms[h] + EPS) * gh_ref[:, h * M_HD:(h + 1) * M_HD] for h in H]
            hm_ref[0, pl.ds(r0[u], L), :] = jnp.concatenate(outs, axis=1).astype(BF16)
            ct_old, n_old = ct, n_row
            for h in H:
                ct = jnp.where(row_w == h, a_inter[h][lst] * ct_old + f[h][lst] * upd[u][h], ct)
                n_row = jnp.where(hmask[h], a_inter[h][lst] * n_old + f[h][lst] * ksum[u][h], n_row)
                m_tile = jnp.where(row8 == h, m_new[h][lst], m_tile)
        return ct, n_row, m_tile

    ct, n_row, m_tile = lax.fori_loop(0, n_chunks // unroll, group, (ct_s[...], n_s[0:1, :], m_s[...]))
    ct_s[...] = ct
    n_s[...] = jnp.broadcast_to(n_row, n_s.shape)
    m_s[...] = m_tile

    @pl.when(j == pl.num_programs(1) - 1)
    def _():
        ct_out_ref[0] = ct
        n_out_ref[0] = jnp.broadcast_to(n_row, n_s.shape)
        m_out_ref[0] = m_tile


def _mlstm(um, conv0, ct0, n0, m0, wconv, bif, gh, *, L, n_valid):
    b, t, _ = um.shape
    blk = min(t, 512)
    kern = functools.partial(_mlstm_kernel, L=L, n_valid=n_valid, blk=blk)
    per_b = lambda shape: pl.BlockSpec((1,) + shape, lambda i, j: (i,) + (0,) * len(shape))
    const = lambda shape: pl.BlockSpec(shape, lambda i, j: (0,) * len(shape))
    return pl.pallas_call(
        kern,
        grid=(b, t // blk),
        in_specs=[pl.BlockSpec((1, blk, UM_W), lambda i, j: (i, j, 0)),
                  per_b((8, 2 * M_WIDTH)), per_b((M_WIDTH, M_HD)), per_b((8, M_WIDTH)), per_b((8, LANES)),
                  const((8, 2 * M_WIDTH)), const((1, LANES)), const((1, M_WIDTH))],
        out_specs=[pl.BlockSpec((1, blk, M_WIDTH), lambda i, j: (i, j, 0)),
                   per_b((8, 2 * M_WIDTH)), per_b((M_WIDTH, M_HD)), per_b((8, M_WIDTH)), per_b((8, LANES))],
        out_shape=[jax.ShapeDtypeStruct((b, t, M_WIDTH), BF16),
                   jax.ShapeDtypeStruct((b, 8, 2 * M_WIDTH), F32),
                   jax.ShapeDtypeStruct((b, M_WIDTH, M_HD), F32),
                   jax.ShapeDtypeStruct((b, 8, M_WIDTH), F32),
                   jax.ShapeDtypeStruct((b, 8, LANES), F32)],
        scratch_shapes=[pltpu.VMEM((blk + 8, 2 * M_WIDTH), F32),
                        pltpu.VMEM((blk, M_WIDTH), F32), pltpu.VMEM((blk, M_WIDTH), F32),
                        pltpu.VMEM((M_WIDTH, M_HD), F32), pltpu.VMEM((8, M_WIDTH), F32),
                        pltpu.VMEM((8, LANES), F32)],
        compiler_params=_cparams(("parallel", "arbitrary")),
        name="mlstm",
    )(um, conv0, ct0, n0, m0, wconv, bif, gh)


def _gla_kernel(ug_ref, s0_ref, w2_ref, bg_ref, gh_ref, hg_ref, s_out_ref, s_s, *, L, n_valid, blk):
    j = pl.program_id(1)

    @pl.when(j == 0)
    def _():
        s_s[...] = s0_ref[0]

    row = _iota((L, L), 0)
    col = _iota((L, L), 1)
    causal = row >= col
    tril = causal.astype(F32)
    lane_k = _iota((1, LANES), 1) // G_DK
    row_k = _iota((LANES, 1), 0) // G_DK
    valid_col = _iota((L, 1), 0) < n_valid
    mid = max(n_valid // 2, 1)

    n_chunks = blk // L
    unroll = CHUNK_UNROLL if n_chunks % CHUNK_UNROLL == 0 else 1
    H = range(G_HEADS)
    U = range(unroll)
    hmask = [lane_k == h for h in H]

    def group(i, s_all):
        r0 = [pl.multiple_of((i * unroll + u) * L, L) for u in U]
        q = [ug_ref[0, pl.ds(r, L), 0:128] * (G_DK ** -0.5) for r in r0]
        k = [ug_ref[0, pl.ds(r, L), 128:256] for r in r0]
        z = [_dot_hi(ug_ref[0, pl.ds(r, L), 768:896], w2_ref[...]) + bg_ref[...] for r in r0]
        g = [_log_sigmoid(x) * (1.0 / G_TAU) for x in z]
        bc = [_dot_hi(tril, x) for x in g]
        c_ref = [x[mid - 1:mid] for x in bc]
        last = [x[n_valid - 1:n_valid] for x in bc]
        qe = [q[u] * jnp.exp(bc[u] - c_ref[u]) for u in U]
        ke = [(k[u] * jnp.exp(c_ref[u] - bc[u])).astype(BF16) for u in U]
        qin = [q[u] * jnp.exp(bc[u]) for u in U]
        kd_t = [_transpose(jnp.where(valid_col, k[u] * jnp.exp(last[u] - bc[u]), 0.0)).astype(BF16) for u in U]
        decay = [jnp.exp(_transpose(jnp.broadcast_to(x, (8, LANES)))[:, 0:1]) for x in last]
        v_h = [[ug_ref[0, pl.ds(r, L), 256 + h * G_DV:256 + (h + 1) * G_DV].astype(BF16) for h in H] for r in r0]
        a = [[jnp.where(causal, _dot_nt(jnp.where(hmask[h], qe[u], 0.0).astype(BF16), ke[u]), 0.0) for h in H]
             for u in U]
        intra = [[_dot(a[u][h].astype(BF16), v_h[u][h]) for h in H] for u in U]
        q_in = [[jnp.where(hmask[h], qin[u], 0.0).astype(BF16) for h in H] for u in U]
        upd = [[_dot(kd_t[u], v_h[u][h]) for h in H] for u in U]
        gate = [[ug_ref[0, pl.ds(r, L), 512 + h * G_DV:512 + (h + 1) * G_DV] for h in H] for r in r0]
        gate = [[x * _sigmoid(x) * gh_ref[:, h * G_DV:(h + 1) * G_DV] for h, x in enumerate(gs)] for gs in gate]
        for u in U:
            s_bf = s_all.astype(BF16)
            o = [_dot(q_in[u][h], s_bf) + intra[u][h] for h in H]
            ms = [jnp.mean(x * x, axis=1, keepdims=True) for x in o]
            outs = [o[h] * lax.rsqrt(ms[h] + EPS) * gate[u][h] for h in H]
            hg_ref[0, pl.ds(r0[u], L), :] = jnp.concatenate(outs, axis=1).astype(BF16)
            s_all = decay[u] * s_all
            for h in H:
                s_all = s_all + jnp.where(row_k == h, upd[u][h], 0.0)
        return s_all

    s_fin = lax.fori_loop(0, n_chunks // unroll, group, s_s[...])
    s_s[...] = s_fin

    @pl.when(j == pl.num_programs(1) - 1)
    def _():
        s_out_ref[0] = s_fin


def _gla(ug, s0, w2p, bg, gh, *, L, n_valid):
    b, t, _ = ug.shape
    blk = min(t, 512)
    kern = functools.partial(_gla_kernel, L=L, n_valid=n_valid, blk=blk)
    return pl.pallas_call(
        kern,
        grid=(b, t // blk),
        in_specs=[pl.BlockSpec((1, blk, UG_W), lambda i, j: (i, j, 0)),
                  pl.BlockSpec((1, LANES, G_DV), lambda i, j: (i, 0, 0)),
                  pl.BlockSpec((LANES, LANES), lambda i, j: (0, 0)),
                  pl.BlockSpec((1, LANES), lambda i, j: (0, 0)),
                  pl.BlockSpec((1, G_WIDTH), lambda i, j: (0, 0))],
        out_specs=[pl.BlockSpec((1, blk, G_WIDTH), lambda i, j: (i, j, 0)),
                   pl.BlockSpec((1, LANES, G_DV), lambda i, j: (i, 0, 0))],
        out_shape=[jax.ShapeDtypeStruct((b, t, G_WIDTH), BF16),
                   jax.ShapeDtypeStruct((b, LANES, G_DV), F32)],
        scratch_shapes=[pltpu.VMEM((LANES, G_DV), F32)],
        compiler_params=_cparams(("parallel", "arbitrary")),
        name="gla",
    )(ug, s0, w2p, bg, gh)


def _group_mean_sq(x, bm_bf):
    return _dot_split(x * x, bm_bf)


def _rope_slab(y, cos, sin_signed):
    lane = _iota(y.shape, 1)
    rot = jnp.where((lane % N_HD) < (N_HD // 2), pltpu.roll(y, 96, axis=1), pltpu.roll(y, 32, axis=1))
    return y * cos + rot * sin_signed


def _norm_rope_slab(x, g, cos, sin_signed, bm_bf):
    y = x * lax.rsqrt(_group_mean_sq(x, bm_bf) + EPS) * g
    return _rope_slab(y, cos, sin_signed)


def _rows_kernel(nq_ref, nsw_ref, ngate_ref, cos_ref, sin_ref, gq_ref, gks_ref, gkw_ref, bm_ref,
                 q_ref, ks_ref, vs_ref, kw_ref, vw_ref, slc_ref, win_ref, gates_ref):
    cos = cos_ref[...]
    sin = sin_ref[...]
    bm = bm_ref[...]
    low = _iota((1, LANES), 1) < N_HD
    for sl in range(4):
        x = nq_ref[0, :, sl * LANES:(sl + 1) * LANES]
        y = _norm_rope_slab(x, gq_ref[:, sl * LANES:(sl + 1) * LANES], cos, sin, bm) * (N_HD ** -0.5 * LOG2E)
        y_sw = pltpu.roll(y, N_HD, axis=1)
        if sl < 2:
            even, odd = jnp.where(low, y, 0.0), jnp.where(low, y_sw, 0.0)
        else:
            even, odd = jnp.where(low, 0.0, y_sw), jnp.where(low, 0.0, y)
        q_ref[0, 2 * sl] = even.astype(BF16)
        q_ref[0, 2 * sl + 1] = odd.astype(BF16)
    ks = _norm_rope_slab(nsw_ref[0, :, 0:128], gks_ref[...], cos, sin, bm)
    vs = nsw_ref[0, :, 128:256]
    kw = _norm_rope_slab(nsw_ref[0, :, 256:384], gkw_ref[...], cos, sin, bm)
    vw = nsw_ref[0, :, 384:512]
    slc_ref[0, :, 0:128] = ks
    slc_ref[0, :, 128:256] = vs
    win_ref[0, :, 0:128] = kw
    win_ref[0, :, 128:256] = vw
    ks_ref[0] = ks.astype(BF16)
    kw_ref[0] = kw.astype(BF16)
    for ref, val in ((vs_ref, vs), (vw_ref, vw)):
        ref[0, 0] = jnp.where(low, val, 1.0).astype(BF16)
        ref[0, 1] = jnp.where(low, 1.0, val).astype(BF16)
    gt = _sigmoid(ngate_ref[0])
    gates_ref[0, 0] = gt
    gates_ref[0, 1] = pltpu.roll(gt, LANES - 3 * N_REP, axis=1)


def _rows(nq, nsw, ngate, cos, sin, gq, gks, gkw, bm):
    b, t, _ = nq.shape
    tm = min(t, 512)
    tok = lambda w: pl.BlockSpec((1, tm, w), lambda i, j: (i, j, 0))
    const = lambda shape: pl.BlockSpec(shape, lambda i, j: (0,) * len(shape))
    k_shape = jax.ShapeDtypeStruct((b, t, LANES), BF16)
    v_shape = jax.ShapeDtypeStruct((b, N_KV, t, LANES), BF16)
    per_group = pl.BlockSpec((1, N_KV, tm, LANES), lambda i, j: (i, 0, j, 0))
    return pl.pallas_call(
        _rows_kernel,
        grid=(b, t // tm),
        in_specs=[tok(NQ_W), tok(NSW_W), tok(NGATE_W),
                  pl.BlockSpec((tm, LANES), lambda i, j: (j, 0)),
                  pl.BlockSpec((tm, LANES), lambda i, j: (j, 0)),
                  const((1, NQ_W)), const((1, LANES)), const((1, LANES)), const((LANES, LANES))],
        out_specs=[pl.BlockSpec((1, N_HEADS, tm, LANES), lambda i, j: (i, 0, j, 0)),
                   tok(LANES), per_group, tok(LANES), per_group, tok(256), tok(256), per_group],
        out_shape=[jax.ShapeDtypeStruct((b, N_HEADS, t, LANES), BF16), k_shape, v_shape, k_shape, v_shape,
                   jax.ShapeDtypeStruct((b, t, 256), F32), jax.ShapeDtypeStruct((b, t, 256), F32),
                   jax.ShapeDtypeStruct((b, N_KV, t, LANES), F32)],
        compiler_params=_cparams(("parallel", "parallel")),
        name="nsa_rows",
    )(nq, nsw, ngate, cos, sin, gq, gks, gkw, bm)


def _block_mlp(x_bf, w1, pe, w2, prev_first):
    p = _dot(x_bf, w1)
    bias = _dot(pe.astype(BF16), w1)
    pa = p[:, :256] + bias[0:1, :256]
    pb = p[:, 256:] + bias[1:2, 256:]
    pa_prev = jnp.where(_iota(pa.shape, 0) == 0, prev_first, pltpu.roll(pa, 1, axis=0))
    pre = pa_prev + pb
    hid = (pre * _sigmoid(pre)).astype(BF16)
    return _dot(hid, w2), pa[pa.shape[0] - 1:, :]


def _compress_kernel(u_ref, w1_ref, pe_ref, w2_ref, gkc_ref, cos_ref, sin_ref, bm_ref, kc_ref, vc_ref):
    outs = []
    for kv in range(2):
        x = jnp.concatenate(
            [u_ref[0, :, j * 256 + kv * LANES:j * 256 + (kv + 1) * LANES] for j in range(CMP_STRIDE)],
            axis=1).astype(BF16)
        outs.append(_block_mlp(x, w1_ref[kv], pe_ref[kv], w2_ref[kv], jnp.zeros((1, 256), F32))[0])
    kc_ref[0] = _norm_rope_slab(outs[0], gkc_ref[...], cos_ref[...], sin_ref[...], bm_ref[...]).astype(BF16)
    vc_ref[0] = outs[1].astype(BF16)


CMP_PAGES = 32
PAGE_UNITS = PAGE_SIZE // CMP_STRIDE


def _compress_paged_kernel(pt_ref, *refs):
    pages = refs[:CMP_PAGES]
    (w1_ref, pe_ref, w2_ref, gkc_ref, cos_ref, sin_ref, bm_ref, kc_ref, vc_ref, t_scr, x_scr, carry_scr) = refs[CMP_PAGES:]

    @pl.when(pl.program_id(1) == 0)
    def _():
        carry_scr[...] = jnp.zeros(carry_scr.shape, F32)

    outs = []
    for kv in range(2):
        for i in range(CMP_PAGES):
            t_scr[i] = pages[i][0, 0, kv].reshape(LANES, PAGE_SIZE).T
            for j in range(CMP_STRIDE):
                x_scr[i * PAGE_UNITS:(i + 1) * PAGE_UNITS, j * LANES:(j + 1) * LANES] = \
                    t_scr[i, pl.ds(j, PAGE_UNITS, stride=CMP_STRIDE), :]
        out, last = _block_mlp(x_scr[...].astype(BF16), w1_ref[kv], pe_ref[kv], w2_ref[kv], carry_scr[kv, 0:1, :])
        carry_scr[kv, 0:1, :] = last
        outs.append(out)
    kc_ref[0] = _norm_rope_slab(outs[0], gkc_ref[...], cos_ref[...], sin_ref[...], bm_ref[...]).astype(BF16)
    vc_ref[0] = outs[1].astype(BF16)


def _compress_paged(page_table, cache_t, layer, w1r, pe_r, w2r, gkc, cos_c, sin_c, bm):
    db, n_pages = page_table.shape
    step_units = CMP_PAGES * PAGE_UNITS
    nu = n_pages * PAGE_UNITS

    def page_spec(i):
        return pl.BlockSpec((1, 1, 2, N_KV, N_HD, PAGE_SIZE),
                            lambda b, c, pt: (layer, pt[b, c * CMP_PAGES + i], 0, 0, 0, 0))

    const = lambda shape: pl.BlockSpec(shape, lambda b, c, pt: (0,) * len(shape))
    tab = pl.BlockSpec((step_units, LANES), lambda b, c, pt: (c, 0))
    out = jax.ShapeDtypeStruct((db, nu, LANES), BF16)
    ospec = pl.BlockSpec((1, step_units, LANES), lambda b, c, pt: (b, c, 0))
    return pl.pallas_call(
        _compress_paged_kernel,
        grid_spec=pltpu.PrefetchScalarGridSpec(
            num_scalar_prefetch=1,
            grid=(db, n_pages // CMP_PAGES),
            in_specs=[page_spec(i) for i in range(CMP_PAGES)] + [
                const((2, 2048, 512)), const((2, 8, 2048)), const((2, 256, LANES)),
                const((1, LANES)), tab, tab, const((LANES, LANES))],
            out_specs=[ospec, ospec],
            scratch_shapes=[pltpu.VMEM((CMP_PAGES, PAGE_SIZE, LANES), F32),
                            pltpu.VMEM((step_units, CMP_STRIDE * LANES), F32),
                            pltpu.VMEM((2, 8, 256), F32)]),
        out_shape=[out, out],
        compiler_params=_cparams(("parallel", "arbitrary")),
        name="nsa_compress_paged",
    )(page_table, *([cache_t] * CMP_PAGES), w1r, pe_r, w2r, gkc, cos_c, sin_c, bm)


def _compress(units, w1r, pe_r, w2r, gkc, cos_c, sin_c, bm):
    b, nu, _ = units.shape
    const = lambda shape: pl.BlockSpec(shape, lambda i: (0,) * len(shape))
    out = jax.ShapeDtypeStruct((b, nu, LANES), BF16)
    return pl.pallas_call(
        _compress_kernel,
        grid=(b,),
        in_specs=[pl.BlockSpec((1, nu, 4096), lambda i: (i, 0, 0)),
                  const((2, 2048, 512)), const((2, 8, 2048)), const((2, 256, LANES)),
                  const((1, LANES)), const((nu, LANES)), const((nu, LANES)), const((LANES, LANES))],
        out_specs=[pl.BlockSpec((1, nu, LANES), lambda i: (i, 0, 0))] * 2,
        out_shape=[out, out],
        compiler_params=_cparams(("parallel",)),
        name="nsa_compress",
    )(units, w1r, pe_r, w2r, gkc, cos_c, sin_c, bm)


def _cmp_kernel(q_ref, kc_ref, vc_ref, c2s_ref, oc_ref, sel_ref, *, tq, pos0, n_cmp, nsel):
    qi = pl.program_id(1)
    nc_pad = kc_ref.shape[1]
    q = q_ref[0].reshape(N_HEADS * tq, LANES)
    s = _dot_nt(q, kc_ref[0]).reshape(N_HEADS, tq, nc_pad)
    qpos = pos0 + qi * tq + _iota((tq, 1), 0)
    c_idx = _iota((tq, nc_pad), 1)
    c_real = jnp.where(c_idx >= 1, c_idx, nc_pad + n_cmp) <= n_cmp
    c_ok = jnp.where(c_real, c_idx * CMP_STRIDE + (CMP_LEN - CMP_STRIDE - 1), jnp.int32(2 ** 30)) <= qpos
    s = s + jnp.where(c_ok, 0.0, NEG)[None]
    p = jnp.exp2(s - jnp.max(s, axis=-1, keepdims=True))
    p = p * ((qpos >= CMP_LEN - 1).astype(F32)[None] / jnp.sum(p, axis=-1, keepdims=True))
    oc_ref[0] = _dot(p.reshape(N_HEADS * tq, nc_pad).astype(BF16), vc_ref[0]).reshape(N_HEADS, tq, LANES)
    psum = jnp.sum(p.reshape(N_KV, N_REP, tq, nc_pad), axis=1).reshape(N_KV * tq, nc_pad)
    imp = _dot_split(psum, c2s_ref[...])
    work = _transpose2d(imp)
    ncol = N_KV * tq
    qpos_row = pos0 + qi * tq + _iota((1, ncol), 1) % tq
    blk = _iota((nsel, ncol), 0)
    cur = qpos_row // SLC_BLOCK
    forced = jnp.where(blk == 0, 1.0, jnp.where(blk == cur, 1.0, jnp.where(blk == cur - 1, 1.0, 0.0)))
    work = jnp.where(blk * SLC_BLOCK <= qpos_row, work + FORCE_BONUS * forced, NEG)
    blk_f = blk.astype(F32)
    for _ in range(SLC_TOPK):
        m = jnp.max(work, axis=0, keepdims=True)
        idx = jnp.min(jnp.where(work == m, blk_f, float(nsel)), axis=0, keepdims=True)
        work = jnp.where(blk_f == idx, -jnp.inf, work)
    sel = jnp.where(work == -jnp.inf, 1.0, 0.0)
    sel_ref[0] = _transpose2d(sel).reshape(N_KV, tq, nsel).astype(BF16)


def _cmp_attn(q_pad, kc, vc, c2s, *, tq, pos0, n_cmp):
    b, _, t, _ = q_pad.shape
    nc_pad = kc.shape[1]
    nsel = c2s.shape[1]
    kern = functools.partial(_cmp_kernel, tq=tq, pos0=pos0, n_cmp=n_cmp, nsel=nsel)
    return pl.pallas_call(
        kern,
        grid=(b, t // tq),
        in_specs=[pl.BlockSpec((1, N_HEADS, tq, LANES), lambda i, j: (i, 0, j, 0)),
                  pl.BlockSpec((1, nc_pad, LANES), lambda i, j: (i, 0, 0)),
                  pl.BlockSpec((1, nc_pad, LANES), lambda i, j: (i, 0, 0)),
                  pl.BlockSpec((nc_pad, nsel), lambda i, j: (0, 0))],
        out_specs=[pl.BlockSpec((1, N_HEADS, tq, LANES), lambda i, j: (i, 0, j, 0)),
                   pl.BlockSpec((1, N_KV, tq, nsel), lambda i, j: (i, 0, j, 0))],
        out_shape=[jax.ShapeDtypeStruct((b, N_HEADS, t, LANES), F32),
                   jax.ShapeDtypeStruct((b, N_KV, t, nsel), BF16)],
        compiler_params=_cparams(("parallel", "parallel")),
        name="nsa_cmp_topk",
    )(q_pad, kc, vc, c2s)


SCORE_DTYPE = F32
SLAB = 64


def _tile_scores(q_parts, k, s_scr, *, tq, width):
    for r in range(N_REP):
        s_scr[r * tq:(r + 1) * tq, 0:width] = _dot_nt(q_parts[r], k).astype(s_scr.dtype)


def _tile_update(v, bias_scr, s_scr, p_scr, m_scr, alpha_scr, acc_scr, *, tq, width):
    nch = width // LANES
    sd = s_scr.dtype

    def chunk(rows, i, c):
        x = s_scr[rows, c * LANES:(c + 1) * LANES]
        if bias_scr is not None:
            x = x + bias_scr[i * SLAB:(i + 1) * SLAB, c * LANES:(c + 1) * LANES]
        return x

    slabs = [(slice(r * tq + i * SLAB, r * tq + (i + 1) * SLAB), i) for r in range(N_REP) for i in range(tq // SLAB)]
    for rows, i in slabs:
        mx = chunk(rows, i, 0)
        for c in range(1, nch):
            mx = jnp.maximum(mx, chunk(rows, i, c))
        m_old = m_scr[rows, :]
        m_new = jnp.maximum(m_old, jnp.max(mx.astype(F32), axis=1, keepdims=True)).astype(sd).astype(F32)
        alpha_scr[rows, :] = jnp.exp2(m_old - m_new)
        m_scr[rows, :] = m_new
    for rows, i in slabs:
        m_new = m_scr[rows, :].astype(sd)
        for c in range(nch):
            p_scr[rows, c * LANES:(c + 1) * LANES] = jnp.exp2(chunk(rows, i, c) - m_new).astype(BF16)
    for r in range(N_REP):
        rs = slice(r * tq, (r + 1) * tq)
        acc_scr[rs, :] = alpha_scr[rs, :] * acc_scr[rs, :] + _dot(p_scr[rs, 0:width], v)


def _slcwin_kernel(q_ref, ks_ref, vs_ref, kw_ref, vw_ref, sel_ref, et_ref, oc_ref, gates_ref, hn_ref,
                   s_scr, bias_scr, p_scr, m_scr, alpha_scr, acc_scr, ow_scr, *, tq):
    g = pl.program_id(1)
    qi = pl.program_id(2)
    rows = N_REP * tq
    q0 = qi * tq
    qpos = q0 + _iota((tq, 1), 0)
    scr = dict(p_scr=p_scr, m_scr=m_scr, alpha_scr=alpha_scr, acc_scr=acc_scr, tq=tq)

    def reset():
        m_scr[...] = jnp.full((rows, LANES), NEG, F32)
        acc_scr[...] = jnp.zeros((rows, LANES), F32)

    def result():
        acc = acc_scr[...]
        return acc / pltpu.roll(acc, N_HD, axis=1)

    wlen = WINDOW + tq
    start = pl.multiple_of(jnp.maximum(q0 - WINDOW, 0), tq)
    reset()
    dist = qpos - (start + _iota((tq, wlen), 1))
    bias_scr[:, 0:wlen] = jnp.where(jnp.where(dist >= 0, dist, WINDOW) < WINDOW, 0.0, NEG).astype(bias_scr.dtype)
    _tile_scores([q_ref[0, r] for r in range(N_REP)], kw_ref[0, pl.ds(start, wlen), :], s_scr, tq=tq, width=wlen)
    _tile_update(vw_ref[0, 0, pl.ds(start, wlen), :], bias_scr, s_scr, width=wlen, **scr)
    ow_scr[...] = result()

    reset()
    sel_m = ((sel_ref[0, 0].astype(F32) - 1.0) * (-NEG)).astype(BF16)
    q_aug = [jnp.concatenate([q_ref[0, r], sel_m], axis=1) for r in range(N_REP)]
    n_kv = (q0 + tq + KV_TILE - 1) // KV_TILE

    def scores(j, dst):
        k0 = pl.multiple_of(j * KV_TILE, KV_TILE)
        k_aug = jnp.concatenate([ks_ref[0, pl.ds(k0, KV_TILE), :], et_ref[pl.ds(k0, KV_TILE), :]], axis=1)
        _tile_scores(q_aug, k_aug, dst, tq=tq, width=KV_TILE)

    def values(j):
        return vs_ref[0, 0, pl.ds(pl.multiple_of(j * KV_TILE, KV_TILE), KV_TILE), :]

    n_plain = n_kv - 1

    def body(j, carry):
        scores(j, s_scr)
        _tile_update(values(j), None, s_scr, width=KV_TILE, **scr)
        return carry

    lax.fori_loop(0, n_plain, body, 0)
    bias_scr[:, 0:KV_TILE] = jnp.where(n_plain * KV_TILE + _iota((tq, KV_TILE), 1) <= qpos, 0.0, NEG).astype(
        bias_scr.dtype)
    scores(n_plain, s_scr)
    _tile_update(values(n_plain), bias_scr, s_scr, width=KV_TILE, **scr)
    o_s_all = result()

    gt = gates_ref[0, 0]
    low = _iota((1, LANES), 1) < N_HD
    vals = []
    for r in range(N_REP):
        rs = slice(r * tq, (r + 1) * tq)
        vals.append(gt[:, 3 * r:3 * r + 1] * oc_ref[0, r] + gt[:, 3 * r + 1:3 * r + 2] * o_s_all[rs]
                    + gt[:, 3 * r + 2:3 * r + 3] * ow_scr[rs, :])
    for pair in range(N_REP // 2):
        a, b = vals[2 * pair], vals[2 * pair + 1]
        a_sw, b_sw = pltpu.roll(a, N_HD, axis=1), pltpu.roll(b, N_HD, axis=1)
        lo = jnp.where(g == 0, a, a_sw)
        hi = jnp.where(g == 0, b_sw, b)
        hn_ref[0, :, pair * LANES:(pair + 1) * LANES] = jnp.where(low, lo, hi).astype(BF16)


def _slcwin(q_pad, ks, vs, kw, vw, sel, e_t, oc, gates, *, tq):
    b, _, t, _ = q_pad.shape
    nsel = sel.shape[3]
    assert nsel == LANES and e_t.shape == (t, LANES)
    rows = N_REP * tq
    wlen = WINDOW + tq
    kern = functools.partial(_slcwin_kernel, tq=tq)
    qspec = pl.BlockSpec((1, N_REP, tq, LANES), lambda i, g, j: (i, g, j, 0))
    keys = pl.BlockSpec((1, t, LANES), lambda i, g, j: (i, 0, 0))
    vals = pl.BlockSpec((1, 1, t, LANES), lambda i, g, j: (i, g, 0, 0))
    return pl.pallas_call(
        kern,
        grid=(b, N_KV, t // tq),
        in_specs=[qspec, keys, vals, keys, vals,
                  pl.BlockSpec((1, 1, tq, nsel), lambda i, g, j: (i, g, j, 0)),
                  pl.BlockSpec((t, LANES), lambda i, g, j: (0, 0)),
                  qspec,
                  pl.BlockSpec((1, 1, tq, LANES), lambda i, g, j: (i, g, j, 0))],
        out_specs=pl.BlockSpec((1, tq, N_REP * N_HD), lambda i, g, j: (i, j, g)),
        out_shape=jax.ShapeDtypeStruct((b, t, N_WIDTH), BF16),
        scratch_shapes=[pltpu.VMEM((rows, wlen), SCORE_DTYPE), pltpu.VMEM((tq, wlen), SCORE_DTYPE),
                        pltpu.VMEM((rows, wlen), BF16),
                        pltpu.VMEM((rows, LANES), F32), pltpu.VMEM((rows, LANES), F32),
                        pltpu.VMEM((rows, LANES), F32), pltpu.VMEM((rows, LANES), F32)],
        compiler_params=_cparams(("parallel", "parallel", "parallel")),
        name="nsa_slc_win",
    )(q_pad, ks, vs, kw, vw, sel, e_t, oc, gates)


SLC_PAGES = 32


def _heads_bias(b2, ts):
    w = b2.shape[1]
    return jnp.broadcast_to(b2.reshape(N_KV, 1, ts, w), (N_KV, N_REP, ts, w)).reshape(N_HEADS * ts, w)


def _slc_sample_kernel(pt_ref, *refs, ts):
    pages = refs[:SLC_PAGES]
    (q_ref, sel_ref, e_ref, enew_ref, newslc_ref, win_ref, newwin_ref, oc_ref, gates_ref,
     hn_ref, m_scr, l_scr, acc_scr) = refs[SLC_PAGES:]
    c = pl.program_id(1)
    rows = N_HEADS * ts
    q = q_ref[0].astype(F32)[:, 0:ts, :].reshape(rows, LANES).astype(BF16)
    sel2 = sel_ref[0].astype(F32)[:, 0:ts, :].reshape(N_KV * ts, sel_ref.shape[3]).astype(BF16)

    @pl.when(c == 0)
    def _():
        m_scr[...] = jnp.full((rows, 1), NEG, F32)
        l_scr[...] = jnp.zeros((rows, 1), F32)
        acc_scr[...] = jnp.zeros((rows, LANES), F32)

    def online(s, pv_fn):
        m_old = m_scr[...]
        m_new = jnp.maximum(m_old, jnp.max(s, axis=1, keepdims=True))
        p = jnp.exp2(s - m_new)
        alpha = jnp.exp2(m_old - m_new)
        l_scr[...] = alpha * l_scr[...] + jnp.sum(p, axis=1, keepdims=True)
        acc_scr[...] = alpha * acc_scr[...] + pv_fn(p.astype(BF16))
        m_scr[...] = m_new

    picked = _dot(sel2, e_ref[...])
    bias = _heads_bias(picked * (-NEG) + NEG, ts)
    k_t = jnp.concatenate([pg[0, 0, 0].reshape(LANES, PAGE_SIZE).astype(BF16) for pg in pages], axis=1)
    v_t = jnp.concatenate([pg[0, 0, 1].reshape(LANES, PAGE_SIZE).astype(BF16) for pg in pages], axis=1)
    online(_dot(q, k_t) + bias, lambda p: _dot_nt(p, v_t))

    @pl.when(c == pl.num_programs(1) - 1)
    def _():
        tok = _iota((N_KV * ts, 1), 0) % ts
        key = _iota((N_KV * ts, PAGE_SIZE), 1)
        new = newslc_ref[0].astype(BF16)
        picked_n = _dot(sel2, enew_ref[...])
        bias_n = _heads_bias(jnp.where(key <= tok, picked_n, 0.0) * (-NEG) + NEG, ts)
        online(_dot_nt(q, new[:, :LANES]) + bias_n, lambda p: _dot(p, new[:, LANES:]))
        kw_t = win_ref[0, 0, 0].reshape(LANES, WINDOW).astype(BF16)
        vw_t = win_ref[0, 0, 1].reshape(LANES, WINDOW).astype(BF16)
        nwin = newwin_ref[0].astype(BF16)
        wkey = _iota((N_KV * ts, WINDOW), 1)
        bias_w = jnp.concatenate([jnp.where(wkey > tok, 0.0, NEG), jnp.where(key <= tok, 0.0, NEG)], axis=1)
        s_w = jnp.concatenate([_dot(q, kw_t), _dot_nt(q, nwin[:, :LANES])], axis=1) + _heads_bias(bias_w, ts)
        p_w = jnp.exp2(s_w - jnp.max(s_w, axis=1, keepdims=True))
        l_w = jnp.sum(p_w, axis=1, keepdims=True)
        p_w = p_w.astype(BF16)
        o_w = (_dot_nt(p_w[:, :WINDOW], vw_t) + _dot(p_w[:, WINDOW:], nwin[:, LANES:])) / l_w
        o_s = acc_scr[...] / l_scr[...]
        low = _iota((1, LANES), 1) < N_HD
        vals = []
        for h in range(N_HEADS):
            g, r = divmod(h, N_REP)
            gt = gates_ref[0, g, 0:ts, :]
            rs = slice(h * ts, (h + 1) * ts)
            vals.append(gt[:, 3 * r:3 * r + 1] * oc_ref[0, h, 0:ts, :] + gt[:, 3 * r + 1:3 * r + 2] * o_s[rs]
                        + gt[:, 3 * r + 2:3 * r + 3] * o_w[rs])
        slabs = []
        for pair in range(N_HEADS // 2):
            a, b = vals[2 * pair], vals[2 * pair + 1]
            if pair < N_REP // 2:
                lo, hi = a, pltpu.roll(b, N_HD, axis=1)
            else:
                lo, hi = pltpu.roll(a, N_HD, axis=1), b
            slabs.append(jnp.where(low, lo, hi))
        out = jnp.concatenate(slabs, axis=1)
        pad = jnp.zeros((hn_ref.shape[1] - ts, N_WIDTH), F32)
        hn_ref[0] = jnp.concatenate([out, pad], axis=0).astype(BF16)


def _slc_sample(page_table, cache_t, layer, q_pad, sel, e_main, e_new, new_slc, win_t, new_win, oc, gates, *, ts):
    db, n_pages = page_table.shape
    tpad = q_pad.shape[2]
    nsel = sel.shape[3]
    n_steps = n_pages // SLC_PAGES
    step_keys = SLC_PAGES * PAGE_SIZE
    rows = N_HEADS * ts

    def page_spec(i):
        return pl.BlockSpec((1, 1, 2, N_KV, N_HD, PAGE_SIZE),
                            lambda b, c, pt: (layer, pt[b, c * SLC_PAGES + i], 0, 0, 0, 0))

    per_b = lambda shape: pl.BlockSpec((1,) + shape, lambda b, c, pt: (b,) + (0,) * len(shape))
    return pl.pallas_call(
        functools.partial(_slc_sample_kernel, ts=ts),
        grid_spec=pltpu.PrefetchScalarGridSpec(
            num_scalar_prefetch=1,
            grid=(db, n_steps),
            in_specs=[page_spec(i) for i in range(SLC_PAGES)] + [
                per_b((N_HEADS, tpad, LANES)), per_b((N_KV, tpad, nsel)),
                pl.BlockSpec((nsel, step_keys), lambda b, c, pt: (0, c)),
                pl.BlockSpec((nsel, PAGE_SIZE), lambda b, c, pt: (0, 0)),
                per_b((PAGE_SIZE, 256)),
                pl.BlockSpec((1, 1, 2, N_KV, N_HD, WINDOW), lambda b, c, pt: (layer, b, 0, 0, 0, 0)),
                per_b((PAGE_SIZE, 256)), per_b((N_HEADS, tpad, LANES)), per_b((N_KV, tpad, LANES))],
            out_specs=per_b((tpad, N_WIDTH)),
            scratch_shapes=[pltpu.VMEM((rows, 1), F32), pltpu.VMEM((rows, 1), F32), pltpu.VMEM((rows, LANES), F32)]),
        out_shape=jax.ShapeDtypeStruct((db, tpad, N_WIDTH), BF16),
        compiler_params=_cparams(("parallel", "arbitrary")),
        name="nsa_slc_sample",
    )(page_table, *([cache_t] * SLC_PAGES), q_pad, sel, e_main, e_new, new_slc, win_t, new_win, oc, gates)


def _rope_tables(pos):
    half = N_HD // 2
    inv = ROPE_THETA ** (-jnp.arange(half, dtype=F32) / half)
    ang = pos.astype(F32)[:, None] * inv[None, :]
    cos, sin = jnp.cos(ang), jnp.sin(ang)
    return jnp.tile(jnp.concatenate([cos, cos], axis=1), (1, 2)), jnp.tile(jnp.concatenate([-sin, sin], axis=1), (1, 2))


def _pad_cols(a, w):
    return jnp.pad(a, ((0, 0), (0, w - a.shape[1])))


def _prep_layer(l, w_norm_mix, w_in, b_mlstm_if, w_mlstm_conv, w_mlstm_hnorm, w_gla_gate2, b_gla_gate,
                w_gla_hnorm, w_qk_norm, w_cmp_pe, w_cmp_1, w_cmp_2, w_out, w_norm_ffn, w_ffn_up, w_ffn_down):
    wi = w_in[l]
    col = lambda n: wi[:, _OFF[n][0]:_OFF[n][0] + _OFF[n][1]]
    um = _pad_cols(jnp.concatenate([col('m_qk'), col('m_v'), col('m_o'), col('m_i'), col('m_f')], axis=1), UM_W)
    ug = _pad_cols(jnp.concatenate([col('g_q'), col('g_k'), col('g_v'), col('g_o'), col('g_lr')], axis=1), UG_W)
    ncmp = jnp.concatenate([col('n_kc'), col('n_vc')], axis=1)
    nsw = jnp.concatenate([col('n_ks'), col('n_vs'), col('n_kw'), col('n_vw')], axis=1)
    ngate = _pad_cols(col('n_gate'), NGATE_W)
    p = {}
    p['w_cat'] = jnp.concatenate([um, ug, col('n_q'), ncmp, nsw, ngate], axis=1).astype(BF16)
    p['g_mix'] = w_norm_mix[l][None, :]
    p['wconv'] = jnp.pad(w_mlstm_conv[l], ((0, 8 - M_CONV), (0, 0)))
    p['bif'] = _pad_cols(b_mlstm_if[l][None, :], LANES)
    p['gh_m'] = w_mlstm_hnorm[l][None, :]
    p['w2p'] = jnp.pad(w_gla_gate2[l], ((0, LANES - G_RANK), (0, 0)))
    p['bg'] = b_gla_gate[l][None, :]
    p['gh_g'] = w_gla_hnorm[l][None, :]
    gqk = w_qk_norm[l]
    p['gq'] = jnp.tile(gqk[0], N_HEADS)[None, :]
    p['gkc'] = jnp.tile(gqk[1], N_KV)[None, :]
    p['gks'] = jnp.tile(gqk[2], N_KV)[None, :]
    p['gkw'] = jnp.tile(gqk[3], N_KV)[None, :]
    w1 = w_cmp_1[l].reshape(2, 2, CMP_STRIDE, N_HD, CMP_HIDDEN)
    eye = jnp.eye(N_KV, dtype=F32)
    w1r = jnp.einsum('khjdc,gf->kjgdhfc', w1, eye).reshape(2, CMP_STRIDE * N_KV * N_HD, 2 * N_KV * CMP_HIDDEN)
    p['w1r'] = w1r.astype(BF16)
    pe = w_cmp_pe[l].reshape(2, 2, CMP_STRIDE, 1, N_HD)
    pe = jnp.broadcast_to(pe, (2, 2, CMP_STRIDE, N_KV, N_HD)).reshape(2, 2, 2048)
    p['pe_r'] = jnp.pad(pe, ((0, 0), (0, 6), (0, 0)))
    p['w2r'] = jnp.einsum('kcd,gf->kgcfd', w_cmp_2[l], eye).reshape(2, N_KV * CMP_HIDDEN, N_KV * N_HD).astype(BF16)
    p['w_out'] = w_out[l].astype(BF16)
    p['g_ffn'] = w_norm_ffn[l][None, :]
    nf = 2
    tf = D_FF // nf
    wu = w_ffn_up[l]
    p['w_up_r'] = jnp.stack([jnp.concatenate([wu[:, f * tf:(f + 1) * tf], wu[:, D_FF + f * tf:D_FF + (f + 1) * tf]],
                                             axis=1) for f in range(nf)]).astype(BF16)
    p['w_dn_r'] = w_ffn_down[l].reshape(nf, tf, D_MODEL).astype(BF16)
    return p


def _cmp2slc(n_cmp, nc_pad, nsel):
    m = np.zeros((nc_pad, nsel), np.float32)
    per = SLC_BLOCK // CMP_STRIDE
    for n in range(n_cmp):
        for u in range(CMP_LEN // CMP_STRIDE):
            m[n + 1, (n + u) // per] += 1.0
    return jnp.asarray(m, BF16)


def _expand_mat(nsel, kvlen):
    return jnp.asarray((np.arange(kvlen)[None, :] // SLC_BLOCK) == np.arange(nsel)[:, None], BF16)


def _group_mean_mat():
    idx = np.arange(LANES) // N_HD
    return jnp.asarray((idx[:, None] == idx[None, :]) / float(N_HD), BF16)


def _mixers(p, um, ug, mstate, gstate, *, L, n_valid):
    conv0, ct0, n0, m0 = mstate
    hm, conv_o, ct_o, n_o, m_o = _mlstm(um, conv0, ct0, n0, m0, p['wconv'], p['bif'], p['gh_m'], L=L, n_valid=n_valid)
    hg, s_o = _gla(ug, gstate, p['w2p'], p['bg'], p['gh_g'], L=L, n_valid=n_valid)
    return hm, hg, (conv_o, ct_o, n_o, m_o), s_o


def _mlstm_state_in(conv, c, n, m):
    b = conv.shape[0]
    conv0 = jnp.pad(conv.astype(F32), ((0, 0), (8 - (M_CONV - 1), 0), (0, 0)))
    ct0 = jnp.swapaxes(c.astype(F32), -1, -2).reshape(b, M_WIDTH, M_HD)
    n0 = jnp.broadcast_to(n.astype(F32).reshape(b, 1, M_WIDTH), (b, 8, M_WIDTH))
    m0 = jnp.broadcast_to(jnp.pad(m.astype(F32), ((0, 0), (0, 8 - M_HEADS)))[:, :, None], (b, 8, LANES))
    return conv0, ct0, n0, m0


def _mlstm_state_out(conv_o, ct_o, n_o, m_o):
    b = conv_o.shape[0]
    c = jnp.swapaxes(ct_o.reshape(b, M_HEADS, M_HD, M_HD), -1, -2)
    return c, n_o[:, 0].reshape(b, M_HEADS, M_HD), m_o[:, :M_HEADS, 0], conv_o[:, 8 - (M_CONV - 1):]


def kernel(x_prompt, x_sample, cache_cmp_kv, cache_slc_kv, state_win_kv, state_mlstm_C, state_mlstm_n,
           state_mlstm_m, state_mlstm_conv, state_gla_S, page_table, w_norm_mix, w_in, b_mlstm_if,
           w_mlstm_conv, w_mlstm_hnorm, w_gla_gate2, b_gla_gate, w_gla_hnorm, w_qk_norm, w_cmp_pe,
           w_cmp_1, w_cmp_2, w_out, w_norm_ffn, w_ffn_up, w_ffn_down):
    b, t, _ = x_prompt.shape
    db, td, _ = x_sample.shape
    depth = w_in.shape[0]
    n_pages = page_table.shape[1]
    past = n_pages * PAGE_SIZE
    win_buf = state_win_kv.shape[2]
    assert t % 512 == 0 and td < CMP_STRIDE and td <= SAMPLE_T and win_buf == WINDOW
    assert n_pages % SLC_PAGES == 0 and n_pages % CMP_PAGES == 0

    bm = _group_mean_mat()
    cos_p, sin_p = _rope_tables(jnp.arange(t))
    nu_p = t // CMP_STRIDE
    ncmp_p = (t - CMP_LEN) // CMP_STRIDE + 1
    cos_cp, sin_cp = _rope_tables(jnp.arange(nu_p) * CMP_STRIDE + CMP_LEN - CMP_STRIDE - 1)
    nsel_p = -(-t // SLC_BLOCK)
    nsel_p = -(-nsel_p // LANES) * LANES
    c2s_p = _cmp2slc(ncmp_p, nu_p, nsel_p)
    e_p = _expand_mat(nsel_p, t).T
    tq_p = 256
    cos_s, sin_s = _rope_tables(past + jnp.arange(SAMPLE_T))
    nu_s = past // CMP_STRIDE
    ncmp_s = (past + td - CMP_LEN) // CMP_STRIDE + 1
    cos_cs, sin_cs = _rope_tables(jnp.arange(nu_s) * CMP_STRIDE + CMP_LEN - CMP_STRIDE - 1)
    nsel_s = -(-(-(-(past + td) // SLC_BLOCK)) // LANES) * LANES
    c2s_s = _cmp2slc(ncmp_s, nu_s, nsel_s)
    e_s = _expand_mat(nsel_s, past + PAGE_SIZE)
    e_s_main, e_s_new = e_s[:, :past], e_s[:, past:]
    row_minor = lambda a: jnp.transpose(a, (0, 1, 3, 4, 5, 2))
    cmp_t, slc_t, win_t = row_minor(cache_cmp_kv), row_minor(cache_slc_kv), row_minor(state_win_kv)

    xp = x_prompt.reshape(b * t, D_MODEL)
    xs = jnp.pad(x_sample, ((0, 0), (0, SAMPLE_T - td), (0, 0))).reshape(db * SAMPLE_T, D_MODEL)

    zero_m = _mlstm_state_in(jnp.zeros((b, M_CONV - 1, 2 * M_WIDTH), F32), jnp.zeros((b, M_HEADS, M_HD, M_HD), F32),
                             jnp.zeros((b, M_HEADS, M_HD), F32), jnp.zeros((b, M_HEADS), F32))
    zero_g = jnp.zeros((b, LANES, G_DV), F32)

    pl_out = [[] for _ in range(8)]
    sl_out = [[] for _ in range(8)]
    for l in range(depth):
        p = _prep_layer(l, w_norm_mix, w_in, b_mlstm_if, w_mlstm_conv, w_mlstm_hnorm, w_gla_gate2, b_gla_gate,
                        w_gla_hnorm, w_qk_norm, w_cmp_pe, w_cmp_1, w_cmp_2, w_out, w_norm_ffn, w_ffn_up, w_ffn_down)
        um, ug, nq, ncmp, nsw, ngate = _pre(xp, p['g_mix'], p['w_cat'])
        r3 = lambda a, bb, tt: a.reshape(bb, tt, a.shape[-1])
        hm, hg, mst, gst = _mixers(p, r3(um, b, t), r3(ug, b, t), zero_m, zero_g, L=M_CHUNK, n_valid=M_CHUNK)
        q_hm, ks, vs, kw, vw, slc_f, win_f, gates = _rows(r3(nq, b, t), r3(nsw, b, t), r3(ngate, b, t), cos_p, sin_p,
                                                          p['gq'], p['gks'], p['gkw'], bm)
        kc, vc = _compress(ncmp.reshape(b, nu_p, 4096), p['w1r'], p['pe_r'], p['w2r'], p['gkc'], cos_cp, sin_cp, bm)
        oc, sel = _cmp_attn(q_hm, kc, vc, c2s_p, tq=tq_p, pos0=0, n_cmp=ncmp_p)
        hn = _slcwin(q_hm, ks, vs, kw, vw, sel, e_p, oc, gates, tq=tq_p)
        xp = _post(xp, hm.reshape(b * t, -1), hg.reshape(b * t, -1), hn.reshape(b * t, -1),
                   p['w_out'], p['g_ffn'], p['w_up_r'], p['w_dn_r'])
        c_o, n_o, m_o, conv_o = _mlstm_state_out(*mst)
        kv6 = lambda a: a.reshape(a.shape[0], a.shape[1], 2, N_KV, N_HD)
        for lst, val in zip(pl_out, (kv6(ncmp.reshape(b, t, 256)), kv6(slc_f), kv6(win_f[:, t - min(WINDOW, t):]),
                                     c_o, n_o, m_o, conv_o, gst.reshape(b, G_HEADS, G_DK, G_DV))):
            lst.append(val)

        um, ug, nq, ncmp, nsw, ngate = _pre(xs, p['g_mix'], p['w_cat'])
        mstate = _mlstm_state_in(state_mlstm_conv[l], state_mlstm_C[l], state_mlstm_n[l], state_mlstm_m[l])
        gstate = state_gla_S[l].astype(F32).reshape(db, LANES, G_DV)
        hm, hg, mst, gst = _mixers(p, r3(um, db, SAMPLE_T), r3(ug, db, SAMPLE_T), mstate, gstate,
                                   L=SAMPLE_T, n_valid=td)
        q_hm, ks_n, vs_n, kw_n, vw_n, slc_f, win_f, gates = _rows(
            r3(nq, db, SAMPLE_T), r3(nsw, db, SAMPLE_T), r3(ngate, db, SAMPLE_T), cos_s, sin_s,
            p['gq'], p['gks'], p['gkw'], bm)
        kc, vc = _compress_paged(page_table, cmp_t, l, p['w1r'], p['pe_r'], p['w2r'], p['gkc'], cos_cs, sin_cs, bm)
        oc, sel = _cmp_attn(q_hm, kc, vc, c2s_s, tq=SAMPLE_T, pos0=past, n_cmp=ncmp_s)
        pad_page = lambda a: jnp.pad(a, ((0, 0), (0, PAGE_SIZE - a.shape[1]), (0, 0)))
        hn = _slc_sample(page_table, slc_t, l, q_hm, sel, e_s_main, e_s_new, pad_page(slc_f), win_t,
                         pad_page(win_f), oc, gates, ts=-(-td // 8) * 8)
        new_win = jnp.concatenate([state_win_kv[l][:, td:].astype(F32), kv6(win_f[:, :td])], axis=1)
        xs = _post(xs, hm.reshape(db * SAMPLE_T, -1), hg.reshape(db * SAMPLE_T, -1), hn.reshape(db * SAMPLE_T, -1),
                   p['w_out'], p['g_ffn'], p['w_up_r'], p['w_dn_r'])
        c_o, n_o, m_o, conv_o = _mlstm_state_out(*mst)
        for lst, val in zip(sl_out, (kv6(ncmp.reshape(db, SAMPLE_T, 256)[:, :td]), kv6(slc_f[:, :td]),
                                     new_win,
                                     c_o, n_o, m_o, conv_o, gst.reshape(db, G_HEADS, G_DK, G_DV))):
            lst.append(val)

    outs_p = [jnp.stack(a) for a in pl_out]
    outs_s = [jnp.stack(a) for a in sl_out]
    y_p = xp.reshape(b, t, D_MODEL)
    y_s = xs.reshape(db, SAMPLE_T, D_MODEL)[:, :td]
    return (y_p, y_s, *outs_p, *outs_s)
```

```python
import functools

import numpy as np
import jax
import jax.numpy as jnp
from jax import lax
from jax.experimental import pallas as pl
from jax.experimental.pallas import tpu as pltpu

F32 = jnp.float32
BF16 = jnp.bfloat16

D_MODEL = 1024
M_HEADS, M_HD, M_WIDTH, M_CONV, M_CHUNK = 4, 64, 256, 4, 64
G_HEADS, G_DK, G_DV, G_WIDTH, G_RANK, G_TAU, G_CHUNK = 4, 32, 64, 256, 16, 16.0, 64
N_HEADS, N_HD, N_KV, N_REP, N_WIDTH = 8, 64, 2, 4, 512
CMP_LEN, CMP_STRIDE, CMP_HIDDEN = 32, 16, 128
SLC_BLOCK, SLC_TOPK, WINDOW = 64, 16, 512
ROPE_THETA = 10000.0
D_FF = 2816
PAGE_SIZE = 128
NEG = -1e30
FORCE_BONUS = 1e4
EPS = 1e-6
LOG2E = 1.4426950408889634

LANES = 128
VMEM_LIMIT = 56 * 1024 * 1024
KV_TILE = 512
SAMPLE_T = 16

_OFF = {}
_o = 0
for _name, _w in (('m_qk', 512), ('m_v', 256), ('m_i', 4), ('m_f', 4), ('m_o', 256),
                  ('g_q', 128), ('g_k', 128), ('g_v', 256), ('g_lr', 16), ('g_o', 256),
                  ('n_q', 512), ('n_kc', 128), ('n_vc', 128), ('n_ks', 128),
                  ('n_vs', 128), ('n_kw', 128), ('n_vw', 128), ('n_gate', 24)):
    _OFF[_name] = (_o, _w)
    _o += _w
UM_W, UG_W, NQ_W, NCMP_W, NSW_W, NGATE_W = 1152, 896, 512, 256, 512, 128


def _cparams(sem):
    return pltpu.CompilerParams(dimension_semantics=sem, vmem_limit_bytes=VMEM_LIMIT)


def _dot(a, b):
    return jnp.dot(a, b, preferred_element_type=F32)


def _dot_nt(a, b):
    return lax.dot_general(a, b, (((1,), (1,)), ((), ())), preferred_element_type=F32)


def _dot_hi(a, b):
    return jnp.dot(a, b, preferred_element_type=F32, precision=lax.Precision.HIGHEST)


def _dot_split(a, b_bf):
    hi = a.astype(BF16)
    lo = (a - hi.astype(F32)).astype(BF16)
    return _dot(hi, b_bf) + _dot(lo, b_bf)


def _log_sigmoid(x):
    return jnp.minimum(x, 0.0) - jnp.log(1.0 + jnp.exp(-jnp.abs(x)))


def _sigmoid(x):
    return 1.0 / (1.0 + jnp.exp(-x))


def _transpose(x):
    r, c = x.shape
    if r < LANES:
        x = jnp.concatenate([x, jnp.zeros((LANES - r, c), x.dtype)], axis=0)
    parts = [x[:, i * LANES:(i + 1) * LANES].T[:, :r] for i in range(c // LANES)]
    return parts[0] if len(parts) == 1 else jnp.concatenate(parts, axis=0)


def _transpose2d(x):
    r, c = x.shape
    rp, cp = -(-r // LANES) * LANES, -(-c // LANES) * LANES
    if cp > c:
        x = jnp.concatenate([x, jnp.zeros((r, cp - c), x.dtype)], axis=1)
    if rp > r:
        x = jnp.concatenate([x, jnp.zeros((rp - r, cp), x.dtype)], axis=0)
    out_rows = []
    for j in range(cp // LANES):
        blocks = [x[i * LANES:(i + 1) * LANES, j * LANES:(j + 1) * LANES].T for i in range(rp // LANES)]
        out_rows.append(blocks[0] if len(blocks) == 1 else jnp.concatenate(blocks, axis=1))
    out = out_rows[0] if len(out_rows) == 1 else jnp.concatenate(out_rows, axis=0)
    return out[:c, :r]


def _iota(shape, dim):
    return lax.broadcasted_iota(jnp.int32, shape, dim)


def _pre_kernel(x_ref, g_ref, w_ref, um_ref, ug_ref, nq_ref, ncmp_ref, nsw_ref, ngate_ref):
    x = x_ref[...]
    h = x * lax.rsqrt(jnp.mean(x * x, axis=-1, keepdims=True) + EPS) * g_ref[...]
    u = _dot(h.astype(BF16), w_ref[...])
    o = 0
    for ref, w in ((um_ref, UM_W), (ug_ref, UG_W), (nq_ref, NQ_W), (ncmp_ref, NCMP_W),
                   (nsw_ref, NSW_W), (ngate_ref, NGATE_W)):
        ref[...] = u[:, o:o + w]
        o += w


def _pre(x2, g, w_cat):
    m = x2.shape[0]
    tm = min(m, 256)
    widths = (UM_W, UG_W, NQ_W, NCMP_W, NSW_W, NGATE_W)
    return pl.pallas_call(
        _pre_kernel,
        grid=(m // tm,),
        in_specs=[pl.BlockSpec((tm, D_MODEL), lambda i: (i, 0)),
                  pl.BlockSpec((1, D_MODEL), lambda i: (0, 0)),
                  pl.BlockSpec((D_MODEL, sum(widths)), lambda i: (0, 0))],
        out_specs=[pl.BlockSpec((tm, w), lambda i: (i, 0)) for w in widths],
        out_shape=[jax.ShapeDtypeStruct((m, w), F32) for w in widths],
        compiler_params=_cparams(("parallel",)),
        name="pre_proj",
    )(x2, g, w_cat)


def _post_kernel(x_ref, hm_ref, hg_ref, hn_ref, wout_ref, g_ref, wup_ref, wdn_ref, o_ref, h2_ref, *, tf):
    @pl.when(pl.program_id(1) == 0)
    def _():
        xn = x_ref[...]
        xn = xn + _dot(hm_ref[...], wout_ref[0:256, :])
        xn = xn + _dot(hg_ref[...], wout_ref[256:512, :])
        xn = xn + _dot(hn_ref[...], wout_ref[512:1024, :])
        o_ref[...] = xn
        h2 = xn * lax.rsqrt(jnp.mean(xn * xn, axis=-1, keepdims=True) + EPS) * g_ref[...]
        h2_ref[...] = h2.astype(BF16)

    au = _dot(h2_ref[...], wup_ref[0])
    a = au[:, :tf]
    act = (a * _sigmoid(a) * au[:, tf:]).astype(BF16)
    o_ref[...] += _dot(act, wdn_ref[0])


def _post(x2, hm, hg, hn, w_out, g, w_up_r, w_dn_r):
    m = x2.shape[0]
    tm = min(m, 512)
    nf, _, tf2 = w_up_r.shape
    tf = tf2 // 2
    return pl.pallas_call(
        functools.partial(_post_kernel, tf=tf),
        grid=(m // tm, nf),
        in_specs=[pl.BlockSpec((tm, D_MODEL), lambda i, f: (i, 0)),
                  pl.BlockSpec((tm, M_WIDTH), lambda i, f: (i, 0)),
                  pl.BlockSpec((tm, G_WIDTH), lambda i, f: (i, 0)),
                  pl.BlockSpec((tm, N_WIDTH), lambda i, f: (i, 0)),
                  pl.BlockSpec((D_MODEL, D_MODEL), lambda i, f: (0, 0)),
                  pl.BlockSpec((1, D_MODEL), lambda i, f: (0, 0)),
                  pl.BlockSpec((1, D_MODEL, tf2), lambda i, f: (f, 0, 0)),
                  pl.BlockSpec((1, tf, D_MODEL), lambda i, f: (f, 0, 0))],
        out_specs=pl.BlockSpec((tm, D_MODEL), lambda i, f: (i, 0)),
        out_shape=jax.ShapeDtypeStruct((m, D_MODEL), F32),
        scratch_shapes=[pltpu.VMEM((tm, D_MODEL), BF16)],
        compiler_params=_cparams(("parallel", "arbitrary")),
        name="post_ffn",
    )(x2, hm, hg, hn, w_out, g, w_up_r, w_dn_r)


CHUNK_UNROLL = 8


def _mlstm_kernel(um_ref, conv0_ref, ct0_ref, n0_ref, m0_ref, wconv_ref, bif_ref, gh_ref,
                  hm_ref, conv_out_ref, ct_out_ref, n_out_ref, m_out_ref,
                  xpad, q_s, k_s, ct_s, n_s, m_s, *, L, n_valid, blk):
    j = pl.program_id(1)

    @pl.when(j == 0)
    def _():
        xpad[0:8, :] = conv0_ref[0]
        ct_s[...] = ct0_ref[0]
        n_s[...] = n0_ref[0]
        m_s[...] = m0_ref[0]

    qk_pre = um_ref[0, :, 0:2 * M_WIDTH]
    xpad[8:8 + blk, :] = qk_pre
    wc = wconv_ref[...]
    acc = (xpad[5:5 + blk, :] * wc[0:1] + xpad[6:6 + blk, :] * wc[1:2]
           + xpad[7:7 + blk, :] * wc[2:3] + qk_pre * wc[3:4])
    qk = acc * _sigmoid(acc)
    q_s[...] = qk[:, :M_WIDTH]
    k_s[...] = qk[:, M_WIDTH:] * (M_HD ** -0.5)
    last = n_valid if blk == L else blk
    conv_out_ref[0] = xpad[last:last + 8, :]
    xpad[0:8, :] = xpad[blk:blk + 8, :]

    row = _iota((L, L), 0)
    col = _iota((L, L), 1)
    causal = row >= col
    tril = causal.astype(F32)
    lane_w = _iota((1, M_WIDTH), 1) // M_HD
    row_w = _iota((M_WIDTH, 1), 0) // M_HD
    valid_col = _iota((L, 1), 0) < n_valid

    row8 = _iota((8, LANES), 0)

    n_chunks = blk // L
    unroll = CHUNK_UNROLL if n_chunks % CHUNK_UNROLL == 0 else 1
    H = range(M_HEADS)
    U = range(unroll)
    hmask = [lane_w == h for h in H]
    lst = slice(n_valid - 1, n_valid)

    def group(i, state):
        ct, n_row, m_tile = state
        r0 = [pl.multiple_of((i * unroll + u) * L, L) for u in U]
        qc = [q_s[pl.ds(r, L), :] for r in r0]
        kc = [k_s[pl.ds(r, L), :] for r in r0]
        act = [um_ref[0, pl.ds(r, L), 1024:1152] + bif_ref[...] for r in r0]
        bcum = [_dot_hi(tril, _log_sigmoid(x)) for x in act]
        act_t = [_transpose(x) for x in act]
        bcum_t = [_transpose(x) for x in bcum]
        k_bf = [x.astype(BF16) for x in kc]
        kt_bf = [_transpose(x).astype(BF16) for x in kc]
        bcol = [[bcum[u][:, 4 + h:5 + h] for h in H] for u in U]
        dmat = [[jnp.where(causal, bcol[u][h] - bcum_t[u][4 + h:5 + h, :] + act_t[u][h:h + 1, :], NEG) for h in H]
                for u in U]
        m_loc = [[jnp.max(dmat[u][h], axis=1, keepdims=True) for h in H] for u in U]
        q_h = [[jnp.where(hmask[h], qc[u], 0.0) for h in H] for u in U]
        q_bf = [[q_h[u][h].astype(BF16) for h in H] for u in U]
        s = [[_dot_nt(q_bf[u][h], k_bf[u]) * jnp.exp(dmat[u][h] - m_loc[u][h]) for h in H] for u in U]
        v_h = [[um_ref[0, pl.ds(r, L), 512 + h * M_HD:512 + (h + 1) * M_HD] for h in H] for r in r0]
        sv = [[_dot(s[u][h].astype(BF16), v_h[u][h].astype(BF16)) for h in H] for u in U]
        ssum = [[jnp.sum(s[u][h], axis=1, keepdims=True) for h in H] for u in U]
        w_l = [[jnp.where(valid_col, jnp.exp(bcol[u][h][lst] - bcol[u][h] + act[u][:, h:h + 1] - m_loc[u][h][lst]), 0.0)
                for h in H] for u in U]
        upd = [[_dot(kt_bf[u], (v_h[u][h] * w_l[u][h]).astype(BF16)) for h in H] for u in U]
        ksum = [[jnp.sum(kc[u] * w_l[u][h], axis=0, keepdims=True) for h in H] for u in U]
        gate = [[_sigmoid(um_ref[0, pl.ds(r, L), 768 + h * M_HD:768 + (h + 1) * M_HD]) for h in H] for r in r0]
        for u in U:
            ct_bf = ct.astype(BF16)
            m_inter = [m_tile[h:h + 1, 0:1] + bcol[u][h] for h in H]
            m_new = [jnp.maximum(m_inter[h], m_loc[u][h]) for h in H]
            f = [jnp.exp(m_loc[u][h] - m_new[h]) for h in H]
            a_inter = [jnp.exp(m_inter[h] - m_new[h]) for h in H]
            num = [a_inter[h] * _dot(q_bf[u][h], ct_bf) + f[h] * sv[u][h] for h in H]
            qn = [jnp.sum(q_h[u][h] * n_row, axis=1, keepdims=True) for h in H]
            den = [a_inter[h] * qn[h] + f[h] * ssum[u][h] for h in H]
            hh = [gate[u][h] * (num[h] / jnp.maximum(jnp.abs(den[h]), jnp.exp(-m_new[h]))) for h in H]
            ms = [jnp.mean(x * x, axis=1, keepdims=True) for x in hh]
            outs = [hh[h] * lax.rsqrt(ms[h] + EPS) * gh_ref[:, h * M_HD:(h + 1) * M_HD] for h in H]
            hm_ref[0, pl.ds(r0[u], L), :] = jnp.concatenate(outs, axis=1).astype(BF16)
            ct_old, n_old = ct, n_row
            for h in H:
                ct = jnp.where(row_w == h, a_inter[h][lst] * ct_old + f[h][lst] * upd[u][h], ct)
                n_row = jnp.where(hmask[h], a_inter[h][lst] * n_old + f[h][lst] * ksum[u][h], n_row)
                m_tile = jnp.where(row8 == h, m_new[h][lst], m_tile)
        return ct, n_row, m_tile

    ct, n_row, m_tile = lax.fori_loop(0, n_chunks // unroll, group, (ct_s[...], n_s[0:1, :], m_s[...]))
    ct_s[...] = ct
    n_s[...] = jnp.broadcast_to(n_row, n_s.shape)
    m_s[...] = m_tile

    @pl.when(j == pl.num_programs(1) - 1)
    def _():
        ct_out_ref[0] = ct
        n_out_ref[0] = jnp.broadcast_to(n_row, n_s.shape)
        m_out_ref[0] = m_tile


def _mlstm(um, conv0, ct0, n0, m0, wconv, bif, gh, *, L, n_valid):
    b, t, _ = um.shape
    blk = min(t, 512)
    kern = functools.partial(_mlstm_kernel, L=L, n_valid=n_valid, blk=blk)
    per_b = lambda shape: pl.BlockSpec((1,) + shape, lambda i, j: (i,) + (0,) * len(shape))
    const = lambda shape: pl.BlockSpec(shape, lambda i, j: (0,) * len(shape))
    return pl.pallas_call(
        kern,
        grid=(b, t // blk),
        in_specs=[pl.BlockSpec((1, blk, UM_W), lambda i, j: (i, j, 0)),
                  per_b((8, 2 * M_WIDTH)), per_b((M_WIDTH, M_HD)), per_b((8, M_WIDTH)), per_b((8, LANES)),
                  const((8, 2 * M_WIDTH)), const((1, LANES)), const((1, M_WIDTH))],
        out_specs=[pl.BlockSpec((1, blk, M_WIDTH), lambda i, j: (i, j, 0)),
                   per_b((8, 2 * M_WIDTH)), per_b((M_WIDTH, M_HD)), per_b((8, M_WIDTH)), per_b((8, LANES))],
        out_shape=[jax.ShapeDtypeStruct((b, t, M_WIDTH), BF16),
                   jax.ShapeDtypeStruct((b, 8, 2 * M_WIDTH), F32),
                   jax.ShapeDtypeStruct((b, M_WIDTH, M_HD), F32),
                   jax.ShapeDtypeStruct((b, 8, M_WIDTH), F32),
                   jax.ShapeDtypeStruct((b, 8, LANES), F32)],
        scratch_shapes=[pltpu.VMEM((blk + 8, 2 * M_WIDTH), F32),
                        pltpu.VMEM((blk, M_WIDTH), F32), pltpu.VMEM((blk, M_WIDTH), F32),
                        pltpu.VMEM((M_WIDTH, M_HD), F32), pltpu.VMEM((8, M_WIDTH), F32),
                        pltpu.VMEM((8, LANES), F32)],
        compiler_params=_cparams(("parallel", "arbitrary")),
        name="mlstm",
    )(um, conv0, ct0, n0, m0, wconv, bif, gh)


def _gla_kernel(ug_ref, s0_ref, w2_ref, bg_ref, gh_ref, hg_ref, s_out_ref, s_s, *, L, n_valid, blk):
    j = pl.program_id(1)

    @pl.when(j == 0)
    def _():
        s_s[...] = s0_ref[0]

    row = _iota((L, L), 0)
    col = _iota((L, L), 1)
    causal = row >= col
    tril = causal.astype(F32)
    lane_k = _iota((1, LANES), 1) // G_DK
    row_k = _iota((LANES, 1), 0) // G_DK
    valid_col = _iota((L, 1), 0) < n_valid
    mid = max(n_valid // 2, 1)

    n_chunks = blk // L
    unroll = CHUNK_UNROLL if n_chunks % CHUNK_UNROLL == 0 else 1
    H = range(G_HEADS)
    U = range(unroll)
    hmask = [lane_k == h for h in H]

    def group(i, s_all):
        r0 = [pl.multiple_of((i * unroll + u) * L, L) for u in U]
        q = [ug_ref[0, pl.ds(r, L), 0:128] * (G_DK ** -0.5) for r in r0]
        k = [ug_ref[0, pl.ds(r, L), 128:256] for r in r0]
        z = [_dot_hi(ug_ref[0, pl.ds(r, L), 768:896], w2_ref[...]) + bg_ref[...] for r in r0]
        g = [_log_sigmoid(x) * (1.0 / G_TAU) for x in z]
        bc = [_dot_hi(tril, x) for x in g]
        c_ref = [x[mid - 1:mid] for x in bc]
        last = [x[n_valid - 1:n_valid] for x in bc]
        qe = [q[u] * jnp.exp(bc[u] - c_ref[u]) for u in U]
        ke = [(k[u] * jnp.exp(c_ref[u] - bc[u])).astype(BF16) for u in U]
        qin = [q[u] * jnp.exp(bc[u]) for u in U]
        kd_t = [_transpose(jnp.where(valid_col, k[u] * jnp.exp(last[u] - bc[u]), 0.0)).astype(BF16) for u in U]
        decay = [jnp.exp(_transpose(jnp.broadcast_to(x, (8, LANES)))[:, 0:1]) for x in last]
        v_h = [[ug_ref[0, pl.ds(r, L), 256 + h * G_DV:256 + (h + 1) * G_DV].astype(BF16) for h in H] for r in r0]
        a = [[jnp.where(causal, _dot_nt(jnp.where(hmask[h], qe[u], 0.0).astype(BF16), ke[u]), 0.0) for h in H]
             for u in U]
        intra = [[_dot(a[u][h].astype(BF16), v_h[u][h]) for h in H] for u in U]
        q_in = [[jnp.where(hmask[h], qin[u], 0.0).astype(BF16) for h in H] for u in U]
        upd = [[_dot(kd_t[u], v_h[u][h]) for h in H] for u in U]
        gate = [[ug_ref[0, pl.ds(r, L), 512 + h * G_DV:512 + (h + 1) * G_DV] for h in H] for r in r0]
        gate = [[x * _sigmoid(x) * gh_ref[:, h * G_DV:(h + 1) * G_DV] for h, x in enumerate(gs)] for gs in gate]
        for u in U:
            s_bf = s_all.astype(BF16)
            o = [_dot(q_in[u][h], s_bf) + intra[u][h] for h in H]
            ms = [jnp.mean(x * x, axis=1, keepdims=True) for x in o]
            outs = [o[h] * lax.rsqrt(ms[h] + EPS) * gate[u][h] for h in H]
            hg_ref[0, pl.ds(r0[u], L), :] = jnp.concatenate(outs, axis=1).astype(BF16)
            s_all = decay[u] * s_all
            for h in H:
                s_all = s_all + jnp.where(row_k == h, upd[u][h], 0.0)
        return s_all

    s_fin = lax.fori_loop(0, n_chunks // unroll, group, s_s[...])
    s_s[...] = s_fin

    @pl.when(j == pl.num_programs(1) - 1)
    def _():
        s_out_ref[0] = s_fin


def _gla(ug, s0, w2p, bg, gh, *, L, n_valid):
    b, t, _ = ug.shape
    blk = min(t, 512)
    kern = functools.partial(_gla_kernel, L=L, n_valid=n_valid, blk=blk)
    return pl.pallas_call(
        kern,
        grid=(b, t // blk),
        in_specs=[pl.BlockSpec((1, blk, UG_W), lambda i, j: (i, j, 0)),
                  pl.BlockSpec((1, LANES, G_DV), lambda i, j: (i, 0, 0)),
                  pl.BlockSpec((LANES, LANES), lambda i, j: (0, 0)),
                  pl.BlockSpec((1, LANES), lambda i, j: (0, 0)),
                  pl.BlockSpec((1, G_WIDTH), lambda i, j: (0, 0))],
        out_specs=[pl.BlockSpec((1, blk, G_WIDTH), lambda i, j: (i, j, 0)),
                   pl.BlockSpec((1, LANES, G_DV), lambda i, j: (i, 0, 0))],
        out_shape=[jax.ShapeDtypeStruct((b, t, G_WIDTH), BF16),
                   jax.ShapeDtypeStruct((b, LANES, G_DV), F32)],
        scratch_shapes=[pltpu.VMEM((LANES, G_DV), F32)],
        compiler_params=_cparams(("parallel", "arbitrary")),
        name="gla",
    )(ug, s0, w2p, bg, gh)


def _group_mean_sq(x, bm_bf):
    return _dot_split(x * x, bm_bf)


def _rope_slab(y, cos, sin_signed):
    lane = _iota(y.shape, 1)
    rot = jnp.where((lane % N_HD) < (N_HD // 2), pltpu.roll(y, 96, axis=1), pltpu.roll(y, 32, axis=1))
    return y * cos + rot * sin_signed


def _norm_rope_slab(x, g, cos, sin_signed, bm_bf):
    y = x * lax.rsqrt(_group_mean_sq(x, bm_bf) + EPS) * g
    return _rope_slab(y, cos, sin_signed)


def _rows_kernel(nq_ref, nsw_ref, ngate_ref, cos_ref, sin_ref, gq_ref, gks_ref, gkw_ref, bm_ref,
                 q_ref, ks_ref, vs_ref, kw_ref, vw_ref, slc_ref, win_ref, gates_ref):
    cos = cos_ref[...]
    sin = sin_ref[...]
    bm = bm_ref[...]
    low = _iota((1, LANES), 1) < N_HD
    for sl in range(4):
        x = nq_ref[0, :, sl * LANES:(sl + 1) * LANES]
        y = _norm_rope_slab(x, gq_ref[:, sl * LANES:(sl + 1) * LANES], cos, sin, bm) * (N_HD ** -0.5 * LOG2E)
        y_sw = pltpu.roll(y, N_HD, axis=1)
        if sl < 2:
            even, odd = jnp.where(low, y, 0.0), jnp.where(low, y_sw, 0.0)
        else:
            even, odd = jnp.where(low, 0.0, y_sw), jnp.where(low, 0.0, y)
        q_ref[0, 2 * sl] = even.astype(BF16)
        q_ref[0, 2 * sl + 1] = odd.astype(BF16)
    ks = _norm_rope_slab(nsw_ref[0, :, 0:128], gks_ref[...], cos, sin, bm)
    vs = nsw_ref[0, :, 128:256]
    kw = _norm_rope_slab(nsw_ref[0, :, 256:384], gkw_ref[...], cos, sin, bm)
    vw = nsw_ref[0, :, 384:512]
    slc_ref[0, :, 0:128] = ks
    slc_ref[0, :, 128:256] = vs
    win_ref[0, :, 0:128] = kw
    win_ref[0, :, 128:256] = vw
    ks_ref[0] = ks.astype(BF16)
    kw_ref[0] = kw.astype(BF16)
    for ref, val in ((vs_ref, vs), (vw_ref, vw)):
        ref[0, 0] = jnp.where(low, val, 1.0).astype(BF16)
        ref[0, 1] = jnp.where(low, 1.0, val).astype(BF16)
    gt = _sigmoid(ngate_ref[0])
    gates_ref[0, 0] = gt
    gates_ref[0, 1] = pltpu.roll(gt, LANES - 3 * N_REP, axis=1)


def _rows(nq, nsw, ngate, cos, sin, gq, gks, gkw, bm):
    b, t, _ = nq.shape
    tm = min(t, 512)
    tok = lambda w: pl.BlockSpec((1, tm, w), lambda i, j: (i, j, 0))
    const = lambda shape: pl.BlockSpec(shape, lambda i, j: (0,) * len(shape))
    k_shape = jax.ShapeDtypeStruct((b, t, LANES), BF16)
    v_shape = jax.ShapeDtypeStruct((b, N_KV, t, LANES), BF16)
    per_group = pl.BlockSpec((1, N_KV, tm, LANES), lambda i, j: (i, 0, j, 0))
    return pl.pallas_call(
        _rows_kernel,
        grid=(b, t // tm),
        in_specs=[tok(NQ_W), tok(NSW_W), tok(NGATE_W),
                  pl.BlockSpec((tm, LANES), lambda i, j: (j, 0)),
                  pl.BlockSpec((tm, LANES), lambda i, j: (j, 0)),
                  const((1, NQ_W)), const((1, LANES)), const((1, LANES)), const((LANES, LANES))],
        out_specs=[pl.BlockSpec((1, N_HEADS, tm, LANES), lambda i, j: (i, 0, j, 0)),
                   tok(LANES), per_group, tok(LANES), per_group, tok(256), tok(256), per_group],
        out_shape=[jax.ShapeDtypeStruct((b, N_HEADS, t, LANES), BF16), k_shape, v_shape, k_shape, v_shape,
                   jax.ShapeDtypeStruct((b, t, 256), F32), jax.ShapeDtypeStruct((b, t, 256), F32),
                   jax.ShapeDtypeStruct((b, N_KV, t, LANES), F32)],
        compiler_params=_cparams(("parallel", "parallel")),
        name="nsa_rows",
    )(nq, nsw, ngate, cos, sin, gq, gks, gkw, bm)


def _block_mlp(x_bf, w1, pe, w2, prev_first):
    p = _dot(x_bf, w1)
    bias = _dot(pe.astype(BF16), w1)
    pa = p[:, :256] + bias[0:1, :256]
    pb = p[:, 256:] + bias[1:2, 256:]
    pa_prev = jnp.where(_iota(pa.shape, 0) == 0, prev_first, pltpu.roll(pa, 1, axis=0))
    pre = pa_prev + pb
    hid = (pre * _sigmoid(pre)).astype(BF16)
    return _dot(hid, w2), pa[pa.shape[0] - 1:, :]


def _compress_kernel(u_ref, w1_ref, pe_ref, w2_ref, gkc_ref, cos_ref, sin_ref, bm_ref, kc_ref, vc_ref):
    outs = []
    for kv in range(2):
        x = jnp.concatenate(
            [u_ref[0, :, j * 256 + kv * LANES:j * 256 + (kv + 1) * LANES] for j in range(CMP_STRIDE)],
            axis=1).astype(BF16)
        outs.append(_block_mlp(x, w1_ref[kv], pe_ref[kv], w2_ref[kv], jnp.zeros((1, 256), F32))[0])
    kc_ref[0] = _norm_rope_slab(outs[0], gkc_ref[...], cos_ref[...], sin_ref[...], bm_ref[...]).astype(BF16)
    vc_ref[0] = outs[1].astype(BF16)


CMP_PAGES = 32
PAGE_UNITS = PAGE_SIZE // CMP_STRIDE


def _compress_paged_kernel(pt_ref, *refs):
    pages = refs[:CMP_PAGES]
    (w1_ref, pe_ref, w2_ref, gkc_ref, cos_ref, sin_ref, bm_ref, kc_ref, vc_ref, t_scr, x_scr, carry_scr) = refs[CMP_PAGES:]

    @pl.when(pl.program_id(1) == 0)
    def _():
        carry_scr[...] = jnp.zeros(carry_scr.shape, F32)

    outs = []
    for kv in range(2):
        for i in range(CMP_PAGES):
            t_scr[i] = pages[i][0, 0, kv].reshape(LANES, PAGE_SIZE).T
            for j in range(CMP_STRIDE):
                x_scr[i * PAGE_UNITS:(i + 1) * PAGE_UNITS, j * LANES:(j + 1) * LANES] = \
                    t_scr[i, pl.ds(j, PAGE_UNITS, stride=CMP_STRIDE), :]
        out, last = _block_mlp(x_scr[...].astype(BF16), w1_ref[kv], pe_ref[kv], w2_ref[kv], carry_scr[kv, 0:1, :])
        carry_scr[kv, 0:1, :] = last
        outs.append(out)
    kc_ref[0] = _norm_rope_slab(outs[0], gkc_ref[...], cos_ref[...], sin_ref[...], bm_ref[...]).astype(BF16)
    vc_ref[0] = outs[1].astype(BF16)


def _compress_paged(page_table, cache_t, layer, w1r, pe_r, w2r, gkc, cos_c, sin_c, bm):
    db, n_pages = page_table.shape
    step_units = CMP_PAGES * PAGE_UNITS
    nu = n_pages * PAGE_UNITS

    def page_spec(i):
        return pl.BlockSpec((1, 1, 2, N_KV, N_HD, PAGE_SIZE),
                            lambda b, c, pt: (layer, pt[b, c * CMP_PAGES + i], 0, 0, 0, 0))

    const = lambda shape: pl.BlockSpec(shape, lambda b, c, pt: (0,) * len(shape))
    tab = pl.BlockSpec((step_units, LANES), lambda b, c, pt: (c, 0))
    out = jax.ShapeDtypeStruct((db, nu, LANES), BF16)
    ospec = pl.BlockSpec((1, step_units, LANES), lambda b, c, pt: (b, c, 0))
    return pl.pallas_call(
        _compress_paged_kernel,
        grid_spec=pltpu.PrefetchScalarGridSpec(
            num_scalar_prefetch=1,
            grid=(db, n_pages // CMP_PAGES),
            in_specs=[page_spec(i) for i in range(CMP_PAGES)] + [
                const((2, 2048, 512)), const((2, 8, 2048)), const((2, 256, LANES)),
                const((1, LANES)), tab, tab, const((LANES, LANES))],
            out_specs=[ospec, ospec],
            scratch_shapes=[pltpu.VMEM((CMP_PAGES, PAGE_SIZE, LANES), F32),
                            pltpu.VMEM((step_units, CMP_STRIDE * LANES), F32),
                            pltpu.VMEM((2, 8, 256), F32)]),
        out_shape=[out, out],
        compiler_params=_cparams(("parallel", "arbitrary")),
        name="nsa_compress_paged",
    )(page_table, *([cache_t] * CMP_PAGES), w1r, pe_r, w2r, gkc, cos_c, sin_c, bm)


def _compress(units, w1r, pe_r, w2r, gkc, cos_c, sin_c, bm):
    b, nu, _ = units.shape
    const = lambda shape: pl.BlockSpec(shape, lambda i: (0,) * len(shape))
    out = jax.ShapeDtypeStruct((b, nu, LANES), BF16)
    return pl.pallas_call(
        _compress_kernel,
        grid=(b,),
        in_specs=[pl.BlockSpec((1, nu, 4096), lambda i: (i, 0, 0)),
                  const((2, 2048, 512)), const((2, 8, 2048)), const((2, 256, LANES)),
                  const((1, LANES)), const((nu, LANES)), const((nu, LANES)), const((LANES, LANES))],
        out_specs=[pl.BlockSpec((1, nu, LANES), lambda i: (i, 0, 0))] * 2,
        out_shape=[out, out],
        compiler_params=_cparams(("parallel",)),
        name="nsa_compress",
    )(units, w1r, pe_r, w2r, gkc, cos_c, sin_c, bm)


def _cmp_kernel(q_ref, kc_ref, vc_ref, c2s_ref, oc_ref, sel_ref, *, tq, pos0, n_cmp, nsel):
    qi = pl.program_id(1)
    nc_pad = kc_ref.shape[1]
    q = q_ref[0].reshape(N_HEADS * tq, LANES)
    s = _dot_nt(q, kc_ref[0]).reshape(N_HEADS, tq, nc_pad)
    qpos = pos0 + qi * tq + _iota((tq, 1), 0)
    c_idx = _iota((tq, nc_pad), 1)
    c_real = jnp.where(c_idx >= 1, c_idx, nc_pad + n_cmp) <= n_cmp
    c_ok = jnp.where(c_real, c_idx * CMP_STRIDE + (CMP_LEN - CMP_STRIDE - 1), jnp.int32(2 ** 30)) <= qpos
    s = s + jnp.where(c_ok, 0.0, NEG)[None]
    p = jnp.exp2(s - jnp.max(s, axis=-1, keepdims=True))
    p = p * ((qpos >= CMP_LEN - 1).astype(F32)[None] / jnp.sum(p, axis=-1, keepdims=True))
    oc_ref[0] = _dot(p.reshape(N_HEADS * tq, nc_pad).astype(BF16), vc_ref[0]).reshape(N_HEADS, tq, LANES)
    psum = jnp.sum(p.reshape(N_KV, N_REP, tq, nc_pad), axis=1).reshape(N_KV * tq, nc_pad)
    imp = _dot_split(psum, c2s_ref[...])
    work = _transpose2d(imp)
    ncol = N_KV * tq
    qpos_row = pos0 + qi * tq + _iota((1, ncol), 1) % tq
    blk = _iota((nsel, ncol), 0)
    cur = qpos_row // SLC_BLOCK
    forced = jnp.where(blk == 0, 1.0, jnp.where(blk == cur, 1.0, jnp.where(blk == cur - 1, 1.0, 0.0)))
    work = jnp.where(blk * SLC_BLOCK <= qpos_row, work + FORCE_BONUS * forced, NEG)
    blk_f = blk.astype(F32)
    for _ in range(SLC_TOPK):
        m = jnp.max(work, axis=0, keepdims=True)
        idx = jnp.min(jnp.where(work == m, blk_f, float(nsel)), axis=0, keepdims=True)
        work = jnp.where(blk_f == idx, -jnp.inf, work)
    sel = jnp.where(work == -jnp.inf, 1.0, 0.0)
    sel_ref[0] = _transpose2d(sel).reshape(N_KV, tq, nsel).astype(BF16)


def _cmp_attn(q_pad, kc, vc, c2s, *, tq, pos0, n_cmp):
    b, _, t, _ = q_pad.shape
    nc_pad = kc.shape[1]
    nsel = c2s.shape[1]
    kern = functools.partial(_cmp_kernel, tq=tq, pos0=pos0, n_cmp=n_cmp, nsel=nsel)
    return pl.pallas_call(
        kern,
        grid=(b, t // tq),
        in_specs=[pl.BlockSpec((1, N_HEADS, tq, LANES), lambda i, j: (i, 0, j, 0)),
                  pl.BlockSpec((1, nc_pad, LANES), lambda i, j: (i, 0, 0)),
                  pl.BlockSpec((1, nc_pad, LANES), lambda i, j: (i, 0, 0)),
                  pl.BlockSpec((nc_pad, nsel), lambda i, j: (0, 0))],
        out_specs=[pl.BlockSpec((1, N_HEADS, tq, LANES), lambda i, j: (i, 0, j, 0)),
                   pl.BlockSpec((1, N_KV, tq, nsel), lambda i, j: (i, 0, j, 0))],
        out_shape=[jax.ShapeDtypeStruct((b, N_HEADS, t, LANES), F32),
                   jax.ShapeDtypeStruct((b, N_KV, t, nsel), BF16)],
        compiler_params=_cparams(("parallel", "parallel")),
        name="nsa_cmp_topk",
    )(q_pad, kc, vc, c2s)


SCORE_DTYPE = F32
SLAB = 64


def _tile_scores(q_parts, k, s_scr, *, tq, width):
    for r in range(N_REP):
        s_scr[r * tq:(r + 1) * tq, 0:width] = _dot_nt(q_parts[r], k).astype(s_scr.dtype)


def _tile_update(v, bias_scr, s_scr, p_scr, m_scr, alpha_scr, acc_scr, *, tq, width):
    nch = width // LANES
    sd = s_scr.dtype

    def chunk(rows, i, c):
        x = s_scr[rows, c * LANES:(c + 1) * LANES]
        if bias_scr is not None:
            x = x + bias_scr[i * SLAB:(i + 1) * SLAB, c * LANES:(c + 1) * LANES]
        return x

    slabs = [(slice(r * tq + i * SLAB, r * tq + (i + 1) * SLAB), i) for r in range(N_REP) for i in range(tq // SLAB)]
    for rows, i in slabs:
        mx = chunk(rows, i, 0)
        for c in range(1, nch):
            mx = jnp.maximum(mx, chunk(rows, i, c))
        m_old = m_scr[rows, :]
        m_new = jnp.maximum(m_old, jnp.max(mx.astype(F32), axis=1, keepdims=True)).astype(sd).astype(F32)
        alpha_scr[rows, :] = jnp.exp2(m_old - m_new)
        m_scr[rows, :] = m_new
    for rows, i in slabs:
        m_new = m_scr[rows, :].astype(sd)
        for c in range(nch):
            p_scr[rows, c * LANES:(c + 1) * LANES] = jnp.exp2(chunk(rows, i, c) - m_new).astype(BF16)
    for r in range(N_REP):
        rs = slice(r * tq, (r + 1) * tq)
        acc_scr[rs, :] = alpha_scr[rs, :] * acc_scr[rs, :] + _dot(p_scr[rs, 0:width], v)


def _slcwin_kernel(q_ref, ks_ref, vs_ref, kw_ref, vw_ref, sel_ref, et_ref, oc_ref, gates_ref, hn_ref,
                   s_scr, bias_scr, p_scr, m_scr, alpha_scr, acc_scr, ow_scr, *, tq):
    g = pl.program_id(1)
    qi = pl.program_id(2)
    rows = N_REP * tq
    q0 = qi * tq
    qpos = q0 + _iota((tq, 1), 0)
    scr = dict(p_scr=p_scr, m_scr=m_scr, alpha_scr=alpha_scr, acc_scr=acc_scr, tq=tq)

    def reset():
        m_scr[...] = jnp.full((rows, LANES), NEG, F32)
        acc_scr[...] = jnp.zeros((rows, LANES), F32)

    def result():
        acc = acc_scr[...]
        return acc / pltpu.roll(acc, N_HD, axis=1)

    wlen = WINDOW + tq
    start = pl.multiple_of(jnp.maximum(q0 - WINDOW, 0), tq)
    reset()
    dist = qpos - (start + _iota((tq, wlen), 1))
    bias_scr[:, 0:wlen] = jnp.where(jnp.where(dist >= 0, dist, WINDOW) < WINDOW, 0.0, NEG).astype(bias_scr.dtype)
    _tile_scores([q_ref[0, r] for r in range(N_REP)], kw_ref[0, pl.ds(start, wlen), :], s_scr, tq=tq, width=wlen)
    _tile_update(vw_ref[0, 0, pl.ds(start, wlen), :], bias_scr, s_scr, width=wlen, **scr)
    ow_scr[...] = result()

    reset()
    sel_m = ((sel_ref[0, 0].astype(F32) - 1.0) * (-NEG)).astype(BF16)
    q_aug = [jnp.concatenate([q_ref[0, r], sel_m], axis=1) for r in range(N_REP)]
    n_kv = (q0 + tq + KV_TILE - 1) // KV_TILE

    def tile(j, width, bias):
        k0 = pl.multiple_of(j * KV_TILE, KV_TILE)
        k_aug = jnp.concatenate([ks_ref[0, pl.ds(k0, width), :], et_ref[pl.ds(k0, width), :]], axis=1)
        _tile_scores(q_aug, k_aug, s_scr, tq=tq, width=width)
        _tile_update(vs_ref[0, 0, pl.ds(k0, width), :], bias, s_scr, width=width, **scr)

    n_plain = n_kv - 1

    def body(j, carry):
        tile(j, KV_TILE, None)
        return carry

    lax.fori_loop(0, n_plain, body, 0)

    def last_tile(width):
        bias_scr[:, 0:width] = jnp.where(n_plain * KV_TILE + _iota((tq, width), 1) <= qpos, 0.0, NEG).astype(
            bias_scr.dtype)
        tile(n_plain, width, bias_scr)

    if KV_TILE % tq == 0 and tq < KV_TILE:
        aligned = q0 % KV_TILE == 0
        pl.when(aligned)(lambda: last_tile(tq))
        pl.when(jnp.logical_not(aligned))(lambda: last_tile(KV_TILE))
    else:
        last_tile(KV_TILE)
    o_s_all = result()

    gt = gates_ref[0, 0]
    low = _iota((1, LANES), 1) < N_HD
    vals = []
    for r in range(N_REP):
        rs = slice(r * tq, (r + 1) * tq)
        vals.append(gt[:, 3 * r:3 * r + 1] * oc_ref[0, r] + gt[:, 3 * r + 1:3 * r + 2] * o_s_all[rs]
                    + gt[:, 3 * r + 2:3 * r + 3] * ow_scr[rs, :])
    for pair in range(N_REP // 2):
        a, b = vals[2 * pair], vals[2 * pair + 1]
        a_sw, b_sw = pltpu.roll(a, N_HD, axis=1), pltpu.roll(b, N_HD, axis=1)
        lo = jnp.where(g == 0, a, a_sw)
        hi = jnp.where(g == 0, b_sw, b)
        hn_ref[0, :, pair * LANES:(pair + 1) * LANES] = jnp.where(low, lo, hi).astype(BF16)


def _slcwin(q_pad, ks, vs, kw, vw, sel, e_t, oc, gates, *, tq):
    b, _, t, _ = q_pad.shape
    nsel = sel.shape[3]
    assert nsel == LANES and e_t.shape == (t, LANES)
    rows = N_REP * tq
    wlen = WINDOW + tq
    kern = functools.partial(_slcwin_kernel, tq=tq)
    qspec = pl.BlockSpec((1, N_REP, tq, LANES), lambda i, g, j: (i, g, j, 0))
    keys = pl.BlockSpec((1, t, LANES), lambda i, g, j: (i, 0, 0))
    vals = pl.BlockSpec((1, 1, t, LANES), lambda i, g, j: (i, g, 0, 0))
    return pl.pallas_call(
        kern,
        grid=(b, N_KV, t // tq),
        in_specs=[qspec, keys, vals, keys, vals,
                  pl.BlockSpec((1, 1, tq, nsel), lambda i, g, j: (i, g, j, 0)),
                  pl.BlockSpec((t, LANES), lambda i, g, j: (0, 0)),
                  qspec,
                  pl.BlockSpec((1, 1, tq, LANES), lambda i, g, j: (i, g, j, 0))],
        out_specs=pl.BlockSpec((1, tq, N_REP * N_HD), lambda i, g, j: (i, j, g)),
        out_shape=jax.ShapeDtypeStruct((b, t, N_WIDTH), BF16),
        scratch_shapes=[pltpu.VMEM((rows, wlen), SCORE_DTYPE), pltpu.VMEM((tq, wlen), SCORE_DTYPE),
                        pltpu.VMEM((rows, wlen), BF16),
                        pltpu.VMEM((rows, LANES), F32), pltpu.VMEM((rows, LANES), F32),
                        pltpu.VMEM((rows, LANES), F32), pltpu.VMEM((rows, LANES), F32)],
        compiler_params=_cparams(("parallel", "parallel", "parallel")),
        name="nsa_slc_win",
    )(q_pad, ks, vs, kw, vw, sel, e_t, oc, gates)


SLC_PAGES = 32


def _heads_bias(b2, ts):
    w = b2.shape[1]
    return jnp.broadcast_to(b2.reshape(N_KV, 1, ts, w), (N_KV, N_REP, ts, w)).reshape(N_HEADS * ts, w)


def _slc_sample_kernel(pt_ref, *refs, ts):
    pages = refs[:SLC_PAGES]
    (q_ref, sel_ref, e_ref, enew_ref, newslc_ref, win_ref, newwin_ref, oc_ref, gates_ref,
     hn_ref, m_scr, l_scr, acc_scr) = refs[SLC_PAGES:]
    c = pl.program_id(1)
    rows = N_HEADS * ts
    q = q_ref[0].astype(F32)[:, 0:ts, :].reshape(rows, LANES).astype(BF16)
    sel2 = sel_ref[0].astype(F32)[:, 0:ts, :].reshape(N_KV * ts, sel_ref.shape[3]).astype(BF16)

    @pl.when(c == 0)
    def _():
        m_scr[...] = jnp.full((rows, 1), NEG, F32)
        l_scr[...] = jnp.zeros((rows, 1), F32)
        acc_scr[...] = jnp.zeros((rows, LANES), F32)

    def online(s, pv_fn):
        m_old = m_scr[...]
        m_new = jnp.maximum(m_old, jnp.max(s, axis=1, keepdims=True))
        p = jnp.exp2(s - m_new)
        alpha = jnp.exp2(m_old - m_new)
        l_scr[...] = alpha * l_scr[...] + jnp.sum(p, axis=1, keepdims=True)
        acc_scr[...] = alpha * acc_scr[...] + pv_fn(p.astype(BF16))
        m_scr[...] = m_new

    picked = _dot(sel2, e_ref[...])
    bias = _heads_bias(picked * (-NEG) + NEG, ts)
    k_t = jnp.concatenate([pg[0, 0, 0].reshape(LANES, PAGE_SIZE).astype(BF16) for pg in pages], axis=1)
    v_t = jnp.concatenate([pg[0, 0, 1].reshape(LANES, PAGE_SIZE).astype(BF16) for pg in pages], axis=1)
    online(_dot(q, k_t) + bias, lambda p: _dot_nt(p, v_t))

    @pl.when(c == pl.num_programs(1) - 1)
    def _():
        tok = _iota((N_KV * ts, 1), 0) % ts
        key = _iota((N_KV * ts, PAGE_SIZE), 1)
        new = newslc_ref[0].astype(BF16)
        picked_n = _dot(sel2, enew_ref[...])
        bias_n = _heads_bias(jnp.where(key <= tok, picked_n, 0.0) * (-NEG) + NEG, ts)
        online(_dot_nt(q, new[:, :LANES]) + bias_n, lambda p: _dot(p, new[:, LANES:]))
        kw_t = win_ref[0, 0, 0].reshape(LANES, WINDOW).astype(BF16)
        vw_t = win_ref[0, 0, 1].reshape(LANES, WINDOW).astype(BF16)
        nwin = newwin_ref[0].astype(BF16)
        wkey = _iota((N_KV * ts, WINDOW), 1)
        bias_w = jnp.concatenate([jnp.where(wkey > tok, 0.0, NEG), jnp.where(key <= tok, 0.0, NEG)], axis=1)
        s_w = jnp.concatenate([_dot(q, kw_t), _dot_nt(q, nwin[:, :LANES])], axis=1) + _heads_bias(bias_w, ts)
        p_w = jnp.exp2(s_w - jnp.max(s_w, axis=1, keepdims=True))
        l_w = jnp.sum(p_w, axis=1, keepdims=True)
        p_w = p_w.astype(BF16)
        o_w = (_dot_nt(p_w[:, :WINDOW], vw_t) + _dot(p_w[:, WINDOW:], nwin[:, LANES:])) / l_w
        o_s = acc_scr[...] / l_scr[...]
        low = _iota((1, LANES), 1) < N_HD
        vals = []
        for h in range(N_HEADS):
            g, r = divmod(h, N_REP)
            gt = gates_ref[0, g, 0:ts, :]
            rs = slice(h * ts, (h + 1) * ts)
            vals.append(gt[:, 3 * r:3 * r + 1] * oc_ref[0, h, 0:ts, :] + gt[:, 3 * r + 1:3 * r + 2] * o_s[rs]
                        + gt[:, 3 * r + 2:3 * r + 3] * o_w[rs])
        slabs = []
        for pair in range(N_HEADS // 2):
            a, b = vals[2 * pair], vals[2 * pair + 1]
            if pair < N_REP // 2:
                lo, hi = a, pltpu.roll(b, N_HD, axis=1)
            else:
                lo, hi = pltpu.roll(a, N_HD, axis=1), b
            slabs.append(jnp.where(low, lo, hi))
        out = jnp.concatenate(slabs, axis=1)
        pad = jnp.zeros((hn_ref.shape[1] - ts, N_WIDTH), F32)
        hn_ref[0] = jnp.concatenate([out, pad], axis=0).astype(BF16)


def _slc_sample(page_table, cache_t, layer, q_pad, sel, e_main, e_new, new_slc, win_t, new_win, oc, gates, *, ts):
    db, n_pages = page_table.shape
    tpad = q_pad.shape[2]
    nsel = sel.shape[3]
    n_steps = n_pages // SLC_PAGES
    step_keys = SLC_PAGES * PAGE_SIZE
    rows = N_HEADS * ts

    def page_spec(i):
        return pl.BlockSpec((1, 1, 2, N_KV, N_HD, PAGE_SIZE),
                            lambda b, c, pt: (layer, pt[b, c * SLC_PAGES + i], 0, 0, 0, 0))

    per_b = lambda shape: pl.BlockSpec((1,) + shape, lambda b, c, pt: (b,) + (0,) * len(shape))
    return pl.pallas_call(
        functools.partial(_slc_sample_kernel, ts=ts),
        grid_spec=pltpu.PrefetchScalarGridSpec(
            num_scalar_prefetch=1,
            grid=(db, n_steps),
            in_specs=[page_spec(i) for i in range(SLC_PAGES)] + [
                per_b((N_HEADS, tpad, LANES)), per_b((N_KV, tpad, nsel)),
                pl.BlockSpec((nsel, step_keys), lambda b, c, pt: (0, c)),
                pl.BlockSpec((nsel, PAGE_SIZE), lambda b, c, pt: (0, 0)),
                per_b((PAGE_SIZE, 256)),
                pl.BlockSpec((1, 1, 2, N_KV, N_HD, WINDOW), lambda b, c, pt: (layer, b, 0, 0, 0, 0)),
                per_b((PAGE_SIZE, 256)), per_b((N_HEADS, tpad, LANES)), per_b((N_KV, tpad, LANES))],
            out_specs=per_b((tpad, N_WIDTH)),
            scratch_shapes=[pltpu.VMEM((rows, 1), F32), pltpu.VMEM((rows, 1), F32), pltpu.VMEM((rows, LANES), F32)]),
        out_shape=jax.ShapeDtypeStruct((db, tpad, N_WIDTH), BF16),
        compiler_params=_cparams(("parallel", "arbitrary")),
        name="nsa_slc_sample",
    )(page_table, *([cache_t] * SLC_PAGES), q_pad, sel, e_main, e_new, new_slc, win_t, new_win, oc, gates)


def _rope_tables(pos):
    half = N_HD // 2
    inv = ROPE_THETA ** (-jnp.arange(half, dtype=F32) / half)
    ang = pos.astype(F32)[:, None] * inv[None, :]
    cos, sin = jnp.cos(ang), jnp.sin(ang)
    return jnp.tile(jnp.concatenate([cos, cos], axis=1), (1, 2)), jnp.tile(jnp.concatenate([-sin, sin], axis=1), (1, 2))


def _pad_cols(a, w):
    return jnp.pad(a, ((0, 0), (0, w - a.shape[1])))


def _prep_layer(l, w_norm_mix, w_in, b_mlstm_if, w_mlstm_conv, w_mlstm_hnorm, w_gla_gate2, b_gla_gate,
                w_gla_hnorm, w_qk_norm, w_cmp_pe, w_cmp_1, w_cmp_2, w_out, w_norm_ffn, w_ffn_up, w_ffn_down):
    wi = w_in[l]
    col = lambda n: wi[:, _OFF[n][0]:_OFF[n][0] + _OFF[n][1]]
    um = _pad_cols(jnp.concatenate([col('m_qk'), col('m_v'), col('m_o'), col('m_i'), col('m_f')], axis=1), UM_W)
    ug = _pad_cols(jnp.concatenate([col('g_q'), col('g_k'), col('g_v'), col('g_o'), col('g_lr')], axis=1), UG_W)
    ncmp = jnp.concatenate([col('n_kc'), col('n_vc')], axis=1)
    nsw = jnp.concatenate([col('n_ks'), col('n_vs'), col('n_kw'), col('n_vw')], axis=1)
    ngate = _pad_cols(col('n_gate'), NGATE_W)
    p = {}
    p['w_cat'] = jnp.concatenate([um, ug, col('n_q'), ncmp, nsw, ngate], axis=1).astype(BF16)
    p['g_mix'] = w_norm_mix[l][None, :]
    p['wconv'] = jnp.pad(w_mlstm_conv[l], ((0, 8 - M_CONV), (0, 0)))
    p['bif'] = _pad_cols(b_mlstm_if[l][None, :], LANES)
    p['gh_m'] = w_mlstm_hnorm[l][None, :]
    p['w2p'] = jnp.pad(w_gla_gate2[l], ((0, LANES - G_RANK), (0, 0)))
    p['bg'] = b_gla_gate[l][None, :]
    p['gh_g'] = w_gla_hnorm[l][None, :]
    gqk = w_qk_norm[l]
    p['gq'] = jnp.tile(gqk[0], N_HEADS)[None, :]
    p['gkc'] = jnp.tile(gqk[1], N_KV)[None, :]
    p['gks'] = jnp.tile(gqk[2], N_KV)[None, :]
    p['gkw'] = jnp.tile(gqk[3], N_KV)[None, :]
    w1 = w_cmp_1[l].reshape(2, 2, CMP_STRIDE, N_HD, CMP_HIDDEN)
    eye = jnp.eye(N_KV, dtype=F32)
    w1r = jnp.einsum('khjdc,gf->kjgdhfc', w1, eye).reshape(2, CMP_STRIDE * N_KV * N_HD, 2 * N_KV * CMP_HIDDEN)
    p['w1r'] = w1r.astype(BF16)
    pe = w_cmp_pe[l].reshape(2, 2, CMP_STRIDE, 1, N_HD)
    pe = jnp.broadcast_to(pe, (2, 2, CMP_STRIDE, N_KV, N_HD)).reshape(2, 2, 2048)
    p['pe_r'] = jnp.pad(pe, ((0, 0), (0, 6), (0, 0)))
    p['w2r'] = jnp.einsum('kcd,gf->kgcfd', w_cmp_2[l], eye).reshape(2, N_KV * CMP_HIDDEN, N_KV * N_HD).astype(BF16)
    p['w_out'] = w_out[l].astype(BF16)
    p['g_ffn'] = w_norm_ffn[l][None, :]
    nf = 2
    tf = D_FF // nf
    wu = w_ffn_up[l]
    p['w_up_r'] = jnp.stack([jnp.concatenate([wu[:, f * tf:(f + 1) * tf], wu[:, D_FF + f * tf:D_FF + (f + 1) * tf]],
                                             axis=1) for f in range(nf)]).astype(BF16)
    p['w_dn_r'] = w_ffn_down[l].reshape(nf, tf, D_MODEL).astype(BF16)
    return p


def _cmp2slc(n_cmp, nc_pad, nsel):
    m = np.zeros((nc_pad, nsel), np.float32)
    per = SLC_BLOCK // CMP_STRIDE
    for n in range(n_cmp):
        for u in range(CMP_LEN // CMP_STRIDE):
            m[n + 1, (n + u) // per] += 1.0
    return jnp.asarray(m, BF16)


def _expand_mat(nsel, kvlen):
    return jnp.asarray((np.arange(kvlen)[None, :] // SLC_BLOCK) == np.arange(nsel)[:, None], BF16)


def _group_mean_mat():
    idx = np.arange(LANES) // N_HD
    return jnp.asarray((idx[:, None] == idx[None, :]) / float(N_HD), BF16)


def _mixers(p, um, ug, mstate, gstate, *, L, n_valid):
    conv0, ct0, n0, m0 = mstate
    hm, conv_o, ct_o, n_o, m_o = _mlstm(um, conv0, ct0, n0, m0, p['wconv'], p['bif'], p['gh_m'], L=L, n_valid=n_valid)
    hg, s_o = _gla(ug, gstate, p['w2p'], p['bg'], p['gh_g'], L=L, n_valid=n_valid)
    return hm, hg, (conv_o, ct_o, n_o, m_o), s_o


def _mlstm_state_in(conv, c, n, m):
    b = conv.shape[0]
    conv0 = jnp.pad(conv.astype(F32), ((0, 0), (8 - (M_CONV - 1), 0), (0, 0)))
    ct0 = jnp.swapaxes(c.astype(F32), -1, -2).reshape(b, M_WIDTH, M_HD)
    n0 = jnp.broadcast_to(n.astype(F32).reshape(b, 1, M_WIDTH), (b, 8, M_WIDTH))
    m0 = jnp.broadcast_to(jnp.pad(m.astype(F32), ((0, 0), (0, 8 - M_HEADS)))[:, :, None], (b, 8, LANES))
    return conv0, ct0, n0, m0


def _mlstm_state_out(conv_o, ct_o, n_o, m_o):
    b = conv_o.shape[0]
    c = jnp.swapaxes(ct_o.reshape(b, M_HEADS, M_HD, M_HD), -1, -2)
    return c, n_o[:, 0].reshape(b, M_HEADS, M_HD), m_o[:, :M_HEADS, 0], conv_o[:, 8 - (M_CONV - 1):]


def kernel(x_prompt, x_sample, cache_cmp_kv, cache_slc_kv, state_win_kv, state_mlstm_C, state_mlstm_n,
           state_mlstm_m, state_mlstm_conv, state_gla_S, page_table, w_norm_mix, w_in, b_mlstm_if,
           w_mlstm_conv, w_mlstm_hnorm, w_gla_gate2, b_gla_gate, w_gla_hnorm, w_qk_norm, w_cmp_pe,
           w_cmp_1, w_cmp_2, w_out, w_norm_ffn, w_ffn_up, w_ffn_down):
    b, t, _ = x_prompt.shape
    db, td, _ = x_sample.shape
    depth = w_in.shape[0]
    n_pages = page_table.shape[1]
    past = n_pages * PAGE_SIZE
    win_buf = state_win_kv.shape[2]
    assert t % 512 == 0 and td < CMP_STRIDE and td <= SAMPLE_T and win_buf == WINDOW
    assert n_pages % SLC_PAGES == 0 and n_pages % CMP_PAGES == 0

    bm = _group_mean_mat()
    cos_p, sin_p = _rope_tables(jnp.arange(t))
    nu_p = t // CMP_STRIDE
    ncmp_p = (t - CMP_LEN) // CMP_STRIDE + 1
    cos_cp, sin_cp = _rope_tables(jnp.arange(nu_p) * CMP_STRIDE + CMP_LEN - CMP_STRIDE - 1)
    nsel_p = -(-t // SLC_BLOCK)
    nsel_p = -(-nsel_p // LANES) * LANES
    c2s_p = _cmp2slc(ncmp_p, nu_p, nsel_p)
    e_p = _expand_mat(nsel_p, t).T
    tq_p = 256
    cos_s, sin_s = _rope_tables(past + jnp.arange(SAMPLE_T))
    nu_s = past // CMP_STRIDE
    ncmp_s = (past + td - CMP_LEN) // CMP_STRIDE + 1
    cos_cs, sin_cs = _rope_tables(jnp.arange(nu_s) * CMP_STRIDE + CMP_LEN - CMP_STRIDE - 1)
    nsel_s = -(-(-(-(past + td) // SLC_BLOCK)) // LANES) * LANES
    c2s_s = _cmp2slc(ncmp_s, nu_s, nsel_s)
    e_s = _expand_mat(nsel_s, past + PAGE_SIZE)
    e_s_main, e_s_new = e_s[:, :past], e_s[:, past:]
    row_minor = lambda a: jnp.transpose(a, (0, 1, 3, 4, 5, 2))
    cmp_t, slc_t, win_t = row_minor(cache_cmp_kv), row_minor(cache_slc_kv), row_minor(state_win_kv)

    xp = x_prompt.reshape(b * t, D_MODEL)
    xs = jnp.pad(x_sample, ((0, 0), (0, SAMPLE_T - td), (0, 0))).reshape(db * SAMPLE_T, D_MODEL)

    zero_m = _mlstm_state_in(jnp.zeros((b, M_CONV - 1, 2 * M_WIDTH), F32), jnp.zeros((b, M_HEADS, M_HD, M_HD), F32),
                             jnp.zeros((b, M_HEADS, M_HD), F32), jnp.zeros((b, M_HEADS), F32))
    zero_g = jnp.zeros((b, LANES, G_DV), F32)

    pl_out = [[] for _ in range(8)]
    sl_out = [[] for _ in range(8)]
    for l in range(depth):
        p = _prep_layer(l, w_norm_mix, w_in, b_mlstm_if, w_mlstm_conv, w_mlstm_hnorm, w_gla_gate2, b_gla_gate,
                        w_gla_hnorm, w_qk_norm, w_cmp_pe, w_cmp_1, w_cmp_2, w_out, w_norm_ffn, w_ffn_up, w_ffn_down)
        um, ug, nq, ncmp, nsw, ngate = _pre(xp, p['g_mix'], p['w_cat'])
        r3 = lambda a, bb, tt: a.reshape(bb, tt, a.shape[-1])
        hm, hg, mst, gst = _mixers(p, r3(um, b, t), r3(ug, b, t), zero_m, zero_g, L=M_CHUNK, n_valid=M_CHUNK)
        q_hm, ks, vs, kw, vw, slc_f, win_f, gates = _rows(r3(nq, b, t), r3(nsw, b, t), r3(ngate, b, t), cos_p, sin_p,
                                                          p['gq'], p['gks'], p['gkw'], bm)
        kc, vc = _compress(ncmp.reshape(b, nu_p, 4096), p['w1r'], p['pe_r'], p['w2r'], p['gkc'], cos_cp, sin_cp, bm)
        oc, sel = _cmp_attn(q_hm, kc, vc, c2s_p, tq=tq_p, pos0=0, n_cmp=ncmp_p)
        hn = _slcwin(q_hm, ks, vs, kw, vw, sel, e_p, oc, gates, tq=tq_p)
        xp = _post(xp, hm.reshape(b * t, -1), hg.reshape(b * t, -1), hn.reshape(b * t, -1),
                   p['w_out'], p['g_ffn'], p['w_up_r'], p['w_dn_r'])
        c_o, n_o, m_o, conv_o = _mlstm_state_out(*mst)
        kv6 = lambda a: a.reshape(a.shape[0], a.shape[1], 2, N_KV, N_HD)
        for lst, val in zip(pl_out, (kv6(ncmp.reshape(b, t, 256)), kv6(slc_f), kv6(win_f[:, t - min(WINDOW, t):]),
                                     c_o, n_o, m_o, conv_o, gst.reshape(b, G_HEADS, G_DK, G_DV))):
            lst.append(val)

        um, ug, nq, ncmp, nsw, ngate = _pre(xs, p['g_mix'], p['w_cat'])
        mstate = _mlstm_state_in(state_mlstm_conv[l], state_mlstm_C[l], state_mlstm_n[l], state_mlstm_m[l])
        gstate = state_gla_S[l].astype(F32).reshape(db, LANES, G_DV)
        hm, hg, mst, gst = _mixers(p, r3(um, db, SAMPLE_T), r3(ug, db, SAMPLE_T), mstate, gstate,
                                   L=SAMPLE_T, n_valid=td)
        q_hm, ks_n, vs_n, kw_n, vw_n, slc_f, win_f, gates = _rows(
            r3(nq, db, SAMPLE_T), r3(nsw, db, SAMPLE_T), r3(ngate, db, SAMPLE_T), cos_s, sin_s,
            p['gq'], p['gks'], p['gkw'], bm)
        kc, vc = _compress_paged(page_table, cmp_t, l, p['w1r'], p['pe_r'], p['w2r'], p['gkc'], cos_cs, sin_cs, bm)
        oc, sel = _cmp_attn(q_hm, kc, vc, c2s_s, tq=SAMPLE_T, pos0=past, n_cmp=ncmp_s)
        pad_page = lambda a: jnp.pad(a, ((0, 0), (0, PAGE_SIZE - a.shape[1]), (0, 0)))
        hn = _slc_sample(page_table, slc_t, l, q_hm, sel, e_s_main, e_s_new, pad_page(slc_f), win_t,
                         pad_page(win_f), oc, gates, ts=-(-td // 8) * 8)
        new_win = jnp.concatenate([state_win_kv[l][:, td:].astype(F32), kv6(win_f[:, :td])], axis=1)
        xs = _post(xs, hm.reshape(db * SAMPLE_T, -1), hg.reshape(db * SAMPLE_T, -1), hn.reshape(db * SAMPLE_T, -1),
                   p['w_out'], p['g_ffn'], p['w_up_r'], p['w_dn_r'])
        c_o, n_o, m_o, conv_o = _mlstm_state_out(*mst)
        for lst, val in zip(sl_out, (kv6(ncmp.reshape(db, SAMPLE_T, 256)[:, :td]), kv6(slc_f[:, :td]),
                                     new_win,
                                     c_o, n_o, m_o, conv_o, gst.reshape(db, G_HEADS, G_DK, G_DV))):
            lst.append(val)

    outs_p = [jnp.stack(a) for a in pl_out]
    outs_s = [jnp.stack(a) for a in sl_out]
    y_p = xp.reshape(b, t, D_MODEL)
    y_s = xs.reshape(db, SAMPLE_T, D_MODEL)[:, :td]
    return (y_p, y_s, *outs_p, *outs_s)
```

```python
import functools

import numpy as np
import jax
import jax.numpy as jnp
from jax import lax
from jax.experimental import pallas as pl
from jax.experimental.pallas import tpu as pltpu

F32 = jnp.float32
BF16 = jnp.bfloat16

D_MODEL = 1024
M_HEADS, M_HD, M_WIDTH, M_CONV, M_CHUNK = 4, 64, 256, 4, 64
G_HEADS, G_DK, G_DV, G_WIDTH, G_RANK, G_TAU, G_CHUNK = 4, 32, 64, 256, 16, 16.0, 64
N_HEADS, N_HD, N_KV, N_REP, N_WIDTH = 8, 64, 2, 4, 512
CMP_LEN, CMP_STRIDE, CMP_HIDDEN = 32, 16, 128
SLC_BLOCK, SLC_TOPK, WINDOW = 64, 16, 512
ROPE_THETA = 10000.0
D_FF = 2816
PAGE_SIZE = 128
NEG = -1e30
FORCE_BONUS = 1e4
EPS = 1e-6
LOG2E = 1.4426950408889634

LANES = 128
VMEM_LIMIT = 56 * 1024 * 1024
KV_TILE = 512
SAMPLE_T = 16

_OFF = {}
_o = 0
for _name, _w in (('m_qk', 512), ('m_v', 256), ('m_i', 4), ('m_f', 4), ('m_o', 256),
                  ('g_q', 128), ('g_k', 128), ('g_v', 256), ('g_lr', 16), ('g_o', 256),
                  ('n_q', 512), ('n_kc', 128), ('n_vc', 128), ('n_ks', 128),
                  ('n_vs', 128), ('n_kw', 128), ('n_vw', 128), ('n_gate', 24)):
    _OFF[_name] = (_o, _w)
    _o += _w
UM_W, UG_W, NQ_W, NCMP_W, NSW_W, NGATE_W = 1152, 896, 512, 256, 512, 128


def _cparams(sem):
    return pltpu.CompilerParams(dimension_semantics=sem, vmem_limit_bytes=VMEM_LIMIT)


def _dot(a, b):
    return jnp.dot(a, b, preferred_element_type=F32)


def _dot_nt(a, b):
    return lax.dot_general(a, b, (((1,), (1,)), ((), ())), preferred_element_type=F32)


def _dot_hi(a, b):
    return jnp.dot(a, b, preferred_element_type=F32, precision=lax.Precision.HIGHEST)


def _dot_split(a, b_bf):
    hi = a.astype(BF16)
    lo = (a - hi.astype(F32)).astype(BF16)
    return _dot(hi, b_bf) + _dot(lo, b_bf)


def _log_sigmoid(x):
    return jnp.minimum(x, 0.0) - jnp.log(1.0 + jnp.exp(-jnp.abs(x)))


def _sigmoid(x):
    return 1.0 / (1.0 + jnp.exp(-x))


def _transpose(x):
    r, c = x.shape
    if r < LANES:
        x = jnp.concatenate([x, jnp.zeros((LANES - r, c), x.dtype)], axis=0)
    parts = [x[:, i * LANES:(i + 1) * LANES].T[:, :r] for i in range(c // LANES)]
    return parts[0] if len(parts) == 1 else jnp.concatenate(parts, axis=0)


def _transpose2d(x):
    r, c = x.shape
    rp, cp = -(-r // LANES) * LANES, -(-c // LANES) * LANES
    if cp > c:
        x = jnp.concatenate([x, jnp.zeros((r, cp - c), x.dtype)], axis=1)
    if rp > r:
        x = jnp.concatenate([x, jnp.zeros((rp - r, cp), x.dtype)], axis=0)
    out_rows = []
    for j in range(cp // LANES):
        blocks = [x[i * LANES:(i + 1) * LANES, j * LANES:(j + 1) * LANES].T for i in range(rp // LANES)]
        out_rows.append(blocks[0] if len(blocks) == 1 else jnp.concatenate(blocks, axis=1))
    out = out_rows[0] if len(out_rows) == 1 else jnp.concatenate(out_rows, axis=0)
    return out[:c, :r]


def _iota(shape, dim):
    return lax.broadcasted_iota(jnp.int32, shape, dim)


def _pre_kernel(x_ref, g_ref, w_ref, um_ref, ug_ref, nq_ref, ncmp_ref, nsw_ref, ngate_ref):
    x = x_ref[...]
    h = x * lax.rsqrt(jnp.mean(x * x, axis=-1, keepdims=True) + EPS) * g_ref[...]
    u = _dot(h.astype(BF16), w_ref[...])
    o = 0
    for ref, w in ((um_ref, UM_W), (ug_ref, UG_W), (nq_ref, NQ_W), (ncmp_ref, NCMP_W),
                   (nsw_ref, NSW_W), (ngate_ref, NGATE_W)):
        ref[...] = u[:, o:o + w]
        o += w


def _pre(x2, g, w_cat):
    m = x2.shape[0]
    tm = min(m, 256)
    widths = (UM_W, UG_W, NQ_W, NCMP_W, NSW_W, NGATE_W)
    return pl.pallas_call(
        _pre_kernel,
        grid=(m // tm,),
        in_specs=[pl.BlockSpec((tm, D_MODEL), lambda i: (i, 0)),
                  pl.BlockSpec((1, D_MODEL), lambda i: (0, 0)),
                  pl.BlockSpec((D_MODEL, sum(widths)), lambda i: (0, 0))],
        out_specs=[pl.BlockSpec((tm, w), lambda i: (i, 0)) for w in widths],
        out_shape=[jax.ShapeDtypeStruct((m, w), F32) for w in widths],
        compiler_params=_cparams(("parallel",)),
        name="pre_proj",
    )(x2, g, w_cat)


def _post_kernel(x_ref, hm_ref, hg_ref, hn_ref, wout_ref, g_ref, wup_ref, wdn_ref, o_ref, h2_ref, *, tf):
    @pl.when(pl.program_id(1) == 0)
    def _():
        xn = x_ref[...]
        xn = xn + _dot(hm_ref[...], wout_ref[0:256, :])
        xn = xn + _dot(hg_ref[...], wout_ref[256:512, :])
        xn = xn + _dot(hn_ref[...], wout_ref[512:1024, :])
        o_ref[...] = xn
        h2 = xn * lax.rsqrt(jnp.mean(xn * xn, axis=-1, keepdims=True) + EPS) * g_ref[...]
        h2_ref[...] = h2.astype(BF16)

    au = _dot(h2_ref[...], wup_ref[0])
    a = au[:, :tf]
    act = (a * _sigmoid(a) * au[:, tf:]).astype(BF16)
    o_ref[...] += _dot(act, wdn_ref[0])


def _post(x2, hm, hg, hn, w_out, g, w_up_r, w_dn_r):
    m = x2.shape[0]
    tm = min(m, 512)
    nf, _, tf2 = w_up_r.shape
    tf = tf2 // 2
    return pl.pallas_call(
        functools.partial(_post_kernel, tf=tf),
        grid=(m // tm, nf),
        in_specs=[pl.BlockSpec((tm, D_MODEL), lambda i, f: (i, 0)),
                  pl.BlockSpec((tm, M_WIDTH), lambda i, f: (i, 0)),
                  pl.BlockSpec((tm, G_WIDTH), lambda i, f: (i, 0)),
                  pl.BlockSpec((tm, N_WIDTH), lambda i, f: (i, 0)),
                  pl.BlockSpec((D_MODEL, D_MODEL), lambda i, f: (0, 0)),
                  pl.BlockSpec((1, D_MODEL), lambda i, f: (0, 0)),
                  pl.BlockSpec((1, D_MODEL, tf2), lambda i, f: (f, 0, 0)),
                  pl.BlockSpec((1, tf, D_MODEL), lambda i, f: (f, 0, 0))],
        out_specs=pl.BlockSpec((tm, D_MODEL), lambda i, f: (i, 0)),
        out_shape=jax.ShapeDtypeStruct((m, D_MODEL), F32),
        scratch_shapes=[pltpu.VMEM((tm, D_MODEL), BF16)],
        compiler_params=_cparams(("parallel", "arbitrary")),
        name="post_ffn",
    )(x2, hm, hg, hn, w_out, g, w_up_r, w_dn_r)


CHUNK_UNROLL = 8


def _mlstm_kernel(um_ref, conv0_ref, ct0_ref, n0_ref, m0_ref, wconv_ref, bif_ref, gh_ref,
                  hm_ref, conv_out_ref, ct_out_ref, n_out_ref, m_out_ref,
                  xpad, q_s, k_s, ct_s, n_s, m_s, *, L, n_valid, blk):
    j = pl.program_id(1)

    @pl.when(j == 0)
    def _():
        xpad[0:8, :] = conv0_ref[0]
        ct_s[...] = ct0_ref[0]
        n_s[...] = n0_ref[0]
        m_s[...] = m0_ref[0]

    qk_pre = um_ref[0, :, 0:2 * M_WIDTH]
    xpad[8:8 + blk, :] = qk_pre
    wc = wconv_ref[...]
    acc = (xpad[5:5 + blk, :] * wc[0:1] + xpad[6:6 + blk, :] * wc[1:2]
           + xpad[7:7 + blk, :] * wc[2:3] + qk_pre * wc[3:4])
    qk = acc * _sigmoid(acc)
    q_s[...] = qk[:, :M_WIDTH]
    k_s[...] = qk[:, M_WIDTH:] * (M_HD ** -0.5)
    last = n_valid if blk == L else blk
    conv_out_ref[0] = xpad[last:last + 8, :]
    xpad[0:8, :] = xpad[blk:blk + 8, :]

    row = _iota((L, L), 0)
    col = _iota((L, L), 1)
    causal = row >= col
    tril = causal.astype(F32)
    lane_w = _iota((1, M_WIDTH), 1) // M_HD
    row_w = _iota((M_WIDTH, 1), 0) // M_HD
    valid_col = _iota((L, 1), 0) < n_valid

    row8 = _iota((8, LANES), 0)

    n_chunks = blk // L
    unroll = CHUNK_UNROLL if n_chunks % CHUNK_UNROLL == 0 else 1
    H = range(M_HEADS)
    U = range(unroll)
    hmask = [lane_w == h for h in H]
    lst = slice(n_valid - 1, n_valid)

    def group(i, state):
        ct, n_row, m_tile = state
        r0 = [pl.multiple_of((i * unroll + u) * L, L) for u in U]
        qc = [q_s[pl.ds(r, L), :] for r in r0]
        kc = [k_s[pl.ds(r, L), :] for r in r0]
        act = [um_ref[0, pl.ds(r, L), 1024:1152] + bif_ref[...] for r in r0]
        bcum = [_dot_hi(tril, _log_sigmoid(x)) for x in act]
        act_t = [_transpose(x) for x in act]
        bcum_t = [_transpose(x) for x in bcum]
        k_bf = [x.astype(BF16) for x in kc]
        kt_bf = [_transpose(x).astype(BF16) for x in kc]
        bcol = [[bcum[u][:, 4 + h:5 + h] for h in H] for u in U]
        dmat = [[jnp.where(causal, bcol[u][h] - bcum_t[u][4 + h:5 + h, :] + act_t[u][h:h + 1, :], NEG) for h in H]
                for u in U]
        m_loc = [[jnp.max(dmat[u][h], axis=1, keepdims=True) for h in H] for u in U]
        q_h = [[jnp.where(hmask[h], qc[u], 0.0) for h in H] for u in U]
        q_bf = [[q_h[u][h].astype(BF16) for h in H] for u in U]
        s = [[_dot_nt(q_bf[u][h], k_bf[u]) * jnp.exp(dmat[u][h] - m_loc[u][h]) for h in H] for u in U]
        v_h = [[um_ref[0, pl.ds(r, L), 512 + h * M_HD:512 + (h + 1) * M_HD] for h in H] for r in r0]
        sv = [[_dot(s[u][h].astype(BF16), v_h[u][h].astype(BF16)) for h in H] for u in U]
        ssum = [[jnp.sum(s[u][h], axis=1, keepdims=True) for h in H] for u in U]
        w_l = [[jnp.where(valid_col, jnp.exp(bcol[u][h][lst] - bcol[u][h] + act[u][:, h:h + 1] - m_loc[u][h][lst]), 0.0)
                for h in H] for u in U]
        upd = [[_dot(kt_bf[u], (v_h[u][h] * w_l[u][h]).astype(BF16)) for h in H] for u in U]
        ksum = [[jnp.sum(kc[u] * w_l[u][h], axis=0, keepdims=True) for h in H] for u in U]
        gate = [[_sigmoid(um_ref[0, pl.ds(r, L), 768 + h * M_HD:768 + (h + 1) * M_HD]) for h in H] for r in r0]
        for u in U:
            ct_bf = ct.astype(BF16)
            m_inter = [m_tile[h:h + 1, 0:1] + bcol[u][h] for h in H]
            m_new = [jnp.maximum(m_inter[h], m_loc[u][h]) for h in H]
            f = [jnp.exp(m_loc[u][h] - m_new[h]) for h in H]
            a_inter = [jnp.exp(m_inter[h] - m_new[h]) for h in H]
            num = [a_inter[h] * _dot(q_bf[u][h], ct_bf) + f[h] * sv[u][h] for h in H]
            qn = [jnp.sum(q_h[u][h] * n_row, axis=1, keepdims=True) for h in H]
            den = [a_inter[h] * qn[h] + f[h] * ssum[u][h] for h in H]
            hh = [gate[u][h] * (num[h] / jnp.maximum(jnp.abs(den[h]), jnp.exp(-m_new[h]))) for h in H]
            ms = [jnp.mean(x * x, axis=1, keepdims=True) for x in hh]
            outs = [hh[h] * lax.rsqrt(ms[h] + EPS) * gh_ref[:, h * M_HD:(h + 1) * M_HD] for h in H]
            hm_ref[0, pl.ds(r0[u], L), :] = jnp.concatenate(outs, axis=1).astype(BF16)
            ct_old, n_old = ct, n_row
            for h in H:
                ct = jnp.where(row_w == h, a_inter[h][lst] * ct_old + f[h][lst] * upd[u][h], ct)
                n_row = jnp.where(hmask[h], a_inter[h][lst] * n_old + f[h][lst] * ksum[u][h], n_row)
                m_tile = jnp.where(row8 == h, m_new[h][lst], m_tile)
        return ct, n_row, m_tile

    ct, n_row, m_tile = lax.fori_loop(0, n_chunks // unroll, group, (ct_s[...], n_s[0:1, :], m_s[...]))
    ct_s[...] = ct
    n_s[...] = jnp.broadcast_to(n_row, n_s.shape)
    m_s[...] = m_tile

    @pl.when(j == pl.num_programs(1) - 1)
    def _():
        ct_out_ref[0] = ct
        n_out_ref[0] = jnp.broadcast_to(n_row, n_s.shape)
        m_out_ref[0] = m_tile


def _mlstm(um, conv0, ct0, n0, m0, wconv, bif, gh, *, L, n_valid):
    b, t, _ = um.shape
    blk = min(t, 512)
    kern = functools.partial(_mlstm_kernel, L=L, n_valid=n_valid, blk=blk)
    per_b = lambda shape: pl.BlockSpec((1,) + shape, lambda i, j: (i,) + (0,) * len(shape))
    const = lambda shape: pl.BlockSpec(shape, lambda i, j: (0,) * len(shape))
    return pl.pallas_call(
        kern,
        grid=(b, t // blk),
        in_specs=[pl.BlockSpec((1, blk, UM_W), lambda i, j: (i, j, 0)),
                  per_b((8, 2 * M_WIDTH)), per_b((M_WIDTH, M_HD)), per_b((8, M_WIDTH)), per_b((8, LANES)),
                  const((8, 2 * M_WIDTH)), const((1, LANES)), const((1, M_WIDTH))],
        out_specs=[pl.BlockSpec((1, blk, M_WIDTH), lambda i, j: (i, j, 0)),
                   per_b((8, 2 * M_WIDTH)), per_b((M_WIDTH, M_HD)), per_b((8, M_WIDTH)), per_b((8, LANES))],
        out_shape=[jax.ShapeDtypeStruct((b, t, M_WIDTH), BF16),
                   jax.ShapeDtypeStruct((b, 8, 2 * M_WIDTH), F32),
                   jax.ShapeDtypeStruct((b, M_WIDTH, M_HD), F32),
                   jax.ShapeDtypeStruct((b, 8, M_WIDTH), F32),
                   jax.ShapeDtypeStruct((b, 8, LANES), F32)],
        scratch_shapes=[pltpu.VMEM((blk + 8, 2 * M_WIDTH), F32),
                        pltpu.VMEM((blk, M_WIDTH), F32), pltpu.VMEM((blk, M_WIDTH), F32),
                        pltpu.VMEM((M_WIDTH, M_HD), F32), pltpu.VMEM((8, M_WIDTH), F32),
                        pltpu.VMEM((8, LANES), F32)],
        compiler_params=_cparams(("parallel", "arbitrary")),
        name="mlstm",
    )(um, conv0, ct0, n0, m0, wconv, bif, gh)


def _gla_kernel(ug_ref, s0_ref, w2_ref, bg_ref, gh_ref, hg_ref, s_out_ref, s_s, *, L, n_valid, blk):
    j = pl.program_id(1)

    @pl.when(j == 0)
    def _():
        s_s[...] = s0_ref[0]

    row = _iota((L, L), 0)
    col = _iota((L, L), 1)
    causal = row >= col
    tril = causal.astype(F32)
    lane_k = _iota((1, LANES), 1) // G_DK
    row_k = _iota((LANES, 1), 0) // G_DK
    valid_col = _iota((L, 1), 0) < n_valid
    mid = max(n_valid // 2, 1)

    n_chunks = blk // L
    unroll = CHUNK_UNROLL if n_chunks % CHUNK_UNROLL == 0 else 1
    H = range(G_HEADS)
    U = range(unroll)
    hmask = [lane_k == h for h in H]

    def group(i, s_all):
        r0 = [pl.multiple_of((i * unroll + u) * L, L) for u in U]
        q = [ug_ref[0, pl.ds(r, L), 0:128] * (G_DK ** -0.5) for r in r0]
        k = [ug_ref[0, pl.ds(r, L), 128:256] for r in r0]
        z = [_dot_hi(ug_ref[0, pl.ds(r, L), 768:896], w2_ref[...]) + bg_ref[...] for r in r0]
        g = [_log_sigmoid(x) * (1.0 / G_TAU) for x in z]
        bc = [_dot_hi(tril, x) for x in g]
        c_ref = [x[mid - 1:mid] for x in bc]
        last = [x[n_valid - 1:n_valid] for x in bc]
        qe = [q[u] * jnp.exp(bc[u] - c_ref[u]) for u in U]
        ke = [(k[u] * jnp.exp(c_ref[u] - bc[u])).astype(BF16) for u in U]
        qin = [q[u] * jnp.exp(bc[u]) for u in U]
        kd_t = [_transpose(jnp.where(valid_col, k[u] * jnp.exp(last[u] - bc[u]), 0.0)).astype(BF16) for u in U]
        decay = [jnp.exp(_transpose(jnp.broadcast_to(x, (8, LANES)))[:, 0:1]) for x in last]
        v_h = [[ug_ref[0, pl.ds(r, L), 256 + h * G_DV:256 + (h + 1) * G_DV].astype(BF16) for h in H] for r in r0]
        a = [[jnp.where(causal, _dot_nt(jnp.where(hmask[h], qe[u], 0.0).astype(BF16), ke[u]), 0.0) for h in H]
             for u in U]
        intra = [[_dot(a[u][h].astype(BF16), v_h[u][h]) for h in H] for u in U]
        q_in = [[jnp.where(hmask[h], qin[u], 0.0).astype(BF16) for h in H] for u in U]
        upd = [[_dot(kd_t[u], v_h[u][h]) for h in H] for u in U]
        gate = [[ug_ref[0, pl.ds(r, L), 512 + h * G_DV:512 + (h + 1) * G_DV] for h in H] for r in r0]
        gate = [[x * _sigmoid(x) * gh_ref[:, h * G_DV:(h + 1) * G_DV] for h, x in enumerate(gs)] for gs in gate]
        for u in U:
            s_bf = s_all.astype(BF16)
            o = [_dot(q_in[u][h], s_bf) + intra[u][h] for h in H]
            ms = [jnp.mean(x * x, axis=1, keepdims=True) for x in o]
            outs = [o[h] * lax.rsqrt(ms[h] + EPS) * gate[u][h] for h in H]
            hg_ref[0, pl.ds(r0[u], L), :] = jnp.concatenate(outs, axis=1).astype(BF16)
            s_all = decay[u] * s_all
            for h in H:
                s_all = s_all + jnp.where(row_k == h, upd[u][h], 0.0)
        return s_all

    s_fin = lax.fori_loop(0, n_chunks // unroll, group, s_s[...])
    s_s[...] = s_fin

    @pl.when(j == pl.num_programs(1) - 1)
    def _():
        s_out_ref[0] = s_fin


def _gla(ug, s0, w2p, bg, gh, *, L, n_valid):
    b, t, _ = ug.shape
    blk = min(t, 512)
    kern = functools.partial(_gla_kernel, L=L, n_valid=n_valid, blk=blk)
    return pl.pallas_call(
        kern,
        grid=(b, t // blk),
        in_specs=[pl.BlockSpec((1, blk, UG_W), lambda i, j: (i, j, 0)),
                  pl.BlockSpec((1, LANES, G_DV), lambda i, j: (i, 0, 0)),
                  pl.BlockSpec((LANES, LANES), lambda i, j: (0, 0)),
                  pl.BlockSpec((1, LANES), lambda i, j: (0, 0)),
                  pl.BlockSpec((1, G_WIDTH), lambda i, j: (0, 0))],
        out_specs=[pl.BlockSpec((1, blk, G_WIDTH), lambda i, j: (i, j, 0)),
                   pl.BlockSpec((1, LANES, G_DV), lambda i, j: (i, 0, 0))],
        out_shape=[jax.ShapeDtypeStruct((b, t, G_WIDTH), BF16),
                   jax.ShapeDtypeStruct((b, LANES, G_DV), F32)],
        scratch_shapes=[pltpu.VMEM((LANES, G_DV), F32)],
        compiler_params=_cparams(("parallel", "arbitrary")),
        name="gla",
    )(ug, s0, w2p, bg, gh)


def _group_mean_sq(x, bm_bf):
    return _dot_split(x * x, bm_bf)


def _rope_slab(y, cos, sin_signed):
    lane = _iota(y.shape, 1)
    rot = jnp.where((lane % N_HD) < (N_HD // 2), pltpu.roll(y, 96, axis=1), pltpu.roll(y, 32, axis=1))
    return y * cos + rot * sin_signed


def _norm_rope_slab(x, g, cos, sin_signed, bm_bf):
    y = x * lax.rsqrt(_group_mean_sq(x, bm_bf) + EPS) * g
    return _rope_slab(y, cos, sin_signed)


def _rows_kernel(nq_ref, nsw_ref, ngate_ref, cos_ref, sin_ref, gq_ref, gks_ref, gkw_ref, bm_ref,
                 q_ref, ks_ref, vs_ref, kw_ref, vw_ref, slc_ref, win_ref, gates_ref):
    cos = cos_ref[...]
    sin = sin_ref[...]
    bm = bm_ref[...]
    low = _iota((1, LANES), 1) < N_HD
    for sl in range(4):
        x = nq_ref[0, :, sl * LANES:(sl + 1) * LANES]
        y = _norm_rope_slab(x, gq_ref[:, sl * LANES:(sl + 1) * LANES], cos, sin, bm) * (N_HD ** -0.5 * LOG2E)
        y_sw = pltpu.roll(y, N_HD, axis=1)
        if sl < 2:
            even, odd = jnp.where(low, y, 0.0), jnp.where(low, y_sw, 0.0)
        else:
            even, odd = jnp.where(low, 0.0, y_sw), jnp.where(low, 0.0, y)
        q_ref[0, 2 * sl] = even.astype(BF16)
        q_ref[0, 2 * sl + 1] = odd.astype(BF16)
    ks = _norm_rope_slab(nsw_ref[0, :, 0:128], gks_ref[...], cos, sin, bm)
    vs = nsw_ref[0, :, 128:256]
    kw = _norm_rope_slab(nsw_ref[0, :, 256:384], gkw_ref[...], cos, sin, bm)
    vw = nsw_ref[0, :, 384:512]
    slc_ref[0, :, 0:128] = ks
    slc_ref[0, :, 128:256] = vs
    win_ref[0, :, 0:128] = kw
    win_ref[0, :, 128:256] = vw
    ks_ref[0] = ks.astype(BF16)
    kw_ref[0] = kw.astype(BF16)
    for ref, val in ((vs_ref, vs), (vw_ref, vw)):
        ref[0, 0] = jnp.where(low, val, 1.0).astype(BF16)
        ref[0, 1] = jnp.where(low, 1.0, val).astype(BF16)
    gt = _sigmoid(ngate_ref[0])
    gates_ref[0, 0] = gt
    gates_ref[0, 1] = pltpu.roll(gt, LANES - 3 * N_REP, axis=1)


def _rows(nq, nsw, ngate, cos, sin, gq, gks, gkw, bm):
    b, t, _ = nq.shape
    tm = min(t, 512)
    tok = lambda w: pl.BlockSpec((1, tm, w), lambda i, j: (i, j, 0))
    const = lambda shape: pl.BlockSpec(shape, lambda i, j: (0,) * len(shape))
    k_shape = jax.ShapeDtypeStruct((b, t, LANES), BF16)
    v_shape = jax.ShapeDtypeStruct((b, N_KV, t, LANES), BF16)
    per_group = pl.BlockSpec((1, N_KV, tm, LANES), lambda i, j: (i, 0, j, 0))
    return pl.pallas_call(
        _rows_kernel,
        grid=(b, t // tm),
        in_specs=[tok(NQ_W), tok(NSW_W), tok(NGATE_W),
                  pl.BlockSpec((tm, LANES), lambda i, j: (j, 0)),
                  pl.BlockSpec((tm, LANES), lambda i, j: (j, 0)),
                  const((1, NQ_W)), const((1, LANES)), const((1, LANES)), const((LANES, LANES))],
        out_specs=[pl.BlockSpec((1, N_HEADS, tm, LANES), lambda i, j: (i, 0, j, 0)),
                   tok(LANES), per_group, tok(LANES), per_group, tok(256), tok(256), per_group],
        out_shape=[jax.ShapeDtypeStruct((b, N_HEADS, t, LANES), BF16), k_shape, v_shape, k_shape, v_shape,
                   jax.ShapeDtypeStruct((b, t, 256), F32), jax.ShapeDtypeStruct((b, t, 256), F32),
                   jax.ShapeDtypeStruct((b, N_KV, t, LANES), F32)],
        compiler_params=_cparams(("parallel", "parallel")),
        name="nsa_rows",
    )(nq, nsw, ngate, cos, sin, gq, gks, gkw, bm)


def _block_mlp(x_bf, w1, pe, w2, prev_first):
    p = _dot(x_bf, w1)
    bias = _dot(pe.astype(BF16), w1)
    pa = p[:, :256] + bias[0:1, :256]
    pb = p[:, 256:] + bias[1:2, 256:]
    pa_prev = jnp.where(_iota(pa.shape, 0) == 0, prev_first, pltpu.roll(pa, 1, axis=0))
    pre = pa_prev + pb
    hid = (pre * _sigmoid(pre)).astype(BF16)
    return _dot(hid, w2), pa[pa.shape[0] - 1:, :]


def _compress_kernel(u_ref, w1_ref, pe_ref, w2_ref, gkc_ref, cos_ref, sin_ref, bm_ref, kc_ref, vc_ref):
    outs = []
    for kv in range(2):
        x = jnp.concatenate(
            [u_ref[0, :, j * 256 + kv * LANES:j * 256 + (kv + 1) * LANES] for j in range(CMP_STRIDE)],
            axis=1).astype(BF16)
        outs.append(_block_mlp(x, w1_ref[kv], pe_ref[kv], w2_ref[kv], jnp.zeros((1, 256), F32))[0])
    kc_ref[0] = _norm_rope_slab(outs[0], gkc_ref[...], cos_ref[...], sin_ref[...], bm_ref[...]).astype(BF16)
    vc_ref[0] = outs[1].astype(BF16)


CMP_PAGES = 64
PAGE_UNITS = PAGE_SIZE // CMP_STRIDE


def _compress_paged_kernel(pt_ref, *refs):
    pages = refs[:CMP_PAGES]
    (w1_ref, pe_ref, w2_ref, gkc_ref, cos_ref, sin_ref, bm_ref, kc_ref, vc_ref, t_scr, x_scr, carry_scr) = refs[CMP_PAGES:]

    @pl.when(pl.program_id(1) == 0)
    def _():
        carry_scr[...] = jnp.zeros(carry_scr.shape, F32)

    outs = []
    for kv in range(2):
        for i in range(CMP_PAGES):
            t_scr[i] = pages[i][0, 0, kv].reshape(LANES, PAGE_SIZE).T
            for j in range(CMP_STRIDE):
                x_scr[i * PAGE_UNITS:(i + 1) * PAGE_UNITS, j * LANES:(j + 1) * LANES] = \
                    t_scr[i, pl.ds(j, PAGE_UNITS, stride=CMP_STRIDE), :]
        out, last = _block_mlp(x_scr[...].astype(BF16), w1_ref[kv], pe_ref[kv], w2_ref[kv], carry_scr[kv, 0:1, :])
        carry_scr[kv, 0:1, :] = last
        outs.append(out)
    kc_ref[0] = _norm_rope_slab(outs[0], gkc_ref[...], cos_ref[...], sin_ref[...], bm_ref[...]).astype(BF16)
    vc_ref[0] = outs[1].astype(BF16)


def _compress_paged(page_table, cache_t, layer, w1r, pe_r, w2r, gkc, cos_c, sin_c, bm):
    db, n_pages = page_table.shape
    step_units = CMP_PAGES * PAGE_UNITS
    nu = n_pages * PAGE_UNITS

    def page_spec(i):
        return pl.BlockSpec((1, 1, 2, N_KV, N_HD, PAGE_SIZE),
                            lambda b, c, pt: (layer, pt[b, c * CMP_PAGES + i], 0, 0, 0, 0))

    const = lambda shape: pl.BlockSpec(shape, lambda b, c, pt: (0,) * len(shape))
    tab = pl.BlockSpec((step_units, LANES), lambda b, c, pt: (c, 0))
    out = jax.ShapeDtypeStruct((db, nu, LANES), BF16)
    ospec = pl.BlockSpec((1, step_units, LANES), lambda b, c, pt: (b, c, 0))
    return pl.pallas_call(
        _compress_paged_kernel,
        grid_spec=pltpu.PrefetchScalarGridSpec(
            num_scalar_prefetch=1,
            grid=(db, n_pages // CMP_PAGES),
            in_specs=[page_spec(i) for i in range(CMP_PAGES)] + [
                const((2, 2048, 512)), const((2, 8, 2048)), const((2, 256, LANES)),
                const((1, LANES)), tab, tab, const((LANES, LANES))],
            out_specs=[ospec, ospec],
            scratch_shapes=[pltpu.VMEM((CMP_PAGES, PAGE_SIZE, LANES), F32),
                            pltpu.VMEM((step_units, CMP_STRIDE * LANES), F32),
                            pltpu.VMEM((2, 8, 256), F32)]),
        out_shape=[out, out],
        compiler_params=_cparams(("parallel", "arbitrary")),
        name="nsa_compress_paged",
    )(page_table, *([cache_t] * CMP_PAGES), w1r, pe_r, w2r, gkc, cos_c, sin_c, bm)


def _compress(units, w1r, pe_r, w2r, gkc, cos_c, sin_c, bm):
    b, nu, _ = units.shape
    const = lambda shape: pl.BlockSpec(shape, lambda i: (0,) * len(shape))
    out = jax.ShapeDtypeStruct((b, nu, LANES), BF16)
    return pl.pallas_call(
        _compress_kernel,
        grid=(b,),
        in_specs=[pl.BlockSpec((1, nu, 4096), lambda i: (i, 0, 0)),
                  const((2, 2048, 512)), const((2, 8, 2048)), const((2, 256, LANES)),
                  const((1, LANES)), const((nu, LANES)), const((nu, LANES)), const((LANES, LANES))],
        out_specs=[pl.BlockSpec((1, nu, LANES), lambda i: (i, 0, 0))] * 2,
        out_shape=[out, out],
        compiler_params=_cparams(("parallel",)),
        name="nsa_compress",
    )(units, w1r, pe_r, w2r, gkc, cos_c, sin_c, bm)


def _cmp_kernel(q_ref, kc_ref, vc_ref, c2s_ref, oc_ref, sel_ref, *, tq, pos0, n_cmp, nsel):
    qi = pl.program_id(1)
    nc_pad = kc_ref.shape[1]
    q = q_ref[0].reshape(N_HEADS * tq, LANES)
    s = _dot_nt(q, kc_ref[0]).reshape(N_HEADS, tq, nc_pad)
    qpos = pos0 + qi * tq + _iota((tq, 1), 0)
    c_idx = _iota((tq, nc_pad), 1)
    c_real = jnp.where(c_idx >= 1, c_idx, nc_pad + n_cmp) <= n_cmp
    c_ok = jnp.where(c_real, c_idx * CMP_STRIDE + (CMP_LEN - CMP_STRIDE - 1), jnp.int32(2 ** 30)) <= qpos
    s = s + jnp.where(c_ok, 0.0, NEG)[None]
    p = jnp.exp2(s - jnp.max(s, axis=-1, keepdims=True))
    p = p * ((qpos >= CMP_LEN - 1).astype(F32)[None] / jnp.sum(p, axis=-1, keepdims=True))
    oc_ref[0] = _dot(p.reshape(N_HEADS * tq, nc_pad).astype(BF16), vc_ref[0]).reshape(N_HEADS, tq, LANES)
    psum = jnp.sum(p.reshape(N_KV, N_REP, tq, nc_pad), axis=1).reshape(N_KV * tq, nc_pad)
    imp = _dot_split(psum, c2s_ref[...])
    work = _transpose2d(imp)
    ncol = N_KV * tq
    qpos_row = pos0 + qi * tq + _iota((1, ncol), 1) % tq
    blk = _iota((nsel, ncol), 0)
    cur = qpos_row // SLC_BLOCK
    forced = jnp.where(blk == 0, 1.0, jnp.where(blk == cur, 1.0, jnp.where(blk == cur - 1, 1.0, 0.0)))
    work = jnp.where(blk * SLC_BLOCK <= qpos_row, work + FORCE_BONUS * forced, NEG)
    blk_f = blk.astype(F32)
    for _ in range(SLC_TOPK):
        m = jnp.max(work, axis=0, keepdims=True)
        idx = jnp.min(jnp.where(work == m, blk_f, float(nsel)), axis=0, keepdims=True)
        work = jnp.where(blk_f == idx, -jnp.inf, work)
    sel = jnp.where(work == -jnp.inf, 1.0, 0.0)
    sel_ref[0] = _transpose2d(sel).reshape(N_KV, tq, nsel).astype(BF16)


def _cmp_attn(q_pad, kc, vc, c2s, *, tq, pos0, n_cmp):
    b, _, t, _ = q_pad.shape
    nc_pad = kc.shape[1]
    nsel = c2s.shape[1]
    kern = functools.partial(_cmp_kernel, tq=tq, pos0=pos0, n_cmp=n_cmp, nsel=nsel)
    return pl.pallas_call(
        kern,
        grid=(b, t // tq),
        in_specs=[pl.BlockSpec((1, N_HEADS, tq, LANES), lambda i, j: (i, 0, j, 0)),
                  pl.BlockSpec((1, nc_pad, LANES), lambda i, j: (i, 0, 0)),
                  pl.BlockSpec((1, nc_pad, LANES), lambda i, j: (i, 0, 0)),
                  pl.BlockSpec((nc_pad, nsel), lambda i, j: (0, 0))],
        out_specs=[pl.BlockSpec((1, N_HEADS, tq, LANES), lambda i, j: (i, 0, j, 0)),
                   pl.BlockSpec((1, N_KV, tq, nsel), lambda i, j: (i, 0, j, 0))],
        out_shape=[jax.ShapeDtypeStruct((b, N_HEADS, t, LANES), F32),
                   jax.ShapeDtypeStruct((b, N_KV, t, nsel), BF16)],
        compiler_params=_cparams(("parallel", "parallel")),
        name="nsa_cmp_topk",
    )(q_pad, kc, vc, c2s)


SCORE_DTYPE = F32
SLAB = 64


def _tile_scores(q_parts, k, s_scr, *, tq, width):
    for r in range(N_REP):
        s_scr[r * tq:(r + 1) * tq, 0:width] = _dot_nt(q_parts[r], k).astype(s_scr.dtype)


def _tile_update(v, bias_scr, s_scr, p_scr, m_scr, alpha_scr, acc_scr, *, tq, width):
    nch = width // LANES
    sd = s_scr.dtype

    def chunk(rows, i, c):
        x = s_scr[rows, c * LANES:(c + 1) * LANES]
        if bias_scr is not None:
            x = x + bias_scr[i * SLAB:(i + 1) * SLAB, c * LANES:(c + 1) * LANES]
        return x

    slabs = [(slice(r * tq + i * SLAB, r * tq + (i + 1) * SLAB), i) for r in range(N_REP) for i in range(tq // SLAB)]
    for rows, i in slabs:
        mx = chunk(rows, i, 0)
        for c in range(1, nch):
            mx = jnp.maximum(mx, chunk(rows, i, c))
        m_old = m_scr[rows, :]
        m_new = jnp.maximum(m_old, jnp.max(mx.astype(F32), axis=1, keepdims=True)).astype(sd).astype(F32)
        alpha_scr[rows, :] = jnp.exp2(m_old - m_new)
        m_scr[rows, :] = m_new
    for rows, i in slabs:
        m_new = m_scr[rows, :].astype(sd)
        for c in range(nch):
            p_scr[rows, c * LANES:(c + 1) * LANES] = jnp.exp2(chunk(rows, i, c) - m_new).astype(BF16)
    for r in range(N_REP):
        rs = slice(r * tq, (r + 1) * tq)
        acc_scr[rs, :] = alpha_scr[rs, :] * acc_scr[rs, :] + _dot(p_scr[rs, 0:width], v)


def _slcwin_kernel(q_ref, ks_ref, vs_ref, kw_ref, vw_ref, sel_ref, et_ref, oc_ref, gates_ref, hn_ref,
                   s_scr, bias_scr, p_scr, m_scr, alpha_scr, acc_scr, ow_scr, *, tq):
    g = pl.program_id(1)
    qi = pl.program_id(2)
    rows = N_REP * tq
    q0 = qi * tq
    qpos = q0 + _iota((tq, 1), 0)
    scr = dict(p_scr=p_scr, m_scr=m_scr, alpha_scr=alpha_scr, acc_scr=acc_scr, tq=tq)

    def reset():
        m_scr[...] = jnp.full((rows, LANES), NEG, F32)
        acc_scr[...] = jnp.zeros((rows, LANES), F32)

    def result():
        acc = acc_scr[...]
        return acc / pltpu.roll(acc, N_HD, axis=1)

    wlen = WINDOW + tq
    start = pl.multiple_of(jnp.maximum(q0 - WINDOW, 0), tq)
    reset()
    dist = qpos - (start + _iota((tq, wlen), 1))
    bias_scr[:, 0:wlen] = jnp.where(jnp.where(dist >= 0, dist, WINDOW) < WINDOW, 0.0, NEG).astype(bias_scr.dtype)
    _tile_scores([q_ref[0, r] for r in range(N_REP)], kw_ref[0, pl.ds(start, wlen), :], s_scr, tq=tq, width=wlen)
    _tile_update(vw_ref[0, 0, pl.ds(start, wlen), :], bias_scr, s_scr, width=wlen, **scr)
    ow_scr[...] = result()

    reset()
    sel_m = ((sel_ref[0, 0].astype(F32) - 1.0) * (-NEG)).astype(BF16)
    q_aug = [jnp.concatenate([q_ref[0, r], sel_m], axis=1) for r in range(N_REP)]
    n_kv = (q0 + tq + KV_TILE - 1) // KV_TILE

    def tile(j, width, bias):
        k0 = pl.multiple_of(j * KV_TILE, KV_TILE)
        k_aug = jnp.concatenate([ks_ref[0, pl.ds(k0, width), :], et_ref[pl.ds(k0, width), :]], axis=1)
        _tile_scores(q_aug, k_aug, s_scr, tq=tq, width=width)
        _tile_update(vs_ref[0, 0, pl.ds(k0, width), :], bias, s_scr, width=width, **scr)

    n_plain = n_kv - 1

    def body(j, carry):
        tile(j, KV_TILE, None)
        return carry

    lax.fori_loop(0, n_plain, body, 0)

    def last_tile(width):
        bias_scr[:, 0:width] = jnp.where(n_plain * KV_TILE + _iota((tq, width), 1) <= qpos, 0.0, NEG).astype(
            bias_scr.dtype)
        tile(n_plain, width, bias_scr)

    if KV_TILE % tq == 0 and tq < KV_TILE:
        aligned = q0 % KV_TILE == 0
        pl.when(aligned)(lambda: last_tile(tq))
        pl.when(jnp.logical_not(aligned))(lambda: last_tile(KV_TILE))
    else:
        last_tile(KV_TILE)
    o_s_all = result()

    gt = gates_ref[0, 0]
    low = _iota((1, LANES), 1) < N_HD
    vals = []
    for r in range(N_REP):
        rs = slice(r * tq, (r + 1) * tq)
        vals.append(gt[:, 3 * r:3 * r + 1] * oc_ref[0, r] + gt[:, 3 * r + 1:3 * r + 2] * o_s_all[rs]
                    + gt[:, 3 * r + 2:3 * r + 3] * ow_scr[rs, :])
    for pair in range(N_REP // 2):
        a, b = vals[2 * pair], vals[2 * pair + 1]
        a_sw, b_sw = pltpu.roll(a, N_HD, axis=1), pltpu.roll(b, N_HD, axis=1)
        lo = jnp.where(g == 0, a, a_sw)
        hi = jnp.where(g == 0, b_sw, b)
        hn_ref[0, :, pair * LANES:(pair + 1) * LANES] = jnp.where(low, lo, hi).astype(BF16)


def _slcwin(q_pad, ks, vs, kw, vw, sel, e_t, oc, gates, *, tq):
    b, _, t, _ = q_pad.shape
    nsel = sel.shape[3]
    assert nsel == LANES and e_t.shape == (t, LANES)
    rows = N_REP * tq
    wlen = WINDOW + tq
    kern = functools.partial(_slcwin_kernel, tq=tq)
    qspec = pl.BlockSpec((1, N_REP, tq, LANES), lambda i, g, j: (i, g, j, 0))
    keys = pl.BlockSpec((1, t, LANES), lambda i, g, j: (i, 0, 0))
    vals = pl.BlockSpec((1, 1, t, LANES), lambda i, g, j: (i, g, 0, 0))
    return pl.pallas_call(
        kern,
        grid=(b, N_KV, t // tq),
        in_specs=[qspec, keys, vals, keys, vals,
                  pl.BlockSpec((1, 1, tq, nsel), lambda i, g, j: (i, g, j, 0)),
                  pl.BlockSpec((t, LANES), lambda i, g, j: (0, 0)),
                  qspec,
                  pl.BlockSpec((1, 1, tq, LANES), lambda i, g, j: (i, g, j, 0))],
        out_specs=pl.BlockSpec((1, tq, N_REP * N_HD), lambda i, g, j: (i, j, g)),
        out_shape=jax.ShapeDtypeStruct((b, t, N_WIDTH), BF16),
        scratch_shapes=[pltpu.VMEM((rows, wlen), SCORE_DTYPE), pltpu.VMEM((tq, wlen), SCORE_DTYPE),
                        pltpu.VMEM((rows, wlen), BF16),
                        pltpu.VMEM((rows, LANES), F32), pltpu.VMEM((rows, LANES), F32),
                        pltpu.VMEM((rows, LANES), F32), pltpu.VMEM((rows, LANES), F32)],
        compiler_params=_cparams(("parallel", "parallel", "parallel")),
        name="nsa_slc_win",
    )(q_pad, ks, vs, kw, vw, sel, e_t, oc, gates)


SLC_PAGES = 64


def _heads_bias(b2, ts):
    w = b2.shape[1]
    return jnp.broadcast_to(b2.reshape(N_KV, 1, ts, w), (N_KV, N_REP, ts, w)).reshape(N_HEADS * ts, w)


def _slc_sample_kernel(pt_ref, *refs, ts):
    pages = refs[:SLC_PAGES]
    (q_ref, sel_ref, e_ref, enew_ref, newslc_ref, win_ref, newwin_ref, oc_ref, gates_ref,
     hn_ref, m_scr, l_scr, acc_scr) = refs[SLC_PAGES:]
    c = pl.program_id(1)
    rows = N_HEADS * ts
    q = q_ref[0].astype(F32)[:, 0:ts, :].reshape(rows, LANES).astype(BF16)
    sel2 = sel_ref[0].astype(F32)[:, 0:ts, :].reshape(N_KV * ts, sel_ref.shape[3]).astype(BF16)

    @pl.when(c == 0)
    def _():
        m_scr[...] = jnp.full((rows, 1), NEG, F32)
        l_scr[...] = jnp.zeros((rows, 1), F32)
        acc_scr[...] = jnp.zeros((rows, LANES), F32)

    def online(s, pv_fn):
        m_old = m_scr[...]
        m_new = jnp.maximum(m_old, jnp.max(s, axis=1, keepdims=True))
        p = jnp.exp2(s - m_new)
        alpha = jnp.exp2(m_old - m_new)
        l_scr[...] = alpha * l_scr[...] + jnp.sum(p, axis=1, keepdims=True)
        acc_scr[...] = alpha * acc_scr[...] + pv_fn(p.astype(BF16))
        m_scr[...] = m_new

    picked = _dot(sel2, e_ref[...])
    bias = _heads_bias(picked * (-NEG) + NEG, ts)
    k_t = jnp.concatenate([pg[0, 0, 0].reshape(LANES, PAGE_SIZE).astype(BF16) for pg in pages], axis=1)
    v_t = jnp.concatenate([pg[0, 0, 1].reshape(LANES, PAGE_SIZE).astype(BF16) for pg in pages], axis=1)
    online(_dot(q, k_t) + bias, lambda p: _dot_nt(p, v_t))

    @pl.when(c == pl.num_programs(1) - 1)
    def _():
        tok = _iota((N_KV * ts, 1), 0) % ts
        key = _iota((N_KV * ts, PAGE_SIZE), 1)
        new = newslc_ref[0].astype(BF16)
        picked_n = _dot(sel2, enew_ref[...])
        bias_n = _heads_bias(jnp.where(key <= tok, picked_n, 0.0) * (-NEG) + NEG, ts)
        online(_dot_nt(q, new[:, :LANES]) + bias_n, lambda p: _dot(p, new[:, LANES:]))
        kw_t = win_ref[0, 0, 0].reshape(LANES, WINDOW).astype(BF16)
        vw_t = win_ref[0, 0, 1].reshape(LANES, WINDOW).astype(BF16)
        nwin = newwin_ref[0].astype(BF16)
        wkey = _iota((N_KV * ts, WINDOW), 1)
        bias_w = jnp.concatenate([jnp.where(wkey > tok, 0.0, NEG), jnp.where(key <= tok, 0.0, NEG)], axis=1)
        s_w = jnp.concatenate([_dot(q, kw_t), _dot_nt(q, nwin[:, :LANES])], axis=1) + _heads_bias(bias_w, ts)
        p_w = jnp.exp2(s_w - jnp.max(s_w, axis=1, keepdims=True))
        l_w = jnp.sum(p_w, axis=1, keepdims=True)
        p_w = p_w.astype(BF16)
        o_w = (_dot_nt(p_w[:, :WINDOW], vw_t) + _dot(p_w[:, WINDOW:], nwin[:, LANES:])) / l_w
        o_s = acc_scr[...] / l_scr[...]
        low = _iota((1, LANES), 1) < N_HD
        vals = []
        for h in range(N_HEADS):
            g, r = divmod(h, N_REP)
            gt = gates_ref[0, g, 0:ts, :]
            rs = slice(h * ts, (h + 1) * ts)
            vals.append(gt[:, 3 * r:3 * r + 1] * oc_ref[0, h, 0:ts, :] + gt[:, 3 * r + 1:3 * r + 2] * o_s[rs]
                        + gt[:, 3 * r + 2:3 * r + 3] * o_w[rs])
        slabs = []
        for pair in range(N_HEADS // 2):
            a, b = vals[2 * pair], vals[2 * pair + 1]
            if pair < N_REP // 2:
                lo, hi = a, pltpu.roll(b, N_HD, axis=1)
            else:
                lo, hi = pltpu.roll(a, N_HD, axis=1), b
            slabs.append(jnp.where(low, lo, hi))
        out = jnp.concatenate(slabs, axis=1)
        pad = jnp.zeros((hn_ref.shape[1] - ts, N_WIDTH), F32)
        hn_ref[0] = jnp.concatenate([out, pad], axis=0).astype(BF16)


def _slc_sample(page_table, cache_t, layer, q_pad, sel, e_main, e_new, new_slc, win_t, new_win, oc, gates, *, ts):
    db, n_pages = page_table.shape
    tpad = q_pad.shape[2]
    nsel = sel.shape[3]
    n_steps = n_pages // SLC_PAGES
    step_keys = SLC_PAGES * PAGE_SIZE
    rows = N_HEADS * ts

    def page_spec(i):
        return pl.BlockSpec((1, 1, 2, N_KV, N_HD, PAGE_SIZE),
                            lambda b, c, pt: (layer, pt[b, c * SLC_PAGES + i], 0, 0, 0, 0))

    per_b = lambda shape: pl.BlockSpec((1,) + shape, lambda b, c, pt: (b,) + (0,) * len(shape))
    return pl.pallas_call(
        functools.partial(_slc_sample_kernel, ts=ts),
        grid_spec=pltpu.PrefetchScalarGridSpec(
            num_scalar_prefetch=1,
            grid=(db, n_steps),
            in_specs=[page_spec(i) for i in range(SLC_PAGES)] + [
                per_b((N_HEADS, tpad, LANES)), per_b((N_KV, tpad, nsel)),
                pl.BlockSpec((nsel, step_keys), lambda b, c, pt: (0, c)),
                pl.BlockSpec((nsel, PAGE_SIZE), lambda b, c, pt: (0, 0)),
                per_b((PAGE_SIZE, 256)),
                pl.BlockSpec((1, 1, 2, N_KV, N_HD, WINDOW), lambda b, c, pt: (layer, b, 0, 0, 0, 0)),
                per_b((PAGE_SIZE, 256)), per_b((N_HEADS, tpad, LANES)), per_b((N_KV, tpad, LANES))],
            out_specs=per_b((tpad, N_WIDTH)),
            scratch_shapes=[pltpu.VMEM((rows, 1), F32), pltpu.VMEM((rows, 1), F32), pltpu.VMEM((rows, LANES), F32)]),
        out_shape=jax.ShapeDtypeStruct((db, tpad, N_WIDTH), BF16),
        compiler_params=_cparams(("parallel", "arbitrary")),
        name="nsa_slc_sample",
    )(page_table, *([cache_t] * SLC_PAGES), q_pad, sel, e_main, e_new, new_slc, win_t, new_win, oc, gates)


def _rope_tables(pos):
    half = N_HD // 2
    inv = ROPE_THETA ** (-jnp.arange(half, dtype=F32) / half)
    ang = pos.astype(F32)[:, None] * inv[None, :]
    cos, sin = jnp.cos(ang), jnp.sin(ang)
    return jnp.tile(jnp.concatenate([cos, cos], axis=1), (1, 2)), jnp.tile(jnp.concatenate([-sin, sin], axis=1), (1, 2))


def _pad_cols(a, w):
    return jnp.pad(a, ((0, 0), (0, w - a.shape[1])))


def _prep_layer(l, w_norm_mix, w_in, b_mlstm_if, w_mlstm_conv, w_mlstm_hnorm, w_gla_gate2, b_gla_gate,
                w_gla_hnorm, w_qk_norm, w_cmp_pe, w_cmp_1, w_cmp_2, w_out, w_norm_ffn, w_ffn_up, w_ffn_down):
    wi = w_in[l]
    col = lambda n: wi[:, _OFF[n][0]:_OFF[n][0] + _OFF[n][1]]
    um = _pad_cols(jnp.concatenate([col('m_qk'), col('m_v'), col('m_o'), col('m_i'), col('m_f')], axis=1), UM_W)
    ug = _pad_cols(jnp.concatenate([col('g_q'), col('g_k'), col('g_v'), col('g_o'), col('g_lr')], axis=1), UG_W)
    ncmp = jnp.concatenate([col('n_kc'), col('n_vc')], axis=1)
    nsw = jnp.concatenate([col('n_ks'), col('n_vs'), col('n_kw'), col('n_vw')], axis=1)
    ngate = _pad_cols(col('n_gate'), NGATE_W)
    p = {}
    p['w_cat'] = jnp.concatenate([um, ug, col('n_q'), ncmp, nsw, ngate], axis=1).astype(BF16)
    p['g_mix'] = w_norm_mix[l][None, :]
    p['wconv'] = jnp.pad(w_mlstm_conv[l], ((0, 8 - M_CONV), (0, 0)))
    p['bif'] = _pad_cols(b_mlstm_if[l][None, :], LANES)
    p['gh_m'] = w_mlstm_hnorm[l][None, :]
    p['w2p'] = jnp.pad(w_gla_gate2[l], ((0, LANES - G_RANK), (0, 0)))
    p['bg'] = b_gla_gate[l][None, :]
    p['gh_g'] = w_gla_hnorm[l][None, :]
    gqk = w_qk_norm[l]
    p['gq'] = jnp.tile(gqk[0], N_HEADS)[None, :]
    p['gkc'] = jnp.tile(gqk[1], N_KV)[None, :]
    p['gks'] = jnp.tile(gqk[2], N_KV)[None, :]
    p['gkw'] = jnp.tile(gqk[3], N_KV)[None, :]
    w1 = w_cmp_1[l].reshape(2, 2, CMP_STRIDE, N_HD, CMP_HIDDEN)
    eye = jnp.eye(N_KV, dtype=F32)
    w1r = jnp.einsum('khjdc,gf->kjgdhfc', w1, eye).reshape(2, CMP_STRIDE * N_KV * N_HD, 2 * N_KV * CMP_HIDDEN)
    p['w1r'] = w1r.astype(BF16)
    pe = w_cmp_pe[l].reshape(2, 2, CMP_STRIDE, 1, N_HD)
    pe = jnp.broadcast_to(pe, (2, 2, CMP_STRIDE, N_KV, N_HD)).reshape(2, 2, 2048)
    p['pe_r'] = jnp.pad(pe, ((0, 0), (0, 6), (0, 0)))
    p['w2r'] = jnp.einsum('kcd,gf->kgcfd', w_cmp_2[l], eye).reshape(2, N_KV * CMP_HIDDEN, N_KV * N_HD).astype(BF16)
    p['w_out'] = w_out[l].astype(BF16)
    p['g_ffn'] = w_norm_ffn[l][None, :]
    nf = 2
    tf = D_FF // nf
    wu = w_ffn_up[l]
    p['w_up_r'] = jnp.stack([jnp.concatenate([wu[:, f * tf:(f + 1) * tf], wu[:, D_FF + f * tf:D_FF + (f + 1) * tf]],
                                             axis=1) for f in range(nf)]).astype(BF16)
    p['w_dn_r'] = w_ffn_down[l].reshape(nf, tf, D_MODEL).astype(BF16)
    return p


def _cmp2slc(n_cmp, nc_pad, nsel):
    m = np.zeros((nc_pad, nsel), np.float32)
    per = SLC_BLOCK // CMP_STRIDE
    for n in range(n_cmp):
        for u in range(CMP_LEN // CMP_STRIDE):
            m[n + 1, (n + u) // per] += 1.0
    return jnp.asarray(m, BF16)


def _expand_mat(nsel, kvlen):
    return jnp.asarray((np.arange(kvlen)[None, :] // SLC_BLOCK) == np.arange(nsel)[:, None], BF16)


def _group_mean_mat():
    idx = np.arange(LANES) // N_HD
    return jnp.asarray((idx[:, None] == idx[None, :]) / float(N_HD), BF16)


def _mixers(p, um, ug, mstate, gstate, *, L, n_valid):
    conv0, ct0, n0, m0 = mstate
    hm, conv_o, ct_o, n_o, m_o = _mlstm(um, conv0, ct0, n0, m0, p['wconv'], p['bif'], p['gh_m'], L=L, n_valid=n_valid)
    hg, s_o = _gla(ug, gstate, p['w2p'], p['bg'], p['gh_g'], L=L, n_valid=n_valid)
    return hm, hg, (conv_o, ct_o, n_o, m_o), s_o


def _mlstm_state_in(conv, c, n, m):
    b = conv.shape[0]
    conv0 = jnp.pad(conv.astype(F32), ((0, 0), (8 - (M_CONV - 1), 0), (0, 0)))
    ct0 = jnp.swapaxes(c.astype(F32), -1, -2).reshape(b, M_WIDTH, M_HD)
    n0 = jnp.broadcast_to(n.astype(F32).reshape(b, 1, M_WIDTH), (b, 8, M_WIDTH))
    m0 = jnp.broadcast_to(jnp.pad(m.astype(F32), ((0, 0), (0, 8 - M_HEADS)))[:, :, None], (b, 8, LANES))
    return conv0, ct0, n0, m0


def _mlstm_state_out(conv_o, ct_o, n_o, m_o):
    b = conv_o.shape[0]
    c = jnp.swapaxes(ct_o.reshape(b, M_HEADS, M_HD, M_HD), -1, -2)
    return c, n_o[:, 0].reshape(b, M_HEADS, M_HD), m_o[:, :M_HEADS, 0], conv_o[:, 8 - (M_CONV - 1):]


def kernel(x_prompt, x_sample, cache_cmp_kv, cache_slc_kv, state_win_kv, state_mlstm_C, state_mlstm_n,
           state_mlstm_m, state_mlstm_conv, state_gla_S, page_table, w_norm_mix, w_in, b_mlstm_if,
           w_mlstm_conv, w_mlstm_hnorm, w_gla_gate2, b_gla_gate, w_gla_hnorm, w_qk_norm, w_cmp_pe,
           w_cmp_1, w_cmp_2, w_out, w_norm_ffn, w_ffn_up, w_ffn_down):
    b, t, _ = x_prompt.shape
    db, td, _ = x_sample.shape
    depth = w_in.shape[0]
    n_pages = page_table.shape[1]
    past = n_pages * PAGE_SIZE
    win_buf = state_win_kv.shape[2]
    assert t % 512 == 0 and td < CMP_STRIDE and td <= SAMPLE_T and win_buf == WINDOW
    assert n_pages % SLC_PAGES == 0 and n_pages % CMP_PAGES == 0

    bm = _group_mean_mat()
    cos_p, sin_p = _rope_tables(jnp.arange(t))
    nu_p = t // CMP_STRIDE
    ncmp_p = (t - CMP_LEN) // CMP_STRIDE + 1
    cos_cp, sin_cp = _rope_tables(jnp.arange(nu_p) * CMP_STRIDE + CMP_LEN - CMP_STRIDE - 1)
    nsel_p = -(-t // SLC_BLOCK)
    nsel_p = -(-nsel_p // LANES) * LANES
    c2s_p = _cmp2slc(ncmp_p, nu_p, nsel_p)
    e_p = _expand_mat(nsel_p, t).T
    tq_p = 256
    cos_s, sin_s = _rope_tables(past + jnp.arange(SAMPLE_T))
    nu_s = past // CMP_STRIDE
    ncmp_s = (past + td - CMP_LEN) // CMP_STRIDE + 1
    cos_cs, sin_cs = _rope_tables(jnp.arange(nu_s) * CMP_STRIDE + CMP_LEN - CMP_STRIDE - 1)
    nsel_s = -(-(-(-(past + td) // SLC_BLOCK)) // LANES) * LANES
    c2s_s = _cmp2slc(ncmp_s, nu_s, nsel_s)
    e_s = _expand_mat(nsel_s, past + PAGE_SIZE)
    e_s_main, e_s_new = e_s[:, :past], e_s[:, past:]
    row_minor = lambda a: jnp.transpose(a, (0, 1, 3, 4, 5, 2))
    cmp_t, slc_t, win_t = row_minor(cache_cmp_kv), row_minor(cache_slc_kv), row_minor(state_win_kv)

    xp = x_prompt.reshape(b * t, D_MODEL)
    xs = jnp.pad(x_sample, ((0, 0), (0, SAMPLE_T - td), (0, 0))).reshape(db * SAMPLE_T, D_MODEL)

    zero_m = _mlstm_state_in(jnp.zeros((b, M_CONV - 1, 2 * M_WIDTH), F32), jnp.zeros((b, M_HEADS, M_HD, M_HD), F32),
                             jnp.zeros((b, M_HEADS, M_HD), F32), jnp.zeros((b, M_HEADS), F32))
    zero_g = jnp.zeros((b, LANES, G_DV), F32)

    pl_out = [[] for _ in range(8)]
    sl_out = [[] for _ in range(8)]
    for l in range(depth):
        p = _prep_layer(l, w_norm_mix, w_in, b_mlstm_if, w_mlstm_conv, w_mlstm_hnorm, w_gla_gate2, b_gla_gate,
                        w_gla_hnorm, w_qk_norm, w_cmp_pe, w_cmp_1, w_cmp_2, w_out, w_norm_ffn, w_ffn_up, w_ffn_down)
        um, ug, nq, ncmp, nsw, ngate = _pre(xp, p['g_mix'], p['w_cat'])
        r3 = lambda a, bb, tt: a.reshape(bb, tt, a.shape[-1])
        hm, hg, mst, gst = _mixers(p, r3(um, b, t), r3(ug, b, t), zero_m, zero_g, L=M_CHUNK, n_valid=M_CHUNK)
        q_hm, ks, vs, kw, vw, slc_f, win_f, gates = _rows(r3(nq, b, t), r3(nsw, b, t), r3(ngate, b, t), cos_p, sin_p,
                                                          p['gq'], p['gks'], p['gkw'], bm)
        kc, vc = _compress(ncmp.reshape(b, nu_p, 4096), p['w1r'], p['pe_r'], p['w2r'], p['gkc'], cos_cp, sin_cp, bm)
        oc, sel = _cmp_attn(q_hm, kc, vc, c2s_p, tq=tq_p, pos0=0, n_cmp=ncmp_p)
        hn = _slcwin(q_hm, ks, vs, kw, vw, sel, e_p, oc, gates, tq=tq_p)
        xp = _post(xp, hm.reshape(b * t, -1), hg.reshape(b * t, -1), hn.reshape(b * t, -1),
                   p['w_out'], p['g_ffn'], p['w_up_r'], p['w_dn_r'])
        c_o, n_o, m_o, conv_o = _mlstm_state_out(*mst)
        kv6 = lambda a: a.reshape(a.shape[0], a.shape[1], 2, N_KV, N_HD)
        for lst, val in zip(pl_out, (kv6(ncmp.reshape(b, t, 256)), kv6(slc_f), kv6(win_f[:, t - min(WINDOW, t):]),
                                     c_o, n_o, m_o, conv_o, gst.reshape(b, G_HEADS, G_DK, G_DV))):
            lst.append(val)

        um, ug, nq, ncmp, nsw, ngate = _pre(xs, p['g_mix'], p['w_cat'])
        mstate = _mlstm_state_in(state_mlstm_conv[l], state_mlstm_C[l], state_mlstm_n[l], state_mlstm_m[l])
        gstate = state_gla_S[l].astype(F32).reshape(db, LANES, G_DV)
        hm, hg, mst, gst = _mixers(p, r3(um, db, SAMPLE_T), r3(ug, db, SAMPLE_T), mstate, gstate,
                                   L=SAMPLE_T, n_valid=td)
        q_hm, ks_n, vs_n, kw_n, vw_n, slc_f, win_f, gates = _rows(
            r3(nq, db, SAMPLE_T), r3(nsw, db, SAMPLE_T), r3(ngate, db, SAMPLE_T), cos_s, sin_s,
            p['gq'], p['gks'], p['gkw'], bm)
        kc, vc = _compress_paged(page_table, cmp_t, l, p['w1r'], p['pe_r'], p['w2r'], p['gkc'], cos_cs, sin_cs, bm)
        oc, sel = _cmp_attn(q_hm, kc, vc, c2s_s, tq=SAMPLE_T, pos0=past, n_cmp=ncmp_s)
        pad_page = lambda a: jnp.pad(a, ((0, 0), (0, PAGE_SIZE - a.shape[1]), (0, 0)))
        hn = _slc_sample(page_table, slc_t, l, q_hm, sel, e_s_main, e_s_new, pad_page(slc_f), win_t,
                         pad_page(win_f), oc, gates, ts=-(-td // 8) * 8)
        new_win = jnp.concatenate([state_win_kv[l][:, td:].astype(F32), kv6(win_f[:, :td])], axis=1)
        xs = _post(xs, hm.reshape(db * SAMPLE_T, -1), hg.reshape(db * SAMPLE_T, -1), hn.reshape(db * SAMPLE_T, -1),
                   p['w_out'], p['g_ffn'], p['w_up_r'], p['w_dn_r'])
        c_o, n_o, m_o, conv_o = _mlstm_state_out(*mst)
        for lst, val in zip(sl_out, (kv6(ncmp.reshape(db, SAMPLE_T, 256)[:, :td]), kv6(slc_f[:, :td]),
                                     new_win,
                                     c_o, n_o, m_o, conv_o, gst.reshape(db, G_HEADS, G_DK, G_DV))):
            lst.append(val)

    outs_p = [jnp.stack(a) for a in pl_out]
    outs_s = [jnp.stack(a) for a in sl_out]
    y_p = xp.reshape(b, t, D_MODEL)
    y_s = xs.reshape(db, SAMPLE_T, D_MODEL)[:, :td]
    return (y_p, y_s, *outs_p, *outs_s)
```
